```python
import math
import jax
import jax.numpy as jnp
from jax import lax
import numpy as np

D_MODEL = 2048
BATCH = 4
SEQ = 2048
DEPTH = 4
DEC_BATCH = 8
DEC_SEQ = 4
PAST_LEN = 16384
PAGE_SIZE = 128

D_A = D_MODEL // 4
S5_GROUP = 16
S5_G = D_A // S5_GROUP
S5_N = 64
D_B = D_MODEL // 4
HG_DK = 128
HG_DV = 128
HG_HEADS = D_B // HG_DV
HG_CHUNK = 64
D_C = D_MODEL // 2
NSA_DH = 128
NSA_HEADS = D_C // NSA_DH
NSA_KVH = max(1, NSA_HEADS // 4)
NSA_REP = NSA_HEADS // NSA_KVH
N_KV = NSA_KVH * NSA_DH
ROPE_DIM = NSA_DH // 4
ROPE_THETA = 500000.0
CMP_STRIDE = 16
CMP_LEN = 2 * CMP_STRIDE
SEL_BLOCK = 64
CMP_PER_SEL = SEL_BLOCK // CMP_STRIDE
N_SEL = 16
WINDOW = 512
Q_BLOCK = 64
FORCE_BONUS = 1000.0
D_FF = ((8 * D_MODEL) // 3 + 127) // 128 * 128
CONV_W = 3
EPS = 1e-6
NEG_INF = -1e30
IN_SPLITS = (D_A, HG_HEADS * HG_DK, HG_HEADS * HG_DK, D_B, D_B, D_C, 6 * N_KV, 3 * NSA_HEADS, 3 * D_MODEL)
N_IN = D_A + 2 * HG_HEADS * HG_DK + 2 * D_B + D_C + 6 * N_KV + 3 * NSA_HEADS + 3 * D_MODEL

kernel_name = 'hybrid_s5_hgrn2_nsa_convffn_step'


def _split_points():
    return [int(v) for v in np.cumsum(IN_SPLITS)[:-1]]


def _rmsnorm(x, w):
    xf = x.astype(jnp.float32)
    y = xf * lax.rsqrt(jnp.mean(xf * xf, axis=-1, keepdims=True) + EPS)
    return (y * w.astype(jnp.float32)).astype(x.dtype)


def _rope(x, pos):
    half = ROPE_DIM // 2
    inv = jnp.exp(jnp.arange(half, dtype=jnp.float32) * (-math.log(ROPE_THETA) / half))
    ang = pos.astype(jnp.float32)[:, None] * inv[None, :]
    cos = jnp.cos(ang)[:, None, :]
    sin = jnp.sin(ang)[:, None, :]
    xr = x[..., :ROPE_DIM].astype(jnp.float32)
    x1, x2 = xr[..., :half], xr[..., half:]
    rot = jnp.concatenate([x1 * cos - x2 * sin, x2 * cos + x1 * sin], axis=-1)
    return jnp.concatenate([rot.astype(x.dtype), x[..., ROPE_DIM:]], axis=-1)


def _masked_softmax(s, mask):
    s = jnp.where(mask, s.astype(jnp.float32), NEG_INF)
    p = jnp.where(mask, jnp.exp(s - jnp.max(s, axis=-1, keepdims=True)), 0.0)
    return p / jnp.maximum(jnp.sum(p, axis=-1, keepdims=True), 1e-30)


def _s5_mixer(u, h0, a_re, a_im, log_dt, b_re, b_im, c_re, c_im, d, w_glu, b_glu):
    f32 = jnp.float32
    bsz, L, _ = u.shape
    st_dtype = h0.dtype
    ug = u.astype(f32).reshape(bsz, L, S5_G, S5_GROUP)
    a_re, a_im = a_re.astype(f32), a_im.astype(f32)
    dt = jnp.exp(log_dt.astype(f32))[:, None]
    mag = jnp.exp(a_re * dt)
    ab_re, ab_im = mag * jnp.cos(a_im * dt), mag * jnp.sin(a_im * dt)
    den = a_re * a_re + a_im * a_im
    cf_re = ((ab_re - 1.0) * a_re + ab_im * a_im) / den
    cf_im = (ab_im * a_re - (ab_re - 1.0) * a_im) / den
    b_re, b_im = b_re.astype(f32), b_im.astype(f32)
    bb_re = cf_re[..., None] * b_re - cf_im[..., None] * b_im
    bb_im = cf_re[..., None] * b_im + cf_im[..., None] * b_re
    bu_re = jnp.einsum('blgc,gnc->blgn', ug, bb_re)
    bu_im = jnp.einsum('blgc,gnc->blgn', ug, bb_im)
    h0 = h0.astype(f32)
    h0_re, h0_im = h0[:, 0], h0[:, 1]
    bu_re = bu_re.at[:, 0].add(ab_re * h0_re - ab_im * h0_im)
    bu_im = bu_im.at[:, 0].add(ab_re * h0_im + ab_im * h0_re)
    a_sc_re = jnp.broadcast_to(ab_re, bu_re.shape)
    a_sc_im = jnp.broadcast_to(ab_im, bu_re.shape)

    def combine(e1, e2):
        a1r, a1i, b1r, b1i = e1
        a2r, a2i, b2r, b2i = e2
        return (a2r * a1r - a2i * a1i, a2r * a1i + a2i * a1r,
                a2r * b1r - a2i * b1i + b2r, a2r * b1i + a2i * b1r + b2i)

    _, _, h_re, h_im = lax.associative_scan(combine, (a_sc_re, a_sc_im, bu_re, bu_im), axis=1)
    y = (jnp.einsum('blgn,gcn->blgc', h_re, c_re.astype(f32))
         - jnp.einsum('blgn,gcn->blgc', h_im, c_im.astype(f32))
         + d.astype(f32) * ug)
    y = jax.nn.gelu(y.reshape(bsz, L, D_A))
    out = y * jax.nn.sigmoid(y @ w_glu.astype(f32) + b_glu.astype(f32))
    h_last = jnp.stack([h_re[:, -1], h_im[:, -1]], axis=1)
    return out.astype(u.dtype), h_last.astype(st_dtype)


def _hgrn2_scan(q, k, v, logf, s0):
    bsz, L, H, _ = q.shape
    V = v.shape[-1]
    c = math.gcd(L, HG_CHUNK)
    n = L // c

    def chunks(t):
        return t.reshape(bsz, n, c, H, t.shape[-1]).transpose(1, 0, 3, 2, 4)

    tri = jnp.tril(jnp.ones((c, c), dtype=bool))[:, :, None]

    def step(S, inp):
        qc, kc, vc, lf = inp
        b = jnp.cumsum(lf, axis=2)
        o = jnp.einsum('bhtk,bhkv->bhtv', qc * jnp.exp(b), S)
        diff = b[:, :, :, None, :] - b[:, :, None, :, :]
        dec = jnp.where(tri, jnp.exp(jnp.where(tri, diff, 0.0)), 0.0)
        att = jnp.einsum('bhtk,bhtsk,bhsk->bhts', qc, dec, kc)
        o = o + jnp.einsum('bhts,bhsv->bhtv', att, vc)
        b_last = b[:, :, -1:, :]
        S = jnp.exp(b_last[:, :, 0, :, None]) * S + jnp.einsum('bhsk,bhsv->bhkv', kc * jnp.exp(b_last - b), vc)
        return S, o

    S, o = lax.scan(step, s0, (chunks(q), chunks(k), chunks(v), chunks(logf)))
    return o.transpose(1, 0, 3, 2, 4).reshape(bsz, L, H, V), S


def _compress(kv, a, w1, b1, w2):
    bsz, Lk = kv.shape[:2]
    n_sub = Lk // CMP_STRIDE
    sub = kv[:, :n_sub * CMP_STRIDE].reshape(bsz, n_sub, CMP_STRIDE, 2, NSA_KVH, NSA_DH)
    first = jnp.einsum('bnjcgd,cjd->bncgd', sub, a[:, :CMP_STRIDE])
    second = jnp.einsum('bnjcgd,cjd->bncgd', sub, a[:, CMP_STRIDE:])
    pooled = first[:, :-1] + second[:, 1:]
    h = jnp.einsum('bncgd,cde->bncge', pooled, w1) + b1[:, None, :]
    return jnp.einsum('bncge,cef->bncgf', jax.nn.gelu(h), w2)


def _sel_blocks(kv):
    bsz, Lk = kv.shape[:2]
    n_slc = -(-Lk // SEL_BLOCK)
    kv = jnp.pad(kv, ((0, 0), (0, n_slc * SEL_BLOCK - Lk), (0, 0), (0, 0), (0, 0)))
    return kv.reshape(bsz, n_slc, SEL_BLOCK, 2, NSA_KVH, NSA_DH).transpose(0, 3, 4, 1, 2, 5)


def _nsa_block(q, gates, qpos, kc, ks_blk, vs_blk, win_kv, win_pos):
    bsz, qb = q.shape[:2]
    scale = NSA_DH ** -0.5
    qg = q.reshape(bsz, qb, NSA_KVH, NSA_REP, NSA_DH)
    n_cmp = kc.shape[1]
    cmp_end = jnp.arange(n_cmp) * CMP_STRIDE + (CMP_LEN - 1)
    s_c = jnp.einsum('bqgrd,bngd->bgrqn', qg, kc[:, :, 0]) * scale
    p_c = _masked_softmax(s_c, cmp_end[None, :] <= qpos[:, None])
    o_c = jnp.einsum('bgrqn,bngd->bqgrd', p_c, kc[:, :, 1])
    n_slc = ks_blk.shape[2]
    imp = jnp.sum(p_c, axis=2)
    imp = jnp.pad(imp, ((0, 0), (0, 0), (0, 0), (0, n_slc * CMP_PER_SEL - n_cmp)))
    imp = imp.reshape(bsz, NSA_KVH, qb, n_slc, CMP_PER_SEL).sum(-1)
    blk = jnp.arange(n_slc)[None, :]
    cur = (qpos // SEL_BLOCK)[:, None]
    forced = (blk == 0) | (blk == cur) | (blk == cur - 1)
    valid = blk * SEL_BLOCK <= qpos[:, None]
    score = jnp.where(valid, imp + FORCE_BONUS * forced, -1.0)
    n_take = min(N_SEL, n_slc)
    _, idx = lax.top_k(score, n_take)
    bi = jnp.arange(bsz)[:, None, None, None]
    gi = jnp.arange(NSA_KVH)[None, :, None, None]
    sk = ks_blk[bi, gi, idx].reshape(bsz, NSA_KVH, qb, n_take * SEL_BLOCK, NSA_DH)
    sv = vs_blk[bi, gi, idx].reshape(bsz, NSA_KVH, qb, n_take * SEL_BLOCK, NSA_DH)
    spos = (idx[..., None] * SEL_BLOCK + jnp.arange(SEL_BLOCK)).reshape(bsz, NSA_KVH, qb, n_take * SEL_BLOCK)
    s_s = jnp.einsum('bqgrd,bgqkd->bgrqk', qg, sk) * scale
    p_s = _masked_softmax(s_s, (spos <= qpos[:, None])[:, :, None])
    o_s = jnp.einsum('bgrqk,bgqkd->bqgrd', p_s, sv)
    wp = win_pos[None, :]
    tq = qpos[:, None]
    mask_w = (wp <= tq) & (wp > tq - WINDOW) & (wp >= 0)
    s_w = jnp.einsum('bqgrd,bkgd->bgrqk', qg, win_kv[:, :, 0]) * scale
    p_w = _masked_softmax(s_w, mask_w)
    o_w = jnp.einsum('bgrqk,bkgd->bqgrd', p_w, win_kv[:, :, 1])
    g = gates.reshape(bsz, qb, NSA_KVH, NSA_REP, 3).astype(jnp.float32)
    o = g[..., 0:1] * o_c + g[..., 1:2] * o_s + g[..., 2:3] * o_w
    return o.reshape(bsz, qb, D_C).astype(q.dtype)


def _nsa_mixer(qx, kvx, gx, pos, past_cmp, past_slc, win_prev, q_norm, k_norm, cmp_a, cmp_w1, cmp_b1, cmp_w2):
    bsz, L = qx.shape[:2]
    q = _rope(_rmsnorm(qx.reshape(bsz, L, NSA_HEADS, NSA_DH), q_norm), pos)
    kv = kvx.reshape(bsz, L, 3, 2, NSA_KVH, NSA_DH)
    rows = []
    for br in range(3):
        k = _rope(_rmsnorm(kv[:, :, br, 0], k_norm[br]), pos)
        rows.append(jnp.stack([k, kv[:, :, br, 1]], axis=2))
    cmp_rows, slc_rows, win_rows = rows
    gates = jax.nn.sigmoid(gx.astype(jnp.float32)).reshape(bsz, L, NSA_HEADS, 3)
    full_cmp = cmp_rows if past_cmp is None else jnp.concatenate([past_cmp, cmp_rows], axis=1)
    full_slc = slc_rows if past_slc is None else jnp.concatenate([past_slc, slc_rows], axis=1)
    kc = _compress(full_cmp, cmp_a, cmp_w1, cmp_b1, cmp_w2)
    kvs = _sel_blocks(full_slc)
    ks_blk, vs_blk = kvs[:, 0], kvs[:, 1]
    if win_prev is None:
        qb = math.gcd(L, Q_BLOCK)
        win_pad = jnp.pad(win_rows, ((0, 0), (WINDOW, 0), (0, 0), (0, 0), (0, 0)))

        def body(i):
            s = i * qb
            return _nsa_block(lax.dynamic_slice_in_dim(q, s, qb, 1),
                              lax.dynamic_slice_in_dim(gates, s, qb, 1),
                              s + jnp.arange(qb), kc, ks_blk, vs_blk,
                              lax.dynamic_slice_in_dim(win_pad, s, WINDOW + qb, 1),
                              s - WINDOW + jnp.arange(WINDOW + qb))

        o = lax.map(body, jnp.arange(L // qb))
        o = o.transpose(1, 0, 2, 3).reshape(bsz, L, D_C)
        win_new = win_rows[:, -min(WINDOW, L):]
    else:
        wb = win_prev.shape[1]
        win_kv = jnp.concatenate([win_prev.astype(win_rows.dtype), win_rows], axis=1)
        win_pos = pos[0] - wb + jnp.arange(wb + L)
        o = _nsa_block(q, gates, pos, kc, ks_blk, vs_blk, win_kv, win_pos)
        win_new = win_kv[:, -wb:]
    return o, cmp_rows, slc_rows, win_new


def _layer(x, c, pos, s5_h0, hg_s0, conv_prev, past_cmp, past_slc, win_prev, lb, p):
    f32 = jnp.float32
    bsz, L, _ = x.shape
    mod = jax.nn.silu(c) @ p['w_mod'] + p['b_mod']
    sh1, sc1, g1, sh2, sc2, g2 = jnp.split(mod[:, None, :], 6, axis=-1)
    h = _rmsnorm(x, p['norm1_w']) * (1.0 + sc1) + sh1
    u_a, hq, hf, hi, hgt, nq, nkv, ngt, mgt = jnp.split(h @ p['w_in'], _split_points(), axis=-1)
    y_a, s5_new = _s5_mixer(u_a, s5_h0, *p['s5'])
    hf = hf.astype(f32).reshape(bsz, L, HG_HEADS, HG_DK)
    lbh = lb.reshape(HG_HEADS, HG_DK)
    pos_lb = lbh > 0
    lb_safe = jnp.where(pos_lb, lbh, 1.0)
    log_sig = jax.nn.log_sigmoid(hf)
    logf = jnp.where(pos_lb, jnp.logaddexp(jnp.log(lb_safe), jnp.log1p(-lbh) + log_sig), log_sig)
    k_b = (1.0 - lbh) * jax.nn.sigmoid(-hf)
    q_b = jax.nn.silu(hq.astype(f32)).reshape(bsz, L, HG_HEADS, HG_DK)
    v_b = hi.astype(f32).reshape(bsz, L, HG_HEADS, HG_DV)
    o_b, hg_new = _hgrn2_scan(q_b, k_b, v_b, logf, hg_s0.astype(f32))
    gate_b = jax.nn.silu(hgt.astype(f32)).reshape(bsz, L, HG_HEADS, HG_DV)
    y_b = (_rmsnorm(o_b, p['hg_norm_w']) * gate_b).reshape(bsz, L, D_B).astype(x.dtype)
    y_c, cmp_rows, slc_rows, win_new = _nsa_mixer(nq, nkv, ngt, pos, past_cmp, past_slc, win_prev, *p['nsa'])
    ga, gb, gc = jnp.split(jax.nn.sigmoid(mgt), 3, axis=-1)
    merged = ga * (y_a @ p['w_branch_a']) + gb * (y_b @ p['w_branch_b']) + gc * (y_c @ p['w_branch_c'])
    x = x + g1 * (merged @ p['w_out'])
    h2 = _rmsnorm(x, p['norm2_w']) * (1.0 + sc2) + sh2
    up_a, up_b = jnp.split(h2 @ p['w_up'], 2, axis=-1)
    ext = jnp.concatenate([conv_prev.astype(up_a.dtype), up_a], axis=1)
    conv = p['conv_b'] + sum(p['conv_w'][j] * ext[:, j:j + L] for j in range(CONV_W))
    x = x + g2 * ((jax.nn.gelu(conv) * up_b) @ p['w_down'])
    return (x, s5_new, hg_new.astype(hg_s0.dtype), cmp_rows, slc_rows, win_new, ext[:, L:].astype(conv_prev.dtype))


def setup_inputs(seed: int = 0) -> dict:
    key = jax.random.key(seed)
    keys = iter(jax.random.split(key, 64))
    f32 = jnp.float32

    def nrm(shape, scale=1.0):
        return jax.random.normal(next(keys), shape, f32) * scale

    def gain(shape):
        return 1.0 + nrm(shape, 0.02)

    n_pages = PAST_LEN // PAGE_SIZE
    n_used = DEC_BATCH * n_pages
    n_pool = n_used + max(1, n_used // 4)
    perm = jax.random.permutation(next(keys), n_pool).astype(jnp.int32)
    page_table = perm[:n_used].reshape(DEC_BATCH, n_pages)
    win_buf = min(WINDOW, PAST_LEN)
    kv_pool_shape = (DEPTH, n_pool, PAGE_SIZE, 2, NSA_KVH, NSA_DH)
    return {
        'x_prompt': nrm((BATCH, SEQ, D_MODEL)),
        'x_sample': nrm((DEC_BATCH, DEC_SEQ, D_MODEL)),
        'cache_cmp': nrm(kv_pool_shape),
        'cache_slc': nrm(kv_pool_shape),
        'state_win': nrm((DEPTH, DEC_BATCH, win_buf, 2, NSA_KVH, NSA_DH)),
        'state_s5': nrm((DEPTH, DEC_BATCH, 2, S5_G, S5_N), 0.1),
        'state_hgrn': nrm((DEPTH, DEC_BATCH, HG_HEADS, HG_DK, HG_DV), 0.5),
        'state_conv': nrm((DEPTH, DEC_BATCH, CONV_W - 1, D_FF)),
        'page_table': page_table,
        'c_prompt': nrm((BATCH, D_MODEL)),
        'c_sample': nrm((DEC_BATCH, D_MODEL)),
        'w_mod': nrm((DEPTH, D_MODEL, 6 * D_MODEL), D_MODEL ** -0.5),
        'b_mod': nrm((DEPTH, 6 * D_MODEL), 0.01),
        'norm1_w': gain((DEPTH, D_MODEL)),
        'norm2_w': gain((DEPTH, D_MODEL)),
        'w_in': nrm((DEPTH, D_MODEL, N_IN), D_MODEL ** -0.5),
        's5_a_re': -0.5 + nrm((DEPTH, S5_G, S5_N), 0.01),
        's5_a_im': math.pi * jnp.arange(S5_N, dtype=f32) + nrm((DEPTH, S5_G, S5_N), 0.01),
        's5_log_dt': jax.random.uniform(next(keys), (DEPTH, S5_G), f32, math.log(1e-3), math.log(1e-1)),
        's5_b_re': nrm((DEPTH, S5_G, S5_N, S5_GROUP), (2 * S5_GROUP) ** -0.5),
        's5_b_im': nrm((DEPTH, S5_G, S5_N, S5_GROUP), (2 * S5_GROUP) ** -0.5),
        's5_c_re': nrm((DEPTH, S5_G, S5_GROUP, S5_N), S5_N ** -0.5),
        's5_c_im': nrm((DEPTH, S5_G, S5_GROUP, S5_N), S5_N ** -0.5),
        's5_d': nrm((DEPTH, S5_G, S5_GROUP)),
        's5_w_glu': nrm((DEPTH, D_A, D_A), D_A ** -0.5),
        's5_b_glu': nrm((DEPTH, D_A), 0.01),
        'hg_lb_logits': nrm((DEPTH, HG_HEADS * HG_DK)),
        'hg_norm_w': gain((DEPTH, HG_DV)),
        'nsa_q_norm': gain((DEPTH, NSA_DH)),
        'nsa_k_norm': gain((DEPTH, 3, NSA_DH)),
        'cmp_a': nrm((DEPTH, 2, CMP_LEN, NSA_DH), CMP_LEN ** -0.5),
        'cmp_w1': nrm((DEPTH, 2, NSA_DH, NSA_DH), NSA_DH ** -0.5),
        'cmp_b1': nrm((DEPTH, 2, NSA_DH), 0.01),
        'cmp_w2': nrm((DEPTH, 2, NSA_DH, NSA_DH), (2.0 / NSA_DH) ** 0.5),
        'w_branch_a': nrm((DEPTH, D_A, D_MODEL), D_A ** -0.5),
        'w_branch_b': nrm((DEPTH, D_B, D_MODEL), D_B ** -0.5),
        'w_branch_c': nrm((DEPTH, D_C, D_MODEL), D_C ** -0.5),
        'w_out': nrm((DEPTH, D_MODEL, D_MODEL), D_MODEL ** -0.5),
        'w_up': nrm((DEPTH, D_MODEL, 2 * D_FF), D_MODEL ** -0.5),
        'conv_w': nrm((DEPTH, CONV_W, D_FF), CONV_W ** -0.5),
        'conv_b': nrm((DEPTH, D_FF), 0.01),
        'w_down': nrm((DEPTH, D_FF, D_MODEL), D_FF ** -0.5),
    }


def reference(x_prompt, x_sample, cache_cmp, cache_slc, state_win, state_s5, state_hgrn, state_conv,
              page_table, c_prompt, c_sample, w_mod, b_mod, norm1_w, norm2_w, w_in,
              s5_a_re, s5_a_im, s5_log_dt, s5_b_re, s5_b_im, s5_c_re, s5_c_im, s5_d, s5_w_glu, s5_b_glu,
              hg_lb_logits, hg_norm_w, nsa_q_norm, nsa_k_norm, cmp_a, cmp_w1, cmp_b1, cmp_w2,
              w_branch_a, w_branch_b, w_branch_c, w_out, w_up, conv_w, conv_b, w_down):
    f32 = jnp.float32
    bsz, L = x_prompt.shape[:2]
    dbsz, dl = x_sample.shape[:2]
    n_pages = page_table.shape[1]
    past_len = n_pages * cache_cmp.shape[2]
    pos_p = jnp.arange(L, dtype=jnp.int32)
    pos_s = past_len + jnp.arange(dl, dtype=jnp.int32)
    probs = jax.nn.softmax(hg_lb_logits.astype(f32), axis=0)
    lower_bounds = jnp.cumsum(probs, axis=0) - probs[0:1]
    s5_zero = jnp.zeros((bsz,) + state_s5.shape[2:], state_s5.dtype)
    hg_zero = jnp.zeros((bsz,) + state_hgrn.shape[2:], state_hgrn.dtype)
    conv_zero = jnp.zeros((bsz,) + state_conv.shape[2:], state_conv.dtype)
    xp, xs = x_prompt, x_sample
    outs_p, outs_s = [], []
    for l in range(DEPTH):
        p = {
            'w_mod': w_mod[l], 'b_mod': b_mod[l], 'norm1_w': norm1_w[l], 'norm2_w': norm2_w[l],
            'w_in': w_in[l],
            's5': (s5_a_re[l], s5_a_im[l], s5_log_dt[l], s5_b_re[l], s5_b_im[l], s5_c_re[l], s5_c_im[l],
                   s5_d[l], s5_w_glu[l], s5_b_glu[l]),
            'hg_norm_w': hg_norm_w[l],
            'nsa': (nsa_q_norm[l], nsa_k_norm[l], cmp_a[l], cmp_w1[l], cmp_b1[l], cmp_w2[l]),
            'w_branch_a': w_branch_a[l], 'w_branch_b': w_branch_b[l], 'w_branch_c': w_branch_c[l],
            'w_out': w_out[l], 'w_up': w_up[l], 'conv_w': conv_w[l], 'conv_b': conv_b[l], 'w_down': w_down[l],
        }
        xp, *new_p = _layer(xp, c_prompt, pos_p, s5_zero, hg_zero, conv_zero, None, None, None,
                            lower_bounds[l], p)
        past_cmp = cache_cmp[l, page_table].reshape(dbsz, past_len, 2, NSA_KVH, NSA_DH)
        past_slc = cache_slc[l, page_table].reshape(dbsz, past_len, 2, NSA_KVH, NSA_DH)
        xs, *new_s = _layer(xs, c_sample, pos_s, state_s5[l], state_hgrn[l], state_conv[l],
                            past_cmp, past_slc, state_win[l], lower_bounds[l], p)
        outs_p.append(new_p)
        outs_s.append(new_s)
    st_p = [jnp.stack([o[i] for o in outs_p]) for i in range(6)]
    st_s = [jnp.stack([o[i] for o in outs_s]) for i in range(6)]
    return (xp, xs, st_p[0], st_s[0], st_p[1], st_s[1], st_p[2], st_s[2], st_p[3], st_s[3], st_p[4], st_s[4], st_p[5], st_s[5])
```

```python
import functools
import math

import jax
import jax.numpy as jnp
from jax import lax
from jax.experimental import pallas as pl
from jax.experimental.pallas import tpu as pltpu

F32 = jnp.float32
BF16 = jnp.bfloat16
I32 = jnp.int32

D_MODEL = 2048
D_A = D_MODEL // 4
S5_GROUP = 16
S5_G = D_A // S5_GROUP
S5_N = 64
S5_W = S5_G * S5_N
D_B = D_MODEL // 4
HG_DK = 128
HG_DV = 128
HG_HEADS = D_B // HG_DV
D_C = D_MODEL // 2
NSA_DH = 128
NSA_HEADS = D_C // NSA_DH
NSA_KVH = 2
NSA_REP = NSA_HEADS // NSA_KVH
N_KV = NSA_KVH * NSA_DH
ROPE_DIM = NSA_DH // 4
ROPE_THETA = 500000.0
CMP_STRIDE = 16
SEL_BLOCK = 64
N_SEL = 16
WINDOW = 512
FORCE_BONUS = 1000.0
D_FF = ((8 * D_MODEL) // 3 + 127) // 128 * 128
CONV_W = 3
EPS = 1e-6
NEG_INF = -1e30
PAGE = 128

LANE = 128
SUBLANE = 8
VMEM_LIMIT = 56 * 1024 * 1024

OFF_MGT = 0
OFF_NQ = 3 * D_MODEL
OFF_NKV = OFF_NQ + D_C
OFF_UA = OFF_NKV + 6 * N_KV
OFF_HQ = OFF_UA + D_A
OFF_HF = OFF_HQ + D_B
OFF_HI = OFF_HF + D_B
OFF_HGT = OFF_HI + D_B
OFF_NGT = OFF_HGT + D_B
NGT_PAD = 256
N_INP = OFF_NGT + NGT_PAD
TN_IN = 1280
D_FFP = 5632
TF = 512
HG_C = 64
HG_SB = 16


def _cparams(sem):
    return pltpu.CompilerParams(dimension_semantics=sem, vmem_limit_bytes=VMEM_LIMIT)


def _dot(a, b):
    return jnp.dot(a, b, preferred_element_type=F32)


def _dot_nt(a, b):
    return lax.dot_general(a, b, (((1,), (1,)), ((), ())), preferred_element_type=F32)


def _dot_tn(a, b):
    return lax.dot_general(a, b, (((0,), (0,)), ((), ())), preferred_element_type=F32)


def _dot2(a, b):
    hi = a.astype(BF16)
    lo = (a - hi.astype(F32)).astype(BF16)
    return _dot(hi, b) + _dot(lo, b)


def _dot3(a, b):
    hi = a.astype(BF16)
    r1 = a - hi.astype(F32)
    mid = r1.astype(BF16)
    lo = (r1 - mid.astype(F32)).astype(BF16)
    return _dot(b, hi) + _dot(b, mid) + _dot(b, lo)


def _rms(x, w):
    return x * lax.rsqrt(jnp.mean(x * x, axis=-1, keepdims=True) + EPS) * w


def _masked_softmax(s, mask):
    s = jnp.where(mask, s, NEG_INF)
    p = jnp.where(mask, jnp.exp(s - jnp.max(s, axis=-1, keepdims=True)), 0.0)
    return p / jnp.maximum(jnp.sum(p, axis=-1, keepdims=True), 1e-30)


def _log_sigmoid(x):
    return jnp.minimum(x, 0.0) - jnp.log1p(jnp.exp(-jnp.abs(x)))


def _logaddexp(a, b):
    return jnp.maximum(a, b) + jnp.log1p(jnp.exp(-jnp.abs(a - b)))


def _mod_kernel(c_ref, w_ref, b_ref, o_ref):
    c = c_ref[...]
    a = (c * jax.nn.sigmoid(c)).astype(BF16)
    o_ref[0] = _dot(a, w_ref[0].astype(BF16)) + b_ref[0]


def _mod_call(c_all, w_mod, b_mod):
    depth, d, n = w_mod.shape
    r = c_all.shape[0]
    tn = 1024
    return pl.pallas_call(
        _mod_kernel,
        grid=(depth, n // tn),
        in_specs=[pl.BlockSpec((r, d), lambda l, j: (0, 0)),
                  pl.BlockSpec((1, d, tn), lambda l, j: (l, 0, j)),
                  pl.BlockSpec((1, 1, tn), lambda l, j: (l, 0, j))],
        out_specs=pl.BlockSpec((1, r, tn), lambda l, j: (l, 0, j)),
        out_shape=jax.ShapeDtypeStruct((depth, r, n), F32),
        compiler_params=_cparams(("arbitrary", "arbitrary")),
        name="mod",
    )(c_all, w_mod, b_mod.reshape(depth, 1, n))


def _inproj_kernel(x_ref, sc_ref, sh_ref, nw_ref, w_ref, o_ref, h_scr):
    @pl.when(pl.program_id(1) == 0)
    def _():
        y = _rms(x_ref[...], nw_ref[...])
        h_scr[...] = (y * (1.0 + sc_ref[0]) + sh_ref[0]).astype(BF16)

    o_ref[...] = _dot(h_scr[...], w_ref[...])


def _mod_spec(modx, tm, rows_per_group, col):
    if modx.shape[1] == 1:
        return pl.BlockSpec((1, 1, D_MODEL), lambda i, *_: ((i * tm) // rows_per_group, 0, col))
    return pl.BlockSpec((1, tm, D_MODEL), lambda i, *_: (0, i, col))


def _inproj_call(x, modx, rows_per_group, nw, w):
    m = x.shape[0]
    tm = min(512, m)
    return pl.pallas_call(
        _inproj_kernel,
        grid=(m // tm, N_INP // TN_IN),
        in_specs=[pl.BlockSpec((tm, D_MODEL), lambda i, j: (i, 0)),
                  _mod_spec(modx, tm, rows_per_group, 1),
                  _mod_spec(modx, tm, rows_per_group, 0),
                  pl.BlockSpec((1, D_MODEL), lambda i, j: (0, 0)),
                  pl.BlockSpec((D_MODEL, TN_IN), lambda i, j: (0, j))],
        out_specs=pl.BlockSpec((tm, TN_IN), lambda i, j: (i, j)),
        out_shape=jax.ShapeDtypeStruct((m, N_INP), F32),
        scratch_shapes=[pltpu.VMEM((tm, D_MODEL), BF16)],
        compiler_params=_cparams(("arbitrary", "arbitrary")),
        name="inproj",
    )(x, modx, modx, nw, w)


S5_LC = 512


def _s5_kernel(u_ref, h0_ref, bbr_ref, bbi_ref, ccr_ref, cci_ref, apr_ref, api_ref, d_ref,
               wg_ref, bg_ref, y_ref, hl_ref, hre, him, car, *, t_rows, last_row):
    @pl.when(pl.program_id(1) == 0)
    def _():
        car[...] = h0_ref[0]

    u = u_ref[...]
    ub = u.astype(BF16)
    hre[...] = _dot(ub, bbr_ref[...])
    him[...] = _dot(ub, bbi_ref[...])
    row = lax.broadcasted_iota(I32, (SUBLANE, S5_LC), 0)
    for c in range(S5_W // S5_LC):
        sl = slice(c * S5_LC, (c + 1) * S5_LC)
        pwr = apr_ref[:, sl]
        pwi = api_ref[:, sl]
        steps = tuple((s, pwr[s - 1:s], pwi[s - 1:s]) for s in (1, 2, 4))

        def body(i, carry, sl=sl, pwr=pwr, pwi=pwi, steps=steps):
            cr, ci = carry
            r0 = pl.multiple_of(i * SUBLANE, SUBLANE)
            xr = hre[pl.ds(r0, SUBLANE), sl]
            xi = him[pl.ds(r0, SUBLANE), sl]
            for s, ar, ai in steps:
                sr = jnp.where(row >= s, pltpu.roll(xr, s, 0), 0.0)
                si = jnp.where(row >= s, pltpu.roll(xi, s, 0), 0.0)
                xr, xi = xr + (ar * sr - ai * si), xi + (ar * si + ai * sr)
            xr, xi = xr + (pwr * cr - pwi * ci), xi + (pwr * ci + pwi * cr)
            hre[pl.ds(r0, SUBLANE), sl] = xr
            him[pl.ds(r0, SUBLANE), sl] = xi
            return xr[SUBLANE - 1:SUBLANE], xi[SUBLANE - 1:SUBLANE]

        cr, ci = lax.fori_loop(0, t_rows // SUBLANE, body, (car[0:1, sl], car[1:2, sl]))
        car[0:1, sl] = cr
        car[1:2, sl] = ci

    lr = t_rows - SUBLANE + last_row
    hl_ref[0, 0:1, :] = hre[lr:lr + 1, :]
    hl_ref[0, 1:2, :] = him[lr:lr + 1, :]
    y = (_dot(hre[...].astype(BF16), ccr_ref[...]) - _dot(him[...].astype(BF16), cci_ref[...])
         + d_ref[...] * u)
    y = jax.nn.gelu(y)
    z = _dot(y.astype(BF16), wg_ref[...]) + bg_ref[...]
    y_ref[...] = (y * jax.nn.sigmoid(z)).astype(BF16)


def _s5_call(u_arr, ucol, nb, seq, h0, p, last_row):
    t_rows = min(256, seq)
    nt = seq // t_rows
    const = lambda shape: pl.BlockSpec(shape, lambda b, t: (0,) * len(shape))
    kern = functools.partial(_s5_kernel, t_rows=t_rows, last_row=last_row)
    return pl.pallas_call(
        kern,
        grid=(nb, nt),
        in_specs=[pl.BlockSpec((t_rows, D_A), lambda b, t: (b * nt + t, ucol)),
                  pl.BlockSpec((1, 2, S5_W), lambda b, t: (b, 0, 0)),
                  const((D_A, S5_W)), const((D_A, S5_W)), const((S5_W, D_A)), const((S5_W, D_A)),
                  const((SUBLANE, S5_W)), const((SUBLANE, S5_W)), const((1, D_A)),
                  const((D_A, D_A)), const((1, D_A))],
        out_specs=[pl.BlockSpec((t_rows, D_A), lambda b, t: (b * nt + t, 0)),
                   pl.BlockSpec((1, 2, S5_W), lambda b, t: (b, 0, 0))],
        out_shape=[jax.ShapeDtypeStruct((nb * seq, D_A), BF16),
                   jax.ShapeDtypeStruct((nb, 2, S5_W), F32)],
        scratch_shapes=[pltpu.VMEM((t_rows, S5_W), F32), pltpu.VMEM((t_rows, S5_W), F32),
                        pltpu.VMEM((2, S5_W), F32)],
        compiler_params=_cparams(("arbitrary", "arbitrary")),
        name="s5",
    )(u_arr, h0, p["bbr"], p["bbi"], p["ccr"], p["cci"], p["apr"], p["api"], p["d"],
      p["wg"], p["bg"])


def _hgrn_kernel(q_ref, f_ref, i_ref, g_ref, s0_ref, lp_ref, nw_ref, y_ref, sl_ref, st,
                 *, l_valid):
    t = pl.program_id(1)
    c = HG_C
    nsb = c // HG_SB

    @pl.when(t == 0)
    def _():
        for h in range(HG_HEADS):
            st[h] = s0_ref[0, h].T

    row = lax.broadcasted_iota(I32, (c, HG_DK), 0)
    col = lax.broadcasted_iota(I32, (c, HG_DK), 1)
    valid = (t * c + row) < l_valid
    tri = (row >= col).astype(BF16)[:, :c]
    srow = lax.broadcasted_iota(I32, (HG_SB, HG_DK), 0)
    ones = jnp.ones((HG_DK, HG_DK), BF16)
    grow = lax.broadcasted_iota(I32, (HG_SB, HG_SB * HG_SB), 0)
    gcol = lax.broadcasted_iota(I32, (HG_SB, HG_SB * HG_SB), 1)
    gsum = (gcol // HG_SB == grow).astype(BF16)
    for h in range(HG_HEADS):
        hs = slice(h * HG_DK, (h + 1) * HG_DK)
        loglb, log1m, oml, pos = (lp_ref[0:1, hs], lp_ref[1:2, hs], lp_ref[2:3, hs],
                                  lp_ref[3:4, hs])
        hf = f_ref[:, hs]
        ls = _log_sigmoid(hf)
        lf = jnp.where(pos > 0.5, _logaddexp(loglb, log1m + ls), ls)
        kk = oml * jax.nn.sigmoid(-hf)
        lf = jnp.where(valid, lf, 0.0)
        kk = jnp.where(valid, kk, 0.0)
        hq = q_ref[:, hs]
        q = hq * jax.nn.sigmoid(hq)
        v = i_ref[:, hs]
        b = _dot3(lf, tri)
        s_t = st[h]
        o_inter = _dot_nt((q * jnp.exp(b)).astype(BF16), s_t.astype(BF16))
        vb = v.astype(BF16)
        o_blocks = []
        for i in range(nsb):
            rs = slice(i * HG_SB, (i + 1) * HG_SB)
            bi, qi, ki, vi = b[rs], q[rs], kk[rs], v[rs]
            o_i = o_inter[rs]
            if i > 0:
                bref = b[i * HG_SB - 1:i * HG_SB]
                qq = qi * jnp.exp(bi - bref)
                kp = jnp.where(row < i * HG_SB, kk * jnp.exp(jnp.minimum(bref - b, 0.0)), 0.0)
                att = _dot_nt(qq.astype(BF16), kp.astype(BF16))
                o_i = o_i + _dot(att.astype(BF16), vb)
            pieces = []
            for tt in range(HG_SB):
                m = srow <= tt
                e = jnp.exp(jnp.where(m, bi[tt:tt + 1] - bi, 0.0))
                pieces.append(jnp.where(m, (qi[tt:tt + 1] * ki) * e, 0.0))
            pmat = jnp.concatenate(pieces, axis=0)
            rsum = _dot(pmat.astype(BF16), ones)
            z = rsum * jnp.concatenate([vi] * HG_SB, axis=0)
            o_i = o_i + _dot2g(gsum, z)
            o_blocks.append(o_i)
        o = jnp.concatenate(o_blocks, axis=0)
        bl = b[c - 1:c]
        kdec = kk * jnp.exp(bl - b)
        st[h] = s_t * jnp.exp(bl) + _dot_tn(vb, kdec.astype(BF16))
        gt = g_ref[:, hs]
        y = _rms(o, nw_ref[...]) * (gt * jax.nn.sigmoid(gt))
        y_ref[:, hs] = y.astype(BF16)

    @pl.when(t == pl.num_programs(1) - 1)
    def _():
        for h in range(HG_HEADS):
            sl_ref[0, h] = st[h].T


def _dot2g(g, z):
    hi = z.astype(BF16)
    lo = (z - hi.astype(F32)).astype(BF16)
    return _dot(g, hi) + _dot(g, lo)


def _hgrn_call(arr, cols, nb, seq_pad, l_valid, s0, lp, nw):
    nt = seq_pad // HG_C
    blk = lambda col: pl.BlockSpec((HG_C, D_B), lambda b, t: (b * nt + t, col))
    kern = functools.partial(_hgrn_kernel, l_valid=l_valid)
    return pl.pallas_call(
        kern,
        grid=(nb, nt),
        in_specs=[blk(cols[0]), blk(cols[1]), blk(cols[2]), blk(cols[3]),
                  pl.BlockSpec((1, HG_HEADS, HG_DK, HG_DV), lambda b, t: (b, 0, 0, 0)),
                  pl.BlockSpec((SUBLANE, D_B), lambda b, t: (0, 0)),
                  pl.BlockSpec((1, HG_DV), lambda b, t: (0, 0))],
        out_specs=[pl.BlockSpec((HG_C, D_B), lambda b, t: (b * nt + t, 0)),
                   pl.BlockSpec((1, HG_HEADS, HG_DK, HG_DV), lambda b, t: (b, 0, 0, 0))],
        out_shape=[jax.ShapeDtypeStruct((nb * seq_pad, D_B), BF16),
                   jax.ShapeDtypeStruct((nb, HG_HEADS, HG_DK, HG_DV), F32)],
        scratch_shapes=[pltpu.VMEM((HG_HEADS, HG_DV, HG_DK), F32)],
        compiler_params=_cparams(("arbitrary", "arbitrary")),
        name="hgrn",
    )(arr, arr, arr, arr, s0, lp, nw)


def _rope(x, cos_t, sin_a, sin_b):
    return (x * cos_t + pltpu.roll(x, NSA_DH - ROPE_DIM // 2, 1) * sin_a
            + pltpu.roll(x, ROPE_DIM // 2, 1) * sin_b)


def _nsa_prep_kernel(q_ref, kc_ref, ks_ref, kw_ref, g_ref, cos_ref, sa_ref, sb_ref, qn_ref, kn_ref,
                     qo_ref, rc_ref, rs_ref, rw_ref, bc_ref, bs_ref, bw_ref, go_ref):
    cos_t, sin_a, sin_b = cos_ref[...], sa_ref[...], sb_ref[...]
    qn = qn_ref[...]
    for h in range(NSA_HEADS):
        hs = slice(h * NSA_DH, (h + 1) * NSA_DH)
        qo_ref[:, hs] = _rope(_rms(q_ref[:, hs], qn), cos_t, sin_a, sin_b).astype(BF16)
    for br, (src, dst, dstb) in enumerate(((kc_ref, rc_ref, bc_ref), (ks_ref, rs_ref, bs_ref),
                                           (kw_ref, rw_ref, bw_ref))):
        kn = kn_ref[br:br + 1, :]
        for g in range(NSA_KVH):
            gs = slice(g * NSA_DH, (g + 1) * NSA_DH)
            k = _rope(_rms(src[:, gs], kn), cos_t, sin_a, sin_b)
            dst[:, gs] = k
            dstb[:, gs] = k.astype(BF16)
        v = src[:, N_KV:2 * N_KV]
        dst[:, N_KV:2 * N_KV] = v
        dstb[:, N_KV:2 * N_KV] = v.astype(BF16)
    go_ref[...] = jax.nn.sigmoid(g_ref[...])


def _nsa_prep_call(hin, rope_tabs, seq, qn, kn):
    m = hin.shape[0]
    tm = min(256, m)
    npos = rope_tabs[0].shape[0] // tm
    rspec = pl.BlockSpec((tm, NSA_DH), lambda i: (i % npos, 0))
    kvb = OFF_NKV // (2 * N_KV)
    return pl.pallas_call(
        _nsa_prep_kernel,
        grid=(m // tm,),
        in_specs=[pl.BlockSpec((tm, D_C), lambda i: (i, OFF_NQ // D_C)),
                  pl.BlockSpec((tm, 2 * N_KV), lambda i: (i, kvb)),
                  pl.BlockSpec((tm, 2 * N_KV), lambda i: (i, kvb + 1)),
                  pl.BlockSpec((tm, 2 * N_KV), lambda i: (i, kvb + 2)),
                  pl.BlockSpec((tm, LANE), lambda i: (i, OFF_NGT // LANE)),
                  rspec, rspec, rspec,
                  pl.BlockSpec((1, NSA_DH), lambda i: (0, 0)),
                  pl.BlockSpec((3, NSA_DH), lambda i: (0, 0))],
        out_specs=[pl.BlockSpec((tm, D_C), lambda i: (i, 0))]
        + [pl.BlockSpec((tm, 2 * N_KV), lambda i: (i, 0))] * 6
        + [pl.BlockSpec((tm, LANE), lambda i: (i, 0))],
        out_shape=[jax.ShapeDtypeStruct((m, D_C), BF16)]
        + [jax.ShapeDtypeStruct((m, 2 * N_KV), F32)] * 3
        + [jax.ShapeDtypeStruct((m, 2 * N_KV), BF16)] * 3
        + [jax.ShapeDtypeStruct((m, LANE), F32)],
        compiler_params=_cparams(("arbitrary",)),
        name="nsa_prep",
    )(hin, hin, hin, hin, hin, *rope_tabs, qn, kn)


KV_TILES = 2 * N_KV // LANE


def _pool_rows(load, n, a1_ref, a2_ref):
    firsts, seconds = [], []
    for c in range(KV_TILES):
        ls = slice(c * LANE, (c + 1) * LANE)
        first = jnp.zeros((n, LANE), F32)
        second = jnp.zeros((n, LANE), F32)
        for j in range(CMP_STRIDE):
            x = load(j, c)
            first = first + x * a1_ref[j:j + 1, ls]
            second = second + x * a2_ref[j:j + 1, ls]
        firsts.append(first)
        seconds.append(second)
    return jnp.concatenate(firsts, axis=1), jnp.concatenate(seconds, axis=1)


def _tile_rows(j, c, n):
    return pl.ds(j * KV_TILES + c, n, stride=CMP_STRIDE * KV_TILES)


def _cmp_mlp(first, second, n, w1_ref, b1_ref, w2_ref, o_ref):
    pooled = first + pltpu.roll(second, n - 1, 0)
    for c in range(2):
        for g in range(NSA_KVH):
            ls = slice((2 * c + g) * NSA_DH, (2 * c + g + 1) * NSA_DH)
            h = _dot(pooled[:, ls].astype(BF16), w1_ref[c]) + b1_ref[c:c + 1, :]
            o_ref[0, :, ls] = _dot(jax.nn.gelu(h).astype(BF16), w2_ref[c]).astype(BF16)


def _cmp_prompt_kernel(x_ref, a1_ref, a2_ref, w1_ref, b1_ref, w2_ref, o_ref, *, n):
    first, second = _pool_rows(lambda j, c: x_ref[_tile_rows(j, c, n), :], n, a1_ref, a2_ref)
    _cmp_mlp(first, second, n, w1_ref, b1_ref, w2_ref, o_ref)


def _cmp_prompt_call(rows, nb, seq, cp):
    n = seq // CMP_STRIDE
    const = lambda shape: pl.BlockSpec(shape, lambda b: (0,) * len(shape))
    return pl.pallas_call(
        functools.partial(_cmp_prompt_kernel, n=n),
        grid=(nb,),
        in_specs=[pl.BlockSpec((seq * KV_TILES, LANE), lambda b: (b, 0)),
                  const((CMP_STRIDE, 2 * N_KV)), const((CMP_STRIDE, 2 * N_KV)),
                  const((2, NSA_DH, NSA_DH)), const((2, NSA_DH)), const((2, NSA_DH, NSA_DH))],
        out_specs=pl.BlockSpec((1, n, 2 * N_KV), lambda b: (b, 0, 0)),
        out_shape=jax.ShapeDtypeStruct((nb, n, 2 * N_KV), BF16),
        compiler_params=_cparams(("arbitrary",)),
        name="cmp_prompt",
    )(rows.reshape(-1, LANE), cp["a1"], cp["a2"], cp["w1"], cp["b1"], cp["w2"])


PG = 16
SUB_PER_PAGE = PAGE // CMP_STRIDE


def _cmp_pool_cache_kernel(pt_ref, *refs):
    pages = refs[:PG]
    a1_ref, a2_ref, f_ref, s_ref = refs[PG:]
    for k in range(PG):
        first, second = _pool_rows(
            lambda j, c, k=k: pages[k][0, 0, _tile_rows(j, c, SUB_PER_PAGE), :],
            SUB_PER_PAGE, a1_ref.at[0], a2_ref.at[0])
        f_ref[0, 0, k * SUB_PER_PAGE:(k + 1) * SUB_PER_PAGE, :] = first
        s_ref[0, 0, k * SUB_PER_PAGE:(k + 1) * SUB_PER_PAGE, :] = second


def _page_specs(n_pages, layer_of, batch_of, group_of, page_shape=(PAGE, 2 * N_KV)):
    def spec(k):
        return pl.BlockSpec(
            (1, 1) + page_shape,
            lambda *a, k=k: (layer_of(*a), a[-1][batch_of(*a) * n_pages + group_of(*a) * PG + k], 0, 0))
    return [spec(k) for k in range(PG)]


def _cmp_pool_cache_call(cache, pt_flat, nb, n_pages, a1, a2):
    depth = cache.shape[0]
    npg = n_pages // PG
    nsub = n_pages * SUB_PER_PAGE
    rows = PG * SUB_PER_PAGE
    grid_spec = pltpu.PrefetchScalarGridSpec(
        num_scalar_prefetch=1,
        grid=(depth, nb, npg),
        in_specs=_page_specs(n_pages, lambda l, b, g, pt: l, lambda l, b, g, pt: b,
                             lambda l, b, g, pt: g, (PAGE * KV_TILES, LANE))
        + [pl.BlockSpec((1, CMP_STRIDE, 2 * N_KV), lambda l, b, g, pt: (l, 0, 0))] * 2,
        out_specs=[pl.BlockSpec((1, 1, rows, 2 * N_KV), lambda l, b, g, pt: (l, b, g, 0))] * 2,
    )
    return pl.pallas_call(
        _cmp_pool_cache_kernel,
        grid_spec=grid_spec,
        out_shape=[jax.ShapeDtypeStruct((depth, nb, nsub, 2 * N_KV), F32)] * 2,
        compiler_params=_cparams(("arbitrary", "arbitrary", "arbitrary")),
        name="cmp_pool_cache",
    )(pt_flat, *([cache] * PG), a1, a2)


def _cmp_mlp_cache_kernel(f_ref, s_ref, w1_ref, b1_ref, w2_ref, o_ref, *, n):
    _cmp_mlp(f_ref[0, 0], s_ref[0, 0], n, w1_ref.at[0], b1_ref.at[0], w2_ref.at[0], o_ref.at[0])


def _cmp_mlp_cache_call(first, second, w1, b1, w2):
    depth, nb, n, _ = first.shape
    blk = pl.BlockSpec((1, 1, n, 2 * N_KV), lambda l, b: (l, b, 0, 0))
    return pl.pallas_call(
        functools.partial(_cmp_mlp_cache_kernel, n=n),
        grid=(depth, nb),
        in_specs=[blk, blk,
                  pl.BlockSpec((1, 2, NSA_DH, NSA_DH), lambda l, b: (l, 0, 0, 0)),
                  pl.BlockSpec((1, 2, NSA_DH), lambda l, b: (l, 0, 0)),
                  pl.BlockSpec((1, 2, NSA_DH, NSA_DH), lambda l, b: (l, 0, 0, 0))],
        out_specs=blk,
        out_shape=jax.ShapeDtypeStruct((depth, nb, n, 2 * N_KV), BF16),
        compiler_params=_cparams(("arbitrary", "arbitrary")),
        name="cmp_mlp_cache",
    )(first, second, w1, b1, w2)


def _select_blocks(imp, qpos, n_slc):
    blk = lax.broadcasted_iota(I32, imp.shape, 1)
    cur = qpos // SEL_BLOCK
    forced = (blk == 0) | (blk == cur) | (blk == cur - 1)
    valid = blk * SEL_BLOCK <= qpos
    score = jnp.where(valid, imp + jnp.where(forced, FORCE_BONUS, 0.0), -1.0)
    score = jnp.where(blk < n_slc, score, -2.0)
    sel = jnp.zeros(imp.shape, F32)
    for _ in range(min(N_SEL, n_slc)):
        mx = jnp.max(score, axis=-1, keepdims=True)
        idx = jnp.min(jnp.where(score == mx, blk, 1 << 30), axis=-1, keepdims=True)
        hit = blk == idx
        sel = jnp.where(hit, 1.0, sel)
        score = jnp.where(hit, -3.0, score)
    return sel


def _cmp_branch(q, kck, kcv, qpos, n_cmp):
    s = _dot_nt(q, kck) * (NSA_DH ** -0.5)
    col = lax.broadcasted_iota(I32, s.shape, 1)
    mask = (col * CMP_STRIDE + (2 * CMP_STRIDE - 1) <= qpos) & (col < n_cmp)
    p = _masked_softmax(s, mask)
    return p, _dot(p.astype(BF16), kcv)


def _nsa_prompt_kernel(q_ref, kc_ref, ks_ref, vs_ref, kw_ref, vw_ref, g_ref, o_ref,
                       *, tq, seq, slab):
    q0 = pl.program_id(1) * tq
    rows = NSA_REP * tq
    ncp = seq // CMP_STRIDE
    n_slc = seq // SEL_BLOCK
    nbp = LANE
    scale = NSA_DH ** -0.5
    gates = g_ref[...]

    def qpos_of(shape):
        return q0 + (lax.broadcasted_iota(I32, shape, 0) & (tq - 1))

    gmat = (lax.broadcasted_iota(I32, (ncp, nbp), 0) // (SEL_BLOCK // CMP_STRIDE)
            == lax.broadcasted_iota(I32, (ncp, nbp), 1)).astype(BF16)
    emat = (lax.broadcasted_iota(I32, (nbp, seq), 1) // SEL_BLOCK
            == lax.broadcasted_iota(I32, (nbp, seq), 0)).astype(BF16)
    s0 = pl.multiple_of(jnp.maximum(q0 + tq - slab, 0), SUBLANE)
    for g in range(NSA_KVH):
        gs = slice(g * NSA_DH, (g + 1) * NSA_DH)
        q4 = jnp.concatenate(
            [q_ref[:, (g * NSA_REP + r) * NSA_DH:(g * NSA_REP + r + 1) * NSA_DH]
             for r in range(NSA_REP)], axis=0)
        p_c, o_c = _cmp_branch(q4, kc_ref[0, :, gs], kc_ref[0, :, N_KV + g * NSA_DH:N_KV + (g + 1) * NSA_DH],
                               qpos_of((rows, ncp)), ncp - 1)
        psum = p_c[0:tq]
        for r in range(1, NSA_REP):
            psum = psum + p_c[r * tq:(r + 1) * tq]
        imp = _dot2(psum, gmat)
        sel = _select_blocks(imp, qpos_of((tq, nbp)), n_slc)
        kmask = _dot(sel.astype(BF16), emat) > 0.5
        kmask = jnp.concatenate([kmask.astype(F32)] * NSA_REP, axis=0) > 0.5
        key = lax.broadcasted_iota(I32, (rows, seq), 1)
        s_s = _dot_nt(q4, ks_ref[:, gs]) * scale
        p_s = _masked_softmax(s_s, kmask & (key <= qpos_of((rows, seq))))
        o_s = _dot(p_s.astype(BF16), vs_ref[:, gs])
        wp = s0 + lax.broadcasted_iota(I32, (rows, slab), 1)
        qp = qpos_of((rows, slab))
        s_w = _dot_nt(q4, kw_ref[pl.ds(s0, slab), gs]) * scale
        p_w = _masked_softmax(s_w, (wp <= qp) & (wp > qp - WINDOW))
        o_w = _dot(p_w.astype(BF16), vw_ref[pl.ds(s0, slab), gs])
        for r in range(NSA_REP):
            h = g * NSA_REP + r
            rs = slice(r * tq, (r + 1) * tq)
            o = (gates[:, 3 * h:3 * h + 1] * o_c[rs] + gates[:, 3 * h + 1:3 * h + 2] * o_s[rs]
                 + gates[:, 3 * h + 2:3 * h + 3] * o_w[rs])
            o_ref[:, h * NSA_DH:(h + 1) * NSA_DH] = o.astype(BF16)


def _nsa_prompt_call(qn, kc, bs, bw, gates, nb, seq):
    tq = 128
    nq = seq // tq
    slab = min(seq, WINDOW + tq)
    kv = lambda col: pl.BlockSpec((seq, N_KV), lambda b, i: (b, col))
    kern = functools.partial(_nsa_prompt_kernel, tq=tq, seq=seq, slab=slab)
    return pl.pallas_call(
        kern,
        grid=(nb, nq),
        in_specs=[pl.BlockSpec((tq, D_C), lambda b, i: (b * nq + i, 0)),
                  pl.BlockSpec((1, seq // CMP_STRIDE, 2 * N_KV), lambda b, i: (b, 0, 0)),
                  kv(0), kv(1), kv(0), kv(1),
                  pl.BlockSpec((tq, LANE), lambda b, i: (b * nq + i, 0))],
        out_specs=pl.BlockSpec((tq, D_C), lambda b, i: (b * nq + i, 0)),
        out_shape=jax.ShapeDtypeStruct((nb * seq, D_C), BF16),
        compiler_params=_cparams(("arbitrary", "arbitrary")),
        name="nsa_prompt",
    )(qn, kc, bs, bs, bw, bw, gates)


def _nsa_s_sel_kernel(q_ref, kc_ref, oc_ref, sel_ref, *, past, dl, n_tiles):
    rows = NSA_REP * dl
    ncp = past // CMP_STRIDE
    n_slc = (past + dl + SEL_BLOCK - 1) // SEL_BLOCK
    nbp = (n_slc + LANE - 1) // LANE * LANE
    per_tile = PG * PAGE // SEL_BLOCK
    gmat = (lax.broadcasted_iota(I32, (ncp, nbp), 0) // (SEL_BLOCK // CMP_STRIDE)
            == lax.broadcasted_iota(I32, (ncp, nbp), 1)).astype(BF16)
    rmat = (lax.broadcasted_iota(I32, (rows, rows), 0) // NSA_REP
            == lax.broadcasted_iota(I32, (rows, rows), 1) // NSA_REP).astype(BF16)

    def qpos_of(shape):
        return past + lax.broadcasted_iota(I32, shape, 0) // NSA_REP

    lane = lax.broadcasted_iota(I32, (rows, LANE), 1)
    for g in range(NSA_KVH):
        gs = slice(g * NSA_DH, (g + 1) * NSA_DH)
        p_c, o_c = _cmp_branch(q_ref[0, g], kc_ref[0, 0, :, gs],
                               kc_ref[0, 0, :, N_KV + g * NSA_DH:N_KV + (g + 1) * NSA_DH],
                               qpos_of((rows, ncp)), ncp - 1)
        oc_ref[0, g] = o_c
        psum = _dot2g(rmat, p_c)
        imp = _dot2(psum, gmat)
        sel = _select_blocks(imp, qpos_of((rows, nbp)), n_slc)
        for tl in range(n_tiles):
            lo = tl * per_tile
            chunk = sel[:, (lo // LANE) * LANE:(lo // LANE + 1) * LANE]
            sh = (LANE - lo % LANE) % LANE
            if sh:
                chunk = pltpu.roll(chunk, sh, 1)
            sel_ref[0, g, tl] = jnp.where(lane < per_tile, chunk, 0.0)


def _nsa_s_sel_call(q16, kc, layer, past, dl):
    nb = q16.shape[0]
    rows = NSA_REP * dl
    n_tiles = past // (PG * PAGE) + 1
    kern = functools.partial(_nsa_s_sel_kernel, past=past, dl=dl, n_tiles=n_tiles)
    return pl.pallas_call(
        kern,
        grid=(nb,),
        in_specs=[pl.BlockSpec((1, NSA_KVH, rows, NSA_DH), lambda b: (b, 0, 0, 0)),
                  pl.BlockSpec((1, 1, past // CMP_STRIDE, 2 * N_KV), lambda b: (layer, b, 0, 0))],
        out_specs=[pl.BlockSpec((1, NSA_KVH, rows, NSA_DH), lambda b: (b, 0, 0, 0)),
                   pl.BlockSpec((1, NSA_KVH, n_tiles, rows, LANE), lambda b: (b, 0, 0, 0, 0))],
        out_shape=[jax.ShapeDtypeStruct((nb, NSA_KVH, rows, NSA_DH), F32),
                   jax.ShapeDtypeStruct((nb, NSA_KVH, n_tiles, rows, LANE), F32)],
        compiler_params=_cparams(("arbitrary",)),
        name="nsa_sample_select",
    )(q16, kc)


def _nsa_s_attn_kernel(pt_ref, *refs, past, dl, npg):
    pages = refs[:PG]
    (q_ref, selt_ref, sell_ref, emat_ref, oc_ref, ns_ref, nw_ref, wp_ref, g_ref,
     o_ref, m_scr, l_scr, a_scr) = refs[PG:]
    pg = pl.program_id(1)
    rows = NSA_REP * dl
    scale = NSA_DH ** -0.5
    wb = wp_ref.shape[1]

    @pl.when(pg == 0)
    def _():
        m_scr[...] = jnp.full(m_scr.shape, NEG_INF, F32)
        l_scr[...] = jnp.zeros(l_scr.shape, F32)
        a_scr[...] = jnp.zeros(a_scr.shape, F32)

    for g in range(NSA_KVH):
        q = q_ref[0, g]
        kmask = _dot(selt_ref[0, g, 0].astype(BF16), emat_ref[...]) > 0.5
        s = jnp.concatenate(
            [_dot_nt(q, pages[k][0, 0, :, g * NSA_DH:(g + 1) * NSA_DH].astype(BF16))
             for k in range(PG)], axis=1) * scale
        s = jnp.where(kmask, s, NEG_INF)
        m_old = m_scr[g]
        m_new = jnp.maximum(m_old, jnp.max(s, axis=-1, keepdims=True))
        p = jnp.where(kmask, jnp.exp(s - m_new), 0.0)
        alpha = jnp.exp(m_old - m_new)
        l_scr[g] = alpha * l_scr[g] + jnp.sum(p, axis=-1, keepdims=True)
        acc = alpha * a_scr[g]
        pb = p.astype(BF16)
        for k in range(PG):
            acc = acc + _dot(pb[:, k * PAGE:(k + 1) * PAGE],
                             pages[k][0, 0, :, N_KV + g * NSA_DH:N_KV + (g + 1) * NSA_DH].astype(BF16))
        a_scr[g] = acc
        m_scr[g] = m_new

    @pl.when(pg == npg - 1)
    def _():
        gates = g_ref[0]
        tok = lax.broadcasted_iota(I32, (rows, SUBLANE), 0) // NSA_REP
        ncol = lax.broadcasted_iota(I32, (rows, SUBLANE), 1)
        for g in range(NSA_KVH):
            gs = slice(g * NSA_DH, (g + 1) * NSA_DH)
            vsl = slice(N_KV + g * NSA_DH, N_KV + (g + 1) * NSA_DH)
            q = q_ref[0, g]
            nmask = (sell_ref[0, g, 0][:, 0:1] > 0.5) & (ncol <= tok) & (ncol < dl)
            s_n = jnp.where(nmask, _dot_nt(q, ns_ref[0, :, gs].astype(BF16)) * scale, NEG_INF)
            m_old = m_scr[g]
            m_new = jnp.maximum(m_old, jnp.max(s_n, axis=-1, keepdims=True))
            p_n = jnp.where(nmask, jnp.exp(s_n - m_new), 0.0)
            alpha = jnp.exp(m_old - m_new)
            lsum = alpha * l_scr[g] + jnp.sum(p_n, axis=-1, keepdims=True)
            acc = alpha * a_scr[g] + _dot(p_n.astype(BF16), ns_ref[0, :, vsl].astype(BF16))
            o_s = acc / jnp.maximum(lsum, 1e-30)
            wrow = lax.broadcasted_iota(I32, (rows, wb), 1)
            wtok = lax.broadcasted_iota(I32, (rows, wb), 0) // NSA_REP
            pmask = (wrow > wtok + (wb - WINDOW)) & (wrow + (past - wb) >= 0)
            s_p = jnp.where(pmask, _dot_nt(q, wp_ref[0, :, gs].astype(BF16)) * scale, NEG_INF)
            wmask = (ncol <= tok) & (ncol < dl)
            s_q = jnp.where(wmask, _dot_nt(q, nw_ref[0, :, gs].astype(BF16)) * scale, NEG_INF)
            mw = jnp.maximum(jnp.max(s_p, axis=-1, keepdims=True), jnp.max(s_q, axis=-1, keepdims=True))
            p_p = jnp.where(pmask, jnp.exp(s_p - mw), 0.0)
            p_q = jnp.where(wmask, jnp.exp(s_q - mw), 0.0)
            den = jnp.maximum(jnp.sum(p_p, axis=-1, keepdims=True) + jnp.sum(p_q, axis=-1, keepdims=True),
                              1e-30)
            o_w = (_dot(p_p.astype(BF16), wp_ref[0, :, vsl].astype(BF16))
                   + _dot(p_q.astype(BF16), nw_ref[0, :, vsl].astype(BF16))) / den
            gg = gates[g]
            o = gg[:, 0:1] * oc_ref[0, g] + gg[:, 1:2] * o_s + gg[:, 2:3] * o_w
            o_ref[0, g] = o.astype(BF16)


def _nsa_s_attn_call(cache, pt_flat, layer, q16, selt, emat, o_c, new_slc, new_win, win_prev, gates16,
                     past, dl):
    nb = q16.shape[0]
    rows = NSA_REP * dl
    n_pages = past // PAGE
    npg = n_pages // PG
    wb = win_prev.shape[1]
    b4 = lambda shape: pl.BlockSpec((1,) + shape, lambda b, g, pt: (b,) + (0,) * len(shape))
    grid_spec = pltpu.PrefetchScalarGridSpec(
        num_scalar_prefetch=1,
        grid=(nb, npg),
        in_specs=_page_specs(n_pages, lambda b, g, pt: layer, lambda b, g, pt: b, lambda b, g, pt: g)
        + [b4((NSA_KVH, rows, NSA_DH)),
           pl.BlockSpec((1, NSA_KVH, 1, rows, LANE), lambda b, g, pt: (b, 0, g, 0, 0)),
           pl.BlockSpec((1, NSA_KVH, 1, rows, LANE), lambda b, g, pt: (b, 0, npg, 0, 0)),
           pl.BlockSpec((LANE, PG * PAGE), lambda b, g, pt: (0, 0)),
           b4((NSA_KVH, rows, NSA_DH)),
           b4((SUBLANE, 2 * N_KV)), b4((SUBLANE, 2 * N_KV)), b4((wb, 2 * N_KV)),
           b4((NSA_KVH, rows, LANE))],
        out_specs=b4((NSA_KVH, rows, NSA_DH)),
        scratch_shapes=[pltpu.VMEM((NSA_KVH, rows, 1), F32), pltpu.VMEM((NSA_KVH, rows, 1), F32),
                        pltpu.VMEM((NSA_KVH, rows, NSA_DH), F32)],
    )
    kern = functools.partial(_nsa_s_attn_kernel, past=past, dl=dl, npg=npg)
    return pl.pallas_call(
        kern,
        grid_spec=grid_spec,
        out_shape=jax.ShapeDtypeStruct((nb, NSA_KVH, rows, NSA_DH), BF16),
        compiler_params=_cparams(("arbitrary", "arbitrary")),
        name="nsa_sample_attn",
    )(pt_flat, *([cache] * PG), q16, selt, selt, emat, o_c, new_slc, new_win, win_prev, gates16)


def _merge_kernel(ya_ref, yb_ref, yc_ref, ga_ref, gb_ref, gc_ref, wa_ref, wb_ref, wc_ref, o_ref):
    m = (jax.nn.sigmoid(ga_ref[...]) * _dot(ya_ref[...], wa_ref[...])
         + jax.nn.sigmoid(gb_ref[...]) * _dot(yb_ref[...], wb_ref[...])
         + jax.nn.sigmoid(gc_ref[...]) * _dot(yc_ref[...], wc_ref[...]))
    o_ref[...] = m.astype(BF16)


def _merge_call(ya, yb, yc, hin, wa, wb, wc):
    m = ya.shape[0]
    tm = min(256, m)
    const = lambda shape: pl.BlockSpec(shape, lambda i: (0, 0))
    gate = lambda col: pl.BlockSpec((tm, D_MODEL), lambda i: (i, col))
    return pl.pallas_call(
        _merge_kernel,
        grid=(m // tm,),
        in_specs=[pl.BlockSpec((tm, D_A), lambda i: (i, 0)), pl.BlockSpec((tm, D_B), lambda i: (i, 0)),
                  pl.BlockSpec((tm, D_C), lambda i: (i, 0)), gate(0), gate(1), gate(2),
                  const((D_A, D_MODEL)), const((D_B, D_MODEL)), const((D_C, D_MODEL))],
        out_specs=pl.BlockSpec((tm, D_MODEL), lambda i: (i, 0)),
        out_shape=jax.ShapeDtypeStruct((m, D_MODEL), BF16),
        compiler_params=_cparams(("arbitrary",)),
        name="merge",
    )(ya, yb, yc, hin, hin, hin, wa, wb, wc)


def _outproj_kernel(m_ref, x_ref, g_ref, w_ref, o_ref):
    o_ref[...] = x_ref[...] + g_ref[0] * _dot(m_ref[...], w_ref[...])


def _outproj_call(merged, x, modx, rows_per_group, w):
    m = x.shape[0]
    tm = min(512, m)
    return pl.pallas_call(
        _outproj_kernel,
        grid=(m // tm,),
        in_specs=[pl.BlockSpec((tm, D_MODEL), lambda i: (i, 0)),
                  pl.BlockSpec((tm, D_MODEL), lambda i: (i, 0)),
                  _mod_spec(modx, tm, rows_per_group, 2),
                  pl.BlockSpec((D_MODEL, D_MODEL), lambda i: (0, 0))],
        out_specs=pl.BlockSpec((tm, D_MODEL), lambda i: (i, 0)),
        out_shape=jax.ShapeDtypeStruct((m, D_MODEL), F32),
        compiler_params=_cparams(("arbitrary",)),
        name="outproj",
    )(merged, x, modx, w)


def _ffn_tail(ua, ua1, ua2, ub, cw_ref, cb_ref, wd_ref):
    conv = cb_ref[...] + cw_ref[0:1, :] * ua2 + cw_ref[1:2, :] * ua1 + cw_ref[2:3, :] * ua
    return _dot((jax.nn.gelu(conv) * ub).astype(BF16), wd_ref[...])


def _ffn_prompt_kernel(x_ref, xh_ref, sc_ref, sh_ref, g_ref, nw_ref, cp_ref, wua_ref, wub_ref, cw_ref,
                       cb_ref, wd_ref, o_ref, tail_ref, h_scr, hh_scr, acc, *, tm, seq):
    f = pl.program_id(1)

    @pl.when(f == 0)
    def _():
        nw, sc, sh = nw_ref[...], sc_ref[0], sh_ref[0]
        h_scr[...] = (_rms(x_ref[...], nw) * (1.0 + sc) + sh).astype(BF16)
        hh_scr[...] = (_rms(xh_ref[...], nw) * (1.0 + sc) + sh).astype(BF16)
        acc[...] = jnp.zeros(acc.shape, F32)

    wua = wua_ref[...]
    ua = _dot(h_scr[...], wua)
    ub = _dot(h_scr[...], wub_ref[...])
    at_start = (pl.program_id(0) * tm) % seq == 0
    prev = jnp.where(at_start, cp_ref[0], _dot(hh_scr[...], wua))
    row = lax.broadcasted_iota(I32, ua.shape, 0)
    p1, p2 = prev[SUBLANE - 1:SUBLANE], prev[SUBLANE - 2:SUBLANE - 1]
    ua1 = jnp.where(row == 0, p1, pltpu.roll(ua, 1, 0))
    ua2 = jnp.where(row == 0, p2, jnp.where(row == 1, p1, pltpu.roll(ua, 2, 0)))
    tail_ref[0] = ua[tm - SUBLANE:tm]
    acc[...] += _ffn_tail(ua, ua1, ua2, ub, cw_ref, cb_ref, wd_ref)

    @pl.when(f == pl.num_programs(1) - 1)
    def _():
        o_ref[...] = x_ref[...] + g_ref[0] * acc[...]


def _ffn_prompt_call(x, modx, seq, nw, cprev, fw):
    m = x.shape[0]
    tm = min(512, m)
    hb = tm // SUBLANE
    kern = functools.partial(_ffn_prompt_kernel, tm=tm, seq=seq)
    return pl.pallas_call(
        kern,
        grid=(m // tm, D_FFP // TF),
        in_specs=[pl.BlockSpec((tm, D_MODEL), lambda i, f: (i, 0)),
                  pl.BlockSpec((SUBLANE, D_MODEL), lambda i, f: (jnp.maximum(i * hb - 1, 0), 0)),
                  _mod_spec(modx, tm, seq, 4), _mod_spec(modx, tm, seq, 3), _mod_spec(modx, tm, seq, 5),
                  pl.BlockSpec((1, D_MODEL), lambda i, f: (0, 0)),
                  pl.BlockSpec((1, SUBLANE, TF), lambda i, f: ((i * tm) // seq, 0, f)),
                  pl.BlockSpec((D_MODEL, TF), lambda i, f: (0, f)),
                  pl.BlockSpec((D_MODEL, TF), lambda i, f: (0, D_FFP // TF + f)),
                  pl.BlockSpec((CONV_W, TF), lambda i, f: (0, f)),
                  pl.BlockSpec((1, TF), lambda i, f: (0, f)),
                  pl.BlockSpec((TF, D_MODEL), lambda i, f: (f, 0))],
        out_specs=[pl.BlockSpec((tm, D_MODEL), lambda i, f: (i, 0)),
                   pl.BlockSpec((1, SUBLANE, TF), lambda i, f: (i, 0, f))],
        out_shape=[jax.ShapeDtypeStruct((m, D_MODEL), F32),
                   jax.ShapeDtypeStruct((m // tm, SUBLANE, D_FFP), F32)],
        scratch_shapes=[pltpu.VMEM((tm, D_MODEL), BF16), pltpu.VMEM((SUBLANE, D_MODEL), BF16),
                        pltpu.VMEM((tm, D_MODEL), F32)],
        compiler_params=_cparams(("arbitrary", "arbitrary")),
        name="ffn_prompt",
    )(x, x, modx, modx, modx, nw, cprev, fw["wu"], fw["wu"], fw["cw"], fw["cb"], fw["wd"])


def _ffn_sample_kernel(x_ref, sc_ref, sh_ref, g_ref, nw_ref, p1_ref, p2_ref, wua_ref, wub_ref, cw_ref,
                       cb_ref, wd_ref, o_ref, ua_ref, h_scr, acc, *, dl):
    f = pl.program_id(0)

    @pl.when(f == 0)
    def _():
        h_scr[...] = (_rms(x_ref[...], nw_ref[...]) * (1.0 + sc_ref[0]) + sh_ref[0]).astype(BF16)
        acc[...] = jnp.zeros(acc.shape, F32)

    ua = _dot(h_scr[...], wua_ref[...])
    ub = _dot(h_scr[...], wub_ref[...])
    tin = lax.broadcasted_iota(I32, ua.shape, 0) % dl
    ua1 = jnp.where(tin >= 1, pltpu.roll(ua, 1, 0), p1_ref[...])
    ua2 = jnp.where(tin >= 2, pltpu.roll(ua, 2, 0), p2_ref[...])
    ua_ref[...] = ua
    acc[...] += _ffn_tail(ua, ua1, ua2, ub, cw_ref, cb_ref, wd_ref)

    @pl.when(f == pl.num_programs(0) - 1)
    def _():
        o_ref[...] = x_ref[...] + g_ref[0] * acc[...]


def _ffn_sample_call(x, modx, dl, nw, p1, p2, fw):
    m = x.shape[0]
    kern = functools.partial(_ffn_sample_kernel, dl=dl)
    full = lambda i: pl.BlockSpec((m, D_MODEL), lambda f: (0, 0))
    return pl.pallas_call(
        kern,
        grid=(D_FFP // TF,),
        in_specs=[full(0),
                  pl.BlockSpec((1, m, D_MODEL), lambda f: (0, 0, 4)),
                  pl.BlockSpec((1, m, D_MODEL), lambda f: (0, 0, 3)),
                  pl.BlockSpec((1, m, D_MODEL), lambda f: (0, 0, 5)),
                  pl.BlockSpec((1, D_MODEL), lambda f: (0, 0)),
                  pl.BlockSpec((m, TF), lambda f: (0, f)), pl.BlockSpec((m, TF), lambda f: (0, f)),
                  pl.BlockSpec((D_MODEL, TF), lambda f: (0, f)),
                  pl.BlockSpec((D_MODEL, TF), lambda f: (0, D_FFP // TF + f)),
                  pl.BlockSpec((CONV_W, TF), lambda f: (0, f)),
                  pl.BlockSpec((1, TF), lambda f: (0, f)),
                  pl.BlockSpec((TF, D_MODEL), lambda f: (f, 0))],
        out_specs=[full(0), pl.BlockSpec((m, TF), lambda f: (0, f))],
        out_shape=[jax.ShapeDtypeStruct((m, D_MODEL), F32),
                   jax.ShapeDtypeStruct((m, D_FFP), F32)],
        scratch_shapes=[pltpu.VMEM((m, D_MODEL), BF16), pltpu.VMEM((m, D_MODEL), F32)],
        compiler_params=_cparams(("arbitrary",)),
        name="ffn_sample",
    )(x, modx, modx, modx, nw, p1, p2, fw["wu"], fw["wu"], fw["cw"], fw["cb"], fw["wd"])


def _rope_tables(pos):
    half = ROPE_DIM // 2
    inv = jnp.exp(jnp.arange(half, dtype=F32) * (-math.log(ROPE_THETA) / half))
    ang = pos.astype(F32)[:, None] * inv[None, :]
    cos, sin = jnp.cos(ang), jnp.sin(ang)
    n = pos.shape[0]
    rest = NSA_DH - ROPE_DIM
    cos_t = jnp.concatenate([cos, cos, jnp.ones((n, rest), F32)], axis=1)
    sin_a = jnp.concatenate([-sin, jnp.zeros((n, NSA_DH - half), F32)], axis=1)
    sin_b = jnp.concatenate([jnp.zeros((n, half), F32), sin, jnp.zeros((n, rest), F32)], axis=1)
    return cos_t, sin_a, sin_b


def _permute_w_in(w_in):
    sizes = (D_A, D_B, D_B, D_B, D_B, D_C, 6 * N_KV, 3 * NSA_HEADS, 3 * D_MODEL)
    offs = [0]
    for s in sizes:
        offs.append(offs[-1] + s)
    seg = lambda k: w_in[:, :, offs[k]:offs[k + 1]]
    pad = jnp.zeros(w_in.shape[:2] + (NGT_PAD - 3 * NSA_HEADS,), w_in.dtype)
    return jnp.concatenate([seg(8), seg(5), seg(6), seg(0), seg(1), seg(2), seg(3), seg(4), seg(7), pad],
                           axis=2).astype(BF16)


def _s5_params(a_re, a_im, log_dt, b_re, b_im, c_re, c_im, d, w_glu, b_glu):
    dt = jnp.exp(log_dt)[:, None]
    mag = jnp.exp(a_re * dt)
    ab_re, ab_im = mag * jnp.cos(a_im * dt), mag * jnp.sin(a_im * dt)
    den = a_re * a_re + a_im * a_im
    cf_re = ((ab_re - 1.0) * a_re + ab_im * a_im) / den
    cf_im = (ab_im * a_re - (ab_re - 1.0) * a_im) / den
    bb_re = cf_re[..., None] * b_re - cf_im[..., None] * b_im
    bb_im = cf_re[..., None] * b_im + cf_im[..., None] * b_re
    eye = jnp.eye(S5_G, dtype=F32)
    bdiag = lambda bb: jnp.einsum("gnc,gh->gchn", bb, eye).reshape(D_A, S5_W).astype(BF16)
    cdiag = lambda cc: jnp.einsum("gcn,gh->gnhc", cc, eye).reshape(S5_W, D_A).astype(BF16)
    pr, pi = ab_re.reshape(1, S5_W), ab_im.reshape(1, S5_W)
    prs, pis = [pr], [pi]
    for _ in range(SUBLANE - 1):
        nr = prs[-1] * pr - pis[-1] * pi
        ni = prs[-1] * pi + pis[-1] * pr
        prs.append(nr)
        pis.append(ni)
    return dict(bbr=bdiag(bb_re), bbi=bdiag(bb_im), ccr=cdiag(c_re), cci=cdiag(c_im),
                apr=jnp.concatenate(prs, axis=0), api=jnp.concatenate(pis, axis=0),
                d=d.reshape(1, D_A), wg=w_glu.astype(BF16), bg=b_glu.reshape(1, D_A))


def _hgrn_params(lb):
    pos = lb > 0
    lb_safe = jnp.where(pos, lb, 1.0)
    z = jnp.zeros_like(lb)
    return jnp.stack([jnp.log(lb_safe), jnp.log1p(-lb), 1.0 - lb, pos.astype(F32), z, z, z, z], axis=0)


def _cmp_params(cmp_a, w1, b1, w2):
    def lanes(a):
        return jnp.concatenate([a[0], a[0], a[1], a[1]], axis=-1)
    return dict(a1=lanes(cmp_a[:, :CMP_STRIDE]), a2=lanes(cmp_a[:, CMP_STRIDE:]),
                w1=w1.astype(BF16), b1=b1, w2=w2.astype(BF16))


def _ffn_params(w_up, conv_w, conv_b, w_down):
    padc = lambda a: jnp.pad(a, [(0, 0)] * (a.ndim - 1) + [(0, D_FFP - D_FF)])
    wu = jnp.concatenate([padc(w_up[:, :D_FF]), padc(w_up[:, D_FF:])], axis=1).astype(BF16)
    wd = jnp.pad(w_down, ((0, D_FFP - D_FF), (0, 0))).astype(BF16)
    return dict(wu=wu, cw=padc(conv_w), cb=padc(conv_b.reshape(1, D_FF)), wd=wd)


def kernel(x_prompt, x_sample, cache_cmp, cache_slc, state_win, state_s5, state_hgrn, state_conv,
           page_table, c_prompt, c_sample, w_mod, b_mod, norm1_w, norm2_w, w_in,
           s5_a_re, s5_a_im, s5_log_dt, s5_b_re, s5_b_im, s5_c_re, s5_c_im, s5_d, s5_w_glu, s5_b_glu,
           hg_lb_logits, hg_norm_w, nsa_q_norm, nsa_k_norm, cmp_a, cmp_w1, cmp_b1, cmp_w2,
           w_branch_a, w_branch_b, w_branch_c, w_out, w_up, conv_w, conv_b, w_down):
    bsz, seq, _ = x_prompt.shape
    dbsz, dl, _ = x_sample.shape
    depth = w_in.shape[0]
    n_pages = page_table.shape[1]
    past = n_pages * PAGE
    wb = state_win.shape[2]
    mp, ms = bsz * seq, dbsz * dl
    rows = NSA_REP * dl
    assert seq % 512 == 0 and seq // SEL_BLOCK <= LANE and n_pages % PG == 0 and dl <= SUBLANE

    probs = jax.nn.softmax(hg_lb_logits.astype(F32), axis=0)
    lower_bounds = jnp.cumsum(probs, axis=0) - probs[0:1]
    w_in_p = _permute_w_in(w_in)
    wa, wbr, wc, wo = (w.astype(BF16) for w in (w_branch_a, w_branch_b, w_branch_c, w_out))
    rope_p = _rope_tables(jnp.arange(seq, dtype=I32))
    rope_s = tuple(jnp.tile(t, (dbsz, 1)) for t in _rope_tables(past + jnp.arange(dl, dtype=I32)))
    pt_flat = page_table.reshape(-1).astype(I32)
    cache_cmp4 = cache_cmp.reshape(depth, -1, PAGE, 2 * N_KV)
    cache_slc4 = cache_slc.reshape(depth, -1, PAGE, 2 * N_KV)
    emat = (jnp.arange(PG * PAGE)[None, :] // SEL_BLOCK == jnp.arange(LANE)[:, None]).astype(BF16)

    nr = -(-(bsz + dbsz) // 16) * 16
    c_all = jnp.concatenate([c_prompt, c_sample, jnp.zeros((nr - bsz - dbsz, D_MODEL), F32)], axis=0)
    mod = _mod_call(c_all, w_mod, b_mod)

    cmp_lanes = lambda a: jnp.concatenate([a[:, 0], a[:, 0], a[:, 1], a[:, 1]], axis=-1)
    first, second = _cmp_pool_cache_call(cache_cmp.reshape(depth, -1, PAGE * KV_TILES, LANE), pt_flat,
                                         dbsz, n_pages,
                                         cmp_lanes(cmp_a[:, :, :CMP_STRIDE]),
                                         cmp_lanes(cmp_a[:, :, CMP_STRIDE:]))
    kc_past = _cmp_mlp_cache_call(first, second, cmp_w1.astype(BF16), cmp_b1, cmp_w2.astype(BF16))

    xp = x_prompt.reshape(mp, D_MODEL)
    xs = x_sample.reshape(ms, D_MODEL)
    outs = [[] for _ in range(12)]
    for l in range(depth):
        s5p = _s5_params(s5_a_re[l], s5_a_im[l], s5_log_dt[l], s5_b_re[l], s5_b_im[l], s5_c_re[l],
                         s5_c_im[l], s5_d[l], s5_w_glu[l], s5_b_glu[l])
        hgp = _hgrn_params(lower_bounds[l])
        hgn = hg_norm_w[l].reshape(1, HG_DV)
        cp = _cmp_params(cmp_a[l], cmp_w1[l], cmp_b1[l], cmp_w2[l])
        fw = _ffn_params(w_up[l], conv_w[l], conv_b[l], w_down[l])
        n1, n2 = norm1_w[l].reshape(1, D_MODEL), norm2_w[l].reshape(1, D_MODEL)
        qn, kn = nsa_q_norm[l].reshape(1, NSA_DH), nsa_k_norm[l]
        modp = mod[l, :bsz].reshape(bsz, 1, 6 * D_MODEL)
        mods = jnp.repeat(mod[l, bsz:bsz + dbsz], dl, axis=0).reshape(1, ms, 6 * D_MODEL)

        hin = _inproj_call(xp, modp, seq, n1, w_in_p[l])
        ya, s5_new = _s5_call(hin, OFF_UA // D_A, bsz, seq, jnp.zeros((bsz, 2, S5_W), F32), s5p,
                              SUBLANE - 1)
        yb, hg_new = _hgrn_call(hin, tuple(o // D_B for o in (OFF_HQ, OFF_HF, OFF_HI, OFF_HGT)), bsz, seq,
                                seq, jnp.zeros((bsz, HG_HEADS, HG_DK, HG_DV), F32), hgp, hgn)
        qo, rc, rs, rw, bc, bs, bw, gates = _nsa_prep_call(hin, rope_p, seq, qn, kn)
        kc = _cmp_prompt_call(rc, bsz, seq, cp)
        yc = _nsa_prompt_call(qo, kc, bs, bw, gates, bsz, seq)
        merged = _merge_call(ya, yb, yc, hin, wa[l], wbr[l], wc[l])
        xp = _outproj_call(merged, xp, modp, seq, wo[l])
        xp, tails = _ffn_prompt_call(xp, modp, seq, n2, jnp.zeros((bsz, SUBLANE, D_FFP), F32), fw)
        conv_new = tails.reshape(bsz, -1, SUBLANE, D_FFP)[:, -1, SUBLANE - (CONV_W - 1):, :D_FF]
        kv5 = lambda r: r.reshape(bsz, seq, 2, NSA_KVH, NSA_DH)
        for i, o in enumerate((s5_new.reshape(bsz, 2, S5_G, S5_N), hg_new, kv5(rc), kv5(rs),
                               kv5(rw)[:, -min(WINDOW, seq):], conv_new)):
            outs[2 * i].append(o)

        hin = _inproj_call(xs, mods, dl, n1, w_in_p[l])
        padt = lambda a, n: jnp.pad(a.reshape(dbsz, dl, -1), ((0, 0), (0, n - dl), (0, 0))).reshape(dbsz * n, -1)
        u_pad = padt(hin[:, OFF_UA:OFF_UA + D_A], SUBLANE)
        ya, s5_new = _s5_call(u_pad, 0, dbsz, SUBLANE, state_s5[l].reshape(dbsz, 2, S5_W), s5p, dl - 1)
        ya = ya.reshape(dbsz, SUBLANE, D_A)[:, :dl].reshape(ms, D_A)
        hg_pad = padt(hin[:, OFF_HQ:OFF_HQ + 4 * D_B], HG_C)
        yb, hg_new = _hgrn_call(hg_pad, (0, 1, 2, 3), dbsz, HG_C, dl, state_hgrn[l], hgp, hgn)
        yb = yb.reshape(dbsz, HG_C, D_B)[:, :dl].reshape(ms, D_B)
        qo, rc, rs, rw, bc, bs, bw, gates = _nsa_prep_call(hin, rope_s, dl, qn, kn)
        q16 = qo.reshape(dbsz, dl, NSA_KVH, NSA_REP, NSA_DH).transpose(0, 2, 1, 3, 4).reshape(
            dbsz, NSA_KVH, rows, NSA_DH)
        o_c, selt = _nsa_s_sel_call(q16, kc_past, l, past, dl)
        g16 = gates[:, :3 * NSA_HEADS].reshape(dbsz, dl, NSA_KVH, NSA_REP, 3).transpose(0, 2, 1, 3, 4)
        g16 = jnp.pad(g16.reshape(dbsz, NSA_KVH, rows, 3), ((0, 0), (0, 0), (0, 0), (0, LANE - 3)))
        pad8 = lambda r: jnp.pad(r.reshape(dbsz, dl, 2 * N_KV), ((0, 0), (0, SUBLANE - dl), (0, 0)))
        win_prev = state_win[l].reshape(dbsz, wb, 2 * N_KV)
        yc = _nsa_s_attn_call(cache_slc4, pt_flat, l, q16, selt, emat, o_c, pad8(rs), pad8(rw), win_prev,
                              g16, past, dl)
        yc = yc.reshape(dbsz, NSA_KVH, dl, NSA_REP, NSA_DH).transpose(0, 2, 1, 3, 4).reshape(ms, D_C)
        merged = _merge_call(ya, yb, yc, hin, wa[l], wbr[l], wc[l])
        xs = _outproj_call(merged, xs, mods, dl, wo[l])
        st = jnp.pad(state_conv[l], ((0, 0), (0, 0), (0, D_FFP - D_FF)))
        tin = jnp.arange(dl)[None, :, None]
        p1 = jnp.where(tin == 0, st[:, 1:2], 0.0).reshape(ms, D_FFP)
        p2 = jnp.where(tin == 0, st[:, 0:1], jnp.where(tin == 1, st[:, 1:2], 0.0)).reshape(ms, D_FFP)
        xs, ua = _ffn_sample_call(xs, mods, dl, n2, p1, p2, fw)
        ext = jnp.concatenate([state_conv[l], ua[:, :D_FF].reshape(dbsz, dl, D_FF)], axis=1)
        kv5 = lambda r: r.reshape(dbsz, dl, 2, NSA_KVH, NSA_DH)
        win_new = jnp.concatenate([state_win[l], kv5(rw)], axis=1)[:, -wb:]
        for i, o in enumerate((s5_new.reshape(dbsz, 2, S5_G, S5_N), hg_new, kv5(rc), kv5(rs), win_new,
                               ext[:, dl:])):
            outs[2 * i + 1].append(o)

    st = [jnp.stack(o) for o in outs]
    return (xp.reshape(bsz, seq, D_MODEL), xs.reshape(dbsz, dl, D_MODEL), *st)
```

```python
import functools
import math

import jax
import jax.numpy as jnp
from jax import lax
from jax.experimental import pallas as pl
from jax.experimental.pallas import tpu as pltpu

F32 = jnp.float32
BF16 = jnp.bfloat16
I32 = jnp.int32

D_MODEL = 2048
D_A = D_MODEL // 4
S5_GROUP = 16
S5_G = D_A // S5_GROUP
S5_N = 64
S5_W = S5_G * S5_N
D_B = D_MODEL // 4
HG_DK = 128
HG_DV = 128
HG_HEADS = D_B // HG_DV
D_C = D_MODEL // 2
NSA_DH = 128
NSA_HEADS = D_C // NSA_DH
NSA_KVH = 2
NSA_REP = NSA_HEADS // NSA_KVH
N_KV = NSA_KVH * NSA_DH
ROPE_DIM = NSA_DH // 4
ROPE_THETA = 500000.0
CMP_STRIDE = 16
SEL_BLOCK = 64
N_SEL = 16
WINDOW = 512
FORCE_BONUS = 1000.0
D_FF = ((8 * D_MODEL) // 3 + 127) // 128 * 128
CONV_W = 3
EPS = 1e-6
NEG_INF = -1e30
PAGE = 128

LANE = 128
SUBLANE = 8
VMEM_LIMIT = 56 * 1024 * 1024

OFF_MGT = 0
OFF_NQ = 3 * D_MODEL
OFF_NKV = OFF_NQ + D_C
OFF_UA = OFF_NKV + 6 * N_KV
OFF_HQ = OFF_UA + D_A
OFF_HF = OFF_HQ + D_B
OFF_HI = OFF_HF + D_B
OFF_HGT = OFF_HI + D_B
OFF_NGT = OFF_HGT + D_B
NGT_PAD = 256
N_INP = OFF_NGT + NGT_PAD
TN_IN = 1280
D_FFP = 5632
TF = 512
HG_C = 64
HG_SB = 16


def _cparams(sem):
    return pltpu.CompilerParams(dimension_semantics=sem, vmem_limit_bytes=VMEM_LIMIT)


def _dot(a, b):
    return jnp.dot(a, b, preferred_element_type=F32)


def _dot_nt(a, b):
    return lax.dot_general(a, b, (((1,), (1,)), ((), ())), preferred_element_type=F32)


def _dot_tn(a, b):
    return lax.dot_general(a, b, (((0,), (0,)), ((), ())), preferred_element_type=F32)


def _dot2(a, b):
    hi = a.astype(BF16)
    lo = (a - hi.astype(F32)).astype(BF16)
    return _dot(hi, b) + _dot(lo, b)


def _dot3(a, b):
    hi = a.astype(BF16)
    r1 = a - hi.astype(F32)
    mid = r1.astype(BF16)
    lo = (r1 - mid.astype(F32)).astype(BF16)
    return _dot(b, hi) + _dot(b, mid) + _dot(b, lo)


def _rms(x, w):
    return x * lax.rsqrt(jnp.mean(x * x, axis=-1, keepdims=True) + EPS) * w


def _masked_softmax(s, mask):
    s = jnp.where(mask, s, NEG_INF)
    p = jnp.where(mask, jnp.exp(s - jnp.max(s, axis=-1, keepdims=True)), 0.0)
    return p / jnp.maximum(jnp.sum(p, axis=-1, keepdims=True), 1e-30)


def _log_sigmoid(x):
    return jnp.minimum(x, 0.0) - jnp.log1p(jnp.exp(-jnp.abs(x)))


def _logaddexp(a, b):
    return jnp.maximum(a, b) + jnp.log1p(jnp.exp(-jnp.abs(a - b)))


def _mod_kernel(c_ref, w_ref, b_ref, o_ref):
    c = c_ref[...]
    a = (c * jax.nn.sigmoid(c)).astype(BF16)
    o_ref[0] = _dot(a, w_ref[0].astype(BF16)) + b_ref[0]


def _mod_call(c_all, w_mod, b_mod):
    depth, d, n = w_mod.shape
    r = c_all.shape[0]
    tn = 1024
    return pl.pallas_call(
        _mod_kernel,
        grid=(depth, n // tn),
        in_specs=[pl.BlockSpec((r, d), lambda l, j: (0, 0)),
                  pl.BlockSpec((1, d, tn), lambda l, j: (l, 0, j)),
                  pl.BlockSpec((1, 1, tn), lambda l, j: (l, 0, j))],
        out_specs=pl.BlockSpec((1, r, tn), lambda l, j: (l, 0, j)),
        out_shape=jax.ShapeDtypeStruct((depth, r, n), F32),
        compiler_params=_cparams(("arbitrary", "arbitrary")),
        name="mod",
    )(c_all, w_mod, b_mod.reshape(depth, 1, n))


def _inproj_kernel(x_ref, sc_ref, sh_ref, nw_ref, w_ref, o_ref, h_scr):
    @pl.when(pl.program_id(1) == 0)
    def _():
        y = _rms(x_ref[...], nw_ref[...])
        h_scr[...] = (y * (1.0 + sc_ref[0]) + sh_ref[0]).astype(BF16)

    o_ref[...] = _dot(h_scr[...], w_ref[0])


def _mod_spec(modx, tm, rows_per_group, col):
    if modx.shape[1] == 1:
        return pl.BlockSpec((1, 1, D_MODEL), lambda i, *_: ((i * tm) // rows_per_group, 0, col))
    return pl.BlockSpec((1, tm, D_MODEL), lambda i, *_: (0, i, col))


def _inproj_call(x, modx, rows_per_group, nw, w, layer):
    m = x.shape[0]
    tm = min(512, m)
    return pl.pallas_call(
        _inproj_kernel,
        grid=(m // tm, N_INP // TN_IN),
        in_specs=[pl.BlockSpec((tm, D_MODEL), lambda i, j: (i, 0)),
                  _mod_spec(modx, tm, rows_per_group, 1),
                  _mod_spec(modx, tm, rows_per_group, 0),
                  pl.BlockSpec((1, D_MODEL), lambda i, j: (0, 0)),
                  pl.BlockSpec((1, D_MODEL, TN_IN), lambda i, j: (layer, 0, j))],
        out_specs=pl.BlockSpec((tm, TN_IN), lambda i, j: (i, j)),
        out_shape=jax.ShapeDtypeStruct((m, N_INP), F32),
        scratch_shapes=[pltpu.VMEM((tm, D_MODEL), BF16)],
        compiler_params=_cparams(("arbitrary", "arbitrary")),
        name="inproj",
    )(x, modx, modx, nw, w)


S5_LC = 512


def _s5_kernel(u_ref, h0_ref, bbr_ref, bbi_ref, ccr_ref, cci_ref, apr_ref, api_ref, d_ref,
               wg_ref, bg_ref, y_ref, hl_ref, hre, him, car, *, t_rows, last_row):
    @pl.when(pl.program_id(1) == 0)
    def _():
        car[...] = h0_ref[0]

    u = u_ref[...]
    ub = u.astype(BF16)
    hre[...] = _dot(ub, bbr_ref[...])
    him[...] = _dot(ub, bbi_ref[...])
    row = lax.broadcasted_iota(I32, (SUBLANE, S5_LC), 0)
    for c in range(S5_W // S5_LC):
        sl = slice(c * S5_LC, (c + 1) * S5_LC)
        pwr = apr_ref[:, sl]
        pwi = api_ref[:, sl]
        steps = tuple((s, pwr[s - 1:s], pwi[s - 1:s]) for s in (1, 2, 4))

        def body(i, carry, sl=sl, pwr=pwr, pwi=pwi, steps=steps):
            cr, ci = carry
            r0 = pl.multiple_of(i * SUBLANE, SUBLANE)
            xr = hre[pl.ds(r0, SUBLANE), sl]
            xi = him[pl.ds(r0, SUBLANE), sl]
            for s, ar, ai in steps:
                sr = jnp.where(row >= s, pltpu.roll(xr, s, 0), 0.0)
                si = jnp.where(row >= s, pltpu.roll(xi, s, 0), 0.0)
                xr, xi = xr + (ar * sr - ai * si), xi + (ar * si + ai * sr)
            xr, xi = xr + (pwr * cr - pwi * ci), xi + (pwr * ci + pwi * cr)
            hre[pl.ds(r0, SUBLANE), sl] = xr
            him[pl.ds(r0, SUBLANE), sl] = xi
            return xr[SUBLANE - 1:SUBLANE], xi[SUBLANE - 1:SUBLANE]

        cr, ci = lax.fori_loop(0, t_rows // SUBLANE, body, (car[0:1, sl], car[1:2, sl]))
        car[0:1, sl] = cr
        car[1:2, sl] = ci

    lr = t_rows - SUBLANE + last_row
    hl_ref[0, 0:1, :] = hre[lr:lr + 1, :]
    hl_ref[0, 1:2, :] = him[lr:lr + 1, :]
    y = (_dot(hre[...].astype(BF16), ccr_ref[...]) - _dot(him[...].astype(BF16), cci_ref[...])
         + d_ref[...] * u)
    y = jax.nn.gelu(y)
    z = _dot(y.astype(BF16), wg_ref[...]) + bg_ref[...]
    y_ref[...] = (y * jax.nn.sigmoid(z)).astype(BF16)


def _s5_call(u_arr, ucol, nb, seq, h0, p, last_row):
    t_rows = min(256, seq)
    nt = seq // t_rows
    const = lambda shape: pl.BlockSpec(shape, lambda b, t: (0,) * len(shape))
    kern = functools.partial(_s5_kernel, t_rows=t_rows, last_row=last_row)
    return pl.pallas_call(
        kern,
        grid=(nb, nt),
        in_specs=[pl.BlockSpec((t_rows, D_A), lambda b, t: (b * nt + t, ucol)),
                  pl.BlockSpec((1, 2, S5_W), lambda b, t: (b, 0, 0)),
                  const((D_A, S5_W)), const((D_A, S5_W)), const((S5_W, D_A)), const((S5_W, D_A)),
                  const((SUBLANE, S5_W)), const((SUBLANE, S5_W)), const((1, D_A)),
                  const((D_A, D_A)), const((1, D_A))],
        out_specs=[pl.BlockSpec((t_rows, D_A), lambda b, t: (b * nt + t, 0)),
                   pl.BlockSpec((1, 2, S5_W), lambda b, t: (b, 0, 0))],
        out_shape=[jax.ShapeDtypeStruct((nb * seq, D_A), BF16),
                   jax.ShapeDtypeStruct((nb, 2, S5_W), F32)],
        scratch_shapes=[pltpu.VMEM((t_rows, S5_W), F32), pltpu.VMEM((t_rows, S5_W), F32),
                        pltpu.VMEM((2, S5_W), F32)],
        compiler_params=_cparams(("arbitrary", "arbitrary")),
        name="s5",
    )(u_arr, h0, p["bbr"], p["bbi"], p["ccr"], p["cci"], p["apr"], p["api"], p["d"],
      p["wg"], p["bg"])


def _hgrn_kernel(q_ref, f_ref, i_ref, g_ref, s0_ref, lp_ref, nw_ref, y_ref, sl_ref, st,
                 *, l_valid):
    t = pl.program_id(1)
    c = HG_C
    nsb = c // HG_SB

    @pl.when(t == 0)
    def _():
        for h in range(HG_HEADS):
            st[h] = s0_ref[0, h].T

    row = lax.broadcasted_iota(I32, (c, HG_DK), 0)
    col = lax.broadcasted_iota(I32, (c, HG_DK), 1)
    valid = (t * c + row) < l_valid
    tri = (row >= col).astype(BF16)[:, :c]
    srow = lax.broadcasted_iota(I32, (HG_SB, HG_DK), 0)
    ones = jnp.ones((HG_DK, HG_DK), BF16)
    grow = lax.broadcasted_iota(I32, (HG_SB, HG_SB * HG_SB), 0)
    gcol = lax.broadcasted_iota(I32, (HG_SB, HG_SB * HG_SB), 1)
    gsum = (gcol // HG_SB == grow).astype(BF16)
    for h in range(HG_HEADS):
        hs = slice(h * HG_DK, (h + 1) * HG_DK)
        loglb, log1m, oml, pos = (lp_ref[0:1, hs], lp_ref[1:2, hs], lp_ref[2:3, hs],
                                  lp_ref[3:4, hs])
        hf = f_ref[:, hs]
        ls = _log_sigmoid(hf)
        lf = jnp.where(pos > 0.5, _logaddexp(loglb, log1m + ls), ls)
        kk = oml * jax.nn.sigmoid(-hf)
        lf = jnp.where(valid, lf, 0.0)
        kk = jnp.where(valid, kk, 0.0)
        hq = q_ref[:, hs]
        q = hq * jax.nn.sigmoid(hq)
        v = i_ref[:, hs]
        b = _dot3(lf, tri)
        s_t = st[h]
        o_inter = _dot_nt((q * jnp.exp(b)).astype(BF16), s_t.astype(BF16))
        vb = v.astype(BF16)
        o_blocks = []
        for i in range(nsb):
            rs = slice(i * HG_SB, (i + 1) * HG_SB)
            bi, qi, ki, vi = b[rs], q[rs], kk[rs], v[rs]
            o_i = o_inter[rs]
            if i > 0:
                bref = b[i * HG_SB - 1:i * HG_SB]
                qq = qi * jnp.exp(bi - bref)
                kp = jnp.where(row < i * HG_SB, kk * jnp.exp(jnp.minimum(bref - b, 0.0)), 0.0)
                att = _dot_nt(qq.astype(BF16), kp.astype(BF16))
                o_i = o_i + _dot(att.astype(BF16), vb)
            pieces = []
            for tt in range(HG_SB):
                m = srow <= tt
                e = jnp.exp(jnp.where(m, bi[tt:tt + 1] - bi, 0.0))
                pieces.append(jnp.where(m, (qi[tt:tt + 1] * ki) * e, 0.0))
            pmat = jnp.concatenate(pieces, axis=0)
            rsum = _dot(pmat.astype(BF16), ones)
            z = rsum * jnp.concatenate([vi] * HG_SB, axis=0)
            o_i = o_i + _dot2g(gsum, z)
            o_blocks.append(o_i)
        o = jnp.concatenate(o_blocks, axis=0)
        bl = b[c - 1:c]
        kdec = kk * jnp.exp(bl - b)
        st[h] = s_t * jnp.exp(bl) + _dot_tn(vb, kdec.astype(BF16))
        gt = g_ref[:, hs]
        y = _rms(o, nw_ref[...]) * (gt * jax.nn.sigmoid(gt))
        y_ref[:, hs] = y.astype(BF16)

    @pl.when(t == pl.num_programs(1) - 1)
    def _():
        for h in range(HG_HEADS):
            sl_ref[0, h] = st[h].T


def _dot2g(g, z):
    hi = z.astype(BF16)
    lo = (z - hi.astype(F32)).astype(BF16)
    return _dot(g, hi) + _dot(g, lo)


def _hgrn_call(arr, cols, nb, seq_pad, l_valid, s0, lp, nw):
    nt = seq_pad // HG_C
    blk = lambda col: pl.BlockSpec((HG_C, D_B), lambda b, t: (b * nt + t, col))
    kern = functools.partial(_hgrn_kernel, l_valid=l_valid)
    return pl.pallas_call(
        kern,
        grid=(nb, nt),
        in_specs=[blk(cols[0]), blk(cols[1]), blk(cols[2]), blk(cols[3]),
                  pl.BlockSpec((1, HG_HEADS, HG_DK, HG_DV), lambda b, t: (b, 0, 0, 0)),
                  pl.BlockSpec((SUBLANE, D_B), lambda b, t: (0, 0)),
                  pl.BlockSpec((1, HG_DV), lambda b, t: (0, 0))],
        out_specs=[pl.BlockSpec((HG_C, D_B), lambda b, t: (b * nt + t, 0)),
                   pl.BlockSpec((1, HG_HEADS, HG_DK, HG_DV), lambda b, t: (b, 0, 0, 0))],
        out_shape=[jax.ShapeDtypeStruct((nb * seq_pad, D_B), BF16),
                   jax.ShapeDtypeStruct((nb, HG_HEADS, HG_DK, HG_DV), F32)],
        scratch_shapes=[pltpu.VMEM((HG_HEADS, HG_DV, HG_DK), F32)],
        compiler_params=_cparams(("arbitrary", "arbitrary")),
        name="hgrn",
    )(arr, arr, arr, arr, s0, lp, nw)


def _rope(x, cos_t, sin_a, sin_b):
    return (x * cos_t + pltpu.roll(x, NSA_DH - ROPE_DIM // 2, 1) * sin_a
            + pltpu.roll(x, ROPE_DIM // 2, 1) * sin_b)


def _nsa_prep_kernel(q_ref, kc_ref, ks_ref, kw_ref, g_ref, cos_ref, sa_ref, sb_ref, qn_ref, kn_ref,
                     qo_ref, rc_ref, rs_ref, rw_ref, bc_ref, bs_ref, bw_ref, go_ref):
    cos_t, sin_a, sin_b = cos_ref[...], sa_ref[...], sb_ref[...]
    qn = qn_ref[...]
    for h in range(NSA_HEADS):
        hs = slice(h * NSA_DH, (h + 1) * NSA_DH)
        qo_ref[:, hs] = _rope(_rms(q_ref[:, hs], qn), cos_t, sin_a, sin_b).astype(BF16)
    for br, (src, dst, dstb) in enumerate(((kc_ref, rc_ref, bc_ref), (ks_ref, rs_ref, bs_ref),
                                           (kw_ref, rw_ref, bw_ref))):
        kn = kn_ref[br:br + 1, :]
        for g in range(NSA_KVH):
            gs = slice(g * NSA_DH, (g + 1) * NSA_DH)
            k = _rope(_rms(src[:, gs], kn), cos_t, sin_a, sin_b)
            dst[:, gs] = k
            dstb[:, gs] = k.astype(BF16)
        v = src[:, N_KV:2 * N_KV]
        dst[:, N_KV:2 * N_KV] = v
        dstb[:, N_KV:2 * N_KV] = v.astype(BF16)
    go_ref[...] = jax.nn.sigmoid(g_ref[...])


def _nsa_prep_call(hin, rope_tabs, seq, qn, kn):
    m = hin.shape[0]
    tm = min(256, m)
    npos = rope_tabs[0].shape[0] // tm
    rspec = pl.BlockSpec((tm, NSA_DH), lambda i: (i % npos, 0))
    kvb = OFF_NKV // (2 * N_KV)
    return pl.pallas_call(
        _nsa_prep_kernel,
        grid=(m // tm,),
        in_specs=[pl.BlockSpec((tm, D_C), lambda i: (i, OFF_NQ // D_C)),
                  pl.BlockSpec((tm, 2 * N_KV), lambda i: (i, kvb)),
                  pl.BlockSpec((tm, 2 * N_KV), lambda i: (i, kvb + 1)),
                  pl.BlockSpec((tm, 2 * N_KV), lambda i: (i, kvb + 2)),
                  pl.BlockSpec((tm, LANE), lambda i: (i, OFF_NGT // LANE)),
                  rspec, rspec, rspec,
                  pl.BlockSpec((1, NSA_DH), lambda i: (0, 0)),
                  pl.BlockSpec((3, NSA_DH), lambda i: (0, 0))],
        out_specs=[pl.BlockSpec((tm, D_C), lambda i: (i, 0))]
        + [pl.BlockSpec((tm, 2 * N_KV), lambda i: (i, 0))] * 6
        + [pl.BlockSpec((tm, LANE), lambda i: (i, 0))],
        out_shape=[jax.ShapeDtypeStruct((m, D_C), BF16)]
        + [jax.ShapeDtypeStruct((m, 2 * N_KV), F32)] * 3
        + [jax.ShapeDtypeStruct((m, 2 * N_KV), BF16)] * 3
        + [jax.ShapeDtypeStruct((m, LANE), F32)],
        compiler_params=_cparams(("arbitrary",)),
        name="nsa_prep",
    )(hin, hin, hin, hin, hin, *rope_tabs, qn, kn)


KV_TILES = 2 * N_KV // LANE


def _pool_rows(load, n, a1_ref, a2_ref):
    firsts, seconds = [], []
    for c in range(KV_TILES):
        ls = slice(c * LANE, (c + 1) * LANE)
        first = jnp.zeros((n, LANE), F32)
        second = jnp.zeros((n, LANE), F32)
        for j in range(CMP_STRIDE):
            x = load(j, c)
            first = first + x * a1_ref[j:j + 1, ls]
            second = second + x * a2_ref[j:j + 1, ls]
        firsts.append(first)
        seconds.append(second)
    return jnp.concatenate(firsts, axis=1), jnp.concatenate(seconds, axis=1)


def _tile_rows(j, c, n):
    return pl.ds(j * KV_TILES + c, n, stride=CMP_STRIDE * KV_TILES)


def _cmp_mlp(first, second, n, w1_ref, b1_ref, w2_ref, o_ref):
    pooled = first + pltpu.roll(second, n - 1, 0)
    for c in range(2):
        for g in range(NSA_KVH):
            ls = slice((2 * c + g) * NSA_DH, (2 * c + g + 1) * NSA_DH)
            h = _dot(pooled[:, ls].astype(BF16), w1_ref[c]) + b1_ref[c:c + 1, :]
            o_ref[0, :, ls] = _dot(jax.nn.gelu(h).astype(BF16), w2_ref[c]).astype(BF16)


def _cmp_prompt_kernel(x_ref, a1_ref, a2_ref, w1_ref, b1_ref, w2_ref, o_ref, *, n):
    first, second = _pool_rows(lambda j, c: x_ref[_tile_rows(j, c, n), :], n, a1_ref, a2_ref)
    _cmp_mlp(first, second, n, w1_ref, b1_ref, w2_ref, o_ref)


def _cmp_prompt_call(rows, nb, seq, cp):
    n = seq // CMP_STRIDE
    const = lambda shape: pl.BlockSpec(shape, lambda b: (0,) * len(shape))
    return pl.pallas_call(
        functools.partial(_cmp_prompt_kernel, n=n),
        grid=(nb,),
        in_specs=[pl.BlockSpec((seq * KV_TILES, LANE), lambda b: (b, 0)),
                  const((CMP_STRIDE, 2 * N_KV)), const((CMP_STRIDE, 2 * N_KV)),
                  const((2, NSA_DH, NSA_DH)), const((2, NSA_DH)), const((2, NSA_DH, NSA_DH))],
        out_specs=pl.BlockSpec((1, n, 2 * N_KV), lambda b: (b, 0, 0)),
        out_shape=jax.ShapeDtypeStruct((nb, n, 2 * N_KV), BF16),
        compiler_params=_cparams(("arbitrary",)),
        name="cmp_prompt",
    )(rows.reshape(-1, LANE), cp["a1"], cp["a2"], cp["w1"], cp["b1"], cp["w2"])


PG = 16
SUB_PER_PAGE = PAGE // CMP_STRIDE


def _cmp_pool_cache_kernel(pt_ref, *refs):
    pages = refs[:PG]
    a1_ref, a2_ref, f_ref, s_ref = refs[PG:]
    for k in range(PG):
        first, second = _pool_rows(
            lambda j, c, k=k: pages[k][0, 0, _tile_rows(j, c, SUB_PER_PAGE), :],
            SUB_PER_PAGE, a1_ref.at[0], a2_ref.at[0])
        f_ref[0, 0, k * SUB_PER_PAGE:(k + 1) * SUB_PER_PAGE, :] = first
        s_ref[0, 0, k * SUB_PER_PAGE:(k + 1) * SUB_PER_PAGE, :] = second


def _page_specs(n_pages, layer_of, batch_of, group_of, page_shape=(PAGE, 2 * N_KV)):
    def spec(k):
        return pl.BlockSpec(
            (1, 1) + page_shape,
            lambda *a, k=k: (layer_of(*a), a[-1][batch_of(*a) * n_pages + group_of(*a) * PG + k], 0, 0))
    return [spec(k) for k in range(PG)]


def _cmp_pool_cache_call(cache, pt_flat, nb, n_pages, a1, a2):
    depth = cache.shape[0]
    npg = n_pages // PG
    nsub = n_pages * SUB_PER_PAGE
    rows = PG * SUB_PER_PAGE
    grid_spec = pltpu.PrefetchScalarGridSpec(
        num_scalar_prefetch=1,
        grid=(depth, nb, npg),
        in_specs=_page_specs(n_pages, lambda l, b, g, pt: l, lambda l, b, g, pt: b,
                             lambda l, b, g, pt: g, (PAGE * KV_TILES, LANE))
        + [pl.BlockSpec((1, CMP_STRIDE, 2 * N_KV), lambda l, b, g, pt: (l, 0, 0))] * 2,
        out_specs=[pl.BlockSpec((1, 1, rows, 2 * N_KV), lambda l, b, g, pt: (l, b, g, 0))] * 2,
    )
    return pl.pallas_call(
        _cmp_pool_cache_kernel,
        grid_spec=grid_spec,
        out_shape=[jax.ShapeDtypeStruct((depth, nb, nsub, 2 * N_KV), F32)] * 2,
        compiler_params=_cparams(("arbitrary", "arbitrary", "arbitrary")),
        name="cmp_pool_cache",
    )(pt_flat, *([cache] * PG), a1, a2)


def _cmp_mlp_cache_kernel(f_ref, s_ref, w1_ref, b1_ref, w2_ref, o_ref, *, n):
    _cmp_mlp(f_ref[0, 0], s_ref[0, 0], n, w1_ref.at[0], b1_ref.at[0], w2_ref.at[0], o_ref.at[0])


def _cmp_mlp_cache_call(first, second, w1, b1, w2):
    depth, nb, n, _ = first.shape
    blk = pl.BlockSpec((1, 1, n, 2 * N_KV), lambda l, b: (l, b, 0, 0))
    return pl.pallas_call(
        functools.partial(_cmp_mlp_cache_kernel, n=n),
        grid=(depth, nb),
        in_specs=[blk, blk,
                  pl.BlockSpec((1, 2, NSA_DH, NSA_DH), lambda l, b: (l, 0, 0, 0)),
                  pl.BlockSpec((1, 2, NSA_DH), lambda l, b: (l, 0, 0)),
                  pl.BlockSpec((1, 2, NSA_DH, NSA_DH), lambda l, b: (l, 0, 0, 0))],
        out_specs=blk,
        out_shape=jax.ShapeDtypeStruct((depth, nb, n, 2 * N_KV), BF16),
        compiler_params=_cparams(("arbitrary", "arbitrary")),
        name="cmp_mlp_cache",
    )(first, second, w1, b1, w2)


def _block_scores(imp, qpos, n_slc):
    blk = lax.broadcasted_iota(I32, imp.shape, 1)
    cur = qpos // SEL_BLOCK
    forced = (blk == 0) | (blk == cur) | (blk == cur - 1)
    valid = blk * SEL_BLOCK <= qpos
    score = jnp.where(valid, imp + jnp.where(forced, FORCE_BONUS, 0.0), -1.0)
    return jnp.where(blk < n_slc, score, -2.0)


def _select_blocks_ranked(imp, qpos, n_slc):
    nb8 = (n_slc + SUBLANE - 1) // SUBLANE * SUBLANE
    st = _block_scores(imp, qpos, n_slc).T[:nb8]
    ridx = lax.broadcasted_iota(I32, st.shape, 0)
    cnt = jnp.zeros(st.shape, F32)
    for i in range(n_slc):
        ri = st[i:i + 1]
        cnt = cnt + jnp.where(ri > st, 1.0, jnp.where(ri == st, jnp.where(ridx > i, 1.0, 0.0), 0.0))
    sel_t = jnp.where((cnt < min(N_SEL, n_slc)) & (ridx < n_slc), 1.0, 0.0)
    if nb8 < LANE:
        sel_t = jnp.concatenate([sel_t, jnp.zeros((LANE - nb8, st.shape[1]), F32)], axis=0)
    return sel_t.T


def _select_blocks(imp, qpos, n_slc):
    blk = lax.broadcasted_iota(I32, imp.shape, 1)
    score = _block_scores(imp, qpos, n_slc)
    sel = jnp.zeros(imp.shape, F32)
    for _ in range(min(N_SEL, n_slc)):
        mx = jnp.max(score, axis=-1, keepdims=True)
        idx = jnp.min(jnp.where(score == mx, blk, 1 << 30), axis=-1, keepdims=True)
        hit = blk == idx
        sel = jnp.where(hit, 1.0, sel)
        score = jnp.where(hit, -3.0, score)
    return sel


def _cmp_branch(q, kck, kcv, qpos, n_cmp):
    s = _dot_nt(q, kck) * (NSA_DH ** -0.5)
    col = lax.broadcasted_iota(I32, s.shape, 1)
    mask = (col * CMP_STRIDE + (2 * CMP_STRIDE - 1) <= qpos) & (col < n_cmp)
    p = _masked_softmax(s, mask)
    return p, _dot(p.astype(BF16), kcv)


NSA_KC = 512


def _nsa_prompt_kernel(q_ref, kc_ref, ks_ref, vs_ref, kw_ref, vw_ref, g_ref, o_ref,
                       bias_scr, m_scr, l_scr, acc_scr, *, tq, seq, slab):
    q0 = pl.program_id(1) * tq
    rows = NSA_REP * tq
    ncp = seq // CMP_STRIDE
    n_slc = seq // SEL_BLOCK
    scale = NSA_DH ** -0.5
    gates = g_ref[...]
    q4s = [jnp.concatenate([q_ref[:, (g * NSA_REP + r) * NSA_DH:(g * NSA_REP + r + 1) * NSA_DH]
                            for r in range(NSA_REP)], axis=0) for g in range(NSA_KVH)]
    qpos_c = q0 + (lax.broadcasted_iota(I32, (rows, ncp), 0) & (tq - 1))
    o_cs, psums = [], []
    for g in range(NSA_KVH):
        p_c, o_c = _cmp_branch(q4s[g], kc_ref[0, :, g * NSA_DH:(g + 1) * NSA_DH],
                               kc_ref[0, :, N_KV + g * NSA_DH:N_KV + (g + 1) * NSA_DH], qpos_c, ncp - 1)
        psum = p_c[0:tq]
        for r in range(1, NSA_REP):
            psum = psum + p_c[r * tq:(r + 1) * tq]
        o_cs.append(o_c)
        psums.append(psum)

    def ranked():
        gmat = (lax.broadcasted_iota(I32, (ncp, LANE), 0) // (SEL_BLOCK // CMP_STRIDE)
                == lax.broadcasted_iota(I32, (ncp, LANE), 1)).astype(BF16)
        imp = _dot2(jnp.concatenate(psums, axis=0), gmat)
        qp = q0 + (lax.broadcasted_iota(I32, (NSA_KVH * tq, LANE), 0) & (tq - 1))
        return _select_blocks_ranked(imp, qp, n_slc)

    sel = lax.cond(q0 + tq > N_SEL * SEL_BLOCK, ranked,
                   lambda: jnp.ones((NSA_KVH * tq, LANE), F32))
    emat = (lax.broadcasted_iota(I32, (LANE, seq), 1) // SEL_BLOCK
            == lax.broadcasted_iota(I32, (LANE, seq), 0)).astype(BF16)
    nk = (q0 + tq + NSA_KC - 1) // NSA_KC
    s0 = pl.multiple_of(jnp.maximum(q0 + tq - slab, 0), SUBLANE)
    key = lax.broadcasted_iota(I32, (tq, seq), 1)
    causal = key <= q0 + lax.broadcasted_iota(I32, (tq, seq), 0)
    wp = s0 + lax.broadcasted_iota(I32, (tq, slab), 1)
    qpw = q0 + lax.broadcasted_iota(I32, (tq, slab), 0)
    wb1 = jnp.where((wp <= qpw) & (wp > qpw - WINDOW), 0.0, NEG_INF)
    wbias = jnp.concatenate([wb1] * NSA_REP, axis=0)
    for g in range(NSA_KVH):
        gs = slice(g * NSA_DH, (g + 1) * NSA_DH)
        q4 = q4s[g]
        keep = (_dot(sel[g * tq:(g + 1) * tq].astype(BF16), emat) > 0.5) & causal
        bias = jnp.where(keep, 0.0, NEG_INF)
        for c in range(seq // NSA_KC):
            for r in range(NSA_REP):
                bias_scr[c, r * tq:(r + 1) * tq, :] = bias[:, c * NSA_KC:(c + 1) * NSA_KC]
        m_scr[...] = jnp.full(m_scr.shape, NEG_INF, F32)
        l_scr[...] = jnp.zeros(l_scr.shape, F32)
        acc_scr[...] = jnp.zeros(acc_scr.shape, F32)

        def body(c, carry, q4=q4, gs=gs):
            k0 = pl.multiple_of(c * NSA_KC, NSA_KC)
            s = _dot_nt(q4, ks_ref[pl.ds(k0, NSA_KC), gs]) * scale + bias_scr[c]
            m_old = m_scr[...]
            m_new = jnp.maximum(m_old, jnp.max(s, axis=-1, keepdims=True))
            p = jnp.exp(s - m_new)
            alpha = jnp.exp(m_old - m_new)
            l_scr[...] = alpha * l_scr[...] + jnp.sum(p, axis=-1, keepdims=True)
            acc_scr[...] = alpha * acc_scr[...] + _dot(p.astype(BF16), vs_ref[pl.ds(k0, NSA_KC), gs])
            m_scr[...] = m_new
            return carry

        lax.fori_loop(0, nk, body, 0)
        o_s = acc_scr[...] / l_scr[...]
        s_w = _dot_nt(q4, kw_ref[pl.ds(s0, slab), gs]) * scale + wbias
        p_w = jnp.exp(s_w - jnp.max(s_w, axis=-1, keepdims=True))
        o_w = (_dot(p_w.astype(BF16), vw_ref[pl.ds(s0, slab), gs])
               / jnp.sum(p_w, axis=-1, keepdims=True))
        o_c = o_cs[g]
        for r in range(NSA_REP):
            h = g * NSA_REP + r
            rs = slice(r * tq, (r + 1) * tq)
            o = (gates[:, 3 * h:3 * h + 1] * o_c[rs] + gates[:, 3 * h + 1:3 * h + 2] * o_s[rs]
                 + gates[:, 3 * h + 2:3 * h + 3] * o_w[rs])
            o_ref[:, h * NSA_DH:(h + 1) * NSA_DH] = o.astype(BF16)


def _nsa_prompt_call(qn, kc, bs, bw, gates, nb, seq):
    tq = 128
    nq = seq // tq
    slab = min(seq, WINDOW + tq)
    kv = lambda col: pl.BlockSpec((seq, N_KV), lambda b, i: (b, col))
    kern = functools.partial(_nsa_prompt_kernel, tq=tq, seq=seq, slab=slab)
    rows = NSA_REP * tq
    return pl.pallas_call(
        kern,
        grid=(nb, nq),
        in_specs=[pl.BlockSpec((tq, D_C), lambda b, i: (b * nq + i, 0)),
                  pl.BlockSpec((1, seq // CMP_STRIDE, 2 * N_KV), lambda b, i: (b, 0, 0)),
                  kv(0), kv(1), kv(0), kv(1),
                  pl.BlockSpec((tq, LANE), lambda b, i: (b * nq + i, 0))],
        out_specs=pl.BlockSpec((tq, D_C), lambda b, i: (b * nq + i, 0)),
        out_shape=jax.ShapeDtypeStruct((nb * seq, D_C), BF16),
        scratch_shapes=[pltpu.VMEM((seq // NSA_KC, rows, NSA_KC), F32), pltpu.VMEM((rows, 1), F32),
                        pltpu.VMEM((rows, 1), F32), pltpu.VMEM((rows, NSA_DH), F32)],
        compiler_params=_cparams(("arbitrary", "arbitrary")),
        name="nsa_prompt",
    )(qn, kc, bs, bs, bw, bw, gates)


def _nsa_s_sel_kernel(q_ref, kc_ref, oc_ref, sel_ref, *, past, dl, n_tiles):
    rows = NSA_REP * dl
    ncp = past // CMP_STRIDE
    n_slc = (past + dl + SEL_BLOCK - 1) // SEL_BLOCK
    nbp = (n_slc + LANE - 1) // LANE * LANE
    per_tile = PG * PAGE // SEL_BLOCK
    gmat = (lax.broadcasted_iota(I32, (ncp, nbp), 0) // (SEL_BLOCK // CMP_STRIDE)
            == lax.broadcasted_iota(I32, (ncp, nbp), 1)).astype(BF16)
    rmat = (lax.broadcasted_iota(I32, (rows, rows), 0) // NSA_REP
            == lax.broadcasted_iota(I32, (rows, rows), 1) // NSA_REP).astype(BF16)

    def qpos_of(shape):
        return past + lax.broadcasted_iota(I32, shape, 0) // NSA_REP

    lane = lax.broadcasted_iota(I32, (rows, LANE), 1)
    for g in range(NSA_KVH):
        gs = slice(g * NSA_DH, (g + 1) * NSA_DH)
        p_c, o_c = _cmp_branch(q_ref[0, g], kc_ref[0, 0, :, gs],
                               kc_ref[0, 0, :, N_KV + g * NSA_DH:N_KV + (g + 1) * NSA_DH],
                               qpos_of((rows, ncp)), ncp - 1)
        oc_ref[0, g] = o_c
        psum = _dot2g(rmat, p_c)
        imp = _dot2(psum, gmat)
        sel = _select_blocks(imp, qpos_of((rows, nbp)), n_slc)
        for tl in range(n_tiles):
            lo = tl * per_tile
            chunk = sel[:, (lo // LANE) * LANE:(lo // LANE + 1) * LANE]
            sh = (LANE - lo % LANE) % LANE
            if sh:
                chunk = pltpu.roll(chunk, sh, 1)
            sel_ref[0, g, tl] = jnp.where(lane < per_tile, chunk, 0.0)


def _nsa_s_sel_call(q16, kc, layer, past, dl):
    nb = q16.shape[0]
    rows = NSA_REP * dl
    n_tiles = past // (PG * PAGE) + 1
    kern = functools.partial(_nsa_s_sel_kernel, past=past, dl=dl, n_tiles=n_tiles)
    return pl.pallas_call(
        kern,
        grid=(nb,),
        in_specs=[pl.BlockSpec((1, NSA_KVH, rows, NSA_DH), lambda b: (b, 0, 0, 0)),
                  pl.BlockSpec((1, 1, past // CMP_STRIDE, 2 * N_KV), lambda b: (layer, b, 0, 0))],
        out_specs=[pl.BlockSpec((1, NSA_KVH, rows, NSA_DH), lambda b: (b, 0, 0, 0)),
                   pl.BlockSpec((1, NSA_KVH, n_tiles, rows, LANE), lambda b: (b, 0, 0, 0, 0))],
        out_shape=[jax.ShapeDtypeStruct((nb, NSA_KVH, rows, NSA_DH), F32),
                   jax.ShapeDtypeStruct((nb, NSA_KVH, n_tiles, rows, LANE), F32)],
        compiler_params=_cparams(("arbitrary",)),
        name="nsa_sample_select",
    )(q16, kc)


def _nsa_s_attn_kernel(pt_ref, *refs, past, dl, npg):
    pages = refs[:PG]
    (q_ref, selt_ref, sell_ref, emat_ref, oc_ref, ns_ref, nw_ref, wp_ref, g_ref,
     o_ref, m_scr, l_scr, a_scr) = refs[PG:]
    pg = pl.program_id(1)
    rows = NSA_REP * dl
    scale = NSA_DH ** -0.5
    wb = wp_ref.shape[2] // KV_TILES

    def tok_rows(ref, c, n):
        return ref[0, 0, pl.ds(c, n, stride=KV_TILES), :].astype(BF16)

    @pl.when(pg == 0)
    def _():
        m_scr[...] = jnp.full(m_scr.shape, NEG_INF, F32)
        l_scr[...] = jnp.zeros(l_scr.shape, F32)
        a_scr[...] = jnp.zeros(a_scr.shape, F32)

    for g in range(NSA_KVH):
        q = q_ref[0, g]
        kmask = _dot(selt_ref[0, g, 0].astype(BF16), emat_ref[...]) > 0.5
        s = jnp.concatenate(
            [_dot_nt(q, tok_rows(pages[k], g, PAGE)) for k in range(PG)], axis=1) * scale
        s = jnp.where(kmask, s, NEG_INF)
        m_old = m_scr[g]
        m_new = jnp.maximum(m_old, jnp.max(s, axis=-1, keepdims=True))
        p = jnp.where(kmask, jnp.exp(s - m_new), 0.0)
        alpha = jnp.exp(m_old - m_new)
        l_scr[g] = alpha * l_scr[g] + jnp.sum(p, axis=-1, keepdims=True)
        acc = alpha * a_scr[g]
        pb = p.astype(BF16)
        for k in range(PG):
            acc = acc + _dot(pb[:, k * PAGE:(k + 1) * PAGE], tok_rows(pages[k], NSA_KVH + g, PAGE))
        a_scr[g] = acc
        m_scr[g] = m_new

    @pl.when(pg == npg - 1)
    def _():
        gates = g_ref[0]
        tok = lax.broadcasted_iota(I32, (rows, SUBLANE), 0) // NSA_REP
        ncol = lax.broadcasted_iota(I32, (rows, SUBLANE), 1)
        for g in range(NSA_KVH):
            gs = slice(g * NSA_DH, (g + 1) * NSA_DH)
            vsl = slice(N_KV + g * NSA_DH, N_KV + (g + 1) * NSA_DH)
            q = q_ref[0, g]
            nmask = (sell_ref[0, g, 0][:, 0:1] > 0.5) & (ncol <= tok) & (ncol < dl)
            s_n = jnp.where(nmask, _dot_nt(q, ns_ref[0, :, gs].astype(BF16)) * scale, NEG_INF)
            m_old = m_scr[g]
            m_new = jnp.maximum(m_old, jnp.max(s_n, axis=-1, keepdims=True))
            p_n = jnp.where(nmask, jnp.exp(s_n - m_new), 0.0)
            alpha = jnp.exp(m_old - m_new)
            lsum = alpha * l_scr[g] + jnp.sum(p_n, axis=-1, keepdims=True)
            acc = alpha * a_scr[g] + _dot(p_n.astype(BF16), ns_ref[0, :, vsl].astype(BF16))
            o_s = acc / jnp.maximum(lsum, 1e-30)
            wrow = lax.broadcasted_iota(I32, (rows, wb), 1)
            wtok = lax.broadcasted_iota(I32, (rows, wb), 0) // NSA_REP
            pmask = (wrow > wtok + (wb - WINDOW)) & (wrow + (past - wb) >= 0)
            s_p = jnp.where(pmask, _dot_nt(q, tok_rows(wp_ref, g, wb)) * scale, NEG_INF)
            wmask = (ncol <= tok) & (ncol < dl)
            s_q = jnp.where(wmask, _dot_nt(q, nw_ref[0, :, gs].astype(BF16)) * scale, NEG_INF)
            mw = jnp.maximum(jnp.max(s_p, axis=-1, keepdims=True), jnp.max(s_q, axis=-1, keepdims=True))
            p_p = jnp.where(pmask, jnp.exp(s_p - mw), 0.0)
            p_q = jnp.where(wmask, jnp.exp(s_q - mw), 0.0)
            den = jnp.maximum(jnp.sum(p_p, axis=-1, keepdims=True) + jnp.sum(p_q, axis=-1, keepdims=True),
                              1e-30)
            o_w = (_dot(p_p.astype(BF16), tok_rows(wp_ref, NSA_KVH + g, wb))
                   + _dot(p_q.astype(BF16), nw_ref[0, :, vsl].astype(BF16))) / den
            gg = gates[g]
            o = gg[:, 0:1] * oc_ref[0, g] + gg[:, 1:2] * o_s + gg[:, 2:3] * o_w
            o_ref[0, g] = o.astype(BF16)


def _nsa_s_attn_call(cache, pt_flat, layer, q16, selt, emat, o_c, new_slc, new_win, win_prev, gates16,
                     past, dl):
    nb = q16.shape[0]
    rows = NSA_REP * dl
    n_pages = past // PAGE
    npg = n_pages // PG
    wrows = win_prev.shape[2]
    b4 = lambda shape: pl.BlockSpec((1,) + shape, lambda b, g, pt: (b,) + (0,) * len(shape))
    grid_spec = pltpu.PrefetchScalarGridSpec(
        num_scalar_prefetch=1,
        grid=(nb, npg),
        in_specs=_page_specs(n_pages, lambda b, g, pt: layer, lambda b, g, pt: b, lambda b, g, pt: g,
                             (PAGE * KV_TILES, LANE))
        + [b4((NSA_KVH, rows, NSA_DH)),
           pl.BlockSpec((1, NSA_KVH, 1, rows, LANE), lambda b, g, pt: (b, 0, g, 0, 0)),
           pl.BlockSpec((1, NSA_KVH, 1, rows, LANE), lambda b, g, pt: (b, 0, npg, 0, 0)),
           pl.BlockSpec((LANE, PG * PAGE), lambda b, g, pt: (0, 0)),
           b4((NSA_KVH, rows, NSA_DH)),
           b4((SUBLANE, 2 * N_KV)), b4((SUBLANE, 2 * N_KV)),
           pl.BlockSpec((1, 1, wrows, LANE), lambda b, g, pt: (layer, b, 0, 0)),
           b4((NSA_KVH, rows, LANE))],
        out_specs=b4((NSA_KVH, rows, NSA_DH)),
        scratch_shapes=[pltpu.VMEM((NSA_KVH, rows, 1), F32), pltpu.VMEM((NSA_KVH, rows, 1), F32),
                        pltpu.VMEM((NSA_KVH, rows, NSA_DH), F32)],
    )
    kern = functools.partial(_nsa_s_attn_kernel, past=past, dl=dl, npg=npg)
    return pl.pallas_call(
        kern,
        grid_spec=grid_spec,
        out_shape=jax.ShapeDtypeStruct((nb, NSA_KVH, rows, NSA_DH), BF16),
        compiler_params=_cparams(("arbitrary", "arbitrary")),
        name="nsa_sample_attn",
    )(pt_flat, *([cache] * PG), q16, selt, selt, emat, o_c, new_slc, new_win, win_prev, gates16)


def _merge_kernel(ya_ref, yb_ref, yc_ref, ga_ref, gb_ref, gc_ref, wa_ref, wb_ref, wc_ref, o_ref):
    m = (jax.nn.sigmoid(ga_ref[...]) * _dot(ya_ref[...], wa_ref[0])
         + jax.nn.sigmoid(gb_ref[...]) * _dot(yb_ref[...], wb_ref[0])
         + jax.nn.sigmoid(gc_ref[...]) * _dot(yc_ref[...], wc_ref[0]))
    o_ref[...] = m.astype(BF16)


def _merge_call(ya, yb, yc, hin, wa, wb, wc, layer):
    m = ya.shape[0]
    tm = min(256, m)
    const = lambda shape: pl.BlockSpec((1,) + shape, lambda i: (layer, 0, 0))
    gate = lambda col: pl.BlockSpec((tm, D_MODEL), lambda i: (i, col))
    return pl.pallas_call(
        _merge_kernel,
        grid=(m // tm,),
        in_specs=[pl.BlockSpec((tm, D_A), lambda i: (i, 0)), pl.BlockSpec((tm, D_B), lambda i: (i, 0)),
                  pl.BlockSpec((tm, D_C), lambda i: (i, 0)), gate(0), gate(1), gate(2),
                  const((D_A, D_MODEL)), const((D_B, D_MODEL)), const((D_C, D_MODEL))],
        out_specs=pl.BlockSpec((tm, D_MODEL), lambda i: (i, 0)),
        out_shape=jax.ShapeDtypeStruct((m, D_MODEL), BF16),
        compiler_params=_cparams(("arbitrary",)),
        name="merge",
    )(ya, yb, yc, hin, hin, hin, wa, wb, wc)


def _outproj_kernel(m_ref, x_ref, g_ref, w_ref, o_ref):
    o_ref[...] = x_ref[...] + g_ref[0] * _dot(m_ref[...], w_ref[0])


def _outproj_call(merged, x, modx, rows_per_group, w, layer):
    m = x.shape[0]
    tm = min(512, m)
    return pl.pallas_call(
        _outproj_kernel,
        grid=(m // tm,),
        in_specs=[pl.BlockSpec((tm, D_MODEL), lambda i: (i, 0)),
                  pl.BlockSpec((tm, D_MODEL), lambda i: (i, 0)),
                  _mod_spec(modx, tm, rows_per_group, 2),
                  pl.BlockSpec((1, D_MODEL, D_MODEL), lambda i: (layer, 0, 0))],
        out_specs=pl.BlockSpec((tm, D_MODEL), lambda i: (i, 0)),
        out_shape=jax.ShapeDtypeStruct((m, D_MODEL), F32),
        compiler_params=_cparams(("arbitrary",)),
        name="outproj",
    )(merged, x, modx, w)


def _ffn_tail(ua, ua1, ua2, ub, cw_ref, cb_ref, wd_ref):
    cw = cw_ref[0]
    conv = cb_ref[0] + cw[0:1] * ua2 + cw[1:2] * ua1 + cw[2:3] * ua
    return _dot((jax.nn.gelu(conv) * ub).astype(BF16), wd_ref[0])


def _ffn_prompt_kernel(x_ref, xh_ref, sc_ref, sh_ref, g_ref, nw_ref, cp_ref, wua_ref, wub_ref, cw_ref,
                       cb_ref, wd_ref, o_ref, tail_ref, h_scr, hh_scr, acc, *, tm, seq):
    f = pl.program_id(1)

    @pl.when(f == 0)
    def _():
        nw, sc, sh = nw_ref[...], sc_ref[0], sh_ref[0]
        h_scr[...] = (_rms(x_ref[...], nw) * (1.0 + sc) + sh).astype(BF16)
        hh_scr[...] = (_rms(xh_ref[...], nw) * (1.0 + sc) + sh).astype(BF16)
        acc[...] = jnp.zeros(acc.shape, F32)

    wua = wua_ref[0]
    ua = _dot(h_scr[...], wua)
    ub = _dot(h_scr[...], wub_ref[0])
    at_start = (pl.program_id(0) * tm) % seq == 0
    prev = jnp.where(at_start, cp_ref[0], _dot(hh_scr[...], wua))
    row = lax.broadcasted_iota(I32, ua.shape, 0)
    p1, p2 = prev[SUBLANE - 1:SUBLANE], prev[SUBLANE - 2:SUBLANE - 1]
    ua1 = jnp.where(row == 0, p1, pltpu.roll(ua, 1, 0))
    ua2 = jnp.where(row == 0, p2, jnp.where(row == 1, p1, pltpu.roll(ua, 2, 0)))
    tail_ref[0] = ua[tm - SUBLANE:tm]
    acc[...] += _ffn_tail(ua, ua1, ua2, ub, cw_ref, cb_ref, wd_ref)

    @pl.when(f == pl.num_programs(1) - 1)
    def _():
        o_ref[...] = x_ref[...] + g_ref[0] * acc[...]


def _ffn_prompt_call(x, modx, seq, nw, cprev, fw, layer):
    m = x.shape[0]
    tm = min(512, m)
    hb = tm // SUBLANE
    kern = functools.partial(_ffn_prompt_kernel, tm=tm, seq=seq)
    return pl.pallas_call(
        kern,
        grid=(m // tm, D_FFP // TF),
        in_specs=[pl.BlockSpec((tm, D_MODEL), lambda i, f: (i, 0)),
                  pl.BlockSpec((SUBLANE, D_MODEL), lambda i, f: (jnp.maximum(i * hb - 1, 0), 0)),
                  _mod_spec(modx, tm, seq, 4), _mod_spec(modx, tm, seq, 3), _mod_spec(modx, tm, seq, 5),
                  pl.BlockSpec((1, D_MODEL), lambda i, f: (0, 0)),
                  pl.BlockSpec((1, SUBLANE, TF), lambda i, f: ((i * tm) // seq, 0, f)),
                  pl.BlockSpec((1, D_MODEL, TF), lambda i, f: (layer, 0, f)),
                  pl.BlockSpec((1, D_MODEL, TF), lambda i, f: (layer, 0, D_FFP // TF + f)),
                  pl.BlockSpec((1, CONV_W, TF), lambda i, f: (layer, 0, f)),
                  pl.BlockSpec((1, 1, TF), lambda i, f: (layer, 0, f)),
                  pl.BlockSpec((1, TF, D_MODEL), lambda i, f: (layer, f, 0))],
        out_specs=[pl.BlockSpec((tm, D_MODEL), lambda i, f: (i, 0)),
                   pl.BlockSpec((1, SUBLANE, TF), lambda i, f: (i, 0, f))],
        out_shape=[jax.ShapeDtypeStruct((m, D_MODEL), F32),
                   jax.ShapeDtypeStruct((m // tm, SUBLANE, D_FFP), F32)],
        scratch_shapes=[pltpu.VMEM((tm, D_MODEL), BF16), pltpu.VMEM((SUBLANE, D_MODEL), BF16),
                        pltpu.VMEM((tm, D_MODEL), F32)],
        compiler_params=_cparams(("arbitrary", "arbitrary")),
        name="ffn_prompt",
    )(x, x, modx, modx, modx, nw, cprev, fw["wu"], fw["wu"], fw["cw"], fw["cb"], fw["wd"])


def _ffn_sample_kernel(x_ref, sc_ref, sh_ref, g_ref, nw_ref, p1_ref, p2_ref, wua_ref, wub_ref, cw_ref,
                       cb_ref, wd_ref, o_ref, ua_ref, h_scr, acc, *, dl):
    f = pl.program_id(0)

    @pl.when(f == 0)
    def _():
        h_scr[...] = (_rms(x_ref[...], nw_ref[...]) * (1.0 + sc_ref[0]) + sh_ref[0]).astype(BF16)
        acc[...] = jnp.zeros(acc.shape, F32)

    ua = _dot(h_scr[...], wua_ref[0])
    ub = _dot(h_scr[...], wub_ref[0])
    tin = lax.broadcasted_iota(I32, ua.shape, 0) % dl
    ua1 = jnp.where(tin >= 1, pltpu.roll(ua, 1, 0), p1_ref[...])
    ua2 = jnp.where(tin >= 2, pltpu.roll(ua, 2, 0), p2_ref[...])
    ua_ref[...] = ua
    acc[...] += _ffn_tail(ua, ua1, ua2, ub, cw_ref, cb_ref, wd_ref)

    @pl.when(f == pl.num_programs(0) - 1)
    def _():
        o_ref[...] = x_ref[...] + g_ref[0] * acc[...]


def _ffn_sample_call(x, modx, dl, nw, p1, p2, fw, layer):
    m = x.shape[0]
    kern = functools.partial(_ffn_sample_kernel, dl=dl)
    full = lambda i: pl.BlockSpec((m, D_MODEL), lambda f: (0, 0))
    return pl.pallas_call(
        kern,
        grid=(D_FFP // TF,),
        in_specs=[full(0),
                  pl.BlockSpec((1, m, D_MODEL), lambda f: (0, 0, 4)),
                  pl.BlockSpec((1, m, D_MODEL), lambda f: (0, 0, 3)),
                  pl.BlockSpec((1, m, D_MODEL), lambda f: (0, 0, 5)),
                  pl.BlockSpec((1, D_MODEL), lambda f: (0, 0)),
                  pl.BlockSpec((m, TF), lambda f: (0, f)), pl.BlockSpec((m, TF), lambda f: (0, f)),
                  pl.BlockSpec((1, D_MODEL, TF), lambda f: (layer, 0, f)),
                  pl.BlockSpec((1, D_MODEL, TF), lambda f: (layer, 0, D_FFP // TF + f)),
                  pl.BlockSpec((1, CONV_W, TF), lambda f: (layer, 0, f)),
                  pl.BlockSpec((1, 1, TF), lambda f: (layer, 0, f)),
                  pl.BlockSpec((1, TF, D_MODEL), lambda f: (layer, f, 0))],
        out_specs=[full(0), pl.BlockSpec((m, TF), lambda f: (0, f))],
        out_shape=[jax.ShapeDtypeStruct((m, D_MODEL), F32),
                   jax.ShapeDtypeStruct((m, D_FFP), F32)],
        scratch_shapes=[pltpu.VMEM((m, D_MODEL), BF16), pltpu.VMEM((m, D_MODEL), F32)],
        compiler_params=_cparams(("arbitrary",)),
        name="ffn_sample",
    )(x, modx, modx, modx, nw, p1, p2, fw["wu"], fw["wu"], fw["cw"], fw["cb"], fw["wd"])


def _rope_tables(pos):
    half = ROPE_DIM // 2
    inv = jnp.exp(jnp.arange(half, dtype=F32) * (-math.log(ROPE_THETA) / half))
    ang = pos.astype(F32)[:, None] * inv[None, :]
    cos, sin = jnp.cos(ang), jnp.sin(ang)
    n = pos.shape[0]
    rest = NSA_DH - ROPE_DIM
    cos_t = jnp.concatenate([cos, cos, jnp.ones((n, rest), F32)], axis=1)
    sin_a = jnp.concatenate([-sin, jnp.zeros((n, NSA_DH - half), F32)], axis=1)
    sin_b = jnp.concatenate([jnp.zeros((n, half), F32), sin, jnp.zeros((n, rest), F32)], axis=1)
    return cos_t, sin_a, sin_b


def _permute_w_in(w_in):
    sizes = (D_A, D_B, D_B, D_B, D_B, D_C, 6 * N_KV, 3 * NSA_HEADS, 3 * D_MODEL)
    offs = [0]
    for s in sizes:
        offs.append(offs[-1] + s)
    seg = lambda k: w_in[:, :, offs[k]:offs[k + 1]]
    pad = jnp.zeros(w_in.shape[:2] + (NGT_PAD - 3 * NSA_HEADS,), BF16)
    return jnp.concatenate([seg(k).astype(BF16) for k in (8, 5, 6, 0, 1, 2, 3, 4, 7)] + [pad], axis=2)


def _s5_params(a_re, a_im, log_dt, b_re, b_im, c_re, c_im, d, w_glu, b_glu):
    dt = jnp.exp(log_dt)[:, None]
    mag = jnp.exp(a_re * dt)
    ab_re, ab_im = mag * jnp.cos(a_im * dt), mag * jnp.sin(a_im * dt)
    den = a_re * a_re + a_im * a_im
    cf_re = ((ab_re - 1.0) * a_re + ab_im * a_im) / den
    cf_im = (ab_im * a_re - (ab_re - 1.0) * a_im) / den
    bb_re = cf_re[..., None] * b_re - cf_im[..., None] * b_im
    bb_im = cf_re[..., None] * b_im + cf_im[..., None] * b_re
    eye = jnp.eye(S5_G, dtype=F32)
    bdiag = lambda bb: jnp.einsum("gnc,gh->gchn", bb, eye).reshape(D_A, S5_W).astype(BF16)
    cdiag = lambda cc: jnp.einsum("gcn,gh->gnhc", cc, eye).reshape(S5_W, D_A).astype(BF16)
    pr, pi = ab_re.reshape(1, S5_W), ab_im.reshape(1, S5_W)
    prs, pis = [pr], [pi]
    for _ in range(SUBLANE - 1):
        nr = prs[-1] * pr - pis[-1] * pi
        ni = prs[-1] * pi + pis[-1] * pr
        prs.append(nr)
        pis.append(ni)
    return dict(bbr=bdiag(bb_re), bbi=bdiag(bb_im), ccr=cdiag(c_re), cci=cdiag(c_im),
                apr=jnp.concatenate(prs, axis=0), api=jnp.concatenate(pis, axis=0),
                d=d.reshape(1, D_A), wg=w_glu.astype(BF16), bg=b_glu.reshape(1, D_A))


def _hgrn_params(lb):
    pos = lb > 0
    lb_safe = jnp.where(pos, lb, 1.0)
    z = jnp.zeros_like(lb)
    return jnp.stack([jnp.log(lb_safe), jnp.log1p(-lb), 1.0 - lb, pos.astype(F32), z, z, z, z], axis=0)


def _cmp_params(cmp_a, w1, b1, w2):
    def lanes(a):
        return jnp.concatenate([a[0], a[0], a[1], a[1]], axis=-1)
    return dict(a1=lanes(cmp_a[:, :CMP_STRIDE]), a2=lanes(cmp_a[:, CMP_STRIDE:]),
                w1=w1.astype(BF16), b1=b1, w2=w2.astype(BF16))


def _ffn_params(w_up, conv_w, conv_b, w_down):
    padc = lambda a: jnp.pad(a, [(0, 0)] * (a.ndim - 1) + [(0, D_FFP - D_FF)])
    wu = jnp.concatenate([padc(w_up[..., :D_FF].astype(BF16)), padc(w_up[..., D_FF:].astype(BF16))], axis=-1)
    wd = jnp.pad(w_down.astype(BF16), ((0, 0), (0, D_FFP - D_FF), (0, 0)))
    return dict(wu=wu, cw=padc(conv_w), cb=padc(conv_b)[:, None, :], wd=wd)


def kernel(x_prompt, x_sample, cache_cmp, cache_slc, state_win, state_s5, state_hgrn, state_conv,
           page_table, c_prompt, c_sample, w_mod, b_mod, norm1_w, norm2_w, w_in,
           s5_a_re, s5_a_im, s5_log_dt, s5_b_re, s5_b_im, s5_c_re, s5_c_im, s5_d, s5_w_glu, s5_b_glu,
           hg_lb_logits, hg_norm_w, nsa_q_norm, nsa_k_norm, cmp_a, cmp_w1, cmp_b1, cmp_w2,
           w_branch_a, w_branch_b, w_branch_c, w_out, w_up, conv_w, conv_b, w_down):
    bsz, seq, _ = x_prompt.shape
    dbsz, dl, _ = x_sample.shape
    depth = w_in.shape[0]
    n_pages = page_table.shape[1]
    past = n_pages * PAGE
    wb = state_win.shape[2]
    mp, ms = bsz * seq, dbsz * dl
    rows = NSA_REP * dl
    assert seq % 512 == 0 and seq // SEL_BLOCK <= LANE and n_pages % PG == 0 and dl <= SUBLANE

    probs = jax.nn.softmax(hg_lb_logits.astype(F32), axis=0)
    lower_bounds = jnp.cumsum(probs, axis=0) - probs[0:1]
    w_in_p = _permute_w_in(w_in)
    wa, wbr, wc, wo = (w.astype(BF16) for w in (w_branch_a, w_branch_b, w_branch_c, w_out))
    fw = _ffn_params(w_up, conv_w, conv_b, w_down)
    rope_p = _rope_tables(jnp.arange(seq, dtype=I32))
    rope_s = tuple(jnp.tile(t, (dbsz, 1)) for t in _rope_tables(past + jnp.arange(dl, dtype=I32)))
    pt_flat = page_table.reshape(-1).astype(I32)
    cache_cmp4 = cache_cmp.reshape(depth, -1, PAGE * KV_TILES, LANE)
    cache_slc4 = cache_slc.reshape(depth, -1, PAGE * KV_TILES, LANE)
    win_prev4 = state_win.reshape(depth, dbsz, wb * KV_TILES, LANE)
    emat = (jnp.arange(PG * PAGE)[None, :] // SEL_BLOCK == jnp.arange(LANE)[:, None]).astype(BF16)

    nr = -(-(bsz + dbsz) // 16) * 16
    c_all = jnp.concatenate([c_prompt, c_sample, jnp.zeros((nr - bsz - dbsz, D_MODEL), F32)], axis=0)
    mod = _mod_call(c_all, w_mod, b_mod)

    cmp_lanes = lambda a: jnp.concatenate([a[:, 0], a[:, 0], a[:, 1], a[:, 1]], axis=-1)
    first, second = _cmp_pool_cache_call(cache_cmp4, pt_flat, dbsz, n_pages,
                                         cmp_lanes(cmp_a[:, :, :CMP_STRIDE]),
                                         cmp_lanes(cmp_a[:, :, CMP_STRIDE:]))
    kc_past = _cmp_mlp_cache_call(first, second, cmp_w1.astype(BF16), cmp_b1, cmp_w2.astype(BF16))

    xp = x_prompt.reshape(mp, D_MODEL)
    xs = x_sample.reshape(ms, D_MODEL)
    outs = [[] for _ in range(12)]
    for l in range(depth):
        s5p = _s5_params(s5_a_re[l], s5_a_im[l], s5_log_dt[l], s5_b_re[l], s5_b_im[l], s5_c_re[l],
                         s5_c_im[l], s5_d[l], s5_w_glu[l], s5_b_glu[l])
        hgp = _hgrn_params(lower_bounds[l])
        hgn = hg_norm_w[l].reshape(1, HG_DV)
        cp = _cmp_params(cmp_a[l], cmp_w1[l], cmp_b1[l], cmp_w2[l])
        n1, n2 = norm1_w[l].reshape(1, D_MODEL), norm2_w[l].reshape(1, D_MODEL)
        qn, kn = nsa_q_norm[l].reshape(1, NSA_DH), nsa_k_norm[l]
        modp = mod[l, :bsz].reshape(bsz, 1, 6 * D_MODEL)
        mods = jnp.repeat(mod[l, bsz:bsz + dbsz], dl, axis=0).reshape(1, ms, 6 * D_MODEL)

        hin = _inproj_call(xp, modp, seq, n1, w_in_p, l)
        ya, s5_new = _s5_call(hin, OFF_UA // D_A, bsz, seq, jnp.zeros((bsz, 2, S5_W), F32), s5p,
                              SUBLANE - 1)
        yb, hg_new = _hgrn_call(hin, tuple(o // D_B for o in (OFF_HQ, OFF_HF, OFF_HI, OFF_HGT)), bsz, seq,
                                seq, jnp.zeros((bsz, HG_HEADS, HG_DK, HG_DV), F32), hgp, hgn)
        qo, rc, rs, rw, bc, bs, bw, gates = _nsa_prep_call(hin, rope_p, seq, qn, kn)
        kc = _cmp_prompt_call(rc, bsz, seq, cp)
        yc = _nsa_prompt_call(qo, kc, bs, bw, gates, bsz, seq)
        merged = _merge_call(ya, yb, yc, hin, wa, wbr, wc, l)
        xp = _outproj_call(merged, xp, modp, seq, wo, l)
        xp, tails = _ffn_prompt_call(xp, modp, seq, n2, jnp.zeros((bsz, SUBLANE, D_FFP), F32), fw, l)
        conv_new = tails.reshape(bsz, -1, SUBLANE, D_FFP)[:, -1, SUBLANE - (CONV_W - 1):, :D_FF]
        kv5 = lambda r: r.reshape(bsz, seq, 2, NSA_KVH, NSA_DH)
        for i, o in enumerate((s5_new.reshape(bsz, 2, S5_G, S5_N), hg_new, kv5(rc), kv5(rs),
                               kv5(rw)[:, -min(WINDOW, seq):], conv_new)):
            outs[2 * i].append(o)

        hin = _inproj_call(xs, mods, dl, n1, w_in_p, l)
        padt = lambda a, n: jnp.pad(a.reshape(dbsz, dl, -1), ((0, 0), (0, n - dl), (0, 0))).reshape(dbsz * n, -1)
        u_pad = padt(hin[:, OFF_UA:OFF_UA + D_A], SUBLANE)
        ya, s5_new = _s5_call(u_pad, 0, dbsz, SUBLANE, state_s5[l].reshape(dbsz, 2, S5_W), s5p, dl - 1)
        ya = ya.reshape(dbsz, SUBLANE, D_A)[:, :dl].reshape(ms, D_A)
        hg_pad = padt(hin[:, OFF_HQ:OFF_HQ + 4 * D_B], HG_C)
        yb, hg_new = _hgrn_call(hg_pad, (0, 1, 2, 3), dbsz, HG_C, dl, state_hgrn[l], hgp, hgn)
        yb = yb.reshape(dbsz, HG_C, D_B)[:, :dl].reshape(ms, D_B)
        qo, rc, rs, rw, bc, bs, bw, gates = _nsa_prep_call(hin, rope_s, dl, qn, kn)
        q16 = qo.reshape(dbsz, dl, NSA_KVH, NSA_REP, NSA_DH).transpose(0, 2, 1, 3, 4).reshape(
            dbsz, NSA_KVH, rows, NSA_DH)
        o_c, selt = _nsa_s_sel_call(q16, kc_past, l, past, dl)
        g16 = gates[:, :3 * NSA_HEADS].reshape(dbsz, dl, NSA_KVH, NSA_REP, 3).transpose(0, 2, 1, 3, 4)
        g16 = jnp.pad(g16.reshape(dbsz, NSA_KVH, rows, 3), ((0, 0), (0, 0), (0, 0), (0, LANE - 3)))
        pad8 = lambda r: jnp.pad(r.reshape(dbsz, dl, 2 * N_KV), ((0, 0), (0, SUBLANE - dl), (0, 0)))
        yc = _nsa_s_attn_call(cache_slc4, pt_flat, l, q16, selt, emat, o_c, pad8(rs), pad8(rw), win_prev4,
                              g16, past, dl)
        yc = yc.reshape(dbsz, NSA_KVH, dl, NSA_REP, NSA_DH).transpose(0, 2, 1, 3, 4).reshape(ms, D_C)
        merged = _merge_call(ya, yb, yc, hin, wa, wbr, wc, l)
        xs = _outproj_call(merged, xs, mods, dl, wo, l)
        st = jnp.pad(state_conv[l], ((0, 0), (0, 0), (0, D_FFP - D_FF)))
        tin = jnp.arange(dl)[None, :, None]
        p1 = jnp.where(tin == 0, st[:, 1:2], 0.0).reshape(ms, D_FFP)
        p2 = jnp.where(tin == 0, st[:, 0:1], jnp.where(tin == 1, st[:, 1:2], 0.0)).reshape(ms, D_FFP)
        xs, ua = _ffn_sample_call(xs, mods, dl, n2, p1, p2, fw, l)
        ext = jnp.concatenate([state_conv[l], ua[:, :D_FF].reshape(dbsz, dl, D_FF)], axis=1)
        kv5 = lambda r: r.reshape(dbsz, dl, 2, NSA_KVH, NSA_DH)
        win_new = jnp.concatenate([state_win[l], kv5(rw)], axis=1)[:, -wb:]
        for i, o in enumerate((s5_new.reshape(dbsz, 2, S5_G, S5_N), hg_new, kv5(rc), kv5(rs), win_new,
                               ext[:, dl:])):
            outs[2 * i + 1].append(o)

    st = [jnp.stack(o) for o in outs]
    return (xp.reshape(bsz, seq, D_MODEL), xs.reshape(dbsz, dl, D_MODEL), *st)
```

```python
import functools
import math

import jax
import jax.numpy as jnp
from jax import lax
from jax.experimental import pallas as pl
from jax.experimental.pallas import tpu as pltpu

F32 = jnp.float32
BF16 = jnp.bfloat16
I32 = jnp.int32

D_MODEL = 2048
D_A = D_MODEL // 4
S5_GROUP = 16
S5_G = D_A // S5_GROUP
S5_N = 64
S5_W = S5_G * S5_N
D_B = D_MODEL // 4
HG_DK = 128
HG_DV = 128
HG_HEADS = D_B // HG_DV
D_C = D_MODEL // 2
NSA_DH = 128
NSA_HEADS = D_C // NSA_DH
NSA_KVH = 2
NSA_REP = NSA_HEADS // NSA_KVH
N_KV = NSA_KVH * NSA_DH
ROPE_DIM = NSA_DH // 4
ROPE_THETA = 500000.0
CMP_STRIDE = 16
SEL_BLOCK = 64
N_SEL = 16
WINDOW = 512
FORCE_BONUS = 1000.0
D_FF = ((8 * D_MODEL) // 3 + 127) // 128 * 128
CONV_W = 3
EPS = 1e-6
NEG_INF = -1e30
PAGE = 128

LANE = 128
SUBLANE = 8
VMEM_LIMIT = 56 * 1024 * 1024

OFF_MGT = 0
OFF_NQ = 3 * D_MODEL
OFF_NKV = OFF_NQ + D_C
OFF_UA = OFF_NKV + 6 * N_KV
OFF_HQ = OFF_UA + D_A
OFF_HF = OFF_HQ + D_B
OFF_HI = OFF_HF + D_B
OFF_HGT = OFF_HI + D_B
OFF_NGT = OFF_HGT + D_B
NGT_PAD = 256
N_INP = OFF_NGT + NGT_PAD
TN_IN = 1280
D_FFP = 5632
TF = 512
HG_C = 64
HG_SB = 16


def _cparams(sem):
    return pltpu.CompilerParams(dimension_semantics=sem, vmem_limit_bytes=VMEM_LIMIT)


def _dot(a, b):
    return jnp.dot(a, b, preferred_element_type=F32)


def _dot_nt(a, b):
    return lax.dot_general(a, b, (((1,), (1,)), ((), ())), preferred_element_type=F32)


def _dot_tn(a, b):
    return lax.dot_general(a, b, (((0,), (0,)), ((), ())), preferred_element_type=F32)


def _dot2(a, b):
    hi = a.astype(BF16)
    lo = (a - hi.astype(F32)).astype(BF16)
    return _dot(hi, b) + _dot(lo, b)


def _dot3(a, b):
    hi = a.astype(BF16)
    r1 = a - hi.astype(F32)
    mid = r1.astype(BF16)
    lo = (r1 - mid.astype(F32)).astype(BF16)
    return _dot(b, hi) + _dot(b, mid) + _dot(b, lo)


def _rms(x, w):
    return x * lax.rsqrt(jnp.mean(x * x, axis=-1, keepdims=True) + EPS) * w


def _masked_softmax(s, mask):
    s = jnp.where(mask, s, NEG_INF)
    p = jnp.where(mask, jnp.exp(s - jnp.max(s, axis=-1, keepdims=True)), 0.0)
    return p / jnp.maximum(jnp.sum(p, axis=-1, keepdims=True), 1e-30)


def _log_sigmoid(x):
    return jnp.minimum(x, 0.0) - jnp.log1p(jnp.exp(-jnp.abs(x)))


def _logaddexp(a, b):
    return jnp.maximum(a, b) + jnp.log1p(jnp.exp(-jnp.abs(a - b)))


def _mod_kernel(c_ref, w_ref, b_ref, o_ref):
    c = c_ref[...]
    a = (c * jax.nn.sigmoid(c)).astype(BF16)
    o_ref[0] = _dot(a, w_ref[0].astype(BF16)) + b_ref[0]


def _mod_call(c_all, w_mod, b_mod):
    depth, d, n = w_mod.shape
    r = c_all.shape[0]
    tn = 1024
    return pl.pallas_call(
        _mod_kernel,
        grid=(depth, n // tn),
        in_specs=[pl.BlockSpec((r, d), lambda l, j: (0, 0)),
                  pl.BlockSpec((1, d, tn), lambda l, j: (l, 0, j)),
                  pl.BlockSpec((1, 1, tn), lambda l, j: (l, 0, j))],
        out_specs=pl.BlockSpec((1, r, tn), lambda l, j: (l, 0, j)),
        out_shape=jax.ShapeDtypeStruct((depth, r, n), F32),
        compiler_params=_cparams(("arbitrary", "arbitrary")),
        name="mod",
    )(c_all, w_mod, b_mod.reshape(depth, 1, n))


def _inproj_kernel(x_ref, sc_ref, sh_ref, nw_ref, w_ref, o_ref, h_scr):
    @pl.when(pl.program_id(1) == 0)
    def _():
        y = _rms(x_ref[...], nw_ref[...])
        h_scr[...] = (y * (1.0 + sc_ref[0]) + sh_ref[0]).astype(BF16)

    o_ref[...] = _dot(h_scr[...], w_ref[0])


def _mod_spec(modx, tm, rows_per_group, col):
    if modx.shape[1] == 1:
        return pl.BlockSpec((1, 1, D_MODEL), lambda i, *_: ((i * tm) // rows_per_group, 0, col))
    return pl.BlockSpec((1, tm, D_MODEL), lambda i, *_: (0, i, col))


def _inproj_call(x, modx, rows_per_group, nw, w, layer):
    m = x.shape[0]
    tm = min(512, m)
    return pl.pallas_call(
        _inproj_kernel,
        grid=(m // tm, N_INP // TN_IN),
        in_specs=[pl.BlockSpec((tm, D_MODEL), lambda i, j: (i, 0)),
                  _mod_spec(modx, tm, rows_per_group, 1),
                  _mod_spec(modx, tm, rows_per_group, 0),
                  pl.BlockSpec((1, D_MODEL), lambda i, j: (0, 0)),
                  pl.BlockSpec((1, D_MODEL, TN_IN), lambda i, j: (layer, 0, j))],
        out_specs=pl.BlockSpec((tm, TN_IN), lambda i, j: (i, j)),
        out_shape=jax.ShapeDtypeStruct((m, N_INP), F32),
        scratch_shapes=[pltpu.VMEM((tm, D_MODEL), BF16)],
        compiler_params=_cparams(("arbitrary", "arbitrary")),
        name="inproj",
    )(x, modx, modx, nw, w)


S5_LC = 512


def _s5_kernel(u_ref, h0_ref, bbr_ref, bbi_ref, ccr_ref, cci_ref, apr_ref, api_ref, d_ref,
               wg_ref, bg_ref, y_ref, hl_ref, hre, him, car, *, t_rows, last_row):
    @pl.when(pl.program_id(1) == 0)
    def _():
        car[...] = h0_ref[0]

    u = u_ref[...]
    ub = u.astype(BF16)
    hre[...] = _dot(ub, bbr_ref[...])
    him[...] = _dot(ub, bbi_ref[...])
    row = lax.broadcasted_iota(I32, (SUBLANE, S5_LC), 0)
    for c in range(S5_W // S5_LC):
        sl = slice(c * S5_LC, (c + 1) * S5_LC)
        pwr = apr_ref[:, sl]
        pwi = api_ref[:, sl]
        steps = tuple((s, pwr[s - 1:s], pwi[s - 1:s]) for s in (1, 2, 4))

        def body(i, carry, sl=sl, pwr=pwr, pwi=pwi, steps=steps):
            cr, ci = carry
            r0 = pl.multiple_of(i * SUBLANE, SUBLANE)
            xr = hre[pl.ds(r0, SUBLANE), sl]
            xi = him[pl.ds(r0, SUBLANE), sl]
            for s, ar, ai in steps:
                sr = jnp.where(row >= s, pltpu.roll(xr, s, 0), 0.0)
                si = jnp.where(row >= s, pltpu.roll(xi, s, 0), 0.0)
                xr, xi = xr + (ar * sr - ai * si), xi + (ar * si + ai * sr)
            xr, xi = xr + (pwr * cr - pwi * ci), xi + (pwr * ci + pwi * cr)
            hre[pl.ds(r0, SUBLANE), sl] = xr
            him[pl.ds(r0, SUBLANE), sl] = xi
            return xr[SUBLANE - 1:SUBLANE], xi[SUBLANE - 1:SUBLANE]

        cr, ci = lax.fori_loop(0, t_rows // SUBLANE, body, (car[0:1, sl], car[1:2, sl]))
        car[0:1, sl] = cr
        car[1:2, sl] = ci

    lr = t_rows - SUBLANE + last_row
    hl_ref[0, 0:1, :] = hre[lr:lr + 1, :]
    hl_ref[0, 1:2, :] = him[lr:lr + 1, :]
    y = (_dot(hre[...].astype(BF16), ccr_ref[...]) - _dot(him[...].astype(BF16), cci_ref[...])
         + d_ref[...] * u)
    y = jax.nn.gelu(y)
    z = _dot(y.astype(BF16), wg_ref[...]) + bg_ref[...]
    y_ref[...] = (y * jax.nn.sigmoid(z)).astype(BF16)


def _s5_call(u_arr, ucol, nb, seq, h0, p, last_row):
    t_rows = min(256, seq)
    nt = seq // t_rows
    const = lambda shape: pl.BlockSpec(shape, lambda b, t: (0,) * len(shape))
    kern = functools.partial(_s5_kernel, t_rows=t_rows, last_row=last_row)
    return pl.pallas_call(
        kern,
        grid=(nb, nt),
        in_specs=[pl.BlockSpec((t_rows, D_A), lambda b, t: (b * nt + t, ucol)),
                  pl.BlockSpec((1, 2, S5_W), lambda b, t: (b, 0, 0)),
                  const((D_A, S5_W)), const((D_A, S5_W)), const((S5_W, D_A)), const((S5_W, D_A)),
                  const((SUBLANE, S5_W)), const((SUBLANE, S5_W)), const((1, D_A)),
                  const((D_A, D_A)), const((1, D_A))],
        out_specs=[pl.BlockSpec((t_rows, D_A), lambda b, t: (b * nt + t, 0)),
                   pl.BlockSpec((1, 2, S5_W), lambda b, t: (b, 0, 0))],
        out_shape=[jax.ShapeDtypeStruct((nb * seq, D_A), BF16),
                   jax.ShapeDtypeStruct((nb, 2, S5_W), F32)],
        scratch_shapes=[pltpu.VMEM((t_rows, S5_W), F32), pltpu.VMEM((t_rows, S5_W), F32),
                        pltpu.VMEM((2, S5_W), F32)],
        compiler_params=_cparams(("arbitrary", "arbitrary")),
        name="s5",
    )(u_arr, h0, p["bbr"], p["bbi"], p["ccr"], p["cci"], p["apr"], p["api"], p["d"],
      p["wg"], p["bg"])


def _hgrn_kernel(q_ref, f_ref, i_ref, g_ref, s0_ref, lp_ref, nw_ref, y_ref, sl_ref, st, pscr, rscr,
                 *, l_valid):
    t = pl.program_id(1)
    c = HG_C
    nsb = c // HG_SB
    nh = HG_HEADS

    @pl.when(t == 0)
    def _():
        for h in range(nh):
            st[h] = s0_ref[0, h].T

    row = lax.broadcasted_iota(I32, (c, D_B), 0)
    valid = (t * c + row) < l_valid
    tri = (lax.broadcasted_iota(I32, (c, c), 0) >= lax.broadcasted_iota(I32, (c, c), 1)).astype(BF16)
    hf = f_ref[...]
    ls = _log_sigmoid(hf)
    lf = jnp.where(lp_ref[3:4, :] > 0.5, _logaddexp(lp_ref[0:1, :], lp_ref[1:2, :] + ls), ls)
    kk = lp_ref[2:3, :] * jax.nn.sigmoid(-hf)
    lf = jnp.where(valid, lf, 0.0)
    kk = jnp.where(valid, kk, 0.0)
    hq = q_ref[...]
    q = hq * jax.nn.sigmoid(hq)
    v = i_ref[...]
    b = _dot3(lf, tri)
    zero_row = jnp.zeros((1, D_B), F32)
    bref = [zero_row] + [b[i * HG_SB - 1:i * HG_SB] for i in range(1, nsb)]
    bend = [b[(j + 1) * HG_SB - 1:(j + 1) * HG_SB] for j in range(nsb)]
    rows16 = lambda r: jnp.broadcast_to(r, (HG_SB, D_B))
    qq = q * jnp.exp(b - jnp.concatenate([rows16(r) for r in bref], axis=0))
    kks = kk * jnp.exp(jnp.concatenate([rows16(r) for r in bend], axis=0) - b)
    lhs = []
    for j in range(nsb - 1):
        dsel = jnp.concatenate(
            [rows16(jnp.exp(bref[i] - bend[j])) if i > j else jnp.zeros((HG_SB, D_B), F32)
             for i in range(nsb)], axis=0)
        lhs.append((qq * dsel).astype(BF16))
    rhs = [jnp.where(row // HG_SB == j, kks, 0.0).astype(BF16) for j in range(nsb - 1)]
    srow = lax.broadcasted_iota(I32, (HG_SB, D_B), 0)
    for i in range(nsb):
        rs = slice(i * HG_SB, (i + 1) * HG_SB)
        bi, qi, ki = b[rs], q[rs], kk[rs]
        for s in range(HG_SB):
            m = srow >= s
            e = jnp.exp(jnp.where(m, bi - bi[s:s + 1], 0.0))
            piece = jnp.where(m, (qi * ki[s:s + 1]) * e, 0.0).astype(BF16)
            for h in range(nh):
                r0 = ((h * nsb + i) * HG_SB + s) * HG_SB
                pscr[r0:r0 + HG_SB, :] = piece[:, h * HG_DK:(h + 1) * HG_DK]
    rscr[...] = _dot(pscr[...], jnp.ones((HG_DK, HG_DK), BF16))
    qe = (q * jnp.exp(b)).astype(BF16)
    vb = v.astype(BF16)
    bl = b[c - 1:c]
    kdec = (kk * jnp.exp(bl - b)).astype(BF16)
    ebl = jnp.exp(bl)
    gt = g_ref[...]
    gate = gt * jax.nn.sigmoid(gt)
    for h in range(nh):
        hs = slice(h * HG_DK, (h + 1) * HG_DK)
        s_t = st[h]
        att = _dot_nt(jnp.concatenate([x[:, hs] for x in lhs], axis=1),
                      jnp.concatenate([x[:, hs] for x in rhs], axis=1))
        o = _dot(att.astype(BF16), vb[:, hs]) + _dot_nt(qe[:, hs], s_t.astype(BF16))
        diag = []
        for i in range(nsb):
            od = jnp.zeros((HG_SB, HG_DV), F32)
            for s in range(HG_SB):
                r0 = ((h * nsb + i) * HG_SB + s) * HG_SB
                od = od + rscr[r0:r0 + HG_SB, :] * v[i * HG_SB + s:i * HG_SB + s + 1, hs]
            diag.append(od)
        o = o + jnp.concatenate(diag, axis=0)
        st[h] = s_t * ebl[:, hs] + _dot_tn(vb[:, hs], kdec[:, hs])
        y_ref[:, hs] = (_rms(o, nw_ref[...]) * gate[:, hs]).astype(BF16)

    @pl.when(t == pl.num_programs(1) - 1)
    def _():
        for h in range(nh):
            sl_ref[0, h] = st[h].T


def _dot2g(g, z):
    hi = z.astype(BF16)
    lo = (z - hi.astype(F32)).astype(BF16)
    return _dot(g, hi) + _dot(g, lo)


def _hgrn_call(arr, cols, nb, seq_pad, l_valid, s0, lp, nw):
    nt = seq_pad // HG_C
    blk = lambda col: pl.BlockSpec((HG_C, D_B), lambda b, t: (b * nt + t, col))
    kern = functools.partial(_hgrn_kernel, l_valid=l_valid)
    return pl.pallas_call(
        kern,
        grid=(nb, nt),
        in_specs=[blk(cols[0]), blk(cols[1]), blk(cols[2]), blk(cols[3]),
                  pl.BlockSpec((1, HG_HEADS, HG_DK, HG_DV), lambda b, t: (b, 0, 0, 0)),
                  pl.BlockSpec((SUBLANE, D_B), lambda b, t: (0, 0)),
                  pl.BlockSpec((1, HG_DV), lambda b, t: (0, 0))],
        out_specs=[pl.BlockSpec((HG_C, D_B), lambda b, t: (b * nt + t, 0)),
                   pl.BlockSpec((1, HG_HEADS, HG_DK, HG_DV), lambda b, t: (b, 0, 0, 0))],
        out_shape=[jax.ShapeDtypeStruct((nb * seq_pad, D_B), BF16),
                   jax.ShapeDtypeStruct((nb, HG_HEADS, HG_DK, HG_DV), F32)],
        scratch_shapes=[pltpu.VMEM((HG_HEADS, HG_DV, HG_DK), F32),
                        pltpu.VMEM((HG_HEADS * HG_C * HG_SB, HG_DK), BF16),
                        pltpu.VMEM((HG_HEADS * HG_C * HG_SB, HG_DK), F32)],
        compiler_params=_cparams(("arbitrary", "arbitrary")),
        name="hgrn",
    )(arr, arr, arr, arr, s0, lp, nw)


def _rope(x, cos_t, sin_a, sin_b):
    return (x * cos_t + pltpu.roll(x, NSA_DH - ROPE_DIM // 2, 1) * sin_a
            + pltpu.roll(x, ROPE_DIM // 2, 1) * sin_b)


def _nsa_prep_kernel(q_ref, kc_ref, ks_ref, kw_ref, g_ref, cos_ref, sa_ref, sb_ref, qn_ref, kn_ref,
                     qo_ref, rc_ref, rs_ref, rw_ref, bc_ref, bs_ref, bw_ref, go_ref):
    cos_t, sin_a, sin_b = cos_ref[...], sa_ref[...], sb_ref[...]
    qn = qn_ref[...]
    for h in range(NSA_HEADS):
        hs = slice(h * NSA_DH, (h + 1) * NSA_DH)
        qo_ref[:, hs] = _rope(_rms(q_ref[:, hs], qn), cos_t, sin_a, sin_b).astype(BF16)
    for br, (src, dst, dstb) in enumerate(((kc_ref, rc_ref, bc_ref), (ks_ref, rs_ref, bs_ref),
                                           (kw_ref, rw_ref, bw_ref))):
        kn = kn_ref[br:br + 1, :]
        tm = src.shape[0]
        for g in range(NSA_KVH):
            gs = slice(g * NSA_DH, (g + 1) * NSA_DH)
            k = _rope(_rms(src[:, gs], kn), cos_t, sin_a, sin_b)
            dst[pl.ds(g, tm, stride=KV_TILES), :] = k
            dstb[:, gs] = k.astype(BF16)
            vs = slice(N_KV + g * NSA_DH, N_KV + (g + 1) * NSA_DH)
            v = src[:, vs]
            dst[pl.ds(NSA_KVH + g, tm, stride=KV_TILES), :] = v
            dstb[:, vs] = v.astype(BF16)
    go_ref[...] = jax.nn.sigmoid(g_ref[...])


def _nsa_prep_call(hin, rope_tabs, seq, qn, kn):
    m = hin.shape[0]
    tm = min(256, m)
    npos = rope_tabs[0].shape[0] // tm
    rspec = pl.BlockSpec((tm, NSA_DH), lambda i: (i % npos, 0))
    kvb = OFF_NKV // (2 * N_KV)
    return pl.pallas_call(
        _nsa_prep_kernel,
        grid=(m // tm,),
        in_specs=[pl.BlockSpec((tm, D_C), lambda i: (i, OFF_NQ // D_C)),
                  pl.BlockSpec((tm, 2 * N_KV), lambda i: (i, kvb)),
                  pl.BlockSpec((tm, 2 * N_KV), lambda i: (i, kvb + 1)),
                  pl.BlockSpec((tm, 2 * N_KV), lambda i: (i, kvb + 2)),
                  pl.BlockSpec((tm, LANE), lambda i: (i, OFF_NGT // LANE)),
                  rspec, rspec, rspec,
                  pl.BlockSpec((1, NSA_DH), lambda i: (0, 0)),
                  pl.BlockSpec((3, NSA_DH), lambda i: (0, 0))],
        out_specs=[pl.BlockSpec((tm, D_C), lambda i: (i, 0))]
        + [pl.BlockSpec((tm * KV_TILES, LANE), lambda i: (i, 0))] * 3
        + [pl.BlockSpec((tm, 2 * N_KV), lambda i: (i, 0))] * 3
        + [pl.BlockSpec((tm, LANE), lambda i: (i, 0))],
        out_shape=[jax.ShapeDtypeStruct((m, D_C), BF16)]
        + [jax.ShapeDtypeStruct((m * KV_TILES, LANE), F32)] * 3
        + [jax.ShapeDtypeStruct((m, 2 * N_KV), BF16)] * 3
        + [jax.ShapeDtypeStruct((m, LANE), F32)],
        compiler_params=_cparams(("arbitrary",)),
        name="nsa_prep",
    )(hin, hin, hin, hin, hin, *rope_tabs, qn, kn)


KV_TILES = 2 * N_KV // LANE


def _pool_rows(load, n, a1_ref, a2_ref):
    firsts, seconds = [], []
    for c in range(KV_TILES):
        ls = slice(c * LANE, (c + 1) * LANE)
        first = jnp.zeros((n, LANE), F32)
        second = jnp.zeros((n, LANE), F32)
        for j in range(CMP_STRIDE):
            x = load(j, c)
            first = first + x * a1_ref[j:j + 1, ls]
            second = second + x * a2_ref[j:j + 1, ls]
        firsts.append(first)
        seconds.append(second)
    return firsts, seconds


def _tile_rows(j, c, n):
    return pl.ds(j * KV_TILES + c, n, stride=CMP_STRIDE * KV_TILES)


def _cmp_mlp(first, second, n, w1_ref, b1_ref, w2_ref, o_ref):
    for c in range(2):
        for g in range(NSA_KVH):
            tile = 2 * c + g
            pooled = first[tile] + pltpu.roll(second[tile], n - 1, 0)
            h = _dot(pooled.astype(BF16), w1_ref[c]) + b1_ref[c:c + 1, :]
            o_ref[0, :, tile * NSA_DH:(tile + 1) * NSA_DH] = _dot(
                jax.nn.gelu(h).astype(BF16), w2_ref[c]).astype(BF16)


def _cmp_prompt_kernel(x_ref, a1_ref, a2_ref, w1_ref, b1_ref, w2_ref, o_ref, *, n):
    first, second = _pool_rows(lambda j, c: x_ref[_tile_rows(j, c, n), :], n, a1_ref, a2_ref)
    _cmp_mlp(first, second, n, w1_ref, b1_ref, w2_ref, o_ref)


def _cmp_prompt_call(rows, nb, seq, cp):
    n = seq // CMP_STRIDE
    const = lambda shape: pl.BlockSpec(shape, lambda b: (0,) * len(shape))
    return pl.pallas_call(
        functools.partial(_cmp_prompt_kernel, n=n),
        grid=(nb,),
        in_specs=[pl.BlockSpec((seq * KV_TILES, LANE), lambda b: (b, 0)),
                  const((CMP_STRIDE, 2 * N_KV)), const((CMP_STRIDE, 2 * N_KV)),
                  const((2, NSA_DH, NSA_DH)), const((2, NSA_DH)), const((2, NSA_DH, NSA_DH))],
        out_specs=pl.BlockSpec((1, n, 2 * N_KV), lambda b: (b, 0, 0)),
        out_shape=jax.ShapeDtypeStruct((nb, n, 2 * N_KV), BF16),
        compiler_params=_cparams(("arbitrary",)),
        name="cmp_prompt",
    )(rows.reshape(-1, LANE), cp["a1"], cp["a2"], cp["w1"], cp["b1"], cp["w2"])


PG = 16
SUB_PER_PAGE = PAGE // CMP_STRIDE


SUB_ROWS = CMP_STRIDE * KV_TILES


def _cmp_pool_cache_kernel(pt_ref, *refs):
    pages = refs[:PG]
    a1_ref, a2_ref, f_ref, s_ref = refs[PG:]
    nv = SUB_ROWS // SUBLANE
    w1 = [a1_ref[0, v * SUBLANE:(v + 1) * SUBLANE, :] for v in range(nv)]
    w2 = [a2_ref[0, v * SUBLANE:(v + 1) * SUBLANE, :] for v in range(nv)]
    low = lax.broadcasted_iota(I32, (SUBLANE, LANE), 0) < KV_TILES

    def fold(acc):
        return acc + pltpu.roll(acc, KV_TILES, 0)

    for k in range(PG):
        for n2 in range(SUB_PER_PAGE // 2):
            halves = []
            for n in (2 * n2, 2 * n2 + 1):
                a1 = a2 = None
                for v in range(nv):
                    x = pages[k][0, 0, n * SUB_ROWS + v * SUBLANE:n * SUB_ROWS + (v + 1) * SUBLANE, :]
                    a1 = x * w1[v] if a1 is None else a1 + x * w1[v]
                    a2 = x * w2[v] if a2 is None else a2 + x * w2[v]
                halves.append((fold(a1), fold(a2)))
            r0 = (k * SUB_PER_PAGE + 2 * n2) * KV_TILES
            f_ref[0, 0, r0:r0 + SUBLANE, :] = jnp.where(low, halves[0][0], halves[1][0])
            s_ref[0, 0, r0:r0 + SUBLANE, :] = jnp.where(low, halves[0][1], halves[1][1])


def _page_specs(n_pages, layer_of, batch_of, group_of, page_shape=(PAGE, 2 * N_KV)):
    def spec(k):
        return pl.BlockSpec(
            (1, 1) + page_shape,
            lambda *a, k=k: (layer_of(*a), a[-1][batch_of(*a) * n_pages + group_of(*a) * PG + k], 0, 0))
    return [spec(k) for k in range(PG)]


def _cmp_pool_cache_call(cache, pt_flat, nb, n_pages, a1, a2):
    depth = cache.shape[0]
    npg = n_pages // PG
    nsub = n_pages * SUB_PER_PAGE
    rows = PG * SUB_PER_PAGE
    grid_spec = pltpu.PrefetchScalarGridSpec(
        num_scalar_prefetch=1,
        grid=(depth, nb, npg),
        in_specs=_page_specs(n_pages, lambda l, b, g, pt: l, lambda l, b, g, pt: b,
                             lambda l, b, g, pt: g, (PAGE * KV_TILES, LANE))
        + [pl.BlockSpec((1, SUB_ROWS, LANE), lambda l, b, g, pt: (l, 0, 0))] * 2,
        out_specs=[pl.BlockSpec((1, 1, rows * KV_TILES, LANE), lambda l, b, g, pt: (l, b, g, 0))] * 2,
    )
    return pl.pallas_call(
        _cmp_pool_cache_kernel,
        grid_spec=grid_spec,
        out_shape=[jax.ShapeDtypeStruct((depth, nb, nsub * KV_TILES, LANE), F32)] * 2,
        compiler_params=_cparams(("arbitrary", "arbitrary", "arbitrary")),
        name="cmp_pool_cache",
    )(pt_flat, *([cache] * PG), a1, a2)


def _cmp_mlp_cache_kernel(f_ref, s_ref, w1_ref, b1_ref, w2_ref, o_ref, *, n):
    tiles = lambda ref: [ref[0, 0, pl.ds(c, n, stride=KV_TILES), :] for c in range(KV_TILES)]
    _cmp_mlp(tiles(f_ref), tiles(s_ref), n, w1_ref.at[0], b1_ref.at[0], w2_ref.at[0], o_ref.at[0])


def _cmp_mlp_cache_call(first, second, w1, b1, w2):
    depth, nb, rows, _ = first.shape
    n = rows // KV_TILES
    blk = pl.BlockSpec((1, 1, n, 2 * N_KV), lambda l, b: (l, b, 0, 0))
    tblk = pl.BlockSpec((1, 1, rows, LANE), lambda l, b: (l, b, 0, 0))
    return pl.pallas_call(
        functools.partial(_cmp_mlp_cache_kernel, n=n),
        grid=(depth, nb),
        in_specs=[tblk, tblk,
                  pl.BlockSpec((1, 2, NSA_DH, NSA_DH), lambda l, b: (l, 0, 0, 0)),
                  pl.BlockSpec((1, 2, NSA_DH), lambda l, b: (l, 0, 0)),
                  pl.BlockSpec((1, 2, NSA_DH, NSA_DH), lambda l, b: (l, 0, 0, 0))],
        out_specs=blk,
        out_shape=jax.ShapeDtypeStruct((depth, nb, n, 2 * N_KV), BF16),
        compiler_params=_cparams(("arbitrary", "arbitrary")),
        name="cmp_mlp_cache",
    )(first, second, w1, b1, w2)


def _block_scores(imp, qpos, n_slc):
    blk = lax.broadcasted_iota(I32, imp.shape, 1)
    cur = qpos // SEL_BLOCK
    forced = (blk == 0) | (blk == cur) | (blk == cur - 1)
    valid = blk * SEL_BLOCK <= qpos
    score = jnp.where(valid, imp + jnp.where(forced, FORCE_BONUS, 0.0), -1.0)
    return jnp.where(blk < n_slc, score, -2.0)


def _select_blocks_ranked(imp, qpos, n_slc):
    nb8 = (n_slc + SUBLANE - 1) // SUBLANE * SUBLANE
    st = _block_scores(imp, qpos, n_slc).T[:nb8]
    ridx = lax.broadcasted_iota(I32, st.shape, 0)
    cnt = jnp.zeros(st.shape, F32)
    for i in range(n_slc):
        ri = st[i:i + 1]
        cnt = cnt + jnp.where(ri > st, 1.0, jnp.where(ri == st, jnp.where(ridx > i, 1.0, 0.0), 0.0))
    sel_t = jnp.where((cnt < min(N_SEL, n_slc)) & (ridx < n_slc), 1.0, 0.0)
    if nb8 < LANE:
        sel_t = jnp.concatenate([sel_t, jnp.zeros((LANE - nb8, st.shape[1]), F32)], axis=0)
    return sel_t.T


def _select_blocks(imp, qpos, n_slc):
    blk = lax.broadcasted_iota(I32, imp.shape, 1)
    score = _block_scores(imp, qpos, n_slc)
    sel = jnp.zeros(imp.shape, F32)
    for _ in range(min(N_SEL, n_slc)):
        mx = jnp.max(score, axis=-1, keepdims=True)
        idx = jnp.min(jnp.where(score == mx, blk, 1 << 30), axis=-1, keepdims=True)
        hit = blk == idx
        sel = jnp.where(hit, 1.0, sel)
        score = jnp.where(hit, -3.0, score)
    return sel


def _cmp_branch(q, kck, kcv, qpos, n_cmp):
    s = _dot_nt(q, kck) * (NSA_DH ** -0.5)
    col = lax.broadcasted_iota(I32, s.shape, 1)
    mask = (col * CMP_STRIDE + (2 * CMP_STRIDE - 1) <= qpos) & (col < n_cmp)
    p = _masked_softmax(s, mask)
    return p, _dot(p.astype(BF16), kcv)


NSA_KC = 512


def _nsa_prompt_kernel(q_ref, kc_ref, ks_ref, vs_ref, kw_ref, vw_ref, g_ref, o_ref,
                       bias_scr, m_scr, l_scr, acc_scr, *, tq, seq, slab):
    q0 = pl.program_id(1) * tq
    rows = NSA_REP * tq
    ncp = seq // CMP_STRIDE
    n_slc = seq // SEL_BLOCK
    scale = NSA_DH ** -0.5
    gates = g_ref[...]
    q4s = [jnp.concatenate([q_ref[:, (g * NSA_REP + r) * NSA_DH:(g * NSA_REP + r + 1) * NSA_DH]
                            for r in range(NSA_REP)], axis=0) for g in range(NSA_KVH)]
    qpos_c = q0 + (lax.broadcasted_iota(I32, (rows, ncp), 0) & (tq - 1))
    o_cs, psums = [], []
    for g in range(NSA_KVH):
        p_c, o_c = _cmp_branch(q4s[g], kc_ref[0, :, g * NSA_DH:(g + 1) * NSA_DH],
                               kc_ref[0, :, N_KV + g * NSA_DH:N_KV + (g + 1) * NSA_DH], qpos_c, ncp - 1)
        psum = p_c[0:tq]
        for r in range(1, NSA_REP):
            psum = psum + p_c[r * tq:(r + 1) * tq]
        o_cs.append(o_c)
        psums.append(psum)

    def ranked():
        gmat = (lax.broadcasted_iota(I32, (ncp, LANE), 0) // (SEL_BLOCK // CMP_STRIDE)
                == lax.broadcasted_iota(I32, (ncp, LANE), 1)).astype(BF16)
        imp = _dot2(jnp.concatenate(psums, axis=0), gmat)
        qp = q0 + (lax.broadcasted_iota(I32, (NSA_KVH * tq, LANE), 0) & (tq - 1))
        return _select_blocks_ranked(imp, qp, n_slc)

    sel = lax.cond(q0 + tq > N_SEL * SEL_BLOCK, ranked,
                   lambda: jnp.ones((NSA_KVH * tq, LANE), F32))
    emat = (lax.broadcasted_iota(I32, (LANE, seq), 1) // SEL_BLOCK
            == lax.broadcasted_iota(I32, (LANE, seq), 0)).astype(BF16)
    nk = (q0 + tq + NSA_KC - 1) // NSA_KC
    s0 = pl.multiple_of(jnp.maximum(q0 + tq - slab, 0), SUBLANE)
    key = lax.broadcasted_iota(I32, (tq, seq), 1)
    causal = key <= q0 + lax.broadcasted_iota(I32, (tq, seq), 0)
    wp = s0 + lax.broadcasted_iota(I32, (tq, slab), 1)
    qpw = q0 + lax.broadcasted_iota(I32, (tq, slab), 0)
    wb1 = jnp.where((wp <= qpw) & (wp > qpw - WINDOW), 0.0, NEG_INF)
    wbias = jnp.concatenate([wb1] * NSA_REP, axis=0)
    for g in range(NSA_KVH):
        gs = slice(g * NSA_DH, (g + 1) * NSA_DH)
        q4 = q4s[g]
        keep = (_dot(sel[g * tq:(g + 1) * tq].astype(BF16), emat) > 0.5) & causal
        bias = jnp.where(keep, 0.0, NEG_INF)
        for c in range(seq // NSA_KC):
            for r in range(NSA_REP):
                bias_scr[c, r * tq:(r + 1) * tq, :] = bias[:, c * NSA_KC:(c + 1) * NSA_KC]
        m_scr[...] = jnp.full(m_scr.shape, NEG_INF, F32)
        l_scr[...] = jnp.zeros(l_scr.shape, F32)
        acc_scr[...] = jnp.zeros(acc_scr.shape, F32)

        def body(c, carry, q4=q4, gs=gs):
            k0 = pl.multiple_of(c * NSA_KC, NSA_KC)
            s = _dot_nt(q4, ks_ref[pl.ds(k0, NSA_KC), gs]) * scale + bias_scr[c]
            m_old = m_scr[...]
            m_new = jnp.maximum(m_old, jnp.max(s, axis=-1, keepdims=True))
            p = jnp.exp(s - m_new)
            alpha = jnp.exp(m_old - m_new)
            l_scr[...] = alpha * l_scr[...] + jnp.sum(p, axis=-1, keepdims=True)
            acc_scr[...] = alpha * acc_scr[...] + _dot(p.astype(BF16), vs_ref[pl.ds(k0, NSA_KC), gs])
            m_scr[...] = m_new
            return carry

        lax.fori_loop(0, nk, body, 0)
        o_s = acc_scr[...] / l_scr[...]
        s_w = _dot_nt(q4, kw_ref[pl.ds(s0, slab), gs]) * scale + wbias
        p_w = jnp.exp(s_w - jnp.max(s_w, axis=-1, keepdims=True))
        o_w = (_dot(p_w.astype(BF16), vw_ref[pl.ds(s0, slab), gs])
               / jnp.sum(p_w, axis=-1, keepdims=True))
        o_c = o_cs[g]
        for r in range(NSA_REP):
            h = g * NSA_REP + r
            rs = slice(r * tq, (r + 1) * tq)
            o = (gates[:, 3 * h:3 * h + 1] * o_c[rs] + gates[:, 3 * h + 1:3 * h + 2] * o_s[rs]
                 + gates[:, 3 * h + 2:3 * h + 3] * o_w[rs])
            o_ref[:, h * NSA_DH:(h + 1) * NSA_DH] = o.astype(BF16)


def _nsa_prompt_call(qn, kc, bs, bw, gates, nb, seq):
    tq = 128
    nq = seq // tq
    slab = min(seq, WINDOW + tq)
    kv = lambda col: pl.BlockSpec((seq, N_KV), lambda b, i: (b, col))
    kern = functools.partial(_nsa_prompt_kernel, tq=tq, seq=seq, slab=slab)
    rows = NSA_REP * tq
    return pl.pallas_call(
        kern,
        grid=(nb, nq),
        in_specs=[pl.BlockSpec((tq, D_C), lambda b, i: (b * nq + i, 0)),
                  pl.BlockSpec((1, seq // CMP_STRIDE, 2 * N_KV), lambda b, i: (b, 0, 0)),
                  kv(0), kv(1), kv(0), kv(1),
                  pl.BlockSpec((tq, LANE), lambda b, i: (b * nq + i, 0))],
        out_specs=pl.BlockSpec((tq, D_C), lambda b, i: (b * nq + i, 0)),
        out_shape=jax.ShapeDtypeStruct((nb * seq, D_C), BF16),
        scratch_shapes=[pltpu.VMEM((seq // NSA_KC, rows, NSA_KC), F32), pltpu.VMEM((rows, 1), F32),
                        pltpu.VMEM((rows, 1), F32), pltpu.VMEM((rows, NSA_DH), F32)],
        compiler_params=_cparams(("arbitrary", "arbitrary")),
        name="nsa_prompt",
    )(qn, kc, bs, bs, bw, bw, gates)


def _nsa_s_sel_kernel(q_ref, kc_ref, oc_ref, sel_ref, *, past, dl, n_tiles):
    rows = NSA_REP * dl
    ncp = past // CMP_STRIDE
    n_slc = (past + dl + SEL_BLOCK - 1) // SEL_BLOCK
    nbp = (n_slc + LANE - 1) // LANE * LANE
    per_tile = PG * PAGE // SEL_BLOCK
    gmat = (lax.broadcasted_iota(I32, (ncp, nbp), 0) // (SEL_BLOCK // CMP_STRIDE)
            == lax.broadcasted_iota(I32, (ncp, nbp), 1)).astype(BF16)
    rmat = (lax.broadcasted_iota(I32, (rows, rows), 0) // NSA_REP
            == lax.broadcasted_iota(I32, (rows, rows), 1) // NSA_REP).astype(BF16)

    def qpos_of(shape):
        return past + lax.broadcasted_iota(I32, shape, 0) // NSA_REP

    lane = lax.broadcasted_iota(I32, (rows, LANE), 1)
    for g in range(NSA_KVH):
        gs = slice(g * NSA_DH, (g + 1) * NSA_DH)
        p_c, o_c = _cmp_branch(q_ref[0, g], kc_ref[0, 0, :, gs],
                               kc_ref[0, 0, :, N_KV + g * NSA_DH:N_KV + (g + 1) * NSA_DH],
                               qpos_of((rows, ncp)), ncp - 1)
        oc_ref[0, g] = o_c
        psum = _dot2g(rmat, p_c)
        imp = _dot2(psum, gmat)
        sel = _select_blocks(imp, qpos_of((rows, nbp)), n_slc)
        for tl in range(n_tiles):
            lo = tl * per_tile
            chunk = sel[:, (lo // LANE) * LANE:(lo // LANE + 1) * LANE]
            sh = (LANE - lo % LANE) % LANE
            if sh:
                chunk = pltpu.roll(chunk, sh, 1)
            sel_ref[0, g, tl] = jnp.where(lane < per_tile, chunk, 0.0)


def _nsa_s_sel_call(q16, kc, layer, past, dl):
    nb = q16.shape[0]
    rows = NSA_REP * dl
    n_tiles = past // (PG * PAGE) + 1
    kern = functools.partial(_nsa_s_sel_kernel, past=past, dl=dl, n_tiles=n_tiles)
    return pl.pallas_call(
        kern,
        grid=(nb,),
        in_specs=[pl.BlockSpec((1, NSA_KVH, rows, NSA_DH), lambda b: (b, 0, 0, 0)),
                  pl.BlockSpec((1, 1, past // CMP_STRIDE, 2 * N_KV), lambda b: (layer, b, 0, 0))],
        out_specs=[pl.BlockSpec((1, NSA_KVH, rows, NSA_DH), lambda b: (b, 0, 0, 0)),
                   pl.BlockSpec((1, NSA_KVH, n_tiles, rows, LANE), lambda b: (b, 0, 0, 0, 0))],
        out_shape=[jax.ShapeDtypeStruct((nb, NSA_KVH, rows, NSA_DH), F32),
                   jax.ShapeDtypeStruct((nb, NSA_KVH, n_tiles, rows, LANE), F32)],
        compiler_params=_cparams(("arbitrary",)),
        name="nsa_sample_select",
    )(q16, kc)


def _nsa_s_attn_kernel(pt_ref, *refs, past, dl, npg):
    pages = refs[:PG]
    (q_ref, selt_ref, sell_ref, emat_ref, oc_ref, ns_ref, nw_ref, wp_ref, g_ref,
     o_ref, m_scr, l_scr, a_scr) = refs[PG:]
    pg = pl.program_id(1)
    rows = NSA_REP * dl
    scale = NSA_DH ** -0.5
    wb = wp_ref.shape[2] // KV_TILES

    def tok_rows(ref, c, n):
        return ref[0, 0, pl.ds(c, n, stride=KV_TILES), :].astype(BF16)

    @pl.when(pg == 0)
    def _():
        m_scr[...] = jnp.full(m_scr.shape, NEG_INF, F32)
        l_scr[...] = jnp.zeros(l_scr.shape, F32)
        a_scr[...] = jnp.zeros(a_scr.shape, F32)

    for g in range(NSA_KVH):
        q = q_ref[0, g]
        kmask = _dot(selt_ref[0, g, 0].astype(BF16), emat_ref[...]) > 0.5
        s = jnp.concatenate(
            [_dot_nt(q, tok_rows(pages[k], g, PAGE)) for k in range(PG)], axis=1) * scale
        s = jnp.where(kmask, s, NEG_INF)
        m_old = m_scr[g]
        m_new = jnp.maximum(m_old, jnp.max(s, axis=-1, keepdims=True))
        p = jnp.where(kmask, jnp.exp(s - m_new), 0.0)
        alpha = jnp.exp(m_old - m_new)
        l_scr[g] = alpha * l_scr[g] + jnp.sum(p, axis=-1, keepdims=True)
        acc = alpha * a_scr[g]
        pb = p.astype(BF16)
        for k in range(PG):
            acc = acc + _dot(pb[:, k * PAGE:(k + 1) * PAGE], tok_rows(pages[k], NSA_KVH + g, PAGE))
        a_scr[g] = acc
        m_scr[g] = m_new

    @pl.when(pg == npg - 1)
    def _():
        gates = g_ref[0]
        tok = lax.broadcasted_iota(I32, (rows, SUBLANE), 0) // NSA_REP
        ncol = lax.broadcasted_iota(I32, (rows, SUBLANE), 1)
        for g in range(NSA_KVH):
            gs = slice(g * NSA_DH, (g + 1) * NSA_DH)
            vsl = slice(N_KV + g * NSA_DH, N_KV + (g + 1) * NSA_DH)
            q = q_ref[0, g]
            nmask = (sell_ref[0, g, 0][:, 0:1] > 0.5) & (ncol <= tok) & (ncol < dl)
            s_n = jnp.where(nmask, _dot_nt(q, ns_ref[0, :, gs].astype(BF16)) * scale, NEG_INF)
            m_old = m_scr[g]
            m_new = jnp.maximum(m_old, jnp.max(s_n, axis=-1, keepdims=True))
            p_n = jnp.where(nmask, jnp.exp(s_n - m_new), 0.0)
            alpha = jnp.exp(m_old - m_new)
            lsum = alpha * l_scr[g] + jnp.sum(p_n, axis=-1, keepdims=True)
            acc = alpha * a_scr[g] + _dot(p_n.astype(BF16), ns_ref[0, :, vsl].astype(BF16))
            o_s = acc / jnp.maximum(lsum, 1e-30)
            wrow = lax.broadcasted_iota(I32, (rows, wb), 1)
            wtok = lax.broadcasted_iota(I32, (rows, wb), 0) // NSA_REP
            pmask = (wrow > wtok + (wb - WINDOW)) & (wrow + (past - wb) >= 0)
            s_p = jnp.where(pmask, _dot_nt(q, tok_rows(wp_ref, g, wb)) * scale, NEG_INF)
            wmask = (ncol <= tok) & (ncol < dl)
            s_q = jnp.where(wmask, _dot_nt(q, nw_ref[0, :, gs].astype(BF16)) * scale, NEG_INF)
            mw = jnp.maximum(jnp.max(s_p, axis=-1, keepdims=True), jnp.max(s_q, axis=-1, keepdims=True))
            p_p = jnp.where(pmask, jnp.exp(s_p - mw), 0.0)
            p_q = jnp.where(wmask, jnp.exp(s_q - mw), 0.0)
            den = jnp.maximum(jnp.sum(p_p, axis=-1, keepdims=True) + jnp.sum(p_q, axis=-1, keepdims=True),
                              1e-30)
            o_w = (_dot(p_p.astype(BF16), tok_rows(wp_ref, NSA_KVH + g, wb))
                   + _dot(p_q.astype(BF16), nw_ref[0, :, vsl].astype(BF16))) / den
            gg = gates[g]
            o = gg[:, 0:1] * oc_ref[0, g] + gg[:, 1:2] * o_s + gg[:, 2:3] * o_w
            o_ref[0, g] = o.astype(BF16)


def _nsa_s_attn_call(cache, pt_flat, layer, q16, selt, emat, o_c, new_slc, new_win, win_prev, gates16,
                     past, dl):
    nb = q16.shape[0]
    rows = NSA_REP * dl
    n_pages = past // PAGE
    npg = n_pages // PG
    wrows = win_prev.shape[2]
    b4 = lambda shape: pl.BlockSpec((1,) + shape, lambda b, g, pt: (b,) + (0,) * len(shape))
    grid_spec = pltpu.PrefetchScalarGridSpec(
        num_scalar_prefetch=1,
        grid=(nb, npg),
        in_specs=_page_specs(n_pages, lambda b, g, pt: layer, lambda b, g, pt: b, lambda b, g, pt: g,
                             (PAGE * KV_TILES, LANE))
        + [b4((NSA_KVH, rows, NSA_DH)),
           pl.BlockSpec((1, NSA_KVH, 1, rows, LANE), lambda b, g, pt: (b, 0, g, 0, 0)),
           pl.BlockSpec((1, NSA_KVH, 1, rows, LANE), lambda b, g, pt: (b, 0, npg, 0, 0)),
           pl.BlockSpec((LANE, PG * PAGE), lambda b, g, pt: (0, 0)),
           b4((NSA_KVH, rows, NSA_DH)),
           b4((SUBLANE, 2 * N_KV)), b4((SUBLANE, 2 * N_KV)),
           pl.BlockSpec((1, 1, wrows, LANE), lambda b, g, pt: (layer, b, 0, 0)),
           b4((NSA_KVH, rows, LANE))],
        out_specs=b4((NSA_KVH, rows, NSA_DH)),
        scratch_shapes=[pltpu.VMEM((NSA_KVH, rows, 1), F32), pltpu.VMEM((NSA_KVH, rows, 1), F32),
                        pltpu.VMEM((NSA_KVH, rows, NSA_DH), F32)],
    )
    kern = functools.partial(_nsa_s_attn_kernel, past=past, dl=dl, npg=npg)
    return pl.pallas_call(
        kern,
        grid_spec=grid_spec,
        out_shape=jax.ShapeDtypeStruct((nb, NSA_KVH, rows, NSA_DH), BF16),
        compiler_params=_cparams(("arbitrary", "arbitrary")),
        name="nsa_sample_attn",
    )(pt_flat, *([cache] * PG), q16, selt, selt, emat, o_c, new_slc, new_win, win_prev, gates16)


def _merge_kernel(ya_ref, yb_ref, yc_ref, ga_ref, gb_ref, gc_ref, wa_ref, wb_ref, wc_ref, o_ref):
    m = (jax.nn.sigmoid(ga_ref[...]) * _dot(ya_ref[...], wa_ref[0])
         + jax.nn.sigmoid(gb_ref[...]) * _dot(yb_ref[...], wb_ref[0])
         + jax.nn.sigmoid(gc_ref[...]) * _dot(yc_ref[...], wc_ref[0]))
    o_ref[...] = m.astype(BF16)


def _merge_call(ya, yb, yc, hin, wa, wb, wc, layer):
    m = ya.shape[0]
    tm = min(256, m)
    const = lambda shape: pl.BlockSpec((1,) + shape, lambda i: (layer, 0, 0))
    gate = lambda col: pl.BlockSpec((tm, D_MODEL), lambda i: (i, col))
    return pl.pallas_call(
        _merge_kernel,
        grid=(m // tm,),
        in_specs=[pl.BlockSpec((tm, D_A), lambda i: (i, 0)), pl.BlockSpec((tm, D_B), lambda i: (i, 0)),
                  pl.BlockSpec((tm, D_C), lambda i: (i, 0)), gate(0), gate(1), gate(2),
                  const((D_A, D_MODEL)), const((D_B, D_MODEL)), const((D_C, D_MODEL))],
        out_specs=pl.BlockSpec((tm, D_MODEL), lambda i: (i, 0)),
        out_shape=jax.ShapeDtypeStruct((m, D_MODEL), BF16),
        compiler_params=_cparams(("arbitrary",)),
        name="merge",
    )(ya, yb, yc, hin, hin, hin, wa, wb, wc)


def _outproj_kernel(m_ref, x_ref, g_ref, w_ref, o_ref):
    o_ref[...] = x_ref[...] + g_ref[0] * _dot(m_ref[...], w_ref[0])


def _outproj_call(merged, x, modx, rows_per_group, w, layer):
    m = x.shape[0]
    tm = min(512, m)
    return pl.pallas_call(
        _outproj_kernel,
        grid=(m // tm,),
        in_specs=[pl.BlockSpec((tm, D_MODEL), lambda i: (i, 0)),
                  pl.BlockSpec((tm, D_MODEL), lambda i: (i, 0)),
                  _mod_spec(modx, tm, rows_per_group, 2),
                  pl.BlockSpec((1, D_MODEL, D_MODEL), lambda i: (layer, 0, 0))],
        out_specs=pl.BlockSpec((tm, D_MODEL), lambda i: (i, 0)),
        out_shape=jax.ShapeDtypeStruct((m, D_MODEL), F32),
        compiler_params=_cparams(("arbitrary",)),
        name="outproj",
    )(merged, x, modx, w)


def _ffn_tail(ua, ua1, ua2, ub, cw_ref, cb_ref, wd_ref):
    cw = cw_ref[0]
    conv = cb_ref[0] + cw[0:1] * ua2 + cw[1:2] * ua1 + cw[2:3] * ua
    return _dot((jax.nn.gelu(conv) * ub).astype(BF16), wd_ref[0])


def _ffn_prompt_kernel(x_ref, xh_ref, sc_ref, sh_ref, g_ref, nw_ref, cp_ref, wua_ref, wub_ref, cw_ref,
                       cb_ref, wd_ref, o_ref, tail_ref, h_scr, hh_scr, acc, *, tm, seq):
    f = pl.program_id(1)

    @pl.when(f == 0)
    def _():
        nw, sc, sh = nw_ref[...], sc_ref[0], sh_ref[0]
        h_scr[...] = (_rms(x_ref[...], nw) * (1.0 + sc) + sh).astype(BF16)
        hh_scr[...] = (_rms(xh_ref[...], nw) * (1.0 + sc) + sh).astype(BF16)
        acc[...] = jnp.zeros(acc.shape, F32)

    wua = wua_ref[0]
    ua = _dot(h_scr[...], wua)
    ub = _dot(h_scr[...], wub_ref[0])
    at_start = (pl.program_id(0) * tm) % seq == 0
    prev = jnp.where(at_start, cp_ref[0], _dot(hh_scr[...], wua))
    row = lax.broadcasted_iota(I32, ua.shape, 0)
    p1, p2 = prev[SUBLANE - 1:SUBLANE], prev[SUBLANE - 2:SUBLANE - 1]
    ua1 = jnp.where(row == 0, p1, pltpu.roll(ua, 1, 0))
    ua2 = jnp.where(row == 0, p2, jnp.where(row == 1, p1, pltpu.roll(ua, 2, 0)))
    tail_ref[0] = ua[tm - SUBLANE:tm]
    acc[...] += _ffn_tail(ua, ua1, ua2, ub, cw_ref, cb_ref, wd_ref)

    @pl.when(f == pl.num_programs(1) - 1)
    def _():
        o_ref[...] = x_ref[...] + g_ref[0] * acc[...]


def _ffn_prompt_call(x, modx, seq, nw, cprev, fw, layer):
    m = x.shape[0]
    tm = min(512, m)
    hb = tm // SUBLANE
    kern = functools.partial(_ffn_prompt_kernel, tm=tm, seq=seq)
    return pl.pallas_call(
        kern,
        grid=(m // tm, D_FFP // TF),
        in_specs=[pl.BlockSpec((tm, D_MODEL), lambda i, f: (i, 0)),
                  pl.BlockSpec((SUBLANE, D_MODEL), lambda i, f: (jnp.maximum(i * hb - 1, 0), 0)),
                  _mod_spec(modx, tm, seq, 4), _mod_spec(modx, tm, seq, 3), _mod_spec(modx, tm, seq, 5),
                  pl.BlockSpec((1, D_MODEL), lambda i, f: (0, 0)),
                  pl.BlockSpec((1, SUBLANE, TF), lambda i, f: ((i * tm) // seq, 0, f)),
                  pl.BlockSpec((1, D_MODEL, TF), lambda i, f: (layer, 0, f)),
                  pl.BlockSpec((1, D_MODEL, TF), lambda i, f: (layer, 0, D_FFP // TF + f)),
                  pl.BlockSpec((1, CONV_W, TF), lambda i, f: (layer, 0, f)),
                  pl.BlockSpec((1, 1, TF), lambda i, f: (layer, 0, f)),
                  pl.BlockSpec((1, TF, D_MODEL), lambda i, f: (layer, f, 0))],
        out_specs=[pl.BlockSpec((tm, D_MODEL), lambda i, f: (i, 0)),
                   pl.BlockSpec((1, SUBLANE, TF), lambda i, f: (i, 0, f))],
        out_shape=[jax.ShapeDtypeStruct((m, D_MODEL), F32),
                   jax.ShapeDtypeStruct((m // tm, SUBLANE, D_FFP), F32)],
        scratch_shapes=[pltpu.VMEM((tm, D_MODEL), BF16), pltpu.VMEM((SUBLANE, D_MODEL), BF16),
                        pltpu.VMEM((tm, D_MODEL), F32)],
        compiler_params=_cparams(("arbitrary", "arbitrary")),
        name="ffn_prompt",
    )(x, x, modx, modx, modx, nw, cprev, fw["wu"], fw["wu"], fw["cw"], fw["cb"], fw["wd"])


def _ffn_sample_kernel(x_ref, sc_ref, sh_ref, g_ref, nw_ref, p1_ref, p2_ref, wua_ref, wub_ref, cw_ref,
                       cb_ref, wd_ref, o_ref, ua_ref, h_scr, acc, *, dl):
    f = pl.program_id(0)

    @pl.when(f == 0)
    def _():
        h_scr[...] = (_rms(x_ref[...], nw_ref[...]) * (1.0 + sc_ref[0]) + sh_ref[0]).astype(BF16)
        acc[...] = jnp.zeros(acc.shape, F32)

    ua = _dot(h_scr[...], wua_ref[0])
    ub = _dot(h_scr[...], wub_ref[0])
    tin = lax.broadcasted_iota(I32, ua.shape, 0) % dl
    ua1 = jnp.where(tin >= 1, pltpu.roll(ua, 1, 0), p1_ref[...])
    ua2 = jnp.where(tin >= 2, pltpu.roll(ua, 2, 0), p2_ref[...])
    ua_ref[...] = ua
    acc[...] += _ffn_tail(ua, ua1, ua2, ub, cw_ref, cb_ref, wd_ref)

    @pl.when(f == pl.num_programs(0) - 1)
    def _():
        o_ref[...] = x_ref[...] + g_ref[0] * acc[...]


def _ffn_sample_call(x, modx, dl, nw, p1, p2, fw, layer):
    m = x.shape[0]
    kern = functools.partial(_ffn_sample_kernel, dl=dl)
    full = lambda i: pl.BlockSpec((m, D_MODEL), lambda f: (0, 0))
    return pl.pallas_call(
        kern,
        grid=(D_FFP // TF,),
        in_specs=[full(0),
                  pl.BlockSpec((1, m, D_MODEL), lambda f: (0, 0, 4)),
                  pl.BlockSpec((1, m, D_MODEL), lambda f: (0, 0, 3)),
                  pl.BlockSpec((1, m, D_MODEL), lambda f: (0, 0, 5)),
                  pl.BlockSpec((1, D_MODEL), lambda f: (0, 0)),
                  pl.BlockSpec((m, TF), lambda f: (0, f)), pl.BlockSpec((m, TF), lambda f: (0, f)),
                  pl.BlockSpec((1, D_MODEL, TF), lambda f: (layer, 0, f)),
                  pl.BlockSpec((1, D_MODEL, TF), lambda f: (layer, 0, D_FFP // TF + f)),
                  pl.BlockSpec((1, CONV_W, TF), lambda f: (layer, 0, f)),
                  pl.BlockSpec((1, 1, TF), lambda f: (layer, 0, f)),
                  pl.BlockSpec((1, TF, D_MODEL), lambda f: (layer, f, 0))],
        out_specs=[full(0), pl.BlockSpec((m, TF), lambda f: (0, f))],
        out_shape=[jax.ShapeDtypeStruct((m, D_MODEL), F32),
                   jax.ShapeDtypeStruct((m, D_FFP), F32)],
        scratch_shapes=[pltpu.VMEM((m, D_MODEL), BF16), pltpu.VMEM((m, D_MODEL), F32)],
        compiler_params=_cparams(("arbitrary",)),
        name="ffn_sample",
    )(x, modx, modx, modx, nw, p1, p2, fw["wu"], fw["wu"], fw["cw"], fw["cb"], fw["wd"])


def _rope_tables(pos):
    half = ROPE_DIM // 2
    inv = jnp.exp(jnp.arange(half, dtype=F32) * (-math.log(ROPE_THETA) / half))
    ang = pos.astype(F32)[:, None] * inv[None, :]
    cos, sin = jnp.cos(ang), jnp.sin(ang)
    n = pos.shape[0]
    rest = NSA_DH - ROPE_DIM
    cos_t = jnp.concatenate([cos, cos, jnp.ones((n, rest), F32)], axis=1)
    sin_a = jnp.concatenate([-sin, jnp.zeros((n, NSA_DH - half), F32)], axis=1)
    sin_b = jnp.concatenate([jnp.zeros((n, half), F32), sin, jnp.zeros((n, rest), F32)], axis=1)
    return cos_t, sin_a, sin_b


_W_IN_SIZES = (D_A, D_B, D_B, D_B, D_B, D_C, 6 * N_KV, 3 * NSA_HEADS, 3 * D_MODEL)
_W_IN_DST = (OFF_UA, OFF_HQ, OFF_HF, OFF_HI, OFF_HGT, OFF_NQ, OFF_NKV, OFF_NGT, OFF_MGT)
N_IN = sum(_W_IN_SIZES)


def _w_in_prep_kernel(x_ref, o_ref):
    src = 0
    for size, dst in zip(_W_IN_SIZES, _W_IN_DST):
        o_ref[0, :, dst:dst + size] = x_ref[0, :, src:src + size].astype(BF16)
        src += size
    tail = OFF_NGT + 3 * NSA_HEADS
    o_ref[0, :, tail:N_INP] = jnp.zeros((o_ref.shape[1], N_INP - tail), BF16)


def _permute_w_in(w_in):
    depth, d, _ = w_in.shape
    r = 256
    return pl.pallas_call(
        _w_in_prep_kernel,
        grid=(depth, d // r),
        in_specs=[pl.BlockSpec((1, r, N_IN), lambda l, i: (l, i, 0))],
        out_specs=pl.BlockSpec((1, r, N_INP), lambda l, i: (l, i, 0)),
        out_shape=jax.ShapeDtypeStruct((depth, d, N_INP), BF16),
        compiler_params=_cparams(("arbitrary", "arbitrary")),
        name="w_in_prep",
    )(w_in)


def _w_up_prep_kernel(x_ref, o_ref):
    o_ref[0, :, :D_FF] = x_ref[0].astype(BF16)
    o_ref[0, :, D_FF:] = jnp.zeros((o_ref.shape[1], D_FFP - D_FF), BF16)


def _pad_w_up(w_up):
    depth, d, _ = w_up.shape
    r = 512
    return pl.pallas_call(
        _w_up_prep_kernel,
        grid=(depth, d // r, 2),
        in_specs=[pl.BlockSpec((1, r, D_FF), lambda l, i, h: (l, i, h))],
        out_specs=pl.BlockSpec((1, r, D_FFP), lambda l, i, h: (l, i, h)),
        out_shape=jax.ShapeDtypeStruct((depth, d, 2 * D_FFP), BF16),
        compiler_params=_cparams(("arbitrary", "arbitrary", "arbitrary")),
        name="w_up_prep",
    )(w_up)


def _s5_params(a_re, a_im, log_dt, b_re, b_im, c_re, c_im, d, w_glu, b_glu):
    dt = jnp.exp(log_dt)[:, None]
    mag = jnp.exp(a_re * dt)
    ab_re, ab_im = mag * jnp.cos(a_im * dt), mag * jnp.sin(a_im * dt)
    den = a_re * a_re + a_im * a_im
    cf_re = ((ab_re - 1.0) * a_re + ab_im * a_im) / den
    cf_im = (ab_im * a_re - (ab_re - 1.0) * a_im) / den
    bb_re = cf_re[..., None] * b_re - cf_im[..., None] * b_im
    bb_im = cf_re[..., None] * b_im + cf_im[..., None] * b_re
    eye = jnp.eye(S5_G, dtype=F32)
    bdiag = lambda bb: jnp.einsum("gnc,gh->gchn", bb, eye).reshape(D_A, S5_W).astype(BF16)
    cdiag = lambda cc: jnp.einsum("gcn,gh->gnhc", cc, eye).reshape(S5_W, D_A).astype(BF16)
    pr, pi = ab_re.reshape(1, S5_W), ab_im.reshape(1, S5_W)
    prs, pis = [pr], [pi]
    for _ in range(SUBLANE - 1):
        nr = prs[-1] * pr - pis[-1] * pi
        ni = prs[-1] * pi + pis[-1] * pr
        prs.append(nr)
        pis.append(ni)
    return dict(bbr=bdiag(bb_re), bbi=bdiag(bb_im), ccr=cdiag(c_re), cci=cdiag(c_im),
                apr=jnp.concatenate(prs, axis=0), api=jnp.concatenate(pis, axis=0),
                d=d.reshape(1, D_A), wg=w_glu.astype(BF16), bg=b_glu.reshape(1, D_A))


def _hgrn_params(lb):
    pos = lb > 0
    lb_safe = jnp.where(pos, lb, 1.0)
    z = jnp.zeros_like(lb)
    return jnp.stack([jnp.log(lb_safe), jnp.log1p(-lb), 1.0 - lb, pos.astype(F32), z, z, z, z], axis=0)


def _cmp_params(cmp_a, w1, b1, w2):
    def lanes(a):
        return jnp.concatenate([a[0], a[0], a[1], a[1]], axis=-1)
    return dict(a1=lanes(cmp_a[:, :CMP_STRIDE]), a2=lanes(cmp_a[:, CMP_STRIDE:]),
                w1=w1.astype(BF16), b1=b1, w2=w2.astype(BF16))


def _ffn_params(w_up, conv_w, conv_b, w_down):
    padc = lambda a: jnp.pad(a, [(0, 0)] * (a.ndim - 1) + [(0, D_FFP - D_FF)])
    wu = _pad_w_up(w_up)
    wd = jnp.pad(w_down.astype(BF16), ((0, 0), (0, D_FFP - D_FF), (0, 0)))
    return dict(wu=wu, cw=padc(conv_w), cb=padc(conv_b)[:, None, :], wd=wd)


def kernel(x_prompt, x_sample, cache_cmp, cache_slc, state_win, state_s5, state_hgrn, state_conv,
           page_table, c_prompt, c_sample, w_mod, b_mod, norm1_w, norm2_w, w_in,
           s5_a_re, s5_a_im, s5_log_dt, s5_b_re, s5_b_im, s5_c_re, s5_c_im, s5_d, s5_w_glu, s5_b_glu,
           hg_lb_logits, hg_norm_w, nsa_q_norm, nsa_k_norm, cmp_a, cmp_w1, cmp_b1, cmp_w2,
           w_branch_a, w_branch_b, w_branch_c, w_out, w_up, conv_w, conv_b, w_down):
    bsz, seq, _ = x_prompt.shape
    dbsz, dl, _ = x_sample.shape
    depth = w_in.shape[0]
    n_pages = page_table.shape[1]
    past = n_pages * PAGE
    wb = state_win.shape[2]
    mp, ms = bsz * seq, dbsz * dl
    rows = NSA_REP * dl
    assert seq % 512 == 0 and seq // SEL_BLOCK <= LANE and n_pages % PG == 0 and dl <= SUBLANE

    probs = jax.nn.softmax(hg_lb_logits.astype(F32), axis=0)
    lower_bounds = jnp.cumsum(probs, axis=0) - probs[0:1]
    w_in_p = _permute_w_in(w_in)
    wa, wbr, wc, wo = (w.astype(BF16) for w in (w_branch_a, w_branch_b, w_branch_c, w_out))
    fw = _ffn_params(w_up, conv_w, conv_b, w_down)
    rope_p = _rope_tables(jnp.arange(seq, dtype=I32))
    rope_s = tuple(jnp.tile(t, (dbsz, 1)) for t in _rope_tables(past + jnp.arange(dl, dtype=I32)))
    pt_flat = page_table.reshape(-1).astype(I32)
    cache_cmp4 = cache_cmp.reshape(depth, -1, PAGE * KV_TILES, LANE)
    cache_slc4 = cache_slc.reshape(depth, -1, PAGE * KV_TILES, LANE)
    win_prev4 = state_win.reshape(depth, dbsz, wb * KV_TILES, LANE)
    emat = (jnp.arange(PG * PAGE)[None, :] // SEL_BLOCK == jnp.arange(LANE)[:, None]).astype(BF16)

    nr = -(-(bsz + dbsz) // 16) * 16
    c_all = jnp.concatenate([c_prompt, c_sample, jnp.zeros((nr - bsz - dbsz, D_MODEL), F32)], axis=0)
    mod = _mod_call(c_all, w_mod, b_mod)

    cmp_lanes = lambda a: jnp.concatenate([a[:, 0], a[:, 0], a[:, 1], a[:, 1]], axis=-1).reshape(
        depth, SUB_ROWS, LANE)
    first, second = _cmp_pool_cache_call(cache_cmp4, pt_flat, dbsz, n_pages,
                                         cmp_lanes(cmp_a[:, :, :CMP_STRIDE]),
                                         cmp_lanes(cmp_a[:, :, CMP_STRIDE:]))
    kc_past = _cmp_mlp_cache_call(first, second, cmp_w1.astype(BF16), cmp_b1, cmp_w2.astype(BF16))

    xp = x_prompt.reshape(mp, D_MODEL)
    xs = x_sample.reshape(ms, D_MODEL)
    outs = [[] for _ in range(12)]
    for l in range(depth):
        s5p = _s5_params(s5_a_re[l], s5_a_im[l], s5_log_dt[l], s5_b_re[l], s5_b_im[l], s5_c_re[l],
                         s5_c_im[l], s5_d[l], s5_w_glu[l], s5_b_glu[l])
        hgp = _hgrn_params(lower_bounds[l])
        hgn = hg_norm_w[l].reshape(1, HG_DV)
        cp = _cmp_params(cmp_a[l], cmp_w1[l], cmp_b1[l], cmp_w2[l])
        n1, n2 = norm1_w[l].reshape(1, D_MODEL), norm2_w[l].reshape(1, D_MODEL)
        qn, kn = nsa_q_norm[l].reshape(1, NSA_DH), nsa_k_norm[l]
        modp = mod[l, :bsz].reshape(bsz, 1, 6 * D_MODEL)
        mods = jnp.repeat(mod[l, bsz:bsz + dbsz], dl, axis=0).reshape(1, ms, 6 * D_MODEL)

        hin = _inproj_call(xp, modp, seq, n1, w_in_p, l)
        ya, s5_new = _s5_call(hin, OFF_UA // D_A, bsz, seq, jnp.zeros((bsz, 2, S5_W), F32), s5p,
                              SUBLANE - 1)
        yb, hg_new = _hgrn_call(hin, tuple(o // D_B for o in (OFF_HQ, OFF_HF, OFF_HI, OFF_HGT)), bsz, seq,
                                seq, jnp.zeros((bsz, HG_HEADS, HG_DK, HG_DV), F32), hgp, hgn)
        qo, rc, rs, rw, bc, bs, bw, gates = _nsa_prep_call(hin, rope_p, seq, qn, kn)
        kc = _cmp_prompt_call(rc, bsz, seq, cp)
        yc = _nsa_prompt_call(qo, kc, bs, bw, gates, bsz, seq)
        merged = _merge_call(ya, yb, yc, hin, wa, wbr, wc, l)
        xp = _outproj_call(merged, xp, modp, seq, wo, l)
        xp, tails = _ffn_prompt_call(xp, modp, seq, n2, jnp.zeros((bsz, SUBLANE, D_FFP), F32), fw, l)
        conv_new = tails.reshape(bsz, -1, SUBLANE, D_FFP)[:, -1, SUBLANE - (CONV_W - 1):, :D_FF]
        kv5 = lambda r: r.reshape(bsz, seq, 2, NSA_KVH, NSA_DH)
        for i, o in enumerate((s5_new.reshape(bsz, 2, S5_G, S5_N), hg_new, kv5(rc), kv5(rs),
                               kv5(rw)[:, -min(WINDOW, seq):], conv_new)):
            outs[2 * i].append(o)

        hin = _inproj_call(xs, mods, dl, n1, w_in_p, l)
        padt = lambda a, n: jnp.pad(a.reshape(dbsz, dl, -1), ((0, 0), (0, n - dl), (0, 0))).reshape(dbsz * n, -1)
        u_pad = padt(hin[:, OFF_UA:OFF_UA + D_A], SUBLANE)
        ya, s5_new = _s5_call(u_pad, 0, dbsz, SUBLANE, state_s5[l].reshape(dbsz, 2, S5_W), s5p, dl - 1)
        ya = ya.reshape(dbsz, SUBLANE, D_A)[:, :dl].reshape(ms, D_A)
        hg_pad = padt(hin[:, OFF_HQ:OFF_HQ + 4 * D_B], HG_C)
        yb, hg_new = _hgrn_call(hg_pad, (0, 1, 2, 3), dbsz, HG_C, dl, state_hgrn[l], hgp, hgn)
        yb = yb.reshape(dbsz, HG_C, D_B)[:, :dl].reshape(ms, D_B)
        qo, rc, rs, rw, bc, bs, bw, gates = _nsa_prep_call(hin, rope_s, dl, qn, kn)
        q16 = qo.reshape(dbsz, dl, NSA_KVH, NSA_REP, NSA_DH).transpose(0, 2, 1, 3, 4).reshape(
            dbsz, NSA_KVH, rows, NSA_DH)
        o_c, selt = _nsa_s_sel_call(q16, kc_past, l, past, dl)
        g16 = gates[:, :3 * NSA_HEADS].reshape(dbsz, dl, NSA_KVH, NSA_REP, 3).transpose(0, 2, 1, 3, 4)
        g16 = jnp.pad(g16.reshape(dbsz, NSA_KVH, rows, 3), ((0, 0), (0, 0), (0, 0), (0, LANE - 3)))
        pad8 = lambda r: jnp.pad(r.reshape(dbsz, dl, 2 * N_KV), ((0, 0), (0, SUBLANE - dl), (0, 0)))
        yc = _nsa_s_attn_call(cache_slc4, pt_flat, l, q16, selt, emat, o_c, pad8(rs), pad8(rw), win_prev4,
                              g16, past, dl)
        yc = yc.reshape(dbsz, NSA_KVH, dl, NSA_REP, NSA_DH).transpose(0, 2, 1, 3, 4).reshape(ms, D_C)
        merged = _merge_call(ya, yb, yc, hin, wa, wbr, wc, l)
        xs = _outproj_call(merged, xs, mods, dl, wo, l)
        st = jnp.pad(state_conv[l], ((0, 0), (0, 0), (0, D_FFP - D_FF)))
        tin = jnp.arange(dl)[None, :, None]
        p1 = jnp.where(tin == 0, st[:, 1:2], 0.0).reshape(ms, D_FFP)
        p2 = jnp.where(tin == 0, st[:, 0:1], jnp.where(tin == 1, st[:, 1:2], 0.0)).reshape(ms, D_FFP)
        xs, ua = _ffn_sample_call(xs, mods, dl, n2, p1, p2, fw, l)
        ext = jnp.concatenate([state_conv[l], ua[:, :D_FF].reshape(dbsz, dl, D_FF)], axis=1)
        kv5 = lambda r: r.reshape(dbsz, dl, 2, NSA_KVH, NSA_DH)
        win_new = jnp.concatenate([state_win[l], kv5(rw)], axis=1)[:, -wb:]
        for i, o in enumerate((s5_new.reshape(dbsz, 2, S5_G, S5_N), hg_new, kv5(rc), kv5(rs), win_new,
                               ext[:, dl:])):
            outs[2 * i + 1].append(o)

    st = [jnp.stack(o) for o in outs]
    return (xp.reshape(bsz, seq, D_MODEL), xs.reshape(dbsz, dl, D_MODEL), *st)
```

```python
import functools
import math

import jax
import jax.numpy as jnp
from jax import lax
from jax.experimental import pallas as pl
from jax.experimental.pallas import tpu as pltpu

F32 = jnp.float32
BF16 = jnp.bfloat16
I32 = jnp.int32

D_MODEL = 2048
D_A = D_MODEL // 4
S5_GROUP = 16
S5_G = D_A // S5_GROUP
S5_N = 64
S5_W = S5_G * S5_N
D_B = D_MODEL // 4
HG_DK = 128
HG_DV = 128
HG_HEADS = D_B // HG_DV
D_C = D_MODEL // 2
NSA_DH = 128
NSA_HEADS = D_C // NSA_DH
NSA_KVH = 2
NSA_REP = NSA_HEADS // NSA_KVH
N_KV = NSA_KVH * NSA_DH
ROPE_DIM = NSA_DH // 4
ROPE_THETA = 500000.0
CMP_STRIDE = 16
SEL_BLOCK = 64
N_SEL = 16
WINDOW = 512
FORCE_BONUS = 1000.0
D_FF = ((8 * D_MODEL) // 3 + 127) // 128 * 128
CONV_W = 3
EPS = 1e-6
NEG_INF = -1e30
PAGE = 128

LANE = 128
SUBLANE = 8
VMEM_LIMIT = 56 * 1024 * 1024

OFF_MGT = 0
OFF_NQ = 3 * D_MODEL
OFF_NKV = OFF_NQ + D_C
OFF_UA = OFF_NKV + 6 * N_KV
OFF_HQ = OFF_UA + D_A
OFF_HF = OFF_HQ + D_B
OFF_HI = OFF_HF + D_B
OFF_HGT = OFF_HI + D_B
OFF_NGT = OFF_HGT + D_B
NGT_PAD = 256
N_INP = OFF_NGT + NGT_PAD
TN_IN = 1280
D_FFP = 5632
TF = 512
HG_C = 64
HG_SB = 16


def _cparams(sem):
    return pltpu.CompilerParams(dimension_semantics=sem, vmem_limit_bytes=VMEM_LIMIT)


def _dot(a, b):
    return jnp.dot(a, b, preferred_element_type=F32)


def _dot_nt(a, b):
    return lax.dot_general(a, b, (((1,), (1,)), ((), ())), preferred_element_type=F32)


def _dot_tn(a, b):
    return lax.dot_general(a, b, (((0,), (0,)), ((), ())), preferred_element_type=F32)


def _dot2(a, b):
    hi = a.astype(BF16)
    lo = (a - hi.astype(F32)).astype(BF16)
    return _dot(hi, b) + _dot(lo, b)


def _dot3(a, b):
    hi = a.astype(BF16)
    r1 = a - hi.astype(F32)
    mid = r1.astype(BF16)
    lo = (r1 - mid.astype(F32)).astype(BF16)
    return _dot(b, hi) + _dot(b, mid) + _dot(b, lo)


def _rms(x, w):
    return x * lax.rsqrt(jnp.mean(x * x, axis=-1, keepdims=True) + EPS) * w


def _masked_softmax(s, mask):
    s = jnp.where(mask, s, NEG_INF)
    p = jnp.where(mask, jnp.exp(s - jnp.max(s, axis=-1, keepdims=True)), 0.0)
    return p / jnp.maximum(jnp.sum(p, axis=-1, keepdims=True), 1e-30)


def _row_reduce(x, combine, reduce):
    n = x.shape[-1]
    if n % LANE or n == LANE:
        return reduce(x, axis=-1, keepdims=True)
    acc = x[:, :LANE]
    for k in range(1, n // LANE):
        acc = combine(acc, x[:, k * LANE:(k + 1) * LANE])
    return reduce(acc, axis=-1, keepdims=True)


def _log_sigmoid(x):
    return jnp.minimum(x, 0.0) - jnp.log1p(jnp.exp(-jnp.abs(x)))


def _logaddexp(a, b):
    return jnp.maximum(a, b) + jnp.log1p(jnp.exp(-jnp.abs(a - b)))


def _mod_kernel(c_ref, w_ref, b_ref, o_ref):
    c = c_ref[...]
    a = (c * jax.nn.sigmoid(c)).astype(BF16)
    o_ref[0] = _dot(a, w_ref[0].astype(BF16)) + b_ref[0]


def _mod_call(c_all, w_mod, b_mod):
    depth, d, n = w_mod.shape
    r = c_all.shape[0]
    tn = 1024
    return pl.pallas_call(
        _mod_kernel,
        grid=(depth, n // tn),
        in_specs=[pl.BlockSpec((r, d), lambda l, j: (0, 0)),
                  pl.BlockSpec((1, d, tn), lambda l, j: (l, 0, j)),
                  pl.BlockSpec((1, 1, tn), lambda l, j: (l, 0, j))],
        out_specs=pl.BlockSpec((1, r, tn), lambda l, j: (l, 0, j)),
        out_shape=jax.ShapeDtypeStruct((depth, r, n), F32),
        compiler_params=_cparams(("arbitrary", "arbitrary")),
        name="mod",
    )(c_all, w_mod, b_mod.reshape(depth, 1, n))


def _inproj_kernel(x_ref, sc_ref, sh_ref, nw_ref, w_ref, o_ref, h_scr):
    @pl.when(pl.program_id(1) == 0)
    def _():
        y = _rms(x_ref[...], nw_ref[...])
        h_scr[...] = (y * (1.0 + sc_ref[0]) + sh_ref[0]).astype(BF16)

    o_ref[...] = _dot_nt(h_scr[...], w_ref[0])


def _mod_spec(modx, tm, rows_per_group, col):
    if modx.shape[1] == 1:
        return pl.BlockSpec((1, 1, D_MODEL), lambda i, *_: ((i * tm) // rows_per_group, 0, col))
    return pl.BlockSpec((1, tm, D_MODEL), lambda i, *_: (0, i, col))


def _inproj_call(x, modx, rows_per_group, nw, w, layer):
    m = x.shape[0]
    tm = min(512, m)
    return pl.pallas_call(
        _inproj_kernel,
        grid=(m // tm, N_INP // TN_IN),
        in_specs=[pl.BlockSpec((tm, D_MODEL), lambda i, j: (i, 0)),
                  _mod_spec(modx, tm, rows_per_group, 1),
                  _mod_spec(modx, tm, rows_per_group, 0),
                  pl.BlockSpec((1, D_MODEL), lambda i, j: (0, 0)),
                  pl.BlockSpec((1, TN_IN, D_MODEL), lambda i, j: (layer, j, 0))],
        out_specs=pl.BlockSpec((tm, TN_IN), lambda i, j: (i, j)),
        out_shape=jax.ShapeDtypeStruct((m, N_INP), F32),
        scratch_shapes=[pltpu.VMEM((tm, D_MODEL), BF16)],
        compiler_params=_cparams(("arbitrary", "arbitrary")),
        name="inproj",
    )(x, modx, modx, nw, w)


S5_LC = 512


def _s5_kernel(u_ref, h0_ref, bbr_ref, bbi_ref, ccr_ref, cci_ref, apr_ref, api_ref, d_ref,
               wg_ref, bg_ref, y_ref, hl_ref, hre, him, car, *, t_rows, last_row):
    @pl.when(pl.program_id(1) == 0)
    def _():
        car[...] = h0_ref[0]

    u = u_ref[...]
    ub = u.astype(BF16)
    hre[...] = _dot(ub, bbr_ref[...])
    him[...] = _dot(ub, bbi_ref[...])
    row = lax.broadcasted_iota(I32, (SUBLANE, S5_LC), 0)
    for c in range(S5_W // S5_LC):
        sl = slice(c * S5_LC, (c + 1) * S5_LC)
        pwr = apr_ref[:, sl]
        pwi = api_ref[:, sl]
        steps = tuple((s, pwr[s - 1:s], pwi[s - 1:s]) for s in (1, 2, 4))

        def body(i, carry, sl=sl, pwr=pwr, pwi=pwi, steps=steps):
            cr, ci = carry
            r0 = pl.multiple_of(i * SUBLANE, SUBLANE)
            xr = hre[pl.ds(r0, SUBLANE), sl]
            xi = him[pl.ds(r0, SUBLANE), sl]
            for s, ar, ai in steps:
                sr = jnp.where(row >= s, pltpu.roll(xr, s, 0), 0.0)
                si = jnp.where(row >= s, pltpu.roll(xi, s, 0), 0.0)
                xr, xi = xr + (ar * sr - ai * si), xi + (ar * si + ai * sr)
            xr, xi = xr + (pwr * cr - pwi * ci), xi + (pwr * ci + pwi * cr)
            hre[pl.ds(r0, SUBLANE), sl] = xr
            him[pl.ds(r0, SUBLANE), sl] = xi
            return xr[SUBLANE - 1:SUBLANE], xi[SUBLANE - 1:SUBLANE]

        cr, ci = lax.fori_loop(0, t_rows // SUBLANE, body, (car[0:1, sl], car[1:2, sl]))
        car[0:1, sl] = cr
        car[1:2, sl] = ci

    lr = t_rows - SUBLANE + last_row
    hl_ref[0, 0:1, :] = hre[lr:lr + 1, :]
    hl_ref[0, 1:2, :] = him[lr:lr + 1, :]
    y = (_dot(hre[...].astype(BF16), ccr_ref[...]) - _dot(him[...].astype(BF16), cci_ref[...])
         + d_ref[...] * u)
    y = jax.nn.gelu(y)
    z = _dot(y.astype(BF16), wg_ref[...]) + bg_ref[...]
    y_ref[...] = (y * jax.nn.sigmoid(z)).astype(BF16)


def _s5_call(u_arr, ucol, nb, seq, h0, p, last_row):
    t_rows = min(256, seq)
    nt = seq // t_rows
    const = lambda shape: pl.BlockSpec(shape, lambda b, t: (0,) * len(shape))
    kern = functools.partial(_s5_kernel, t_rows=t_rows, last_row=last_row)
    return pl.pallas_call(
        kern,
        grid=(nb, nt),
        in_specs=[pl.BlockSpec((t_rows, D_A), lambda b, t: (b * nt + t, ucol)),
                  pl.BlockSpec((1, 2, S5_W), lambda b, t: (b, 0, 0)),
                  const((D_A, S5_W)), const((D_A, S5_W)), const((S5_W, D_A)), const((S5_W, D_A)),
                  const((SUBLANE, S5_W)), const((SUBLANE, S5_W)), const((1, D_A)),
                  const((D_A, D_A)), const((1, D_A))],
        out_specs=[pl.BlockSpec((t_rows, D_A), lambda b, t: (b * nt + t, 0)),
                   pl.BlockSpec((1, 2, S5_W), lambda b, t: (b, 0, 0))],
        out_shape=[jax.ShapeDtypeStruct((nb * seq, D_A), BF16),
                   jax.ShapeDtypeStruct((nb, 2, S5_W), F32)],
        scratch_shapes=[pltpu.VMEM((t_rows, S5_W), F32), pltpu.VMEM((t_rows, S5_W), F32),
                        pltpu.VMEM((2, S5_W), F32)],
        compiler_params=_cparams(("arbitrary", "arbitrary")),
        name="s5",
    )(u_arr, h0, p["bbr"], p["bbi"], p["ccr"], p["cci"], p["apr"], p["api"], p["d"],
      p["wg"], p["bg"])


def _hgrn_kernel(q_ref, f_ref, i_ref, g_ref, s0_ref, lp_ref, nw_ref, y_ref, sl_ref, st, pscr, rscr,
                 *, l_valid):
    t = pl.program_id(1)
    c = HG_C
    nsb = c // HG_SB
    nh = HG_HEADS

    @pl.when(t == 0)
    def _():
        for h in range(nh):
            st[h] = s0_ref[0, h].T

    row = lax.broadcasted_iota(I32, (c, D_B), 0)
    valid = (t * c + row) < l_valid
    tri = (lax.broadcasted_iota(I32, (c, c), 0) >= lax.broadcasted_iota(I32, (c, c), 1)).astype(BF16)
    hf = f_ref[...]
    ls = _log_sigmoid(hf)
    lf = jnp.where(lp_ref[3:4, :] > 0.5, _logaddexp(lp_ref[0:1, :], lp_ref[1:2, :] + ls), ls)
    kk = lp_ref[2:3, :] * jax.nn.sigmoid(-hf)
    lf = jnp.where(valid, lf, 0.0)
    kk = jnp.where(valid, kk, 0.0)
    hq = q_ref[...]
    q = hq * jax.nn.sigmoid(hq)
    v = i_ref[...]
    b = _dot3(lf, tri)
    zero_row = jnp.zeros((1, D_B), F32)
    bref = [zero_row] + [b[i * HG_SB - 1:i * HG_SB] for i in range(1, nsb)]
    bend = [b[(j + 1) * HG_SB - 1:(j + 1) * HG_SB] for j in range(nsb)]
    rows16 = lambda r: jnp.broadcast_to(r, (HG_SB, D_B))
    qq = q * jnp.exp(b - jnp.concatenate([rows16(r) for r in bref], axis=0))
    kks = kk * jnp.exp(jnp.concatenate([rows16(r) for r in bend], axis=0) - b)
    lhs = []
    for j in range(nsb - 1):
        dsel = jnp.concatenate(
            [rows16(jnp.exp(bref[i] - bend[j])) if i > j else jnp.zeros((HG_SB, D_B), F32)
             for i in range(nsb)], axis=0)
        lhs.append((qq * dsel).astype(BF16))
    rhs = [jnp.where(row // HG_SB == j, kks, 0.0).astype(BF16) for j in range(nsb - 1)]
    srow = lax.broadcasted_iota(I32, (HG_SB, D_B), 0)
    for i in range(nsb):
        rs = slice(i * HG_SB, (i + 1) * HG_SB)
        bi, qi, ki = b[rs], q[rs], kk[rs]
        for s in range(HG_SB):
            m = srow >= s
            e = jnp.exp(jnp.where(m, bi - bi[s:s + 1], 0.0))
            piece = jnp.where(m, (qi * ki[s:s + 1]) * e, 0.0).astype(BF16)
            for h in range(nh):
                r0 = ((h * nsb + i) * HG_SB + s) * HG_SB
                pscr[r0:r0 + HG_SB, :] = piece[:, h * HG_DK:(h + 1) * HG_DK]
    rscr[...] = _dot(pscr[...], jnp.ones((HG_DK, HG_DK), BF16))
    qe = (q * jnp.exp(b)).astype(BF16)
    vb = v.astype(BF16)
    bl = b[c - 1:c]
    kdec = (kk * jnp.exp(bl - b)).astype(BF16)
    ebl = jnp.exp(bl)
    gt = g_ref[...]
    gate = gt * jax.nn.sigmoid(gt)
    for h in range(nh):
        hs = slice(h * HG_DK, (h + 1) * HG_DK)
        s_t = st[h]
        att = _dot_nt(jnp.concatenate([x[:, hs] for x in lhs], axis=1),
                      jnp.concatenate([x[:, hs] for x in rhs], axis=1))
        o = _dot(att.astype(BF16), vb[:, hs]) + _dot_nt(qe[:, hs], s_t.astype(BF16))
        diag = []
        for i in range(nsb):
            od = jnp.zeros((HG_SB, HG_DV), F32)
            for s in range(HG_SB):
                r0 = ((h * nsb + i) * HG_SB + s) * HG_SB
                od = od + rscr[r0:r0 + HG_SB, :] * v[i * HG_SB + s:i * HG_SB + s + 1, hs]
            diag.append(od)
        o = o + jnp.concatenate(diag, axis=0)
        st[h] = s_t * ebl[:, hs] + _dot_tn(vb[:, hs], kdec[:, hs])
        y_ref[:, hs] = (_rms(o, nw_ref[...]) * gate[:, hs]).astype(BF16)

    @pl.when(t == pl.num_programs(1) - 1)
    def _():
        for h in range(nh):
            sl_ref[0, h] = st[h].T


def _dot2g(g, z):
    hi = z.astype(BF16)
    lo = (z - hi.astype(F32)).astype(BF16)
    return _dot(g, hi) + _dot(g, lo)


def _hgrn_call(arr, cols, nb, seq_pad, l_valid, s0, lp, nw):
    nt = seq_pad // HG_C
    blk = lambda col: pl.BlockSpec((HG_C, D_B), lambda b, t: (b * nt + t, col))
    kern = functools.partial(_hgrn_kernel, l_valid=l_valid)
    return pl.pallas_call(
        kern,
        grid=(nb, nt),
        in_specs=[blk(cols[0]), blk(cols[1]), blk(cols[2]), blk(cols[3]),
                  pl.BlockSpec((1, HG_HEADS, HG_DK, HG_DV), lambda b, t: (b, 0, 0, 0)),
                  pl.BlockSpec((SUBLANE, D_B), lambda b, t: (0, 0)),
                  pl.BlockSpec((1, HG_DV), lambda b, t: (0, 0))],
        out_specs=[pl.BlockSpec((HG_C, D_B), lambda b, t: (b * nt + t, 0)),
                   pl.BlockSpec((1, HG_HEADS, HG_DK, HG_DV), lambda b, t: (b, 0, 0, 0))],
        out_shape=[jax.ShapeDtypeStruct((nb * seq_pad, D_B), BF16),
                   jax.ShapeDtypeStruct((nb, HG_HEADS, HG_DK, HG_DV), F32)],
        scratch_shapes=[pltpu.VMEM((HG_HEADS, HG_DV, HG_DK), F32),
                        pltpu.VMEM((HG_HEADS * HG_C * HG_SB, HG_DK), BF16),
                        pltpu.VMEM((HG_HEADS * HG_C * HG_SB, HG_DK), F32)],
        compiler_params=_cparams(("arbitrary", "arbitrary")),
        name="hgrn",
    )(arr, arr, arr, arr, s0, lp, nw)


def _rope(x, cos_t, sin_a, sin_b):
    return (x * cos_t + pltpu.roll(x, NSA_DH - ROPE_DIM // 2, 1) * sin_a
            + pltpu.roll(x, ROPE_DIM // 2, 1) * sin_b)


def _nsa_prep_kernel(q_ref, kc_ref, ks_ref, kw_ref, g_ref, cos_ref, sa_ref, sb_ref, qn_ref, kn_ref,
                     qo_ref, rc_ref, rs_ref, rw_ref, bc_ref, bs_ref, bw_ref, go_ref):
    cos_t, sin_a, sin_b = cos_ref[...], sa_ref[...], sb_ref[...]
    qn = qn_ref[...]
    for h in range(NSA_HEADS):
        hs = slice(h * NSA_DH, (h + 1) * NSA_DH)
        qo_ref[:, hs] = _rope(_rms(q_ref[:, hs], qn), cos_t, sin_a, sin_b).astype(BF16)
    for br, (src, dst, dstb) in enumerate(((kc_ref, rc_ref, bc_ref), (ks_ref, rs_ref, bs_ref),
                                           (kw_ref, rw_ref, bw_ref))):
        kn = kn_ref[br:br + 1, :]
        tm = src.shape[0]
        for g in range(NSA_KVH):
            gs = slice(g * NSA_DH, (g + 1) * NSA_DH)
            k = _rope(_rms(src[:, gs], kn), cos_t, sin_a, sin_b)
            dst[pl.ds(g, tm, stride=KV_TILES), :] = k
            dstb[:, gs] = k.astype(BF16)
            vs = slice(N_KV + g * NSA_DH, N_KV + (g + 1) * NSA_DH)
            v = src[:, vs]
            dst[pl.ds(NSA_KVH + g, tm, stride=KV_TILES), :] = v
            dstb[:, vs] = v.astype(BF16)
    go_ref[...] = jax.nn.sigmoid(g_ref[...])


def _nsa_prep_call(hin, rope_tabs, seq, qn, kn):
    m = hin.shape[0]
    tm = min(256, m)
    npos = rope_tabs[0].shape[0] // tm
    rspec = pl.BlockSpec((tm, NSA_DH), lambda i: (i % npos, 0))
    kvb = OFF_NKV // (2 * N_KV)
    return pl.pallas_call(
        _nsa_prep_kernel,
        grid=(m // tm,),
        in_specs=[pl.BlockSpec((tm, D_C), lambda i: (i, OFF_NQ // D_C)),
                  pl.BlockSpec((tm, 2 * N_KV), lambda i: (i, kvb)),
                  pl.BlockSpec((tm, 2 * N_KV), lambda i: (i, kvb + 1)),
                  pl.BlockSpec((tm, 2 * N_KV), lambda i: (i, kvb + 2)),
                  pl.BlockSpec((tm, LANE), lambda i: (i, OFF_NGT // LANE)),
                  rspec, rspec, rspec,
                  pl.BlockSpec((1, NSA_DH), lambda i: (0, 0)),
                  pl.BlockSpec((3, NSA_DH), lambda i: (0, 0))],
        out_specs=[pl.BlockSpec((tm, D_C), lambda i: (i, 0))]
        + [pl.BlockSpec((tm * KV_TILES, LANE), lambda i: (i, 0))] * 3
        + [pl.BlockSpec((tm, 2 * N_KV), lambda i: (i, 0))] * 3
        + [pl.BlockSpec((tm, LANE), lambda i: (i, 0))],
        out_shape=[jax.ShapeDtypeStruct((m, D_C), BF16)]
        + [jax.ShapeDtypeStruct((m * KV_TILES, LANE), F32)] * 3
        + [jax.ShapeDtypeStruct((m, 2 * N_KV), BF16)] * 3
        + [jax.ShapeDtypeStruct((m, LANE), F32)],
        compiler_params=_cparams(("arbitrary",)),
        name="nsa_prep",
    )(hin, hin, hin, hin, hin, *rope_tabs, qn, kn)


KV_TILES = 2 * N_KV // LANE


def _pool_rows(load, n, a1_ref, a2_ref):
    firsts, seconds = [], []
    for c in range(KV_TILES):
        ls = slice(c * LANE, (c + 1) * LANE)
        first = jnp.zeros((n, LANE), F32)
        second = jnp.zeros((n, LANE), F32)
        for j in range(CMP_STRIDE):
            x = load(j, c)
            first = first + x * a1_ref[j:j + 1, ls]
            second = second + x * a2_ref[j:j + 1, ls]
        firsts.append(first)
        seconds.append(second)
    return firsts, seconds


def _tile_rows(j, c, n):
    return pl.ds(j * KV_TILES + c, n, stride=CMP_STRIDE * KV_TILES)


def _cmp_mlp(first, second, n, w1_ref, b1_ref, w2_ref, o_ref):
    for c in range(2):
        for g in range(NSA_KVH):
            tile = 2 * c + g
            pooled = first[tile] + pltpu.roll(second[tile], n - 1, 0)
            h = _dot(pooled.astype(BF16), w1_ref[c]) + b1_ref[c:c + 1, :]
            o_ref[0, :, tile * NSA_DH:(tile + 1) * NSA_DH] = _dot(
                jax.nn.gelu(h).astype(BF16), w2_ref[c]).astype(BF16)


def _cmp_prompt_kernel(x_ref, a1_ref, a2_ref, w1_ref, b1_ref, w2_ref, o_ref, *, n):
    first, second = _pool_rows(lambda j, c: x_ref[_tile_rows(j, c, n), :], n, a1_ref, a2_ref)
    _cmp_mlp(first, second, n, w1_ref, b1_ref, w2_ref, o_ref)


def _cmp_prompt_call(rows, nb, seq, cp):
    n = seq // CMP_STRIDE
    const = lambda shape: pl.BlockSpec(shape, lambda b: (0,) * len(shape))
    return pl.pallas_call(
        functools.partial(_cmp_prompt_kernel, n=n),
        grid=(nb,),
        in_specs=[pl.BlockSpec((seq * KV_TILES, LANE), lambda b: (b, 0)),
                  const((CMP_STRIDE, 2 * N_KV)), const((CMP_STRIDE, 2 * N_KV)),
                  const((2, NSA_DH, NSA_DH)), const((2, NSA_DH)), const((2, NSA_DH, NSA_DH))],
        out_specs=pl.BlockSpec((1, n, 2 * N_KV), lambda b: (b, 0, 0)),
        out_shape=jax.ShapeDtypeStruct((nb, n, 2 * N_KV), BF16),
        compiler_params=_cparams(("arbitrary",)),
        name="cmp_prompt",
    )(rows.reshape(-1, LANE), cp["a1"], cp["a2"], cp["w1"], cp["b1"], cp["w2"])


PG = 16
SUB_PER_PAGE = PAGE // CMP_STRIDE


SUB_ROWS = CMP_STRIDE * KV_TILES


def _cmp_pool_cache_kernel(pt_ref, *refs):
    pages = refs[:PG]
    a1_ref, a2_ref, f_ref, s_ref = refs[PG:]
    nv = SUB_ROWS // SUBLANE
    w1 = [a1_ref[0, v * SUBLANE:(v + 1) * SUBLANE, :] for v in range(nv)]
    w2 = [a2_ref[0, v * SUBLANE:(v + 1) * SUBLANE, :] for v in range(nv)]
    low = lax.broadcasted_iota(I32, (SUBLANE, LANE), 0) < KV_TILES

    def fold(acc):
        return acc + pltpu.roll(acc, KV_TILES, 0)

    for k in range(PG):
        for n2 in range(SUB_PER_PAGE // 2):
            halves = []
            for n in (2 * n2, 2 * n2 + 1):
                a1 = a2 = None
                for v in range(nv):
                    x = pages[k][0, 0, n * SUB_ROWS + v * SUBLANE:n * SUB_ROWS + (v + 1) * SUBLANE, :]
                    a1 = x * w1[v] if a1 is None else a1 + x * w1[v]
                    a2 = x * w2[v] if a2 is None else a2 + x * w2[v]
                halves.append((fold(a1), fold(a2)))
            r0 = (k * SUB_PER_PAGE + 2 * n2) * KV_TILES
            f_ref[0, 0, r0:r0 + SUBLANE, :] = jnp.where(low, halves[0][0], halves[1][0])
            s_ref[0, 0, r0:r0 + SUBLANE, :] = jnp.where(low, halves[0][1], halves[1][1])


def _page_specs(n_pages, layer_of, batch_of, group_of, page_shape=(PAGE, 2 * N_KV)):
    def spec(k):
        return pl.BlockSpec(
            (1, 1) + page_shape,
            lambda *a, k=k: (layer_of(*a), a[-1][batch_of(*a) * n_pages + group_of(*a) * PG + k], 0, 0))
    return [spec(k) for k in range(PG)]


def _cmp_pool_cache_call(cache, pt_flat, nb, n_pages, a1, a2):
    depth = cache.shape[0]
    npg = n_pages // PG
    nsub = n_pages * SUB_PER_PAGE
    rows = PG * SUB_PER_PAGE
    grid_spec = pltpu.PrefetchScalarGridSpec(
        num_scalar_prefetch=1,
        grid=(depth, nb, npg),
        in_specs=_page_specs(n_pages, lambda l, b, g, pt: l, lambda l, b, g, pt: b,
                             lambda l, b, g, pt: g, (PAGE * KV_TILES, LANE))
        + [pl.BlockSpec((1, SUB_ROWS, LANE), lambda l, b, g, pt: (l, 0, 0))] * 2,
        out_specs=[pl.BlockSpec((1, 1, rows * KV_TILES, LANE), lambda l, b, g, pt: (l, b, g, 0))] * 2,
    )
    return pl.pallas_call(
        _cmp_pool_cache_kernel,
        grid_spec=grid_spec,
        out_shape=[jax.ShapeDtypeStruct((depth, nb, nsub * KV_TILES, LANE), F32)] * 2,
        compiler_params=_cparams(("arbitrary", "arbitrary", "arbitrary")),
        name="cmp_pool_cache",
    )(pt_flat, *([cache] * PG), a1, a2)


def _cmp_mlp_cache_kernel(f_ref, s_ref, w1_ref, b1_ref, w2_ref, o_ref, *, n):
    tiles = lambda ref: [ref[0, 0, pl.ds(c, n, stride=KV_TILES), :] for c in range(KV_TILES)]
    _cmp_mlp(tiles(f_ref), tiles(s_ref), n, w1_ref.at[0], b1_ref.at[0], w2_ref.at[0], o_ref.at[0])


def _cmp_mlp_cache_call(first, second, w1, b1, w2):
    depth, nb, rows, _ = first.shape
    n = rows // KV_TILES
    blk = pl.BlockSpec((1, 1, n, 2 * N_KV), lambda l, b: (l, b, 0, 0))
    tblk = pl.BlockSpec((1, 1, rows, LANE), lambda l, b: (l, b, 0, 0))
    return pl.pallas_call(
        functools.partial(_cmp_mlp_cache_kernel, n=n),
        grid=(depth, nb),
        in_specs=[tblk, tblk,
                  pl.BlockSpec((1, 2, NSA_DH, NSA_DH), lambda l, b: (l, 0, 0, 0)),
                  pl.BlockSpec((1, 2, NSA_DH), lambda l, b: (l, 0, 0)),
                  pl.BlockSpec((1, 2, NSA_DH, NSA_DH), lambda l, b: (l, 0, 0, 0))],
        out_specs=blk,
        out_shape=jax.ShapeDtypeStruct((depth, nb, n, 2 * N_KV), BF16),
        compiler_params=_cparams(("arbitrary", "arbitrary")),
        name="cmp_mlp_cache",
    )(first, second, w1, b1, w2)


def _block_scores(imp, qpos, n_slc):
    blk = lax.broadcasted_iota(I32, imp.shape, 1)
    cur = qpos // SEL_BLOCK
    forced = (blk == 0) | (blk == cur) | (blk == cur - 1)
    valid = blk * SEL_BLOCK <= qpos
    score = jnp.where(valid, imp + jnp.where(forced, FORCE_BONUS, 0.0), -1.0)
    return jnp.where(blk < n_slc, score, -2.0)


def _select_blocks_ranked(imp, qpos, n_slc):
    nb8 = (n_slc + SUBLANE - 1) // SUBLANE * SUBLANE
    st = _block_scores(imp, qpos, n_slc).T[:nb8]
    ridx = lax.broadcasted_iota(I32, st.shape, 0)
    cnt = jnp.zeros(st.shape, F32)
    for i in range(n_slc):
        ri = st[i:i + 1]
        cnt = cnt + jnp.where(ri > st, 1.0, jnp.where(ri == st, jnp.where(ridx > i, 1.0, 0.0), 0.0))
    sel_t = jnp.where((cnt < min(N_SEL, n_slc)) & (ridx < n_slc), 1.0, 0.0)
    if nb8 < LANE:
        sel_t = jnp.concatenate([sel_t, jnp.zeros((LANE - nb8, st.shape[1]), F32)], axis=0)
    return sel_t.T


def _select_blocks(imp, qpos, n_slc):
    blk = lax.broadcasted_iota(I32, imp.shape, 1)
    score = _block_scores(imp, qpos, n_slc)
    sel = jnp.zeros(imp.shape, F32)
    for _ in range(min(N_SEL, n_slc)):
        mx = jnp.max(score, axis=-1, keepdims=True)
        idx = jnp.min(jnp.where(score == mx, blk, 1 << 30), axis=-1, keepdims=True)
        hit = blk == idx
        sel = jnp.where(hit, 1.0, sel)
        score = jnp.where(hit, -3.0, score)
    return sel


def _cmp_branch(q, kck, kcv, qpos, n_cmp):
    s = _dot_nt(q, kck) * (NSA_DH ** -0.5)
    col = lax.broadcasted_iota(I32, s.shape, 1)
    mask = (col * CMP_STRIDE + (2 * CMP_STRIDE - 1) <= qpos) & (col < n_cmp)
    p = _masked_softmax(s, mask)
    return p, _dot(p.astype(BF16), kcv)


NSA_KC = 1024


def _nsa_prompt_kernel(q_ref, kc_ref, ks_ref, vs_ref, kw_ref, vw_ref, g_ref, o_ref,
                       bias_scr, m_scr, l_scr, acc_scr, *, tq, seq, slab):
    q0 = pl.program_id(1) * tq
    rows = NSA_REP * tq
    ncp = seq // CMP_STRIDE
    n_slc = seq // SEL_BLOCK
    scale = NSA_DH ** -0.5
    gates = g_ref[...]
    q4s = [jnp.concatenate([q_ref[:, (g * NSA_REP + r) * NSA_DH:(g * NSA_REP + r + 1) * NSA_DH]
                            for r in range(NSA_REP)], axis=0) for g in range(NSA_KVH)]
    qpos_c = q0 + (lax.broadcasted_iota(I32, (rows, ncp), 0) & (tq - 1))
    o_cs, psums = [], []
    for g in range(NSA_KVH):
        p_c, o_c = _cmp_branch(q4s[g], kc_ref[0, :, g * NSA_DH:(g + 1) * NSA_DH],
                               kc_ref[0, :, N_KV + g * NSA_DH:N_KV + (g + 1) * NSA_DH], qpos_c, ncp - 1)
        psum = p_c[0:tq]
        for r in range(1, NSA_REP):
            psum = psum + p_c[r * tq:(r + 1) * tq]
        o_cs.append(o_c)
        psums.append(psum)

    def ranked():
        gmat = (lax.broadcasted_iota(I32, (ncp, LANE), 0) // (SEL_BLOCK // CMP_STRIDE)
                == lax.broadcasted_iota(I32, (ncp, LANE), 1)).astype(BF16)
        imp = _dot2(jnp.concatenate(psums, axis=0), gmat)
        qp = q0 + (lax.broadcasted_iota(I32, (NSA_KVH * tq, LANE), 0) & (tq - 1))
        return _select_blocks_ranked(imp, qp, n_slc)

    sel = lax.cond(q0 + tq > N_SEL * SEL_BLOCK, ranked,
                   lambda: jnp.ones((NSA_KVH * tq, LANE), F32))
    emat = (lax.broadcasted_iota(I32, (LANE, seq), 1) // SEL_BLOCK
            == lax.broadcasted_iota(I32, (LANE, seq), 0)).astype(BF16)
    nk = (q0 + tq + NSA_KC - 1) // NSA_KC
    s0 = pl.multiple_of(jnp.maximum(q0 + tq - slab, 0), SUBLANE)
    key = lax.broadcasted_iota(I32, (tq, seq), 1)
    causal = key <= q0 + lax.broadcasted_iota(I32, (tq, seq), 0)
    wp = s0 + lax.broadcasted_iota(I32, (tq, slab), 1)
    qpw = q0 + lax.broadcasted_iota(I32, (tq, slab), 0)
    wb1 = jnp.where((wp <= qpw) & (wp > qpw - WINDOW), 0.0, NEG_INF)

    def add_bias(s, bias):
        return jnp.concatenate([s[r * tq:(r + 1) * tq] + bias for r in range(NSA_REP)], axis=0)

    for g in range(NSA_KVH):
        keep = (_dot(sel[g * tq:(g + 1) * tq].astype(BF16), emat) > 0.5) & causal
        bias = jnp.where(keep, 0.0, NEG_INF)
        for c in range(seq // NSA_KC):
            bias_scr[g, c] = bias[:, c * NSA_KC:(c + 1) * NSA_KC]
    m_scr[...] = jnp.full(m_scr.shape, NEG_INF, F32)
    l_scr[...] = jnp.zeros(l_scr.shape, F32)
    acc_scr[...] = jnp.zeros(acc_scr.shape, F32)

    def body(c, carry):
        k0 = pl.multiple_of(c * NSA_KC, NSA_KC)
        for g in range(NSA_KVH):
            gs = slice(g * NSA_DH, (g + 1) * NSA_DH)
            s = add_bias(_dot_nt(q4s[g], ks_ref[pl.ds(k0, NSA_KC), gs]) * scale, bias_scr[g, c])
            m_old = m_scr[g]
            m_new = jnp.maximum(m_old, _row_reduce(s, jnp.maximum, jnp.max))
            p = jnp.exp(s - m_new)
            alpha = jnp.exp(m_old - m_new)
            l_scr[g] = alpha * l_scr[g] + _row_reduce(p, jnp.add, jnp.sum)
            acc_scr[g] = alpha * acc_scr[g] + _dot(p.astype(BF16), vs_ref[pl.ds(k0, NSA_KC), gs])
            m_scr[g] = m_new
        return carry

    lax.fori_loop(0, nk, body, 0)
    for g in range(NSA_KVH):
        gs = slice(g * NSA_DH, (g + 1) * NSA_DH)
        q4 = q4s[g]
        o_s = acc_scr[g] / l_scr[g]
        s_w = add_bias(_dot_nt(q4, kw_ref[pl.ds(s0, slab), gs]) * scale, wb1)
        p_w = jnp.exp(s_w - _row_reduce(s_w, jnp.maximum, jnp.max))
        o_w = (_dot(p_w.astype(BF16), vw_ref[pl.ds(s0, slab), gs])
               / _row_reduce(p_w, jnp.add, jnp.sum))
        o_c = o_cs[g]
        for r in range(NSA_REP):
            h = g * NSA_REP + r
            rs = slice(r * tq, (r + 1) * tq)
            o = (gates[:, 3 * h:3 * h + 1] * o_c[rs] + gates[:, 3 * h + 1:3 * h + 2] * o_s[rs]
                 + gates[:, 3 * h + 2:3 * h + 3] * o_w[rs])
            o_ref[:, h * NSA_DH:(h + 1) * NSA_DH] = o.astype(BF16)


def _nsa_prompt_call(qn, kc, bs, bw, gates, nb, seq):
    tq = 128
    nq = seq // tq
    slab = min(seq, WINDOW + tq)
    kv = lambda col: pl.BlockSpec((seq, N_KV), lambda b, i: (b, col))
    kern = functools.partial(_nsa_prompt_kernel, tq=tq, seq=seq, slab=slab)
    rows = NSA_REP * tq
    return pl.pallas_call(
        kern,
        grid=(nb, nq),
        in_specs=[pl.BlockSpec((tq, D_C), lambda b, i: (b * nq + i, 0)),
                  pl.BlockSpec((1, seq // CMP_STRIDE, 2 * N_KV), lambda b, i: (b, 0, 0)),
                  kv(0), kv(1), kv(0), kv(1),
                  pl.BlockSpec((tq, LANE), lambda b, i: (b * nq + i, 0))],
        out_specs=pl.BlockSpec((tq, D_C), lambda b, i: (b * nq + i, 0)),
        out_shape=jax.ShapeDtypeStruct((nb * seq, D_C), BF16),
        scratch_shapes=[pltpu.VMEM((NSA_KVH, seq // NSA_KC, tq, NSA_KC), F32),
                        pltpu.VMEM((NSA_KVH, rows, 1), F32), pltpu.VMEM((NSA_KVH, rows, 1), F32),
                        pltpu.VMEM((NSA_KVH, rows, NSA_DH), F32)],
        compiler_params=_cparams(("arbitrary", "arbitrary")),
        name="nsa_prompt",
    )(qn, kc, bs, bs, bw, bw, gates)


def _nsa_s_sel_kernel(q_ref, kc_ref, oc_ref, sel_ref, flag_ref, *, past, dl, n_tiles):
    rows = NSA_REP * dl
    ncp = past // CMP_STRIDE
    n_slc = (past + dl + SEL_BLOCK - 1) // SEL_BLOCK
    nbp = (n_slc + LANE - 1) // LANE * LANE
    per_tile = PG * PAGE // SEL_BLOCK
    gmat = (lax.broadcasted_iota(I32, (ncp, nbp), 0) // (SEL_BLOCK // CMP_STRIDE)
            == lax.broadcasted_iota(I32, (ncp, nbp), 1)).astype(BF16)
    rmat = (lax.broadcasted_iota(I32, (rows, rows), 0) // NSA_REP
            == lax.broadcasted_iota(I32, (rows, rows), 1) // NSA_REP).astype(BF16)

    def qpos_of(shape):
        return past + lax.broadcasted_iota(I32, shape, 0) // NSA_REP

    lane = lax.broadcasted_iota(I32, (rows, LANE), 1)
    used = jnp.zeros((1, nbp), F32)
    for g in range(NSA_KVH):
        gs = slice(g * NSA_DH, (g + 1) * NSA_DH)
        p_c, o_c = _cmp_branch(q_ref[0, g], kc_ref[0, 0, :, gs],
                               kc_ref[0, 0, :, N_KV + g * NSA_DH:N_KV + (g + 1) * NSA_DH],
                               qpos_of((rows, ncp)), ncp - 1)
        oc_ref[0, g] = o_c
        psum = _dot2g(rmat, p_c)
        imp = _dot2(psum, gmat)
        sel = _select_blocks(imp, qpos_of((rows, nbp)), n_slc)
        used = jnp.maximum(used, jnp.max(sel, axis=0, keepdims=True))
        for tl in range(n_tiles):
            lo = tl * per_tile
            chunk = sel[:, (lo // LANE) * LANE:(lo // LANE + 1) * LANE]
            sh = (LANE - lo % LANE) % LANE
            if sh:
                chunk = pltpu.roll(chunk, sh, 1)
            sel_ref[0, g, tl] = jnp.where(lane < per_tile, chunk, 0.0)
    flag_ref[0] = (used > 0.5).astype(I32)


def _nsa_s_sel_call(q16, kc, layer, past, dl):
    nb = q16.shape[0]
    rows = NSA_REP * dl
    n_tiles = past // (PG * PAGE) + 1
    n_slc = (past + dl + SEL_BLOCK - 1) // SEL_BLOCK
    nbp = (n_slc + LANE - 1) // LANE * LANE
    kern = functools.partial(_nsa_s_sel_kernel, past=past, dl=dl, n_tiles=n_tiles)
    return pl.pallas_call(
        kern,
        grid=(nb,),
        in_specs=[pl.BlockSpec((1, NSA_KVH, rows, NSA_DH), lambda b: (b, 0, 0, 0)),
                  pl.BlockSpec((1, 1, past // CMP_STRIDE, 2 * N_KV), lambda b: (layer, b, 0, 0))],
        out_specs=[pl.BlockSpec((1, NSA_KVH, rows, NSA_DH), lambda b: (b, 0, 0, 0)),
                   pl.BlockSpec((1, NSA_KVH, n_tiles, rows, LANE), lambda b: (b, 0, 0, 0, 0)),
                   pl.BlockSpec((1, 1, nbp), lambda b: (b, 0, 0))],
        out_shape=[jax.ShapeDtypeStruct((nb, NSA_KVH, rows, NSA_DH), F32),
                   jax.ShapeDtypeStruct((nb, NSA_KVH, n_tiles, rows, LANE), F32),
                   jax.ShapeDtypeStruct((nb, 1, nbp), I32)],
        compiler_params=_cparams(("arbitrary",)),
        name="nsa_sample_select",
    )(q16, kc)


def _page_used(fl_ref, b, page, nbp):
    blk = b * nbp + page * (PAGE // SEL_BLOCK)
    return (fl_ref[blk] + fl_ref[blk + 1]) > 0


def _nsa_s_attn_kernel(pt_ref, fl_ref, *refs, past, dl, npg, nbp):
    pages = refs[:PG]
    (q_ref, selt_ref, sell_ref, emat_ref, oc_ref, ns_ref, nw_ref, wp_ref, g_ref,
     o_ref, m_scr, l_scr, a_scr) = refs[PG:]
    pg = pl.program_id(1)
    rows = NSA_REP * dl
    scale = NSA_DH ** -0.5
    wb = wp_ref.shape[2] // KV_TILES

    def tok_rows(ref, c, n):
        return ref[0, 0, pl.ds(c, n, stride=KV_TILES), :].astype(BF16)

    @pl.when(pg == 0)
    def _():
        m_scr[...] = jnp.full(m_scr.shape, NEG_INF, F32)
        l_scr[...] = jnp.zeros(l_scr.shape, F32)
        a_scr[...] = jnp.zeros(a_scr.shape, F32)

    for k in range(PG):
        @pl.when(_page_used(fl_ref, pl.program_id(0), pg * PG + k, nbp))
        def _(k=k):
            for g in range(NSA_KVH):
                q = q_ref[0, g]
                kmask = _dot(selt_ref[0, g, 0].astype(BF16),
                             emat_ref[:, k * PAGE:(k + 1) * PAGE]) > 0.5
                s = jnp.where(kmask, _dot_nt(q, tok_rows(pages[k], g, PAGE)) * scale, NEG_INF)
                m_old = m_scr[g]
                m_new = jnp.maximum(m_old, jnp.max(s, axis=-1, keepdims=True))
                p = jnp.where(kmask, jnp.exp(s - m_new), 0.0)
                alpha = jnp.exp(m_old - m_new)
                l_scr[g] = alpha * l_scr[g] + jnp.sum(p, axis=-1, keepdims=True)
                a_scr[g] = alpha * a_scr[g] + _dot(p.astype(BF16), tok_rows(pages[k], NSA_KVH + g, PAGE))
                m_scr[g] = m_new

    @pl.when(pg == npg - 1)
    def _():
        gates = g_ref[0]
        tok = lax.broadcasted_iota(I32, (rows, SUBLANE), 0) // NSA_REP
        ncol = lax.broadcasted_iota(I32, (rows, SUBLANE), 1)
        for g in range(NSA_KVH):
            gs = slice(g * NSA_DH, (g + 1) * NSA_DH)
            vsl = slice(N_KV + g * NSA_DH, N_KV + (g + 1) * NSA_DH)
            q = q_ref[0, g]
            nmask = (sell_ref[0, g, 0][:, 0:1] > 0.5) & (ncol <= tok) & (ncol < dl)
            s_n = jnp.where(nmask, _dot_nt(q, ns_ref[0, :, gs].astype(BF16)) * scale, NEG_INF)
            m_old = m_scr[g]
            m_new = jnp.maximum(m_old, jnp.max(s_n, axis=-1, keepdims=True))
            p_n = jnp.where(nmask, jnp.exp(s_n - m_new), 0.0)
            alpha = jnp.exp(m_old - m_new)
            lsum = alpha * l_scr[g] + jnp.sum(p_n, axis=-1, keepdims=True)
            acc = alpha * a_scr[g] + _dot(p_n.astype(BF16), ns_ref[0, :, vsl].astype(BF16))
            o_s = acc / jnp.maximum(lsum, 1e-30)
            wrow = lax.broadcasted_iota(I32, (rows, wb), 1)
            wtok = lax.broadcasted_iota(I32, (rows, wb), 0) // NSA_REP
            pmask = (wrow > wtok + (wb - WINDOW)) & (wrow + (past - wb) >= 0)
            s_p = jnp.where(pmask, _dot_nt(q, tok_rows(wp_ref, g, wb)) * scale, NEG_INF)
            wmask = (ncol <= tok) & (ncol < dl)
            s_q = jnp.where(wmask, _dot_nt(q, nw_ref[0, :, gs].astype(BF16)) * scale, NEG_INF)
            mw = jnp.maximum(jnp.max(s_p, axis=-1, keepdims=True), jnp.max(s_q, axis=-1, keepdims=True))
            p_p = jnp.where(pmask, jnp.exp(s_p - mw), 0.0)
            p_q = jnp.where(wmask, jnp.exp(s_q - mw), 0.0)
            den = jnp.maximum(jnp.sum(p_p, axis=-1, keepdims=True) + jnp.sum(p_q, axis=-1, keepdims=True),
                              1e-30)
            o_w = (_dot(p_p.astype(BF16), tok_rows(wp_ref, NSA_KVH + g, wb))
                   + _dot(p_q.astype(BF16), nw_ref[0, :, vsl].astype(BF16))) / den
            gg = gates[g]
            o = gg[:, 0:1] * oc_ref[0, g] + gg[:, 1:2] * o_s + gg[:, 2:3] * o_w
            o_ref[0, g] = o.astype(BF16)


def _nsa_s_attn_call(cache, pt_flat, flags, layer, q16, selt, emat, o_c, new_slc, new_win, win_prev,
                     gates16, past, dl):
    nb = q16.shape[0]
    rows = NSA_REP * dl
    n_pages = past // PAGE
    npg = n_pages // PG
    wrows = win_prev.shape[2]
    nbp = flags.shape[-1]
    b4 = lambda shape: pl.BlockSpec((1,) + shape, lambda b, g, pt, fl: (b,) + (0,) * len(shape))

    def page_spec(k):
        def index(b, g, pt, fl):
            page = g * PG + k
            return (layer, jnp.where(_page_used(fl, b, page, nbp), pt[b * n_pages + page], 0), 0, 0)
        return pl.BlockSpec((1, 1, PAGE * KV_TILES, LANE), index)

    grid_spec = pltpu.PrefetchScalarGridSpec(
        num_scalar_prefetch=2,
        grid=(nb, npg),
        in_specs=[page_spec(k) for k in range(PG)]
        + [b4((NSA_KVH, rows, NSA_DH)),
           pl.BlockSpec((1, NSA_KVH, 1, rows, LANE), lambda b, g, pt, fl: (b, 0, g, 0, 0)),
           pl.BlockSpec((1, NSA_KVH, 1, rows, LANE), lambda b, g, pt, fl: (b, 0, npg, 0, 0)),
           pl.BlockSpec((LANE, PG * PAGE), lambda b, g, pt, fl: (0, 0)),
           b4((NSA_KVH, rows, NSA_DH)),
           b4((SUBLANE, 2 * N_KV)), b4((SUBLANE, 2 * N_KV)),
           pl.BlockSpec((1, 1, wrows, LANE), lambda b, g, pt, fl: (layer, b, 0, 0)),
           b4((NSA_KVH, rows, LANE))],
        out_specs=b4((NSA_KVH, rows, NSA_DH)),
        scratch_shapes=[pltpu.VMEM((NSA_KVH, rows, 1), F32), pltpu.VMEM((NSA_KVH, rows, 1), F32),
                        pltpu.VMEM((NSA_KVH, rows, NSA_DH), F32)],
    )
    kern = functools.partial(_nsa_s_attn_kernel, past=past, dl=dl, npg=npg, nbp=nbp)
    return pl.pallas_call(
        kern,
        grid_spec=grid_spec,
        out_shape=jax.ShapeDtypeStruct((nb, NSA_KVH, rows, NSA_DH), BF16),
        compiler_params=_cparams(("arbitrary", "arbitrary")),
        name="nsa_sample_attn",
    )(pt_flat, flags.reshape(-1), *([cache] * PG), q16, selt, selt, emat, o_c, new_slc, new_win, win_prev, gates16)


def _merge_kernel(ya_ref, yb_ref, yc_ref, ga_ref, gb_ref, gc_ref, wa_ref, wb_ref, wc_ref, o_ref):
    m = (jax.nn.sigmoid(ga_ref[...]) * _dot(ya_ref[...], wa_ref[0])
         + jax.nn.sigmoid(gb_ref[...]) * _dot(yb_ref[...], wb_ref[0])
         + jax.nn.sigmoid(gc_ref[...]) * _dot(yc_ref[...], wc_ref[0]))
    o_ref[...] = m.astype(BF16)


def _merge_call(ya, yb, yc, hin, wa, wb, wc, layer):
    m = ya.shape[0]
    tm = min(256, m)
    const = lambda shape: pl.BlockSpec((1,) + shape, lambda i: (layer, 0, 0))
    gate = lambda col: pl.BlockSpec((tm, D_MODEL), lambda i: (i, col))
    return pl.pallas_call(
        _merge_kernel,
        grid=(m // tm,),
        in_specs=[pl.BlockSpec((tm, D_A), lambda i: (i, 0)), pl.BlockSpec((tm, D_B), lambda i: (i, 0)),
                  pl.BlockSpec((tm, D_C), lambda i: (i, 0)), gate(0), gate(1), gate(2),
                  const((D_A, D_MODEL)), const((D_B, D_MODEL)), const((D_C, D_MODEL))],
        out_specs=pl.BlockSpec((tm, D_MODEL), lambda i: (i, 0)),
        out_shape=jax.ShapeDtypeStruct((m, D_MODEL), BF16),
        compiler_params=_cparams(("arbitrary",)),
        name="merge",
    )(ya, yb, yc, hin, hin, hin, wa, wb, wc)


def _outproj_kernel(m_ref, x_ref, g_ref, w_ref, o_ref):
    o_ref[...] = x_ref[...] + g_ref[0] * _dot(m_ref[...], w_ref[0])


def _outproj_call(merged, x, modx, rows_per_group, w, layer):
    m = x.shape[0]
    tm = min(512, m)
    return pl.pallas_call(
        _outproj_kernel,
        grid=(m // tm,),
        in_specs=[pl.BlockSpec((tm, D_MODEL), lambda i: (i, 0)),
                  pl.BlockSpec((tm, D_MODEL), lambda i: (i, 0)),
                  _mod_spec(modx, tm, rows_per_group, 2),
                  pl.BlockSpec((1, D_MODEL, D_MODEL), lambda i: (layer, 0, 0))],
        out_specs=pl.BlockSpec((tm, D_MODEL), lambda i: (i, 0)),
        out_shape=jax.ShapeDtypeStruct((m, D_MODEL), F32),
        compiler_params=_cparams(("arbitrary",)),
        name="outproj",
    )(merged, x, modx, w)


def _ffn_tail(ua, ua1, ua2, ub, cw_ref, cb_ref, wd_ref):
    cw = cw_ref[0]
    conv = cb_ref[0] + cw[0:1] * ua2 + cw[1:2] * ua1 + cw[2:3] * ua
    return _dot((jax.nn.gelu(conv) * ub).astype(BF16), wd_ref[0])


def _ffn_prompt_kernel(x_ref, xh_ref, sc_ref, sh_ref, g_ref, nw_ref, cp_ref, wua_ref, wub_ref, cw_ref,
                       cb_ref, wd_ref, o_ref, tail_ref, h_scr, hh_scr, acc, *, tm, seq):
    f = pl.program_id(1)

    @pl.when(f == 0)
    def _():
        nw, sc, sh = nw_ref[...], sc_ref[0], sh_ref[0]
        h_scr[...] = (_rms(x_ref[...], nw) * (1.0 + sc) + sh).astype(BF16)
        hh_scr[...] = (_rms(xh_ref[...], nw) * (1.0 + sc) + sh).astype(BF16)
        acc[...] = jnp.zeros(acc.shape, F32)

    wua = wua_ref[0]
    ua = _dot(h_scr[...], wua)
    ub = _dot(h_scr[...], wub_ref[0])
    at_start = (pl.program_id(0) * tm) % seq == 0
    prev = jnp.where(at_start, cp_ref[0], _dot(hh_scr[...], wua))
    row = lax.broadcasted_iota(I32, ua.shape, 0)
    p1, p2 = prev[SUBLANE - 1:SUBLANE], prev[SUBLANE - 2:SUBLANE - 1]
    ua1 = jnp.where(row == 0, p1, pltpu.roll(ua, 1, 0))
    ua2 = jnp.where(row == 0, p2, jnp.where(row == 1, p1, pltpu.roll(ua, 2, 0)))
    tail_ref[0] = ua[tm - SUBLANE:tm]
    acc[...] += _ffn_tail(ua, ua1, ua2, ub, cw_ref, cb_ref, wd_ref)

    @pl.when(f == pl.num_programs(1) - 1)
    def _():
        o_ref[...] = x_ref[...] + g_ref[0] * acc[...]


def _ffn_prompt_call(x, modx, seq, nw, cprev, fw, layer):
    m = x.shape[0]
    tm = min(512, m)
    hb = tm // SUBLANE
    kern = functools.partial(_ffn_prompt_kernel, tm=tm, seq=seq)
    return pl.pallas_call(
        kern,
        grid=(m // tm, D_FFP // TF),
        in_specs=[pl.BlockSpec((tm, D_MODEL), lambda i, f: (i, 0)),
                  pl.BlockSpec((SUBLANE, D_MODEL), lambda i, f: (jnp.maximum(i * hb - 1, 0), 0)),
                  _mod_spec(modx, tm, seq, 4), _mod_spec(modx, tm, seq, 3), _mod_spec(modx, tm, seq, 5),
                  pl.BlockSpec((1, D_MODEL), lambda i, f: (0, 0)),
                  pl.BlockSpec((1, SUBLANE, TF), lambda i, f: ((i * tm) // seq, 0, f)),
                  pl.BlockSpec((1, D_MODEL, TF), lambda i, f: (layer, 0, f)),
                  pl.BlockSpec((1, D_MODEL, TF), lambda i, f: (layer, 0, D_FFP // TF + f)),
                  pl.BlockSpec((1, CONV_W, TF), lambda i, f: (layer, 0, f)),
                  pl.BlockSpec((1, 1, TF), lambda i, f: (layer, 0, f)),
                  pl.BlockSpec((1, TF, D_MODEL), lambda i, f: (layer, f, 0))],
        out_specs=[pl.BlockSpec((tm, D_MODEL), lambda i, f: (i, 0)),
                   pl.BlockSpec((1, SUBLANE, TF), lambda i, f: (i, 0, f))],
        out_shape=[jax.ShapeDtypeStruct((m, D_MODEL), F32),
                   jax.ShapeDtypeStruct((m // tm, SUBLANE, D_FFP), F32)],
        scratch_shapes=[pltpu.VMEM((tm, D_MODEL), BF16), pltpu.VMEM((SUBLANE, D_MODEL), BF16),
                        pltpu.VMEM((tm, D_MODEL), F32)],
        compiler_params=_cparams(("arbitrary", "arbitrary")),
        name="ffn_prompt",
    )(x, x, modx, modx, modx, nw, cprev, fw["wu"], fw["wu"], fw["cw"], fw["cb"], fw["wd"])


def _ffn_sample_kernel(x_ref, sc_ref, sh_ref, g_ref, nw_ref, p1_ref, p2_ref, wua_ref, wub_ref, cw_ref,
                       cb_ref, wd_ref, o_ref, ua_ref, h_scr, acc, *, dl):
    f = pl.program_id(0)

    @pl.when(f == 0)
    def _():
        h_scr[...] = (_rms(x_ref[...], nw_ref[...]) * (1.0 + sc_ref[0]) + sh_ref[0]).astype(BF16)
        acc[...] = jnp.zeros(acc.shape, F32)

    ua = _dot(h_scr[...], wua_ref[0])
    ub = _dot(h_scr[...], wub_ref[0])
    tin = lax.broadcasted_iota(I32, ua.shape, 0) % dl
    ua1 = jnp.where(tin >= 1, pltpu.roll(ua, 1, 0), p1_ref[...])
    ua2 = jnp.where(tin >= 2, pltpu.roll(ua, 2, 0), p2_ref[...])
    ua_ref[...] = ua
    acc[...] += _ffn_tail(ua, ua1, ua2, ub, cw_ref, cb_ref, wd_ref)

    @pl.when(f == pl.num_programs(0) - 1)
    def _():
        o_ref[...] = x_ref[...] + g_ref[0] * acc[...]


def _ffn_sample_call(x, modx, dl, nw, p1, p2, fw, layer):
    m = x.shape[0]
    kern = functools.partial(_ffn_sample_kernel, dl=dl)
    full = lambda i: pl.BlockSpec((m, D_MODEL), lambda f: (0, 0))
    return pl.pallas_call(
        kern,
        grid=(D_FFP // TF,),
        in_specs=[full(0),
                  pl.BlockSpec((1, m, D_MODEL), lambda f: (0, 0, 4)),
                  pl.BlockSpec((1, m, D_MODEL), lambda f: (0, 0, 3)),
                  pl.BlockSpec((1, m, D_MODEL), lambda f: (0, 0, 5)),
                  pl.BlockSpec((1, D_MODEL), lambda f: (0, 0)),
                  pl.BlockSpec((m, TF), lambda f: (0, f)), pl.BlockSpec((m, TF), lambda f: (0, f)),
                  pl.BlockSpec((1, D_MODEL, TF), lambda f: (layer, 0, f)),
                  pl.BlockSpec((1, D_MODEL, TF), lambda f: (layer, 0, D_FFP // TF + f)),
                  pl.BlockSpec((1, CONV_W, TF), lambda f: (layer, 0, f)),
                  pl.BlockSpec((1, 1, TF), lambda f: (layer, 0, f)),
                  pl.BlockSpec((1, TF, D_MODEL), lambda f: (layer, f, 0))],
        out_specs=[full(0), pl.BlockSpec((m, TF), lambda f: (0, f))],
        out_shape=[jax.ShapeDtypeStruct((m, D_MODEL), F32),
                   jax.ShapeDtypeStruct((m, D_FFP), F32)],
        scratch_shapes=[pltpu.VMEM((m, D_MODEL), BF16), pltpu.VMEM((m, D_MODEL), F32)],
        compiler_params=_cparams(("arbitrary",)),
        name="ffn_sample",
    )(x, modx, modx, modx, nw, p1, p2, fw["wu"], fw["wu"], fw["cw"], fw["cb"], fw["wd"])


def _rope_tables(pos):
    half = ROPE_DIM // 2
    inv = jnp.exp(jnp.arange(half, dtype=F32) * (-math.log(ROPE_THETA) / half))
    ang = pos.astype(F32)[:, None] * inv[None, :]
    cos, sin = jnp.cos(ang), jnp.sin(ang)
    n = pos.shape[0]
    rest = NSA_DH - ROPE_DIM
    cos_t = jnp.concatenate([cos, cos, jnp.ones((n, rest), F32)], axis=1)
    sin_a = jnp.concatenate([-sin, jnp.zeros((n, NSA_DH - half), F32)], axis=1)
    sin_b = jnp.concatenate([jnp.zeros((n, half), F32), sin, jnp.zeros((n, rest), F32)], axis=1)
    return cos_t, sin_a, sin_b


_W_IN_SIZES = (D_A, D_B, D_B, D_B, D_B, D_C, 6 * N_KV, 3 * NSA_HEADS, 3 * D_MODEL)
_W_IN_DST = (OFF_UA, OFF_HQ, OFF_HF, OFF_HI, OFF_HGT, OFF_NQ, OFF_NKV, OFF_NGT, OFF_MGT)
N_IN = sum(_W_IN_SIZES)


W_IN_RB = 256


def _w_in_src_row(i):
    src = jnp.int32(0)
    start = 0
    for size, dst in zip(_W_IN_SIZES, _W_IN_DST):
        inside = (i * W_IN_RB >= dst) & (i * W_IN_RB < dst + max(size, W_IN_RB))
        src = jnp.where(inside, start + i * W_IN_RB - dst, src)
        start += size
    return src


def _w_in_prep_kernel(x_ref, o_ref):
    i = pl.program_id(1)
    x = x_ref[0].astype(BF16)
    row = lax.broadcasted_iota(I32, x.shape, 0)
    o_ref[0] = jnp.where((i * W_IN_RB == OFF_NGT) & (row >= 3 * NSA_HEADS), jnp.zeros_like(x), x)


def _permute_w_in(w_in):
    depth, d, _ = w_in.shape
    w_t = jnp.swapaxes(w_in, 1, 2)
    return pl.pallas_call(
        _w_in_prep_kernel,
        grid=(depth, N_INP // W_IN_RB),
        in_specs=[pl.BlockSpec((pl.Element(1), pl.Element(W_IN_RB), pl.Element(d)),
                               lambda l, i: (l, pl.multiple_of(_w_in_src_row(i), SUBLANE), 0))],
        out_specs=pl.BlockSpec((1, W_IN_RB, d), lambda l, i: (l, i, 0)),
        out_shape=jax.ShapeDtypeStruct((depth, N_INP, d), BF16),
        compiler_params=_cparams(("arbitrary", "arbitrary")),
        name="w_in_prep",
    )(w_t)


def _w_up_prep_kernel(x_ref, o_ref):
    o_ref[0, :, :D_FF] = x_ref[0].astype(BF16)
    o_ref[0, :, D_FF:] = jnp.zeros((o_ref.shape[1], D_FFP - D_FF), BF16)


def _pad_w_up(w_up):
    depth, d, _ = w_up.shape
    r = 512
    return pl.pallas_call(
        _w_up_prep_kernel,
        grid=(depth, d // r, 2),
        in_specs=[pl.BlockSpec((1, r, D_FF), lambda l, i, h: (l, i, h))],
        out_specs=pl.BlockSpec((1, r, D_FFP), lambda l, i, h: (l, i, h)),
        out_shape=jax.ShapeDtypeStruct((depth, d, 2 * D_FFP), BF16),
        compiler_params=_cparams(("arbitrary", "arbitrary", "arbitrary")),
        name="w_up_prep",
    )(w_up)


def _s5_params(a_re, a_im, log_dt, b_re, b_im, c_re, c_im, d, w_glu, b_glu):
    dt = jnp.exp(log_dt)[:, None]
    mag = jnp.exp(a_re * dt)
    ab_re, ab_im = mag * jnp.cos(a_im * dt), mag * jnp.sin(a_im * dt)
    den = a_re * a_re + a_im * a_im
    cf_re = ((ab_re - 1.0) * a_re + ab_im * a_im) / den
    cf_im = (ab_im * a_re - (ab_re - 1.0) * a_im) / den
    bb_re = cf_re[..., None] * b_re - cf_im[..., None] * b_im
    bb_im = cf_re[..., None] * b_im + cf_im[..., None] * b_re
    eye = jnp.eye(S5_G, dtype=F32)
    bdiag = lambda bb: jnp.einsum("gnc,gh->gchn", bb, eye).reshape(D_A, S5_W).astype(BF16)
    cdiag = lambda cc: jnp.einsum("gcn,gh->gnhc", cc, eye).reshape(S5_W, D_A).astype(BF16)
    pr, pi = ab_re.reshape(1, S5_W), ab_im.reshape(1, S5_W)
    prs, pis = [pr], [pi]
    for _ in range(SUBLANE - 1):
        nr = prs[-1] * pr - pis[-1] * pi
        ni = prs[-1] * pi + pis[-1] * pr
        prs.append(nr)
        pis.append(ni)
    return dict(bbr=bdiag(bb_re), bbi=bdiag(bb_im), ccr=cdiag(c_re), cci=cdiag(c_im),
                apr=jnp.concatenate(prs, axis=0), api=jnp.concatenate(pis, axis=0),
                d=d.reshape(1, D_A), wg=w_glu.astype(BF16), bg=b_glu.reshape(1, D_A))


def _hgrn_params(lb):
    pos = lb > 0
    lb_safe = jnp.where(pos, lb, 1.0)
    z = jnp.zeros_like(lb)
    return jnp.stack([jnp.log(lb_safe), jnp.log1p(-lb), 1.0 - lb, pos.astype(F32), z, z, z, z], axis=0)


def _cmp_params(cmp_a, w1, b1, w2):
    def lanes(a):
        return jnp.concatenate([a[0], a[0], a[1], a[1]], axis=-1)
    return dict(a1=lanes(cmp_a[:, :CMP_STRIDE]), a2=lanes(cmp_a[:, CMP_STRIDE:]),
                w1=w1.astype(BF16), b1=b1, w2=w2.astype(BF16))


def _ffn_params(w_up, conv_w, conv_b, w_down):
    padc = lambda a: jnp.pad(a, [(0, 0)] * (a.ndim - 1) + [(0, D_FFP - D_FF)])
    wu = _pad_w_up(w_up)
    wd = jnp.pad(w_down.astype(BF16), ((0, 0), (0, D_FFP - D_FF), (0, 0)))
    return dict(wu=wu, cw=padc(conv_w), cb=padc(conv_b)[:, None, :], wd=wd)


def kernel(x_prompt, x_sample, cache_cmp, cache_slc, state_win, state_s5, state_hgrn, state_conv,
           page_table, c_prompt, c_sample, w_mod, b_mod, norm1_w, norm2_w, w_in,
           s5_a_re, s5_a_im, s5_log_dt, s5_b_re, s5_b_im, s5_c_re, s5_c_im, s5_d, s5_w_glu, s5_b_glu,
           hg_lb_logits, hg_norm_w, nsa_q_norm, nsa_k_norm, cmp_a, cmp_w1, cmp_b1, cmp_w2,
           w_branch_a, w_branch_b, w_branch_c, w_out, w_up, conv_w, conv_b, w_down):
    bsz, seq, _ = x_prompt.shape
    dbsz, dl, _ = x_sample.shape
    depth = w_in.shape[0]
    n_pages = page_table.shape[1]
    past = n_pages * PAGE
    wb = state_win.shape[2]
    mp, ms = bsz * seq, dbsz * dl
    rows = NSA_REP * dl
    assert seq % 512 == 0 and seq // SEL_BLOCK <= LANE and n_pages % PG == 0 and dl <= SUBLANE

    probs = jax.nn.softmax(hg_lb_logits.astype(F32), axis=0)
    lower_bounds = jnp.cumsum(probs, axis=0) - probs[0:1]
    w_in_p = _permute_w_in(w_in)
    wa, wbr, wc, wo = (w.astype(BF16) for w in (w_branch_a, w_branch_b, w_branch_c, w_out))
    fw = _ffn_params(w_up, conv_w, conv_b, w_down)
    rope_p = _rope_tables(jnp.arange(seq, dtype=I32))
    rope_s = tuple(jnp.tile(t, (dbsz, 1)) for t in _rope_tables(past + jnp.arange(dl, dtype=I32)))
    pt_flat = page_table.reshape(-1).astype(I32)
    cache_cmp4 = cache_cmp.reshape(depth, -1, PAGE * KV_TILES, LANE)
    cache_slc4 = cache_slc.reshape(depth, -1, PAGE * KV_TILES, LANE)
    win_prev4 = state_win.reshape(depth, dbsz, wb * KV_TILES, LANE)
    emat = (jnp.arange(PG * PAGE)[None, :] // SEL_BLOCK == jnp.arange(LANE)[:, None]).astype(BF16)

    nr = -(-(bsz + dbsz) // 16) * 16
    c_all = jnp.concatenate([c_prompt, c_sample, jnp.zeros((nr - bsz - dbsz, D_MODEL), F32)], axis=0)
    mod = _mod_call(c_all, w_mod, b_mod)

    cmp_lanes = lambda a: jnp.concatenate([a[:, 0], a[:, 0], a[:, 1], a[:, 1]], axis=-1).reshape(
        depth, SUB_ROWS, LANE)
    first, second = _cmp_pool_cache_call(cache_cmp4, pt_flat, dbsz, n_pages,
                                         cmp_lanes(cmp_a[:, :, :CMP_STRIDE]),
                                         cmp_lanes(cmp_a[:, :, CMP_STRIDE:]))
    kc_past = _cmp_mlp_cache_call(first, second, cmp_w1.astype(BF16), cmp_b1, cmp_w2.astype(BF16))

    xp = x_prompt.reshape(mp, D_MODEL)
    xs = x_sample.reshape(ms, D_MODEL)
    outs = [[] for _ in range(12)]
    for l in range(depth):
        s5p = _s5_params(s5_a_re[l], s5_a_im[l], s5_log_dt[l], s5_b_re[l], s5_b_im[l], s5_c_re[l],
                         s5_c_im[l], s5_d[l], s5_w_glu[l], s5_b_glu[l])
        hgp = _hgrn_params(lower_bounds[l])
        hgn = hg_norm_w[l].reshape(1, HG_DV)
        cp = _cmp_params(cmp_a[l], cmp_w1[l], cmp_b1[l], cmp_w2[l])
        n1, n2 = norm1_w[l].reshape(1, D_MODEL), norm2_w[l].reshape(1, D_MODEL)
        qn, kn = nsa_q_norm[l].reshape(1, NSA_DH), nsa_k_norm[l]
        modp = mod[l, :bsz].reshape(bsz, 1, 6 * D_MODEL)
        mods = jnp.repeat(mod[l, bsz:bsz + dbsz], dl, axis=0).reshape(1, ms, 6 * D_MODEL)

        hin = _inproj_call(xp, modp, seq, n1, w_in_p, l)
        ya, s5_new = _s5_call(hin, OFF_UA // D_A, bsz, seq, jnp.zeros((bsz, 2, S5_W), F32), s5p,
                              SUBLANE - 1)
        yb, hg_new = _hgrn_call(hin, tuple(o // D_B for o in (OFF_HQ, OFF_HF, OFF_HI, OFF_HGT)), bsz, seq,
                                seq, jnp.zeros((bsz, HG_HEADS, HG_DK, HG_DV), F32), hgp, hgn)
        qo, rc, rs, rw, bc, bs, bw, gates = _nsa_prep_call(hin, rope_p, seq, qn, kn)
        kc = _cmp_prompt_call(rc, bsz, seq, cp)
        yc = _nsa_prompt_call(qo, kc, bs, bw, gates, bsz, seq)
        merged = _merge_call(ya, yb, yc, hin, wa, wbr, wc, l)
        xp = _outproj_call(merged, xp, modp, seq, wo, l)
        xp, tails = _ffn_prompt_call(xp, modp, seq, n2, jnp.zeros((bsz, SUBLANE, D_FFP), F32), fw, l)
        conv_new = tails.reshape(bsz, -1, SUBLANE, D_FFP)[:, -1, SUBLANE - (CONV_W - 1):, :D_FF]
        kv5 = lambda r: r.reshape(bsz, seq, 2, NSA_KVH, NSA_DH)
        for i, o in enumerate((s5_new.reshape(bsz, 2, S5_G, S5_N), hg_new, kv5(rc), kv5(rs),
                               kv5(rw)[:, -min(WINDOW, seq):], conv_new)):
            outs[2 * i].append(o)

        hin = _inproj_call(xs, mods, dl, n1, w_in_p, l)
        padt = lambda a, n: jnp.pad(a.reshape(dbsz, dl, -1), ((0, 0), (0, n - dl), (0, 0))).reshape(dbsz * n, -1)
        u_pad = padt(hin[:, OFF_UA:OFF_UA + D_A], SUBLANE)
        ya, s5_new = _s5_call(u_pad, 0, dbsz, SUBLANE, state_s5[l].reshape(dbsz, 2, S5_W), s5p, dl - 1)
        ya = ya.reshape(dbsz, SUBLANE, D_A)[:, :dl].reshape(ms, D_A)
        hg_pad = padt(hin[:, OFF_HQ:OFF_HQ + 4 * D_B], HG_C)
        yb, hg_new = _hgrn_call(hg_pad, (0, 1, 2, 3), dbsz, HG_C, dl, state_hgrn[l], hgp, hgn)
        yb = yb.reshape(dbsz, HG_C, D_B)[:, :dl].reshape(ms, D_B)
        qo, rc, rs, rw, bc, bs, bw, gates = _nsa_prep_call(hin, rope_s, dl, qn, kn)
        q16 = qo.reshape(dbsz, dl, NSA_KVH, NSA_REP, NSA_DH).transpose(0, 2, 1, 3, 4).reshape(
            dbsz, NSA_KVH, rows, NSA_DH)
        o_c, selt, flags = _nsa_s_sel_call(q16, kc_past, l, past, dl)
        g16 = gates[:, :3 * NSA_HEADS].reshape(dbsz, dl, NSA_KVH, NSA_REP, 3).transpose(0, 2, 1, 3, 4)
        g16 = jnp.pad(g16.reshape(dbsz, NSA_KVH, rows, 3), ((0, 0), (0, 0), (0, 0), (0, LANE - 3)))
        pad8 = lambda r: jnp.pad(r.reshape(dbsz, dl, 2 * N_KV), ((0, 0), (0, SUBLANE - dl), (0, 0)))
        yc = _nsa_s_attn_call(cache_slc4, pt_flat, flags, l, q16, selt, emat, o_c, pad8(rs), pad8(rw),
                              win_prev4, g16, past, dl)
        yc = yc.reshape(dbsz, NSA_KVH, dl, NSA_REP, NSA_DH).transpose(0, 2, 1, 3, 4).reshape(ms, D_C)
        merged = _merge_call(ya, yb, yc, hin, wa, wbr, wc, l)
        xs = _outproj_call(merged, xs, mods, dl, wo, l)
        st = jnp.pad(state_conv[l], ((0, 0), (0, 0), (0, D_FFP - D_FF)))
        tin = jnp.arange(dl)[None, :, None]
        p1 = jnp.where(tin == 0, st[:, 1:2], 0.0).reshape(ms, D_FFP)
        p2 = jnp.where(tin == 0, st[:, 0:1], jnp.where(tin == 1, st[:, 1:2], 0.0)).reshape(ms, D_FFP)
        xs, ua = _ffn_sample_call(xs, mods, dl, n2, p1, p2, fw, l)
        ext = jnp.concatenate([state_conv[l], ua[:, :D_FF].reshape(dbsz, dl, D_FF)], axis=1)
        kv5 = lambda r: r.reshape(dbsz, dl, 2, NSA_KVH, NSA_DH)
        win_new = jnp.concatenate([state_win[l], kv5(rw)], axis=1)[:, -wb:]
        for i, o in enumerate((s5_new.reshape(dbsz, 2, S5_G, S5_N), hg_new, kv5(rc), kv5(rs), win_new,
                               ext[:, dl:])):
            outs[2 * i + 1].append(o)

    st = [jnp.stack(o) for o in outs]
    return (xp.reshape(bsz, seq, D_MODEL), xs.reshape(dbsz, dl, D_MODEL), *st)
```

```python
import functools
import math

import jax
import jax.numpy as jnp
from jax import lax
from jax.experimental import pallas as pl
from jax.experimental.pallas import tpu as pltpu

F32 = jnp.float32
BF16 = jnp.bfloat16
I32 = jnp.int32

D_MODEL = 2048
D_A = D_MODEL // 4
S5_GROUP = 16
S5_G = D_A // S5_GROUP
S5_N = 64
S5_W = S5_G * S5_N
D_B = D_MODEL // 4
HG_DK = 128
HG_DV = 128
HG_HEADS = D_B // HG_DV
D_C = D_MODEL // 2
NSA_DH = 128
NSA_HEADS = D_C // NSA_DH
NSA_KVH = 2
NSA_REP = NSA_HEADS // NSA_KVH
N_KV = NSA_KVH * NSA_DH
ROPE_DIM = NSA_DH // 4
ROPE_THETA = 500000.0
CMP_STRIDE = 16
SEL_BLOCK = 64
N_SEL = 16
WINDOW = 512
FORCE_BONUS = 1000.0
D_FF = ((8 * D_MODEL) // 3 + 127) // 128 * 128
CONV_W = 3
EPS = 1e-6
NEG_INF = -1e30
PAGE = 128

LANE = 128
SUBLANE = 8
VMEM_LIMIT = 56 * 1024 * 1024

OFF_MGT = 0
OFF_NQ = 3 * D_MODEL
OFF_NKV = OFF_NQ + D_C
OFF_UA = OFF_NKV + 6 * N_KV
OFF_HQ = OFF_UA + D_A
OFF_HF = OFF_HQ + D_B
OFF_HI = OFF_HF + D_B
OFF_HGT = OFF_HI + D_B
OFF_NGT = OFF_HGT + D_B
NGT_PAD = 256
N_INP = OFF_NGT + NGT_PAD
TN_IN = 1280
D_FFP = 5632
TF = 512
HG_C = 64
HG_SB = 16


def _cparams(sem):
    return pltpu.CompilerParams(dimension_semantics=sem, vmem_limit_bytes=VMEM_LIMIT)


def _dot(a, b):
    return jnp.dot(a, b, preferred_element_type=F32)


def _dot_nt(a, b):
    return lax.dot_general(a, b, (((1,), (1,)), ((), ())), preferred_element_type=F32)


def _dot_tn(a, b):
    return lax.dot_general(a, b, (((0,), (0,)), ((), ())), preferred_element_type=F32)


def _dot2(a, b):
    hi = a.astype(BF16)
    lo = (a - hi.astype(F32)).astype(BF16)
    return _dot(hi, b) + _dot(lo, b)


def _dot3(a, b):
    hi = a.astype(BF16)
    r1 = a - hi.astype(F32)
    mid = r1.astype(BF16)
    lo = (r1 - mid.astype(F32)).astype(BF16)
    return _dot(b, hi) + _dot(b, mid) + _dot(b, lo)


def _rms(x, w):
    return x * lax.rsqrt(jnp.mean(x * x, axis=-1, keepdims=True) + EPS) * w


def _masked_softmax(s, mask):
    s = jnp.where(mask, s, NEG_INF)
    p = jnp.where(mask, jnp.exp(s - jnp.max(s, axis=-1, keepdims=True)), 0.0)
    return p / jnp.maximum(jnp.sum(p, axis=-1, keepdims=True), 1e-30)


def _row_reduce(x, combine, reduce):
    n = x.shape[-1]
    if n % LANE or n == LANE:
        return reduce(x, axis=-1, keepdims=True)
    acc = x[:, :LANE]
    for k in range(1, n // LANE):
        acc = combine(acc, x[:, k * LANE:(k + 1) * LANE])
    return reduce(acc, axis=-1, keepdims=True)


def _log_sigmoid(x):
    return jnp.minimum(x, 0.0) - jnp.log1p(jnp.exp(-jnp.abs(x)))


def _logaddexp(a, b):
    return jnp.maximum(a, b) + jnp.log1p(jnp.exp(-jnp.abs(a - b)))


def _mod_kernel(c_ref, w_ref, b_ref, o_ref):
    c = c_ref[...]
    a = (c * jax.nn.sigmoid(c)).astype(BF16)
    o_ref[0] = _dot(a, w_ref[0].astype(BF16)) + b_ref[0]


def _mod_call(c_all, w_mod, b_mod):
    depth, d, n = w_mod.shape
    r = c_all.shape[0]
    tn = 1024
    return pl.pallas_call(
        _mod_kernel,
        grid=(depth, n // tn),
        in_specs=[pl.BlockSpec((r, d), lambda l, j: (0, 0)),
                  pl.BlockSpec((1, d, tn), lambda l, j: (l, 0, j)),
                  pl.BlockSpec((1, 1, tn), lambda l, j: (l, 0, j))],
        out_specs=pl.BlockSpec((1, r, tn), lambda l, j: (l, 0, j)),
        out_shape=jax.ShapeDtypeStruct((depth, r, n), F32),
        compiler_params=_cparams(("arbitrary", "arbitrary")),
        name="mod",
    )(c_all, w_mod, b_mod.reshape(depth, 1, n))


def _inproj_kernel(x_ref, sc_ref, sh_ref, nw_ref, w_ref, o_ref, h_scr):
    @pl.when(pl.program_id(1) == 0)
    def _():
        y = _rms(x_ref[...], nw_ref[...])
        h_scr[...] = (y * (1.0 + sc_ref[0]) + sh_ref[0]).astype(BF16)

    o_ref[...] = _dot_nt(h_scr[...], w_ref[0])


def _mod_spec(modx, tm, rows_per_group, col):
    if modx.shape[1] == 1:
        return pl.BlockSpec((1, 1, D_MODEL), lambda i, *_: ((i * tm) // rows_per_group, 0, col))
    return pl.BlockSpec((1, tm, D_MODEL), lambda i, *_: (0, i, col))


def _inproj_call(x, modx, rows_per_group, nw, w, layer):
    m = x.shape[0]
    tm = min(512, m)
    return pl.pallas_call(
        _inproj_kernel,
        grid=(m // tm, N_INP // TN_IN),
        in_specs=[pl.BlockSpec((tm, D_MODEL), lambda i, j: (i, 0)),
                  _mod_spec(modx, tm, rows_per_group, 1),
                  _mod_spec(modx, tm, rows_per_group, 0),
                  pl.BlockSpec((1, D_MODEL), lambda i, j: (0, 0)),
                  pl.BlockSpec((1, TN_IN, D_MODEL), lambda i, j: (layer, j, 0))],
        out_specs=pl.BlockSpec((tm, TN_IN), lambda i, j: (i, j)),
        out_shape=jax.ShapeDtypeStruct((m, N_INP), F32),
        scratch_shapes=[pltpu.VMEM((tm, D_MODEL), BF16)],
        compiler_params=_cparams(("arbitrary", "arbitrary")),
        name="inproj",
    )(x, modx, modx, nw, w)


S5_LC = 512


def _s5_kernel(u_ref, h0_ref, bbr_ref, bbi_ref, ccr_ref, cci_ref, apr_ref, api_ref, d_ref,
               wg_ref, bg_ref, y_ref, hl_ref, hre, him, car, *, t_rows, last_row):
    @pl.when(pl.program_id(1) == 0)
    def _():
        car[...] = h0_ref[0]

    u = u_ref[...]
    ub = u.astype(BF16)
    hre[...] = _dot(ub, bbr_ref[...])
    him[...] = _dot(ub, bbi_ref[...])
    row = lax.broadcasted_iota(I32, (SUBLANE, S5_LC), 0)
    for c in range(S5_W // S5_LC):
        sl = slice(c * S5_LC, (c + 1) * S5_LC)
        pwr = apr_ref[:, sl]
        pwi = api_ref[:, sl]
        steps = tuple((s, pwr[s - 1:s], pwi[s - 1:s]) for s in (1, 2, 4))

        def body(i, carry, sl=sl, pwr=pwr, pwi=pwi, steps=steps):
            cr, ci = carry
            r0 = pl.multiple_of(i * SUBLANE, SUBLANE)
            xr = hre[pl.ds(r0, SUBLANE), sl]
            xi = him[pl.ds(r0, SUBLANE), sl]
            for s, ar, ai in steps:
                sr = jnp.where(row >= s, pltpu.roll(xr, s, 0), 0.0)
                si = jnp.where(row >= s, pltpu.roll(xi, s, 0), 0.0)
                xr, xi = xr + (ar * sr - ai * si), xi + (ar * si + ai * sr)
            xr, xi = xr + (pwr * cr - pwi * ci), xi + (pwr * ci + pwi * cr)
            hre[pl.ds(r0, SUBLANE), sl] = xr
            him[pl.ds(r0, SUBLANE), sl] = xi
            return xr[SUBLANE - 1:SUBLANE], xi[SUBLANE - 1:SUBLANE]

        cr, ci = lax.fori_loop(0, t_rows // SUBLANE, body, (car[0:1, sl], car[1:2, sl]))
        car[0:1, sl] = cr
        car[1:2, sl] = ci

    lr = t_rows - SUBLANE + last_row
    hl_ref[0, 0:1, :] = hre[lr:lr + 1, :]
    hl_ref[0, 1:2, :] = him[lr:lr + 1, :]
    y = (_dot(hre[...].astype(BF16), ccr_ref[...]) - _dot(him[...].astype(BF16), cci_ref[...])
         + d_ref[...] * u)
    y = jax.nn.gelu(y)
    z = _dot(y.astype(BF16), wg_ref[...]) + bg_ref[...]
    y_ref[...] = (y * jax.nn.sigmoid(z)).astype(BF16)


def _s5_call(u_arr, ucol, nb, seq, h0, p, last_row):
    t_rows = min(256, seq)
    nt = seq // t_rows
    const = lambda shape: pl.BlockSpec(shape, lambda b, t: (0,) * len(shape))
    kern = functools.partial(_s5_kernel, t_rows=t_rows, last_row=last_row)
    return pl.pallas_call(
        kern,
        grid=(nb, nt),
        in_specs=[pl.BlockSpec((t_rows, D_A), lambda b, t: (b * nt + t, ucol)),
                  pl.BlockSpec((1, 2, S5_W), lambda b, t: (b, 0, 0)),
                  const((D_A, S5_W)), const((D_A, S5_W)), const((S5_W, D_A)), const((S5_W, D_A)),
                  const((SUBLANE, S5_W)), const((SUBLANE, S5_W)), const((1, D_A)),
                  const((D_A, D_A)), const((1, D_A))],
        out_specs=[pl.BlockSpec((t_rows, D_A), lambda b, t: (b * nt + t, 0)),
                   pl.BlockSpec((1, 2, S5_W), lambda b, t: (b, 0, 0))],
        out_shape=[jax.ShapeDtypeStruct((nb * seq, D_A), BF16),
                   jax.ShapeDtypeStruct((nb, 2, S5_W), F32)],
        scratch_shapes=[pltpu.VMEM((t_rows, S5_W), F32), pltpu.VMEM((t_rows, S5_W), F32),
                        pltpu.VMEM((2, S5_W), F32)],
        compiler_params=_cparams(("arbitrary", "arbitrary")),
        name="s5",
    )(u_arr, h0, p["bbr"], p["bbi"], p["ccr"], p["cci"], p["apr"], p["api"], p["d"],
      p["wg"], p["bg"])


def _hgrn_kernel(q_ref, f_ref, i_ref, g_ref, s0_ref, lp_ref, nw_ref, y_ref, sl_ref, st, pscr, rscr,
                 *, l_valid, chunks):
    t = pl.program_id(1)
    c = HG_C
    nsb = c // HG_SB
    nh = HG_HEADS

    @pl.when(t == 0)
    def _():
        for h in range(nh):
            st[h] = s0_ref[0, h].T

    row = lax.broadcasted_iota(I32, (c, D_B), 0)
    tri = (lax.broadcasted_iota(I32, (c, c), 0) >= lax.broadcasted_iota(I32, (c, c), 1)).astype(BF16)
    srow = lax.broadcasted_iota(I32, (HG_SB, D_B), 0)
    rows16 = lambda r: jnp.broadcast_to(r, (HG_SB, D_B))

    def chunk(ci, carry):
        c0 = pl.multiple_of(ci * c, c)
        crows = pl.ds(c0, c)
        valid = (t * (chunks * c) + c0 + row) < l_valid
        hf = f_ref[crows, :]
        ls = _log_sigmoid(hf)
        lf = jnp.where(lp_ref[3:4, :] > 0.5, _logaddexp(lp_ref[0:1, :], lp_ref[1:2, :] + ls), ls)
        kk = lp_ref[2:3, :] * jax.nn.sigmoid(-hf)
        lf = jnp.where(valid, lf, 0.0)
        kk = jnp.where(valid, kk, 0.0)
        hq = q_ref[crows, :]
        q = hq * jax.nn.sigmoid(hq)
        v = i_ref[crows, :]
        b = _dot3(lf, tri)
        zero_row = jnp.zeros((1, D_B), F32)
        bref = [zero_row] + [b[i * HG_SB - 1:i * HG_SB] for i in range(1, nsb)]
        bend = [b[(j + 1) * HG_SB - 1:(j + 1) * HG_SB] for j in range(nsb)]
        qq = q * jnp.exp(b - jnp.concatenate([rows16(r) for r in bref], axis=0))
        kks = kk * jnp.exp(jnp.concatenate([rows16(r) for r in bend], axis=0) - b)
        lhs = []
        for j in range(nsb - 1):
            dsel = jnp.concatenate(
                [rows16(jnp.exp(bref[i] - bend[j])) if i > j else jnp.zeros((HG_SB, D_B), F32)
                 for i in range(nsb)], axis=0)
            lhs.append((qq * dsel).astype(BF16))
        rhs = [jnp.where(row // HG_SB == j, kks, 0.0).astype(BF16) for j in range(nsb - 1)]
        for i in range(nsb):
            rs = slice(i * HG_SB, (i + 1) * HG_SB)
            bi, qi, ki = b[rs], q[rs], kk[rs]
            for s in range(HG_SB):
                m = srow >= s
                e = jnp.exp(jnp.where(m, bi - bi[s:s + 1], 0.0))
                piece = jnp.where(m, (qi * ki[s:s + 1]) * e, 0.0).astype(BF16)
                for h in range(nh):
                    r0 = ((h * nsb + i) * HG_SB + s) * HG_SB
                    pscr[r0:r0 + HG_SB, :] = piece[:, h * HG_DK:(h + 1) * HG_DK]
        rscr[...] = _dot(pscr[...], jnp.ones((HG_DK, HG_DK), BF16))
        qe = (q * jnp.exp(b)).astype(BF16)
        vb = v.astype(BF16)
        bl = b[c - 1:c]
        kdec = (kk * jnp.exp(bl - b)).astype(BF16)
        ebl = jnp.exp(bl)
        gt = g_ref[crows, :]
        gate = gt * jax.nn.sigmoid(gt)
        for h in range(nh):
            hs = slice(h * HG_DK, (h + 1) * HG_DK)
            s_t = st[h]
            att = _dot_nt(jnp.concatenate([x[:, hs] for x in lhs], axis=1),
                          jnp.concatenate([x[:, hs] for x in rhs], axis=1))
            o = _dot(att.astype(BF16), vb[:, hs]) + _dot_nt(qe[:, hs], s_t.astype(BF16))
            diag = []
            for i in range(nsb):
                od = jnp.zeros((HG_SB, HG_DV), F32)
                for s in range(HG_SB):
                    r0 = ((h * nsb + i) * HG_SB + s) * HG_SB
                    od = od + rscr[r0:r0 + HG_SB, :] * v[i * HG_SB + s:i * HG_SB + s + 1, hs]
                diag.append(od)
            o = o + jnp.concatenate(diag, axis=0)
            st[h] = s_t * ebl[:, hs] + _dot_tn(vb[:, hs], kdec[:, hs])
            y_ref[crows, hs] = (_rms(o, nw_ref[...]) * gate[:, hs]).astype(BF16)
        return carry

    lax.fori_loop(0, chunks, chunk, 0)

    @pl.when(t == pl.num_programs(1) - 1)
    def _():
        for h in range(nh):
            sl_ref[0, h] = st[h].T


def _dot2g(g, z):
    hi = z.astype(BF16)
    lo = (z - hi.astype(F32)).astype(BF16)
    return _dot(g, hi) + _dot(g, lo)


def _hgrn_call(arr, cols, nb, seq_pad, l_valid, s0, lp, nw):
    chunks = 4 if seq_pad % (4 * HG_C) == 0 else 1
    rows = chunks * HG_C
    nt = seq_pad // rows
    blk = lambda col: pl.BlockSpec((rows, D_B), lambda b, t: (b * nt + t, col))
    kern = functools.partial(_hgrn_kernel, l_valid=l_valid, chunks=chunks)
    return pl.pallas_call(
        kern,
        grid=(nb, nt),
        in_specs=[blk(cols[0]), blk(cols[1]), blk(cols[2]), blk(cols[3]),
                  pl.BlockSpec((1, HG_HEADS, HG_DK, HG_DV), lambda b, t: (b, 0, 0, 0)),
                  pl.BlockSpec((SUBLANE, D_B), lambda b, t: (0, 0)),
                  pl.BlockSpec((1, HG_DV), lambda b, t: (0, 0))],
        out_specs=[pl.BlockSpec((rows, D_B), lambda b, t: (b * nt + t, 0)),
                   pl.BlockSpec((1, HG_HEADS, HG_DK, HG_DV), lambda b, t: (b, 0, 0, 0))],
        out_shape=[jax.ShapeDtypeStruct((nb * seq_pad, D_B), BF16),
                   jax.ShapeDtypeStruct((nb, HG_HEADS, HG_DK, HG_DV), F32)],
        scratch_shapes=[pltpu.VMEM((HG_HEADS, HG_DV, HG_DK), F32),
                        pltpu.VMEM((HG_HEADS * HG_C * HG_SB, HG_DK), BF16),
                        pltpu.VMEM((HG_HEADS * HG_C * HG_SB, HG_DK), F32)],
        compiler_params=_cparams(("arbitrary", "arbitrary")),
        name="hgrn",
    )(arr, arr, arr, arr, s0, lp, nw)


def _rope(x, cos_t, sin_a, sin_b):
    return (x * cos_t + pltpu.roll(x, NSA_DH - ROPE_DIM // 2, 1) * sin_a
            + pltpu.roll(x, ROPE_DIM // 2, 1) * sin_b)


def _nsa_prep_kernel(q_ref, kc_ref, ks_ref, kw_ref, g_ref, cos_ref, sa_ref, sb_ref, qn_ref, kn_ref,
                     qo_ref, rc_ref, rs_ref, rw_ref, bc_ref, bs_ref, bw_ref, go_ref):
    cos_t, sin_a, sin_b = cos_ref[...], sa_ref[...], sb_ref[...]
    qn = qn_ref[...]
    for h in range(NSA_HEADS):
        hs = slice(h * NSA_DH, (h + 1) * NSA_DH)
        qo_ref[:, hs] = _rope(_rms(q_ref[:, hs], qn), cos_t, sin_a, sin_b).astype(BF16)
    for br, (src, dst, dstb) in enumerate(((kc_ref, rc_ref, bc_ref), (ks_ref, rs_ref, bs_ref),
                                           (kw_ref, rw_ref, bw_ref))):
        kn = kn_ref[br:br + 1, :]
        tm = src.shape[0]
        for g in range(NSA_KVH):
            gs = slice(g * NSA_DH, (g + 1) * NSA_DH)
            k = _rope(_rms(src[:, gs], kn), cos_t, sin_a, sin_b)
            dst[pl.ds(g, tm, stride=KV_TILES), :] = k
            dstb[:, gs] = k.astype(BF16)
            vs = slice(N_KV + g * NSA_DH, N_KV + (g + 1) * NSA_DH)
            v = src[:, vs]
            dst[pl.ds(NSA_KVH + g, tm, stride=KV_TILES), :] = v
            dstb[:, vs] = v.astype(BF16)
    go_ref[...] = jax.nn.sigmoid(g_ref[...])


def _nsa_prep_call(hin, rope_tabs, seq, qn, kn):
    m = hin.shape[0]
    tm = min(256, m)
    npos = rope_tabs[0].shape[0] // tm
    rspec = pl.BlockSpec((tm, NSA_DH), lambda i: (i % npos, 0))
    kvb = OFF_NKV // (2 * N_KV)
    return pl.pallas_call(
        _nsa_prep_kernel,
        grid=(m // tm,),
        in_specs=[pl.BlockSpec((tm, D_C), lambda i: (i, OFF_NQ // D_C)),
                  pl.BlockSpec((tm, 2 * N_KV), lambda i: (i, kvb)),
                  pl.BlockSpec((tm, 2 * N_KV), lambda i: (i, kvb + 1)),
                  pl.BlockSpec((tm, 2 * N_KV), lambda i: (i, kvb + 2)),
                  pl.BlockSpec((tm, LANE), lambda i: (i, OFF_NGT // LANE)),
                  rspec, rspec, rspec,
                  pl.BlockSpec((1, NSA_DH), lambda i: (0, 0)),
                  pl.BlockSpec((3, NSA_DH), lambda i: (0, 0))],
        out_specs=[pl.BlockSpec((tm, D_C), lambda i: (i, 0))]
        + [pl.BlockSpec((tm * KV_TILES, LANE), lambda i: (i, 0))] * 3
        + [pl.BlockSpec((tm, 2 * N_KV), lambda i: (i, 0))] * 3
        + [pl.BlockSpec((tm, LANE), lambda i: (i, 0))],
        out_shape=[jax.ShapeDtypeStruct((m, D_C), BF16)]
        + [jax.ShapeDtypeStruct((m * KV_TILES, LANE), F32)] * 3
        + [jax.ShapeDtypeStruct((m, 2 * N_KV), BF16)] * 3
        + [jax.ShapeDtypeStruct((m, LANE), F32)],
        compiler_params=_cparams(("arbitrary",)),
        name="nsa_prep",
    )(hin, hin, hin, hin, hin, *rope_tabs, qn, kn)


KV_TILES = 2 * N_KV // LANE


def _pool_rows(load, n, a1_ref, a2_ref):
    firsts, seconds = [], []
    for c in range(KV_TILES):
        ls = slice(c * LANE, (c + 1) * LANE)
        first = jnp.zeros((n, LANE), F32)
        second = jnp.zeros((n, LANE), F32)
        for j in range(CMP_STRIDE):
            x = load(j, c)
            first = first + x * a1_ref[j:j + 1, ls]
            second = second + x * a2_ref[j:j + 1, ls]
        firsts.append(first)
        seconds.append(second)
    return firsts, seconds


def _tile_rows(j, c, n):
    return pl.ds(j * KV_TILES + c, n, stride=CMP_STRIDE * KV_TILES)


def _cmp_mlp(first, second, n, w1_ref, b1_ref, w2_ref, o_ref):
    for c in range(2):
        for g in range(NSA_KVH):
            tile = 2 * c + g
            pooled = first[tile] + pltpu.roll(second[tile], n - 1, 0)
            h = _dot(pooled.astype(BF16), w1_ref[c]) + b1_ref[c:c + 1, :]
            o_ref[0, :, tile * NSA_DH:(tile + 1) * NSA_DH] = _dot(
                jax.nn.gelu(h).astype(BF16), w2_ref[c]).astype(BF16)


def _cmp_prompt_kernel(x_ref, a1_ref, a2_ref, w1_ref, b1_ref, w2_ref, o_ref, *, n):
    first, second = _pool_rows(lambda j, c: x_ref[_tile_rows(j, c, n), :], n, a1_ref, a2_ref)
    _cmp_mlp(first, second, n, w1_ref, b1_ref, w2_ref, o_ref)


def _cmp_prompt_call(rows, nb, seq, cp):
    n = seq // CMP_STRIDE
    const = lambda shape: pl.BlockSpec(shape, lambda b: (0,) * len(shape))
    return pl.pallas_call(
        functools.partial(_cmp_prompt_kernel, n=n),
        grid=(nb,),
        in_specs=[pl.BlockSpec((seq * KV_TILES, LANE), lambda b: (b, 0)),
                  const((CMP_STRIDE, 2 * N_KV)), const((CMP_STRIDE, 2 * N_KV)),
                  const((2, NSA_DH, NSA_DH)), const((2, NSA_DH)), const((2, NSA_DH, NSA_DH))],
        out_specs=pl.BlockSpec((1, n, 2 * N_KV), lambda b: (b, 0, 0)),
        out_shape=jax.ShapeDtypeStruct((nb, n, 2 * N_KV), BF16),
        compiler_params=_cparams(("arbitrary",)),
        name="cmp_prompt",
    )(rows.reshape(-1, LANE), cp["a1"], cp["a2"], cp["w1"], cp["b1"], cp["w2"])


PG = 16
SUB_PER_PAGE = PAGE // CMP_STRIDE


SUB_ROWS = CMP_STRIDE * KV_TILES


def _cmp_pool_cache_kernel(pt_ref, *refs):
    pages = refs[:PG]
    a1_ref, a2_ref, f_ref, s_ref = refs[PG:]
    nv = SUB_ROWS // SUBLANE
    w1 = [a1_ref[0, v * SUBLANE:(v + 1) * SUBLANE, :] for v in range(nv)]
    w2 = [a2_ref[0, v * SUBLANE:(v + 1) * SUBLANE, :] for v in range(nv)]
    low = lax.broadcasted_iota(I32, (SUBLANE, LANE), 0) < KV_TILES

    def fold(acc):
        return acc + pltpu.roll(acc, KV_TILES, 0)

    for k in range(PG):
        for n2 in range(SUB_PER_PAGE // 2):
            halves = []
            for n in (2 * n2, 2 * n2 + 1):
                a1 = a2 = None
                for v in range(nv):
                    x = pages[k][0, 0, n * SUB_ROWS + v * SUBLANE:n * SUB_ROWS + (v + 1) * SUBLANE, :]
                    a1 = x * w1[v] if a1 is None else a1 + x * w1[v]
                    a2 = x * w2[v] if a2 is None else a2 + x * w2[v]
                halves.append((fold(a1), fold(a2)))
            r0 = (k * SUB_PER_PAGE + 2 * n2) * KV_TILES
            f_ref[0, 0, r0:r0 + SUBLANE, :] = jnp.where(low, halves[0][0], halves[1][0])
            s_ref[0, 0, r0:r0 + SUBLANE, :] = jnp.where(low, halves[0][1], halves[1][1])


def _page_specs(n_pages, layer_of, batch_of, group_of, page_shape=(PAGE, 2 * N_KV)):
    def spec(k):
        return pl.BlockSpec(
            (1, 1) + page_shape,
            lambda *a, k=k: (layer_of(*a), a[-1][batch_of(*a) * n_pages + group_of(*a) * PG + k], 0, 0))
    return [spec(k) for k in range(PG)]


def _cmp_pool_cache_call(cache, pt_flat, nb, n_pages, a1, a2):
    depth = cache.shape[0]
    npg = n_pages // PG
    nsub = n_pages * SUB_PER_PAGE
    rows = PG * SUB_PER_PAGE
    grid_spec = pltpu.PrefetchScalarGridSpec(
        num_scalar_prefetch=1,
        grid=(depth, nb, npg),
        in_specs=_page_specs(n_pages, lambda l, b, g, pt: l, lambda l, b, g, pt: b,
                             lambda l, b, g, pt: g, (PAGE * KV_TILES, LANE))
        + [pl.BlockSpec((1, SUB_ROWS, LANE), lambda l, b, g, pt: (l, 0, 0))] * 2,
        out_specs=[pl.BlockSpec((1, 1, rows * KV_TILES, LANE), lambda l, b, g, pt: (l, b, g, 0))] * 2,
    )
    return pl.pallas_call(
        _cmp_pool_cache_kernel,
        grid_spec=grid_spec,
        out_shape=[jax.ShapeDtypeStruct((depth, nb, nsub * KV_TILES, LANE), F32)] * 2,
        compiler_params=_cparams(("arbitrary", "arbitrary", "arbitrary")),
        name="cmp_pool_cache",
    )(pt_flat, *([cache] * PG), a1, a2)


def _cmp_mlp_cache_kernel(f_ref, s_ref, w1_ref, b1_ref, w2_ref, o_ref, *, n):
    tiles = lambda ref: [ref[0, 0, pl.ds(c, n, stride=KV_TILES), :] for c in range(KV_TILES)]
    _cmp_mlp(tiles(f_ref), tiles(s_ref), n, w1_ref.at[0], b1_ref.at[0], w2_ref.at[0], o_ref.at[0])


def _cmp_mlp_cache_call(first, second, w1, b1, w2):
    depth, nb, rows, _ = first.shape
    n = rows // KV_TILES
    blk = pl.BlockSpec((1, 1, n, 2 * N_KV), lambda l, b: (l, b, 0, 0))
    tblk = pl.BlockSpec((1, 1, rows, LANE), lambda l, b: (l, b, 0, 0))
    return pl.pallas_call(
        functools.partial(_cmp_mlp_cache_kernel, n=n),
        grid=(depth, nb),
        in_specs=[tblk, tblk,
                  pl.BlockSpec((1, 2, NSA_DH, NSA_DH), lambda l, b: (l, 0, 0, 0)),
                  pl.BlockSpec((1, 2, NSA_DH), lambda l, b: (l, 0, 0)),
                  pl.BlockSpec((1, 2, NSA_DH, NSA_DH), lambda l, b: (l, 0, 0, 0))],
        out_specs=blk,
        out_shape=jax.ShapeDtypeStruct((depth, nb, n, 2 * N_KV), BF16),
        compiler_params=_cparams(("arbitrary", "arbitrary")),
        name="cmp_mlp_cache",
    )(first, second, w1, b1, w2)


def _block_scores(imp, qpos, n_slc):
    blk = lax.broadcasted_iota(I32, imp.shape, 1)
    cur = qpos // SEL_BLOCK
    forced = (blk == 0) | (blk == cur) | (blk == cur - 1)
    valid = blk * SEL_BLOCK <= qpos
    score = jnp.where(valid, imp + jnp.where(forced, FORCE_BONUS, 0.0), -1.0)
    return jnp.where(blk < n_slc, score, -2.0)


def _select_blocks_ranked(imp, qpos, n_slc):
    nb8 = (n_slc + SUBLANE - 1) // SUBLANE * SUBLANE
    st = _block_scores(imp, qpos, n_slc).T[:nb8]
    ridx = lax.broadcasted_iota(I32, st.shape, 0)
    cnt = jnp.zeros(st.shape, F32)
    for i in range(n_slc):
        ri = st[i:i + 1]
        cnt = cnt + jnp.where(ri > st, 1.0, jnp.where(ri == st, jnp.where(ridx > i, 1.0, 0.0), 0.0))
    sel_t = jnp.where((cnt < min(N_SEL, n_slc)) & (ridx < n_slc), 1.0, 0.0)
    if nb8 < LANE:
        sel_t = jnp.concatenate([sel_t, jnp.zeros((LANE - nb8, st.shape[1]), F32)], axis=0)
    return sel_t.T


def _select_blocks(imp, qpos, n_slc):
    blk = lax.broadcasted_iota(I32, imp.shape, 1)
    score = _block_scores(imp, qpos, n_slc)
    sel = jnp.zeros(imp.shape, F32)
    for _ in range(min(N_SEL, n_slc)):
        mx = jnp.max(score, axis=-1, keepdims=True)
        idx = jnp.min(jnp.where(score == mx, blk, 1 << 30), axis=-1, keepdims=True)
        hit = blk == idx
        sel = jnp.where(hit, 1.0, sel)
        score = jnp.where(hit, -3.0, score)
    return sel


def _cmp_branch(q, kck, kcv, qpos, n_cmp):
    s = _dot_nt(q, kck) * (NSA_DH ** -0.5)
    col = lax.broadcasted_iota(I32, s.shape, 1)
    mask = (col * CMP_STRIDE + (2 * CMP_STRIDE - 1) <= qpos) & (col < n_cmp)
    p = _masked_softmax(s, mask)
    return p, _dot(p.astype(BF16), kcv)


NSA_KC = 1024


def _nsa_prompt_kernel(q_ref, kc_ref, ks_ref, vs_ref, kw_ref, vw_ref, g_ref, o_ref,
                       bias_scr, m_scr, l_scr, acc_scr, *, tq, seq, slab):
    q0 = pl.program_id(1) * tq
    rows = NSA_REP * tq
    ncp = seq // CMP_STRIDE
    n_slc = seq // SEL_BLOCK
    scale = NSA_DH ** -0.5
    gates = g_ref[...]
    q4s = [jnp.concatenate([q_ref[:, (g * NSA_REP + r) * NSA_DH:(g * NSA_REP + r + 1) * NSA_DH]
                            for r in range(NSA_REP)], axis=0) for g in range(NSA_KVH)]
    qpos_c = q0 + (lax.broadcasted_iota(I32, (rows, ncp), 0) & (tq - 1))
    o_cs, psums = [], []
    for g in range(NSA_KVH):
        p_c, o_c = _cmp_branch(q4s[g], kc_ref[0, :, g * NSA_DH:(g + 1) * NSA_DH],
                               kc_ref[0, :, N_KV + g * NSA_DH:N_KV + (g + 1) * NSA_DH], qpos_c, ncp - 1)
        psum = p_c[0:tq]
        for r in range(1, NSA_REP):
            psum = psum + p_c[r * tq:(r + 1) * tq]
        o_cs.append(o_c)
        psums.append(psum)

    def ranked():
        gmat = (lax.broadcasted_iota(I32, (ncp, LANE), 0) // (SEL_BLOCK // CMP_STRIDE)
                == lax.broadcasted_iota(I32, (ncp, LANE), 1)).astype(BF16)
        imp = _dot2(jnp.concatenate(psums, axis=0), gmat)
        qp = q0 + (lax.broadcasted_iota(I32, (NSA_KVH * tq, LANE), 0) & (tq - 1))
        return _select_blocks_ranked(imp, qp, n_slc)

    sel = lax.cond(q0 + tq > N_SEL * SEL_BLOCK, ranked,
                   lambda: jnp.ones((NSA_KVH * tq, LANE), F32))
    emat = (lax.broadcasted_iota(I32, (LANE, seq), 1) // SEL_BLOCK
            == lax.broadcasted_iota(I32, (LANE, seq), 0)).astype(BF16)
    nk = (q0 + tq + NSA_KC - 1) // NSA_KC
    s0 = pl.multiple_of(jnp.maximum(q0 + tq - slab, 0), SUBLANE)
    key = lax.broadcasted_iota(I32, (tq, seq), 1)
    causal = key <= q0 + lax.broadcasted_iota(I32, (tq, seq), 0)
    wp = s0 + lax.broadcasted_iota(I32, (tq, slab), 1)
    qpw = q0 + lax.broadcasted_iota(I32, (tq, slab), 0)
    wb1 = jnp.where((wp <= qpw) & (wp > qpw - WINDOW), 0.0, NEG_INF)

    def add_bias(s, bias):
        return jnp.concatenate([s[r * tq:(r + 1) * tq] + bias for r in range(NSA_REP)], axis=0)

    for g in range(NSA_KVH):
        keep = (_dot(sel[g * tq:(g + 1) * tq].astype(BF16), emat) > 0.5) & causal
        bias = jnp.where(keep, 0.0, NEG_INF)
        for c in range(seq // NSA_KC):
            bias_scr[g, c] = bias[:, c * NSA_KC:(c + 1) * NSA_KC]
    m_scr[...] = jnp.full(m_scr.shape, NEG_INF, F32)
    l_scr[...] = jnp.zeros(l_scr.shape, F32)
    acc_scr[...] = jnp.zeros(acc_scr.shape, F32)

    def body(c, carry):
        k0 = pl.multiple_of(c * NSA_KC, NSA_KC)
        for g in range(NSA_KVH):
            gs = slice(g * NSA_DH, (g + 1) * NSA_DH)
            s = add_bias(_dot_nt(q4s[g], ks_ref[pl.ds(k0, NSA_KC), gs]) * scale, bias_scr[g, c])
            m_old = m_scr[g]
            m_new = jnp.maximum(m_old, _row_reduce(s, jnp.maximum, jnp.max))
            p = jnp.exp(s - m_new)
            alpha = jnp.exp(m_old - m_new)
            l_scr[g] = alpha * l_scr[g] + _row_reduce(p, jnp.add, jnp.sum)
            acc_scr[g] = alpha * acc_scr[g] + _dot(p.astype(BF16), vs_ref[pl.ds(k0, NSA_KC), gs])
            m_scr[g] = m_new
        return carry

    lax.fori_loop(0, nk, body, 0)
    for g in range(NSA_KVH):
        gs = slice(g * NSA_DH, (g + 1) * NSA_DH)
        q4 = q4s[g]
        o_s = acc_scr[g] / l_scr[g]
        s_w = add_bias(_dot_nt(q4, kw_ref[pl.ds(s0, slab), gs]) * scale, wb1)
        p_w = jnp.exp(s_w - _row_reduce(s_w, jnp.maximum, jnp.max))
        o_w = (_dot(p_w.astype(BF16), vw_ref[pl.ds(s0, slab), gs])
               / _row_reduce(p_w, jnp.add, jnp.sum))
        o_c = o_cs[g]
        for r in range(NSA_REP):
            h = g * NSA_REP + r
            rs = slice(r * tq, (r + 1) * tq)
            o = (gates[:, 3 * h:3 * h + 1] * o_c[rs] + gates[:, 3 * h + 1:3 * h + 2] * o_s[rs]
                 + gates[:, 3 * h + 2:3 * h + 3] * o_w[rs])
            o_ref[:, h * NSA_DH:(h + 1) * NSA_DH] = o.astype(BF16)


def _nsa_prompt_call(qn, kc, bs, bw, gates, nb, seq):
    tq = 128
    nq = seq // tq
    slab = min(seq, WINDOW + tq)
    kv = lambda col: pl.BlockSpec((seq, N_KV), lambda b, i: (b, col))
    kern = functools.partial(_nsa_prompt_kernel, tq=tq, seq=seq, slab=slab)
    rows = NSA_REP * tq
    return pl.pallas_call(
        kern,
        grid=(nb, nq),
        in_specs=[pl.BlockSpec((tq, D_C), lambda b, i: (b * nq + i, 0)),
                  pl.BlockSpec((1, seq // CMP_STRIDE, 2 * N_KV), lambda b, i: (b, 0, 0)),
                  kv(0), kv(1), kv(0), kv(1),
                  pl.BlockSpec((tq, LANE), lambda b, i: (b * nq + i, 0))],
        out_specs=pl.BlockSpec((tq, D_C), lambda b, i: (b * nq + i, 0)),
        out_shape=jax.ShapeDtypeStruct((nb * seq, D_C), BF16),
        scratch_shapes=[pltpu.VMEM((NSA_KVH, seq // NSA_KC, tq, NSA_KC), F32),
                        pltpu.VMEM((NSA_KVH, rows, 1), F32), pltpu.VMEM((NSA_KVH, rows, 1), F32),
                        pltpu.VMEM((NSA_KVH, rows, NSA_DH), F32)],
        compiler_params=_cparams(("arbitrary", "arbitrary")),
        name="nsa_prompt",
    )(qn, kc, bs, bs, bw, bw, gates)


def _nsa_s_sel_kernel(q_ref, kc_ref, oc_ref, sel_ref, *, past, dl, n_tiles):
    rows = NSA_REP * dl
    ncp = past // CMP_STRIDE
    n_slc = (past + dl + SEL_BLOCK - 1) // SEL_BLOCK
    nbp = (n_slc + LANE - 1) // LANE * LANE
    per_tile = PG * PAGE // SEL_BLOCK
    gmat = (lax.broadcasted_iota(I32, (ncp, nbp), 0) // (SEL_BLOCK // CMP_STRIDE)
            == lax.broadcasted_iota(I32, (ncp, nbp), 1)).astype(BF16)
    rmat = (lax.broadcasted_iota(I32, (rows, rows), 0) // NSA_REP
            == lax.broadcasted_iota(I32, (rows, rows), 1) // NSA_REP).astype(BF16)

    def qpos_of(shape):
        return past + lax.broadcasted_iota(I32, shape, 0) // NSA_REP

    lane = lax.broadcasted_iota(I32, (rows, LANE), 1)
    for g in range(NSA_KVH):
        gs = slice(g * NSA_DH, (g + 1) * NSA_DH)
        p_c, o_c = _cmp_branch(q_ref[0, g], kc_ref[0, 0, :, gs],
                               kc_ref[0, 0, :, N_KV + g * NSA_DH:N_KV + (g + 1) * NSA_DH],
                               qpos_of((rows, ncp)), ncp - 1)
        oc_ref[0, g] = o_c
        psum = _dot2g(rmat, p_c)
        imp = _dot2(psum, gmat)
        sel = _select_blocks(imp, qpos_of((rows, nbp)), n_slc)
        for tl in range(n_tiles):
            lo = tl * per_tile
            chunk = sel[:, (lo // LANE) * LANE:(lo // LANE + 1) * LANE]
            sh = (LANE - lo % LANE) % LANE
            if sh:
                chunk = pltpu.roll(chunk, sh, 1)
            sel_ref[0, g, tl] = jnp.where(lane < per_tile, chunk, 0.0)


def _nsa_s_sel_call(q16, kc, layer, past, dl):
    nb = q16.shape[0]
    rows = NSA_REP * dl
    n_tiles = past // (PG * PAGE) + 1
    kern = functools.partial(_nsa_s_sel_kernel, past=past, dl=dl, n_tiles=n_tiles)
    return pl.pallas_call(
        kern,
        grid=(nb,),
        in_specs=[pl.BlockSpec((1, NSA_KVH, rows, NSA_DH), lambda b: (b, 0, 0, 0)),
                  pl.BlockSpec((1, 1, past // CMP_STRIDE, 2 * N_KV), lambda b: (layer, b, 0, 0))],
        out_specs=[pl.BlockSpec((1, NSA_KVH, rows, NSA_DH), lambda b: (b, 0, 0, 0)),
                   pl.BlockSpec((1, NSA_KVH, n_tiles, rows, LANE), lambda b: (b, 0, 0, 0, 0))],
        out_shape=[jax.ShapeDtypeStruct((nb, NSA_KVH, rows, NSA_DH), F32),
                   jax.ShapeDtypeStruct((nb, NSA_KVH, n_tiles, rows, LANE), F32)],
        compiler_params=_cparams(("arbitrary",)),
        name="nsa_sample_select",
    )(q16, kc)


def _nsa_s_attn_kernel(pt_ref, *refs, past, dl, npg):
    pages = refs[:PG]
    (q_ref, selt_ref, sell_ref, emat_ref, oc_ref, ns_ref, nw_ref, wp_ref, g_ref,
     o_ref, m_scr, l_scr, a_scr) = refs[PG:]
    pg = pl.program_id(1)
    rows = NSA_REP * dl
    scale = NSA_DH ** -0.5
    wb = wp_ref.shape[2] // KV_TILES

    def tok_rows(ref, c, n):
        return ref[0, 0, pl.ds(c, n, stride=KV_TILES), :].astype(BF16)

    @pl.when(pg == 0)
    def _():
        m_scr[...] = jnp.full(m_scr.shape, NEG_INF, F32)
        l_scr[...] = jnp.zeros(l_scr.shape, F32)
        a_scr[...] = jnp.zeros(a_scr.shape, F32)

    for g in range(NSA_KVH):
        q = q_ref[0, g]
        kmask = _dot(selt_ref[0, g, 0].astype(BF16), emat_ref[...]) > 0.5
        s = jnp.concatenate(
            [_dot_nt(q, tok_rows(pages[k], g, PAGE)) for k in range(PG)], axis=1) * scale
        s = jnp.where(kmask, s, NEG_INF)
        m_old = m_scr[g]
        m_new = jnp.maximum(m_old, jnp.max(s, axis=-1, keepdims=True))
        p = jnp.where(kmask, jnp.exp(s - m_new), 0.0)
        alpha = jnp.exp(m_old - m_new)
        l_scr[g] = alpha * l_scr[g] + jnp.sum(p, axis=-1, keepdims=True)
        acc = alpha * a_scr[g]
        pb = p.astype(BF16)
        for k in range(PG):
            acc = acc + _dot(pb[:, k * PAGE:(k + 1) * PAGE], tok_rows(pages[k], NSA_KVH + g, PAGE))
        a_scr[g] = acc
        m_scr[g] = m_new

    @pl.when(pg == npg - 1)
    def _():
        gates = g_ref[0]
        tok = lax.broadcasted_iota(I32, (rows, SUBLANE), 0) // NSA_REP
        ncol = lax.broadcasted_iota(I32, (rows, SUBLANE), 1)
        for g in range(NSA_KVH):
            gs = slice(g * NSA_DH, (g + 1) * NSA_DH)
            vsl = slice(N_KV + g * NSA_DH, N_KV + (g + 1) * NSA_DH)
            q = q_ref[0, g]
            nmask = (sell_ref[0, g, 0][:, 0:1] > 0.5) & (ncol <= tok) & (ncol < dl)
            s_n = jnp.where(nmask, _dot_nt(q, ns_ref[0, :, gs].astype(BF16)) * scale, NEG_INF)
            m_old = m_scr[g]
            m_new = jnp.maximum(m_old, jnp.max(s_n, axis=-1, keepdims=True))
            p_n = jnp.where(nmask, jnp.exp(s_n - m_new), 0.0)
            alpha = jnp.exp(m_old - m_new)
            lsum = alpha * l_scr[g] + jnp.sum(p_n, axis=-1, keepdims=True)
            acc = alpha * a_scr[g] + _dot(p_n.astype(BF16), ns_ref[0, :, vsl].astype(BF16))
            o_s = acc / jnp.maximum(lsum, 1e-30)
            wrow = lax.broadcasted_iota(I32, (rows, wb), 1)
            wtok = lax.broadcasted_iota(I32, (rows, wb), 0) // NSA_REP
            pmask = (wrow > wtok + (wb - WINDOW)) & (wrow + (past - wb) >= 0)
            s_p = jnp.where(pmask, _dot_nt(q, tok_rows(wp_ref, g, wb)) * scale, NEG_INF)
            wmask = (ncol <= tok) & (ncol < dl)
            s_q = jnp.where(wmask, _dot_nt(q, nw_ref[0, :, gs].astype(BF16)) * scale, NEG_INF)
            mw = jnp.maximum(jnp.max(s_p, axis=-1, keepdims=True), jnp.max(s_q, axis=-1, keepdims=True))
            p_p = jnp.where(pmask, jnp.exp(s_p - mw), 0.0)
            p_q = jnp.where(wmask, jnp.exp(s_q - mw), 0.0)
            den = jnp.maximum(jnp.sum(p_p, axis=-1, keepdims=True) + jnp.sum(p_q, axis=-1, keepdims=True),
                              1e-30)
            o_w = (_dot(p_p.astype(BF16), tok_rows(wp_ref, NSA_KVH + g, wb))
                   + _dot(p_q.astype(BF16), nw_ref[0, :, vsl].astype(BF16))) / den
            gg = gates[g]
            o = gg[:, 0:1] * oc_ref[0, g] + gg[:, 1:2] * o_s + gg[:, 2:3] * o_w
            o_ref[0, g] = o.astype(BF16)


def _nsa_s_attn_call(cache, pt_flat, layer, q16, selt, emat, o_c, new_slc, new_win, win_prev, gates16,
                     past, dl):
    nb = q16.shape[0]
    rows = NSA_REP * dl
    n_pages = past // PAGE
    npg = n_pages // PG
    wrows = win_prev.shape[2]
    b4 = lambda shape: pl.BlockSpec((1,) + shape, lambda b, g, pt: (b,) + (0,) * len(shape))
    grid_spec = pltpu.PrefetchScalarGridSpec(
        num_scalar_prefetch=1,
        grid=(nb, npg),
        in_specs=_page_specs(n_pages, lambda b, g, pt: layer, lambda b, g, pt: b, lambda b, g, pt: g,
                             (PAGE * KV_TILES, LANE))
        + [b4((NSA_KVH, rows, NSA_DH)),
           pl.BlockSpec((1, NSA_KVH, 1, rows, LANE), lambda b, g, pt: (b, 0, g, 0, 0)),
           pl.BlockSpec((1, NSA_KVH, 1, rows, LANE), lambda b, g, pt: (b, 0, npg, 0, 0)),
           pl.BlockSpec((LANE, PG * PAGE), lambda b, g, pt: (0, 0)),
           b4((NSA_KVH, rows, NSA_DH)),
           b4((SUBLANE, 2 * N_KV)), b4((SUBLANE, 2 * N_KV)),
           pl.BlockSpec((1, 1, wrows, LANE), lambda b, g, pt: (layer, b, 0, 0)),
           b4((NSA_KVH, rows, LANE))],
        out_specs=b4((NSA_KVH, rows, NSA_DH)),
        scratch_shapes=[pltpu.VMEM((NSA_KVH, rows, 1), F32), pltpu.VMEM((NSA_KVH, rows, 1), F32),
                        pltpu.VMEM((NSA_KVH, rows, NSA_DH), F32)],
    )
    kern = functools.partial(_nsa_s_attn_kernel, past=past, dl=dl, npg=npg)
    return pl.pallas_call(
        kern,
        grid_spec=grid_spec,
        out_shape=jax.ShapeDtypeStruct((nb, NSA_KVH, rows, NSA_DH), BF16),
        compiler_params=_cparams(("arbitrary", "arbitrary")),
        name="nsa_sample_attn",
    )(pt_flat, *([cache] * PG), q16, selt, selt, emat, o_c, new_slc, new_win, win_prev, gates16)


def _merge_kernel(ya_ref, yb_ref, yc_ref, ga_ref, gb_ref, gc_ref, wa_ref, wb_ref, wc_ref, o_ref):
    m = (jax.nn.sigmoid(ga_ref[...]) * _dot(ya_ref[...], wa_ref[0])
         + jax.nn.sigmoid(gb_ref[...]) * _dot(yb_ref[...], wb_ref[0])
         + jax.nn.sigmoid(gc_ref[...]) * _dot(yc_ref[...], wc_ref[0]))
    o_ref[...] = m.astype(BF16)


def _merge_call(ya, yb, yc, hin, wa, wb, wc, layer):
    m = ya.shape[0]
    tm = min(256, m)
    const = lambda shape: pl.BlockSpec((1,) + shape, lambda i: (layer, 0, 0))
    gate = lambda col: pl.BlockSpec((tm, D_MODEL), lambda i: (i, col))
    return pl.pallas_call(
        _merge_kernel,
        grid=(m // tm,),
        in_specs=[pl.BlockSpec((tm, D_A), lambda i: (i, 0)), pl.BlockSpec((tm, D_B), lambda i: (i, 0)),
                  pl.BlockSpec((tm, D_C), lambda i: (i, 0)), gate(0), gate(1), gate(2),
                  const((D_A, D_MODEL)), const((D_B, D_MODEL)), const((D_C, D_MODEL))],
        out_specs=pl.BlockSpec((tm, D_MODEL), lambda i: (i, 0)),
        out_shape=jax.ShapeDtypeStruct((m, D_MODEL), BF16),
        compiler_params=_cparams(("arbitrary",)),
        name="merge",
    )(ya, yb, yc, hin, hin, hin, wa, wb, wc)


def _outproj_kernel(m_ref, x_ref, g_ref, w_ref, o_ref):
    o_ref[...] = x_ref[...] + g_ref[0] * _dot(m_ref[...], w_ref[0])


def _outproj_call(merged, x, modx, rows_per_group, w, layer):
    m = x.shape[0]
    tm = min(512, m)
    return pl.pallas_call(
        _outproj_kernel,
        grid=(m // tm,),
        in_specs=[pl.BlockSpec((tm, D_MODEL), lambda i: (i, 0)),
                  pl.BlockSpec((tm, D_MODEL), lambda i: (i, 0)),
                  _mod_spec(modx, tm, rows_per_group, 2),
                  pl.BlockSpec((1, D_MODEL, D_MODEL), lambda i: (layer, 0, 0))],
        out_specs=pl.BlockSpec((tm, D_MODEL), lambda i: (i, 0)),
        out_shape=jax.ShapeDtypeStruct((m, D_MODEL), F32),
        compiler_params=_cparams(("arbitrary",)),
        name="outproj",
    )(merged, x, modx, w)


def _ffn_tail(ua, ua1, ua2, ub, cw_ref, cb_ref, wd_ref):
    cw = cw_ref[0]
    conv = cb_ref[0] + cw[0:1] * ua2 + cw[1:2] * ua1 + cw[2:3] * ua
    return _dot((jax.nn.gelu(conv) * ub).astype(BF16), wd_ref[0])


def _ffn_prompt_kernel(x_ref, xh_ref, sc_ref, sh_ref, g_ref, nw_ref, cp_ref, wua_ref, wub_ref, cw_ref,
                       cb_ref, wd_ref, o_ref, tail_ref, h_scr, hh_scr, acc, *, tm, seq):
    f = pl.program_id(1)

    @pl.when(f == 0)
    def _():
        nw, sc, sh = nw_ref[...], sc_ref[0], sh_ref[0]
        h_scr[...] = (_rms(x_ref[...], nw) * (1.0 + sc) + sh).astype(BF16)
        hh_scr[...] = (_rms(xh_ref[...], nw) * (1.0 + sc) + sh).astype(BF16)
        acc[...] = jnp.zeros(acc.shape, F32)

    wua = wua_ref[0]
    ua = _dot(h_scr[...], wua)
    ub = _dot(h_scr[...], wub_ref[0])
    at_start = (pl.program_id(0) * tm) % seq == 0
    prev = jnp.where(at_start, cp_ref[0], _dot(hh_scr[...], wua))
    row = lax.broadcasted_iota(I32, ua.shape, 0)
    p1, p2 = prev[SUBLANE - 1:SUBLANE], prev[SUBLANE - 2:SUBLANE - 1]
    ua1 = jnp.where(row == 0, p1, pltpu.roll(ua, 1, 0))
    ua2 = jnp.where(row == 0, p2, jnp.where(row == 1, p1, pltpu.roll(ua, 2, 0)))
    tail_ref[0] = ua[tm - SUBLANE:tm]
    acc[...] += _ffn_tail(ua, ua1, ua2, ub, cw_ref, cb_ref, wd_ref)

    @pl.when(f == pl.num_programs(1) - 1)
    def _():
        o_ref[...] = x_ref[...] + g_ref[0] * acc[...]


def _ffn_prompt_call(x, modx, seq, nw, cprev, fw, layer):
    m = x.shape[0]
    tm = min(512, m)
    hb = tm // SUBLANE
    kern = functools.partial(_ffn_prompt_kernel, tm=tm, seq=seq)
    return pl.pallas_call(
        kern,
        grid=(m // tm, D_FFP // TF),
        in_specs=[pl.BlockSpec((tm, D_MODEL), lambda i, f: (i, 0)),
                  pl.BlockSpec((SUBLANE, D_MODEL), lambda i, f: (jnp.maximum(i * hb - 1, 0), 0)),
                  _mod_spec(modx, tm, seq, 4), _mod_spec(modx, tm, seq, 3), _mod_spec(modx, tm, seq, 5),
                  pl.BlockSpec((1, D_MODEL), lambda i, f: (0, 0)),
                  pl.BlockSpec((1, SUBLANE, TF), lambda i, f: ((i * tm) // seq, 0, f)),
                  pl.BlockSpec((1, D_MODEL, TF), lambda i, f: (layer, 0, f)),
                  pl.BlockSpec((1, D_MODEL, TF), lambda i, f: (layer, 0, D_FFP // TF + f)),
                  pl.BlockSpec((1, CONV_W, TF), lambda i, f: (layer, 0, f)),
                  pl.BlockSpec((1, 1, TF), lambda i, f: (layer, 0, f)),
                  pl.BlockSpec((1, TF, D_MODEL), lambda i, f: (layer, f, 0))],
        out_specs=[pl.BlockSpec((tm, D_MODEL), lambda i, f: (i, 0)),
                   pl.BlockSpec((1, SUBLANE, TF), lambda i, f: (i, 0, f))],
        out_shape=[jax.ShapeDtypeStruct((m, D_MODEL), F32),
                   jax.ShapeDtypeStruct((m // tm, SUBLANE, D_FFP), F32)],
        scratch_shapes=[pltpu.VMEM((tm, D_MODEL), BF16), pltpu.VMEM((SUBLANE, D_MODEL), BF16),
                        pltpu.VMEM((tm, D_MODEL), F32)],
        compiler_params=_cparams(("arbitrary", "arbitrary")),
        name="ffn_prompt",
    )(x, x, modx, modx, modx, nw, cprev, fw["wu"], fw["wu"], fw["cw"], fw["cb"], fw["wd"])


def _ffn_sample_kernel(x_ref, sc_ref, sh_ref, g_ref, nw_ref, p1_ref, p2_ref, wua_ref, wub_ref, cw_ref,
                       cb_ref, wd_ref, o_ref, ua_ref, h_scr, acc, *, dl):
    f = pl.program_id(0)

    @pl.when(f == 0)
    def _():
        h_scr[...] = (_rms(x_ref[...], nw_ref[...]) * (1.0 + sc_ref[0]) + sh_ref[0]).astype(BF16)
        acc[...] = jnp.zeros(acc.shape, F32)

    ua = _dot(h_scr[...], wua_ref[0])
    ub = _dot(h_scr[...], wub_ref[0])
    tin = lax.broadcasted_iota(I32, ua.shape, 0) % dl
    ua1 = jnp.where(tin >= 1, pltpu.roll(ua, 1, 0), p1_ref[...])
    ua2 = jnp.where(tin >= 2, pltpu.roll(ua, 2, 0), p2_ref[...])
    ua_ref[...] = ua
    acc[...] += _ffn_tail(ua, ua1, ua2, ub, cw_ref, cb_ref, wd_ref)

    @pl.when(f == pl.num_programs(0) - 1)
    def _():
        o_ref[...] = x_ref[...] + g_ref[0] * acc[...]


def _ffn_sample_call(x, modx, dl, nw, p1, p2, fw, layer):
    m = x.shape[0]
    kern = functools.partial(_ffn_sample_kernel, dl=dl)
    full = lambda i: pl.BlockSpec((m, D_MODEL), lambda f: (0, 0))
    return pl.pallas_call(
        kern,
        grid=(D_FFP // TF,),
        in_specs=[full(0),
                  pl.BlockSpec((1, m, D_MODEL), lambda f: (0, 0, 4)),
                  pl.BlockSpec((1, m, D_MODEL), lambda f: (0, 0, 3)),
                  pl.BlockSpec((1, m, D_MODEL), lambda f: (0, 0, 5)),
                  pl.BlockSpec((1, D_MODEL), lambda f: (0, 0)),
                  pl.BlockSpec((m, TF), lambda f: (0, f)), pl.BlockSpec((m, TF), lambda f: (0, f)),
                  pl.BlockSpec((1, D_MODEL, TF), lambda f: (layer, 0, f)),
                  pl.BlockSpec((1, D_MODEL, TF), lambda f: (layer, 0, D_FFP // TF + f)),
                  pl.BlockSpec((1, CONV_W, TF), lambda f: (layer, 0, f)),
                  pl.BlockSpec((1, 1, TF), lambda f: (layer, 0, f)),
                  pl.BlockSpec((1, TF, D_MODEL), lambda f: (layer, f, 0))],
        out_specs=[full(0), pl.BlockSpec((m, TF), lambda f: (0, f))],
        out_shape=[jax.ShapeDtypeStruct((m, D_MODEL), F32),
                   jax.ShapeDtypeStruct((m, D_FFP), F32)],
        scratch_shapes=[pltpu.VMEM((m, D_MODEL), BF16), pltpu.VMEM((m, D_MODEL), F32)],
        compiler_params=_cparams(("arbitrary",)),
        name="ffn_sample",
    )(x, modx, modx, modx, nw, p1, p2, fw["wu"], fw["wu"], fw["cw"], fw["cb"], fw["wd"])


def _rope_tables(pos):
    half = ROPE_DIM // 2
    inv = jnp.exp(jnp.arange(half, dtype=F32) * (-math.log(ROPE_THETA) / half))
    ang = pos.astype(F32)[:, None] * inv[None, :]
    cos, sin = jnp.cos(ang), jnp.sin(ang)
    n = pos.shape[0]
    rest = NSA_DH - ROPE_DIM
    cos_t = jnp.concatenate([cos, cos, jnp.ones((n, rest), F32)], axis=1)
    sin_a = jnp.concatenate([-sin, jnp.zeros((n, NSA_DH - half), F32)], axis=1)
    sin_b = jnp.concatenate([jnp.zeros((n, half), F32), sin, jnp.zeros((n, rest), F32)], axis=1)
    return cos_t, sin_a, sin_b


_W_IN_SIZES = (D_A, D_B, D_B, D_B, D_B, D_C, 6 * N_KV, 3 * NSA_HEADS, 3 * D_MODEL)
_W_IN_DST = (OFF_UA, OFF_HQ, OFF_HF, OFF_HI, OFF_HGT, OFF_NQ, OFF_NKV, OFF_NGT, OFF_MGT)
N_IN = sum(_W_IN_SIZES)


W_IN_RB = 256


def _w_in_src_row(i):
    src = jnp.int32(0)
    start = 0
    for size, dst in zip(_W_IN_SIZES, _W_IN_DST):
        inside = (i * W_IN_RB >= dst) & (i * W_IN_RB < dst + max(size, W_IN_RB))
        src = jnp.where(inside, start + i * W_IN_RB - dst, src)
        start += size
    return src


def _w_in_prep_kernel(x_ref, o_ref):
    i = pl.program_id(1)
    x = x_ref[0].astype(BF16)
    row = lax.broadcasted_iota(I32, x.shape, 0)
    o_ref[0] = jnp.where((i * W_IN_RB == OFF_NGT) & (row >= 3 * NSA_HEADS), jnp.zeros_like(x), x)


def _permute_w_in(w_in):
    depth, d, _ = w_in.shape
    w_t = jnp.swapaxes(w_in, 1, 2)
    return pl.pallas_call(
        _w_in_prep_kernel,
        grid=(depth, N_INP // W_IN_RB),
        in_specs=[pl.BlockSpec((pl.Element(1), pl.Element(W_IN_RB), pl.Element(d)),
                               lambda l, i: (l, pl.multiple_of(_w_in_src_row(i), SUBLANE), 0))],
        out_specs=pl.BlockSpec((1, W_IN_RB, d), lambda l, i: (l, i, 0)),
        out_shape=jax.ShapeDtypeStruct((depth, N_INP, d), BF16),
        compiler_params=_cparams(("arbitrary", "arbitrary")),
        name="w_in_prep",
    )(w_t)


def _w_up_prep_kernel(x_ref, o_ref):
    o_ref[0, :, :D_FF] = x_ref[0].astype(BF16)
    o_ref[0, :, D_FF:] = jnp.zeros((o_ref.shape[1], D_FFP - D_FF), BF16)


def _pad_w_up(w_up):
    depth, d, _ = w_up.shape
    r = 512
    return pl.pallas_call(
        _w_up_prep_kernel,
        grid=(depth, d // r, 2),
        in_specs=[pl.BlockSpec((1, r, D_FF), lambda l, i, h: (l, i, h))],
        out_specs=pl.BlockSpec((1, r, D_FFP), lambda l, i, h: (l, i, h)),
        out_shape=jax.ShapeDtypeStruct((depth, d, 2 * D_FFP), BF16),
        compiler_params=_cparams(("arbitrary", "arbitrary", "arbitrary")),
        name="w_up_prep",
    )(w_up)


def _s5_params(a_re, a_im, log_dt, b_re, b_im, c_re, c_im, d, w_glu, b_glu):
    dt = jnp.exp(log_dt)[:, None]
    mag = jnp.exp(a_re * dt)
    ab_re, ab_im = mag * jnp.cos(a_im * dt), mag * jnp.sin(a_im * dt)
    den = a_re * a_re + a_im * a_im
    cf_re = ((ab_re - 1.0) * a_re + ab_im * a_im) / den
    cf_im = (ab_im * a_re - (ab_re - 1.0) * a_im) / den
    bb_re = cf_re[..., None] * b_re - cf_im[..., None] * b_im
    bb_im = cf_re[..., None] * b_im + cf_im[..., None] * b_re
    eye = jnp.eye(S5_G, dtype=F32)
    bdiag = lambda bb: jnp.einsum("gnc,gh->gchn", bb, eye).reshape(D_A, S5_W).astype(BF16)
    cdiag = lambda cc: jnp.einsum("gcn,gh->gnhc", cc, eye).reshape(S5_W, D_A).astype(BF16)
    pr, pi = ab_re.reshape(1, S5_W), ab_im.reshape(1, S5_W)
    prs, pis = [pr], [pi]
    for _ in range(SUBLANE - 1):
        nr = prs[-1] * pr - pis[-1] * pi
        ni = prs[-1] * pi + pis[-1] * pr
        prs.append(nr)
        pis.append(ni)
    return dict(bbr=bdiag(bb_re), bbi=bdiag(bb_im), ccr=cdiag(c_re), cci=cdiag(c_im),
                apr=jnp.concatenate(prs, axis=0), api=jnp.concatenate(pis, axis=0),
                d=d.reshape(1, D_A), wg=w_glu.astype(BF16), bg=b_glu.reshape(1, D_A))


def _hgrn_params(lb):
    pos = lb > 0
    lb_safe = jnp.where(pos, lb, 1.0)
    z = jnp.zeros_like(lb)
    return jnp.stack([jnp.log(lb_safe), jnp.log1p(-lb), 1.0 - lb, pos.astype(F32), z, z, z, z], axis=0)


def _cmp_params(cmp_a, w1, b1, w2):
    def lanes(a):
        return jnp.concatenate([a[0], a[0], a[1], a[1]], axis=-1)
    return dict(a1=lanes(cmp_a[:, :CMP_STRIDE]), a2=lanes(cmp_a[:, CMP_STRIDE:]),
                w1=w1.astype(BF16), b1=b1, w2=w2.astype(BF16))


def _ffn_params(w_up, conv_w, conv_b, w_down):
    padc = lambda a: jnp.pad(a, [(0, 0)] * (a.ndim - 1) + [(0, D_FFP - D_FF)])
    wu = _pad_w_up(w_up)
    wd = jnp.pad(w_down.astype(BF16), ((0, 0), (0, D_FFP - D_FF), (0, 0)))
    return dict(wu=wu, cw=padc(conv_w), cb=padc(conv_b)[:, None, :], wd=wd)


def kernel(x_prompt, x_sample, cache_cmp, cache_slc, state_win, state_s5, state_hgrn, state_conv,
           page_table, c_prompt, c_sample, w_mod, b_mod, norm1_w, norm2_w, w_in,
           s5_a_re, s5_a_im, s5_log_dt, s5_b_re, s5_b_im, s5_c_re, s5_c_im, s5_d, s5_w_glu, s5_b_glu,
           hg_lb_logits, hg_norm_w, nsa_q_norm, nsa_k_norm, cmp_a, cmp_w1, cmp_b1, cmp_w2,
           w_branch_a, w_branch_b, w_branch_c, w_out, w_up, conv_w, conv_b, w_down):
    bsz, seq, _ = x_prompt.shape
    dbsz, dl, _ = x_sample.shape
    depth = w_in.shape[0]
    n_pages = page_table.shape[1]
    past = n_pages * PAGE
    wb = state_win.shape[2]
    mp, ms = bsz * seq, dbsz * dl
    rows = NSA_REP * dl
    assert seq % 512 == 0 and seq // SEL_BLOCK <= LANE and n_pages % PG == 0 and dl <= SUBLANE

    probs = jax.nn.softmax(hg_lb_logits.astype(F32), axis=0)
    lower_bounds = jnp.cumsum(probs, axis=0) - probs[0:1]
    w_in_p = _permute_w_in(w_in)
    wa, wbr, wc, wo = (w.astype(BF16) for w in (w_branch_a, w_branch_b, w_branch_c, w_out))
    fw = _ffn_params(w_up, conv_w, conv_b, w_down)
    rope_p = _rope_tables(jnp.arange(seq, dtype=I32))
    rope_s = tuple(jnp.tile(t, (dbsz, 1)) for t in _rope_tables(past + jnp.arange(dl, dtype=I32)))
    pt_flat = page_table.reshape(-1).astype(I32)
    cache_cmp4 = cache_cmp.reshape(depth, -1, PAGE * KV_TILES, LANE)
    cache_slc4 = cache_slc.reshape(depth, -1, PAGE * KV_TILES, LANE)
    win_prev4 = state_win.reshape(depth, dbsz, wb * KV_TILES, LANE)
    emat = (jnp.arange(PG * PAGE)[None, :] // SEL_BLOCK == jnp.arange(LANE)[:, None]).astype(BF16)

    nr = -(-(bsz + dbsz) // 16) * 16
    c_all = jnp.concatenate([c_prompt, c_sample, jnp.zeros((nr - bsz - dbsz, D_MODEL), F32)], axis=0)
    mod = _mod_call(c_all, w_mod, b_mod)

    cmp_lanes = lambda a: jnp.concatenate([a[:, 0], a[:, 0], a[:, 1], a[:, 1]], axis=-1).reshape(
        depth, SUB_ROWS, LANE)
    first, second = _cmp_pool_cache_call(cache_cmp4, pt_flat, dbsz, n_pages,
                                         cmp_lanes(cmp_a[:, :, :CMP_STRIDE]),
                                         cmp_lanes(cmp_a[:, :, CMP_STRIDE:]))
    kc_past = _cmp_mlp_cache_call(first, second, cmp_w1.astype(BF16), cmp_b1, cmp_w2.astype(BF16))

    xp = x_prompt.reshape(mp, D_MODEL)
    xs = x_sample.reshape(ms, D_MODEL)
    outs = [[] for _ in range(12)]
    for l in range(depth):
        s5p = _s5_params(s5_a_re[l], s5_a_im[l], s5_log_dt[l], s5_b_re[l], s5_b_im[l], s5_c_re[l],
                         s5_c_im[l], s5_d[l], s5_w_glu[l], s5_b_glu[l])
        hgp = _hgrn_params(lower_bounds[l])
        hgn = hg_norm_w[l].reshape(1, HG_DV)
        cp = _cmp_params(cmp_a[l], cmp_w1[l], cmp_b1[l], cmp_w2[l])
        n1, n2 = norm1_w[l].reshape(1, D_MODEL), norm2_w[l].reshape(1, D_MODEL)
        qn, kn = nsa_q_norm[l].reshape(1, NSA_DH), nsa_k_norm[l]
        modp = mod[l, :bsz].reshape(bsz, 1, 6 * D_MODEL)
        mods = jnp.repeat(mod[l, bsz:bsz + dbsz], dl, axis=0).reshape(1, ms, 6 * D_MODEL)

        hin = _inproj_call(xp, modp, seq, n1, w_in_p, l)
        ya, s5_new = _s5_call(hin, OFF_UA // D_A, bsz, seq, jnp.zeros((bsz, 2, S5_W), F32), s5p,
                              SUBLANE - 1)
        yb, hg_new = _hgrn_call(hin, tuple(o // D_B for o in (OFF_HQ, OFF_HF, OFF_HI, OFF_HGT)), bsz, seq,
                                seq, jnp.zeros((bsz, HG_HEADS, HG_DK, HG_DV), F32), hgp, hgn)
        qo, rc, rs, rw, bc, bs, bw, gates = _nsa_prep_call(hin, rope_p, seq, qn, kn)
        kc = _cmp_prompt_call(rc, bsz, seq, cp)
        yc = _nsa_prompt_call(qo, kc, bs, bw, gates, bsz, seq)
        merged = _merge_call(ya, yb, yc, hin, wa, wbr, wc, l)
        xp = _outproj_call(merged, xp, modp, seq, wo, l)
        xp, tails = _ffn_prompt_call(xp, modp, seq, n2, jnp.zeros((bsz, SUBLANE, D_FFP), F32), fw, l)
        conv_new = tails.reshape(bsz, -1, SUBLANE, D_FFP)[:, -1, SUBLANE - (CONV_W - 1):, :D_FF]
        kv5 = lambda r: r.reshape(bsz, seq, 2, NSA_KVH, NSA_DH)
        for i, o in enumerate((s5_new.reshape(bsz, 2, S5_G, S5_N), hg_new, kv5(rc), kv5(rs),
                               kv5(rw)[:, -min(WINDOW, seq):], conv_new)):
            outs[2 * i].append(o)

        hin = _inproj_call(xs, mods, dl, n1, w_in_p, l)
        padt = lambda a, n: jnp.pad(a.reshape(dbsz, dl, -1), ((0, 0), (0, n - dl), (0, 0))).reshape(dbsz * n, -1)
        u_pad = padt(hin[:, OFF_UA:OFF_UA + D_A], SUBLANE)
        ya, s5_new = _s5_call(u_pad, 0, dbsz, SUBLANE, state_s5[l].reshape(dbsz, 2, S5_W), s5p, dl - 1)
        ya = ya.reshape(dbsz, SUBLANE, D_A)[:, :dl].reshape(ms, D_A)
        hg_pad = padt(hin[:, OFF_HQ:OFF_HQ + 4 * D_B], HG_C)
        yb, hg_new = _hgrn_call(hg_pad, (0, 1, 2, 3), dbsz, HG_C, dl, state_hgrn[l], hgp, hgn)
        yb = yb.reshape(dbsz, HG_C, D_B)[:, :dl].reshape(ms, D_B)
        qo, rc, rs, rw, bc, bs, bw, gates = _nsa_prep_call(hin, rope_s, dl, qn, kn)
        q16 = qo.reshape(dbsz, dl, NSA_KVH, NSA_REP, NSA_DH).transpose(0, 2, 1, 3, 4).reshape(
            dbsz, NSA_KVH, rows, NSA_DH)
        o_c, selt = _nsa_s_sel_call(q16, kc_past, l, past, dl)
        g16 = gates[:, :3 * NSA_HEADS].reshape(dbsz, dl, NSA_KVH, NSA_REP, 3).transpose(0, 2, 1, 3, 4)
        g16 = jnp.pad(g16.reshape(dbsz, NSA_KVH, rows, 3), ((0, 0), (0, 0), (0, 0), (0, LANE - 3)))
        pad8 = lambda r: jnp.pad(r.reshape(dbsz, dl, 2 * N_KV), ((0, 0), (0, SUBLANE - dl), (0, 0)))
        yc = _nsa_s_attn_call(cache_slc4, pt_flat, l, q16, selt, emat, o_c, pad8(rs), pad8(rw), win_prev4,
                              g16, past, dl)
        yc = yc.reshape(dbsz, NSA_KVH, dl, NSA_REP, NSA_DH).transpose(0, 2, 1, 3, 4).reshape(ms, D_C)
        merged = _merge_call(ya, yb, yc, hin, wa, wbr, wc, l)
        xs = _outproj_call(merged, xs, mods, dl, wo, l)
        st = jnp.pad(state_conv[l], ((0, 0), (0, 0), (0, D_FFP - D_FF)))
        tin = jnp.arange(dl)[None, :, None]
        p1 = jnp.where(tin == 0, st[:, 1:2], 0.0).reshape(ms, D_FFP)
        p2 = jnp.where(tin == 0, st[:, 0:1], jnp.where(tin == 1, st[:, 1:2], 0.0)).reshape(ms, D_FFP)
        xs, ua = _ffn_sample_call(xs, mods, dl, n2, p1, p2, fw, l)
        ext = jnp.concatenate([state_conv[l], ua[:, :D_FF].reshape(dbsz, dl, D_FF)], axis=1)
        kv5 = lambda r: r.reshape(dbsz, dl, 2, NSA_KVH, NSA_DH)
        win_new = jnp.concatenate([state_win[l], kv5(rw)], axis=1)[:, -wb:]
        for i, o in enumerate((s5_new.reshape(dbsz, 2, S5_G, S5_N), hg_new, kv5(rc), kv5(rs), win_new,
                               ext[:, dl:])):
            outs[2 * i + 1].append(o)

    st = [jnp.stack(o) for o in outs]
    return (xp.reshape(bsz, seq, D_MODEL), xs.reshape(dbsz, dl, D_MODEL), *st)
```

```python
import functools
import math

import jax
import jax.numpy as jnp
from jax import lax
from jax.experimental import pallas as pl
from jax.experimental.pallas import tpu as pltpu

F32 = jnp.float32
BF16 = jnp.bfloat16
I32 = jnp.int32

D_MODEL = 2048
D_A = D_MODEL // 4
S5_GROUP = 16
S5_G = D_A // S5_GROUP
S5_N = 64
S5_W = S5_G * S5_N
D_B = D_MODEL // 4
HG_DK = 128
HG_DV = 128
HG_HEADS = D_B // HG_DV
D_C = D_MODEL // 2
NSA_DH = 128
NSA_HEADS = D_C // NSA_DH
NSA_KVH = 2
NSA_REP = NSA_HEADS // NSA_KVH
N_KV = NSA_KVH * NSA_DH
ROPE_DIM = NSA_DH // 4
ROPE_THETA = 500000.0
CMP_STRIDE = 16
SEL_BLOCK = 64
N_SEL = 16
WINDOW = 512
FORCE_BONUS = 1000.0
D_FF = ((8 * D_MODEL) // 3 + 127) // 128 * 128
CONV_W = 3
EPS = 1e-6
NEG_INF = -1e30
PAGE = 128

LANE = 128
SUBLANE = 8
VMEM_LIMIT = 56 * 1024 * 1024

OFF_MGT = 0
OFF_NQ = 3 * D_MODEL
OFF_NKV = OFF_NQ + D_C
OFF_UA = OFF_NKV + 6 * N_KV
OFF_HQ = OFF_UA + D_A
OFF_HF = OFF_HQ + D_B
OFF_HI = OFF_HF + D_B
OFF_HGT = OFF_HI + D_B
OFF_NGT = OFF_HGT + D_B
NGT_PAD = 256
N_INP = OFF_NGT + NGT_PAD
TN_IN = 1280
D_FFP = 5632
TF = 512
HG_C = 64
HG_SB = 16


def _cparams(sem):
    return pltpu.CompilerParams(dimension_semantics=sem, vmem_limit_bytes=VMEM_LIMIT)


def _dot(a, b):
    return jnp.dot(a, b, preferred_element_type=F32)


def _dot_nt(a, b):
    return lax.dot_general(a, b, (((1,), (1,)), ((), ())), preferred_element_type=F32)


def _dot_tn(a, b):
    return lax.dot_general(a, b, (((0,), (0,)), ((), ())), preferred_element_type=F32)


def _dot2(a, b):
    hi = a.astype(BF16)
    lo = (a - hi.astype(F32)).astype(BF16)
    return _dot(hi, b) + _dot(lo, b)


def _dot3(a, b):
    hi = a.astype(BF16)
    r1 = a - hi.astype(F32)
    mid = r1.astype(BF16)
    lo = (r1 - mid.astype(F32)).astype(BF16)
    return _dot(b, hi) + _dot(b, mid) + _dot(b, lo)


def _rms(x, w):
    return x * lax.rsqrt(jnp.mean(x * x, axis=-1, keepdims=True) + EPS) * w


def _masked_softmax(s, mask):
    s = jnp.where(mask, s, NEG_INF)
    p = jnp.where(mask, jnp.exp(s - jnp.max(s, axis=-1, keepdims=True)), 0.0)
    return p / jnp.maximum(jnp.sum(p, axis=-1, keepdims=True), 1e-30)


def _row_reduce(x, combine, reduce):
    n = x.shape[-1]
    if n % LANE or n == LANE:
        return reduce(x, axis=-1, keepdims=True)
    acc = x[:, :LANE]
    for k in range(1, n // LANE):
        acc = combine(acc, x[:, k * LANE:(k + 1) * LANE])
    return reduce(acc, axis=-1, keepdims=True)


def _log_sigmoid(x):
    return jnp.minimum(x, 0.0) - jnp.log1p(jnp.exp(-jnp.abs(x)))


def _logaddexp(a, b):
    return jnp.maximum(a, b) + jnp.log1p(jnp.exp(-jnp.abs(a - b)))


def _mod_kernel(c_ref, w_ref, b_ref, o_ref):
    c = c_ref[...]
    a = (c * jax.nn.sigmoid(c)).astype(BF16)
    o_ref[0] = _dot(a, w_ref[0].astype(BF16)) + b_ref[0]


def _mod_call(c_all, w_mod, b_mod):
    depth, d, n = w_mod.shape
    r = c_all.shape[0]
    tn = 1024
    return pl.pallas_call(
        _mod_kernel,
        grid=(depth, n // tn),
        in_specs=[pl.BlockSpec((r, d), lambda l, j: (0, 0)),
                  pl.BlockSpec((1, d, tn), lambda l, j: (l, 0, j)),
                  pl.BlockSpec((1, 1, tn), lambda l, j: (l, 0, j))],
        out_specs=pl.BlockSpec((1, r, tn), lambda l, j: (l, 0, j)),
        out_shape=jax.ShapeDtypeStruct((depth, r, n), F32),
        compiler_params=_cparams(("arbitrary", "arbitrary")),
        name="mod",
    )(c_all, w_mod, b_mod.reshape(depth, 1, n))


def _inproj_kernel(x_ref, sc_ref, sh_ref, nw_ref, w_ref, o_ref, h_scr):
    @pl.when(pl.program_id(1) == 0)
    def _():
        y = _rms(x_ref[...], nw_ref[...])
        h_scr[...] = (y * (1.0 + sc_ref[0]) + sh_ref[0]).astype(BF16)

    o_ref[...] = _dot_nt(h_scr[...], w_ref[0])


def _mod_spec(modx, tm, rows_per_group, col):
    if modx.shape[1] == 1:
        return pl.BlockSpec((1, 1, D_MODEL), lambda i, *_: ((i * tm) // rows_per_group, 0, col))
    return pl.BlockSpec((1, tm, D_MODEL), lambda i, *_: (0, i, col))


def _inproj_call(x, modx, rows_per_group, nw, w, layer):
    m = x.shape[0]
    tm = min(1024, m)
    return pl.pallas_call(
        _inproj_kernel,
        grid=(m // tm, N_INP // TN_IN),
        in_specs=[pl.BlockSpec((tm, D_MODEL), lambda i, j: (i, 0)),
                  _mod_spec(modx, tm, rows_per_group, 1),
                  _mod_spec(modx, tm, rows_per_group, 0),
                  pl.BlockSpec((1, D_MODEL), lambda i, j: (0, 0)),
                  pl.BlockSpec((1, TN_IN, D_MODEL), lambda i, j: (layer, j, 0))],
        out_specs=pl.BlockSpec((tm, TN_IN), lambda i, j: (i, j)),
        out_shape=jax.ShapeDtypeStruct((m, N_INP), F32),
        scratch_shapes=[pltpu.VMEM((tm, D_MODEL), BF16)],
        compiler_params=_cparams(("arbitrary", "arbitrary")),
        name="inproj",
    )(x, modx, modx, nw, w)


S5_LC = 512


def _s5_kernel(u_ref, h0_ref, bbr_ref, bbi_ref, ccr_ref, cci_ref, apr_ref, api_ref, d_ref,
               wg_ref, bg_ref, y_ref, hl_ref, hre, him, car, *, t_rows, last_row):
    @pl.when(pl.program_id(1) == 0)
    def _():
        car[...] = h0_ref[0]

    u = u_ref[...]
    ub = u.astype(BF16)
    nblk = D_A // LANE
    for j in range(nblk):
        us, ss = slice(j * LANE, (j + 1) * LANE), slice(j * S5_LC, (j + 1) * S5_LC)
        hre[:, ss] = _dot(ub[:, us], bbr_ref[us, ss])
        him[:, ss] = _dot(ub[:, us], bbi_ref[us, ss])
    row = lax.broadcasted_iota(I32, (SUBLANE, S5_LC), 0)
    for c in range(S5_W // S5_LC):
        sl = slice(c * S5_LC, (c + 1) * S5_LC)
        pwr = apr_ref[:, sl]
        pwi = api_ref[:, sl]
        steps = tuple((s, pwr[s - 1:s], pwi[s - 1:s]) for s in (1, 2, 4))

        def body(i, carry, sl=sl, pwr=pwr, pwi=pwi, steps=steps):
            cr, ci = carry
            r0 = pl.multiple_of(i * SUBLANE, SUBLANE)
            xr = hre[pl.ds(r0, SUBLANE), sl]
            xi = him[pl.ds(r0, SUBLANE), sl]
            for s, ar, ai in steps:
                sr = jnp.where(row >= s, pltpu.roll(xr, s, 0), 0.0)
                si = jnp.where(row >= s, pltpu.roll(xi, s, 0), 0.0)
                xr, xi = xr + (ar * sr - ai * si), xi + (ar * si + ai * sr)
            xr, xi = xr + (pwr * cr - pwi * ci), xi + (pwr * ci + pwi * cr)
            hre[pl.ds(r0, SUBLANE), sl] = xr
            him[pl.ds(r0, SUBLANE), sl] = xi
            return xr[SUBLANE - 1:SUBLANE], xi[SUBLANE - 1:SUBLANE]

        cr, ci = lax.fori_loop(0, t_rows // SUBLANE, body, (car[0:1, sl], car[1:2, sl]))
        car[0:1, sl] = cr
        car[1:2, sl] = ci

    lr = t_rows - SUBLANE + last_row
    hl_ref[0, 0:1, :] = hre[lr:lr + 1, :]
    hl_ref[0, 1:2, :] = him[lr:lr + 1, :]
    y = jnp.concatenate(
        [_dot(hre[:, j * S5_LC:(j + 1) * S5_LC].astype(BF16),
              ccr_ref[j * S5_LC:(j + 1) * S5_LC, j * LANE:(j + 1) * LANE])
         - _dot(him[:, j * S5_LC:(j + 1) * S5_LC].astype(BF16),
                cci_ref[j * S5_LC:(j + 1) * S5_LC, j * LANE:(j + 1) * LANE])
         for j in range(nblk)], axis=1) + d_ref[...] * u
    y = jax.nn.gelu(y)
    z = _dot(y.astype(BF16), wg_ref[...]) + bg_ref[...]
    y_ref[...] = (y * jax.nn.sigmoid(z)).astype(BF16)


def _s5_call(u_arr, ucol, nb, seq, h0, p, last_row):
    t_rows = min(256, seq)
    nt = seq // t_rows
    const = lambda shape: pl.BlockSpec(shape, lambda b, t: (0,) * len(shape))
    kern = functools.partial(_s5_kernel, t_rows=t_rows, last_row=last_row)
    return pl.pallas_call(
        kern,
        grid=(nb, nt),
        in_specs=[pl.BlockSpec((t_rows, D_A), lambda b, t: (b * nt + t, ucol)),
                  pl.BlockSpec((1, 2, S5_W), lambda b, t: (b, 0, 0)),
                  const((D_A, S5_W)), const((D_A, S5_W)), const((S5_W, D_A)), const((S5_W, D_A)),
                  const((SUBLANE, S5_W)), const((SUBLANE, S5_W)), const((1, D_A)),
                  const((D_A, D_A)), const((1, D_A))],
        out_specs=[pl.BlockSpec((t_rows, D_A), lambda b, t: (b * nt + t, 0)),
                   pl.BlockSpec((1, 2, S5_W), lambda b, t: (b, 0, 0))],
        out_shape=[jax.ShapeDtypeStruct((nb * seq, D_A), BF16),
                   jax.ShapeDtypeStruct((nb, 2, S5_W), F32)],
        scratch_shapes=[pltpu.VMEM((t_rows, S5_W), F32), pltpu.VMEM((t_rows, S5_W), F32),
                        pltpu.VMEM((2, S5_W), F32)],
        compiler_params=_cparams(("arbitrary", "arbitrary")),
        name="s5",
    )(u_arr, h0, p["bbr"], p["bbi"], p["ccr"], p["cci"], p["apr"], p["api"], p["d"],
      p["wg"], p["bg"])


def _hgrn_kernel(q_ref, f_ref, i_ref, g_ref, s0_ref, lp_ref, nw_ref, y_ref, sl_ref, st, pscr, rscr,
                 *, l_valid, chunks):
    t = pl.program_id(1)
    c = HG_C
    nsb = c // HG_SB
    nh = HG_HEADS

    @pl.when(t == 0)
    def _():
        for h in range(nh):
            st[h] = s0_ref[0, h].T

    row = lax.broadcasted_iota(I32, (c, D_B), 0)
    tri = (lax.broadcasted_iota(I32, (c, c), 0) >= lax.broadcasted_iota(I32, (c, c), 1)).astype(BF16)
    srow = lax.broadcasted_iota(I32, (HG_SB, D_B), 0)
    rows16 = lambda r: jnp.broadcast_to(r, (HG_SB, D_B))

    def chunk(ci, carry):
        c0 = pl.multiple_of(ci * c, c)
        crows = pl.ds(c0, c)
        valid = (t * (chunks * c) + c0 + row) < l_valid
        hf = f_ref[crows, :]
        ls = _log_sigmoid(hf)
        lf = jnp.where(lp_ref[3:4, :] > 0.5, _logaddexp(lp_ref[0:1, :], lp_ref[1:2, :] + ls), ls)
        kk = lp_ref[2:3, :] * jax.nn.sigmoid(-hf)
        lf = jnp.where(valid, lf, 0.0)
        kk = jnp.where(valid, kk, 0.0)
        hq = q_ref[crows, :]
        q = hq * jax.nn.sigmoid(hq)
        v = i_ref[crows, :]
        b = _dot3(lf, tri)
        zero_row = jnp.zeros((1, D_B), F32)
        bref = [zero_row] + [b[i * HG_SB - 1:i * HG_SB] for i in range(1, nsb)]
        bend = [b[(j + 1) * HG_SB - 1:(j + 1) * HG_SB] for j in range(nsb)]
        qq = q * jnp.exp(b - jnp.concatenate([rows16(r) for r in bref], axis=0))
        kks = kk * jnp.exp(jnp.concatenate([rows16(r) for r in bend], axis=0) - b)
        lhs = []
        for j in range(nsb - 1):
            dsel = jnp.concatenate(
                [rows16(jnp.exp(bref[i] - bend[j])) if i > j else jnp.zeros((HG_SB, D_B), F32)
                 for i in range(nsb)], axis=0)
            lhs.append((qq * dsel).astype(BF16))
        rhs = [jnp.where(row // HG_SB == j, kks, 0.0).astype(BF16) for j in range(nsb - 1)]
        for i in range(nsb):
            rs = slice(i * HG_SB, (i + 1) * HG_SB)
            bi, qi, ki = b[rs], q[rs], kk[rs]
            for s in range(HG_SB):
                m = srow >= s
                e = jnp.exp(jnp.where(m, bi - bi[s:s + 1], 0.0))
                piece = jnp.where(m, (qi * ki[s:s + 1]) * e, 0.0).astype(BF16)
                for h in range(nh):
                    r0 = ((h * nsb + i) * HG_SB + s) * HG_SB
                    pscr[r0:r0 + HG_SB, :] = piece[:, h * HG_DK:(h + 1) * HG_DK]
        rscr[...] = _dot(pscr[...], jnp.ones((HG_DK, HG_DK), BF16))
        qe = (q * jnp.exp(b)).astype(BF16)
        vb = v.astype(BF16)
        bl = b[c - 1:c]
        kdec = (kk * jnp.exp(bl - b)).astype(BF16)
        ebl = jnp.exp(bl)
        gt = g_ref[crows, :]
        gate = gt * jax.nn.sigmoid(gt)
        for h in range(nh):
            hs = slice(h * HG_DK, (h + 1) * HG_DK)
            s_t = st[h]
            att = _dot_nt(jnp.concatenate([x[:, hs] for x in lhs], axis=1),
                          jnp.concatenate([x[:, hs] for x in rhs], axis=1))
            o = _dot(att.astype(BF16), vb[:, hs]) + _dot_nt(qe[:, hs], s_t.astype(BF16))
            diag = []
            for i in range(nsb):
                od = jnp.zeros((HG_SB, HG_DV), F32)
                for s in range(HG_SB):
                    r0 = ((h * nsb + i) * HG_SB + s) * HG_SB
                    od = od + rscr[r0:r0 + HG_SB, :] * v[i * HG_SB + s:i * HG_SB + s + 1, hs]
                diag.append(od)
            o = o + jnp.concatenate(diag, axis=0)
            st[h] = s_t * ebl[:, hs] + _dot_tn(vb[:, hs], kdec[:, hs])
            y_ref[crows, hs] = (_rms(o, nw_ref[...]) * gate[:, hs]).astype(BF16)
        return carry

    lax.fori_loop(0, chunks, chunk, 0)

    @pl.when(t == pl.num_programs(1) - 1)
    def _():
        for h in range(nh):
            sl_ref[0, h] = st[h].T


def _dot2g(g, z):
    hi = z.astype(BF16)
    lo = (z - hi.astype(F32)).astype(BF16)
    return _dot(g, hi) + _dot(g, lo)


def _hgrn_call(arr, cols, nb, seq_pad, l_valid, s0, lp, nw):
    chunks = 4 if seq_pad % (4 * HG_C) == 0 else 1
    rows = chunks * HG_C
    nt = seq_pad // rows
    blk = lambda col: pl.BlockSpec((rows, D_B), lambda b, t: (b * nt + t, col))
    kern = functools.partial(_hgrn_kernel, l_valid=l_valid, chunks=chunks)
    return pl.pallas_call(
        kern,
        grid=(nb, nt),
        in_specs=[blk(cols[0]), blk(cols[1]), blk(cols[2]), blk(cols[3]),
                  pl.BlockSpec((1, HG_HEADS, HG_DK, HG_DV), lambda b, t: (b, 0, 0, 0)),
                  pl.BlockSpec((SUBLANE, D_B), lambda b, t: (0, 0)),
                  pl.BlockSpec((1, HG_DV), lambda b, t: (0, 0))],
        out_specs=[pl.BlockSpec((rows, D_B), lambda b, t: (b * nt + t, 0)),
                   pl.BlockSpec((1, HG_HEADS, HG_DK, HG_DV), lambda b, t: (b, 0, 0, 0))],
        out_shape=[jax.ShapeDtypeStruct((nb * seq_pad, D_B), BF16),
                   jax.ShapeDtypeStruct((nb, HG_HEADS, HG_DK, HG_DV), F32)],
        scratch_shapes=[pltpu.VMEM((HG_HEADS, HG_DV, HG_DK), F32),
                        pltpu.VMEM((HG_HEADS * HG_C * HG_SB, HG_DK), BF16),
                        pltpu.VMEM((HG_HEADS * HG_C * HG_SB, HG_DK), F32)],
        compiler_params=_cparams(("arbitrary", "arbitrary")),
        name="hgrn",
    )(arr, arr, arr, arr, s0, lp, nw)


def _rope(x, cos_t, sin_a, sin_b):
    return (x * cos_t + pltpu.roll(x, NSA_DH - ROPE_DIM // 2, 1) * sin_a
            + pltpu.roll(x, ROPE_DIM // 2, 1) * sin_b)


def _nsa_prep_kernel(q_ref, kc_ref, ks_ref, kw_ref, g_ref, cos_ref, sa_ref, sb_ref, qn_ref, kn_ref,
                     qo_ref, rc_ref, rs_ref, rw_ref, bc_ref, bs_ref, bw_ref, go_ref):
    cos_t, sin_a, sin_b = cos_ref[...], sa_ref[...], sb_ref[...]
    qn = qn_ref[...]
    for h in range(NSA_HEADS):
        hs = slice(h * NSA_DH, (h + 1) * NSA_DH)
        qo_ref[:, hs] = _rope(_rms(q_ref[:, hs], qn), cos_t, sin_a, sin_b).astype(BF16)
    for br, (src, dst, dstb) in enumerate(((kc_ref, rc_ref, bc_ref), (ks_ref, rs_ref, bs_ref),
                                           (kw_ref, rw_ref, bw_ref))):
        kn = kn_ref[br:br + 1, :]
        tm = src.shape[0]
        for g in range(NSA_KVH):
            gs = slice(g * NSA_DH, (g + 1) * NSA_DH)
            k = _rope(_rms(src[:, gs], kn), cos_t, sin_a, sin_b)
            dst[pl.ds(g, tm, stride=KV_TILES), :] = k
            dstb[:, gs] = k.astype(BF16)
            vs = slice(N_KV + g * NSA_DH, N_KV + (g + 1) * NSA_DH)
            v = src[:, vs]
            dst[pl.ds(NSA_KVH + g, tm, stride=KV_TILES), :] = v
            dstb[:, vs] = v.astype(BF16)
    go_ref[...] = jax.nn.sigmoid(g_ref[...])


def _nsa_prep_call(hin, rope_tabs, seq, qn, kn, layer=0, depth=1, row_bufs=None):
    m = hin.shape[0]
    tm = min(256, m)
    npos = rope_tabs[0].shape[0] // tm
    rspec = pl.BlockSpec((tm, NSA_DH), lambda i: (i % npos, 0))
    kvb = OFF_NKV // (2 * N_KV)
    n_in = 10
    extra = () if row_bufs is None else tuple(row_bufs)

    def kern(*refs):
        _nsa_prep_kernel(*refs[:n_in], *refs[n_in + len(extra):])

    return pl.pallas_call(
        kern,
        grid=(m // tm,),
        input_output_aliases={n_in + k: 1 + k for k in range(len(extra))},
        in_specs=[pl.BlockSpec((tm, D_C), lambda i: (i, OFF_NQ // D_C)),
                  pl.BlockSpec((tm, 2 * N_KV), lambda i: (i, kvb)),
                  pl.BlockSpec((tm, 2 * N_KV), lambda i: (i, kvb + 1)),
                  pl.BlockSpec((tm, 2 * N_KV), lambda i: (i, kvb + 2)),
                  pl.BlockSpec((tm, LANE), lambda i: (i, OFF_NGT // LANE)),
                  rspec, rspec, rspec,
                  pl.BlockSpec((1, NSA_DH), lambda i: (0, 0)),
                  pl.BlockSpec((3, NSA_DH), lambda i: (0, 0))]
        + [pl.BlockSpec(memory_space=pl.ANY)] * len(extra),
        out_specs=[pl.BlockSpec((tm, D_C), lambda i: (i, 0))]
        + [pl.BlockSpec((tm * KV_TILES, LANE), lambda i: (layer * (m // tm) + i, 0))] * 3
        + [pl.BlockSpec((tm, 2 * N_KV), lambda i: (i, 0))] * 3
        + [pl.BlockSpec((tm, LANE), lambda i: (i, 0))],
        out_shape=[jax.ShapeDtypeStruct((m, D_C), BF16)]
        + [jax.ShapeDtypeStruct((depth * m * KV_TILES, LANE), F32)] * 3
        + [jax.ShapeDtypeStruct((m, 2 * N_KV), BF16)] * 3
        + [jax.ShapeDtypeStruct((m, LANE), F32)],
        compiler_params=_cparams(("arbitrary",)),
        name="nsa_prep",
    )(hin, hin, hin, hin, hin, *rope_tabs, qn, kn, *extra)


KV_TILES = 2 * N_KV // LANE


def _pool_rows(load, n, a1_ref, a2_ref):
    firsts, seconds = [], []
    for c in range(KV_TILES):
        ls = slice(c * LANE, (c + 1) * LANE)
        first = jnp.zeros((n, LANE), F32)
        second = jnp.zeros((n, LANE), F32)
        for j in range(CMP_STRIDE):
            x = load(j, c)
            first = first + x * a1_ref[j:j + 1, ls]
            second = second + x * a2_ref[j:j + 1, ls]
        firsts.append(first)
        seconds.append(second)
    return firsts, seconds


def _tile_rows(j, c, n):
    return pl.ds(j * KV_TILES + c, n, stride=CMP_STRIDE * KV_TILES)


def _cmp_mlp(first, second, n, w1_ref, b1_ref, w2_ref, o_ref):
    for c in range(2):
        for g in range(NSA_KVH):
            tile = 2 * c + g
            pooled = first[tile] + pltpu.roll(second[tile], n - 1, 0)
            h = _dot(pooled.astype(BF16), w1_ref[c]) + b1_ref[c:c + 1, :]
            o_ref[0, :, tile * NSA_DH:(tile + 1) * NSA_DH] = _dot(
                jax.nn.gelu(h).astype(BF16), w2_ref[c]).astype(BF16)


def _cmp_prompt_kernel(x_ref, a1_ref, a2_ref, w1_ref, b1_ref, w2_ref, o_ref, *, n):
    first, second = _pool_rows(lambda j, c: x_ref[_tile_rows(j, c, n), :], n, a1_ref, a2_ref)
    _cmp_mlp(first, second, n, w1_ref, b1_ref, w2_ref, o_ref)


def _cmp_prompt_call(rows, nb, seq, cp, layer=0):
    n = seq // CMP_STRIDE
    const = lambda shape: pl.BlockSpec(shape, lambda b: (0,) * len(shape))
    return pl.pallas_call(
        functools.partial(_cmp_prompt_kernel, n=n),
        grid=(nb,),
        in_specs=[pl.BlockSpec((seq * KV_TILES, LANE), lambda b: (layer * nb + b, 0)),
                  const((CMP_STRIDE, 2 * N_KV)), const((CMP_STRIDE, 2 * N_KV)),
                  const((2, NSA_DH, NSA_DH)), const((2, NSA_DH)), const((2, NSA_DH, NSA_DH))],
        out_specs=pl.BlockSpec((1, n, 2 * N_KV), lambda b: (b, 0, 0)),
        out_shape=jax.ShapeDtypeStruct((nb, n, 2 * N_KV), BF16),
        compiler_params=_cparams(("arbitrary",)),
        name="cmp_prompt",
    )(rows.reshape(-1, LANE), cp["a1"], cp["a2"], cp["w1"], cp["b1"], cp["w2"])


PG = 16
SUB_PER_PAGE = PAGE // CMP_STRIDE


SUB_ROWS = CMP_STRIDE * KV_TILES


def _cmp_pool_cache_kernel(pt_ref, *refs):
    pages = refs[:PG]
    a1_ref, a2_ref, f_ref, s_ref = refs[PG:]
    nv = SUB_ROWS // SUBLANE
    w1 = [a1_ref[0, v * SUBLANE:(v + 1) * SUBLANE, :] for v in range(nv)]
    w2 = [a2_ref[0, v * SUBLANE:(v + 1) * SUBLANE, :] for v in range(nv)]
    low = lax.broadcasted_iota(I32, (SUBLANE, LANE), 0) < KV_TILES

    def fold(acc):
        return acc + pltpu.roll(acc, KV_TILES, 0)

    for k in range(PG):
        for n2 in range(SUB_PER_PAGE // 2):
            halves = []
            for n in (2 * n2, 2 * n2 + 1):
                a1 = a2 = None
                for v in range(nv):
                    x = pages[k][0, 0, n * SUB_ROWS + v * SUBLANE:n * SUB_ROWS + (v + 1) * SUBLANE, :]
                    a1 = x * w1[v] if a1 is None else a1 + x * w1[v]
                    a2 = x * w2[v] if a2 is None else a2 + x * w2[v]
                halves.append((fold(a1), fold(a2)))
            r0 = (k * SUB_PER_PAGE + 2 * n2) * KV_TILES
            f_ref[0, 0, r0:r0 + SUBLANE, :] = jnp.where(low, halves[0][0], halves[1][0])
            s_ref[0, 0, r0:r0 + SUBLANE, :] = jnp.where(low, halves[0][1], halves[1][1])


def _page_specs(n_pages, layer_of, batch_of, group_of, page_shape=(PAGE, 2 * N_KV)):
    def spec(k):
        return pl.BlockSpec(
            (1, 1) + page_shape,
            lambda *a, k=k: (layer_of(*a), a[-1][batch_of(*a) * n_pages + group_of(*a) * PG + k], 0, 0))
    return [spec(k) for k in range(PG)]


def _cmp_pool_cache_call(cache, pt_flat, nb, n_pages, a1, a2):
    depth = cache.shape[0]
    npg = n_pages // PG
    nsub = n_pages * SUB_PER_PAGE
    rows = PG * SUB_PER_PAGE
    grid_spec = pltpu.PrefetchScalarGridSpec(
        num_scalar_prefetch=1,
        grid=(depth, nb, npg),
        in_specs=_page_specs(n_pages, lambda l, b, g, pt: l, lambda l, b, g, pt: b,
                             lambda l, b, g, pt: g, (PAGE * KV_TILES, LANE))
        + [pl.BlockSpec((1, SUB_ROWS, LANE), lambda l, b, g, pt: (l, 0, 0))] * 2,
        out_specs=[pl.BlockSpec((1, 1, rows * KV_TILES, LANE), lambda l, b, g, pt: (l, b, g, 0))] * 2,
    )
    return pl.pallas_call(
        _cmp_pool_cache_kernel,
        grid_spec=grid_spec,
        out_shape=[jax.ShapeDtypeStruct((depth, nb, nsub * KV_TILES, LANE), F32)] * 2,
        compiler_params=_cparams(("arbitrary", "arbitrary", "arbitrary")),
        name="cmp_pool_cache",
    )(pt_flat, *([cache] * PG), a1, a2)


def _cmp_mlp_cache_kernel(f_ref, s_ref, w1_ref, b1_ref, w2_ref, o_ref, *, n):
    tiles = lambda ref: [ref[0, 0, pl.ds(c, n, stride=KV_TILES), :] for c in range(KV_TILES)]
    _cmp_mlp(tiles(f_ref), tiles(s_ref), n, w1_ref.at[0], b1_ref.at[0], w2_ref.at[0], o_ref.at[0])


def _cmp_mlp_cache_call(first, second, w1, b1, w2):
    depth, nb, rows, _ = first.shape
    n = rows // KV_TILES
    blk = pl.BlockSpec((1, 1, n, 2 * N_KV), lambda l, b: (l, b, 0, 0))
    tblk = pl.BlockSpec((1, 1, rows, LANE), lambda l, b: (l, b, 0, 0))
    return pl.pallas_call(
        functools.partial(_cmp_mlp_cache_kernel, n=n),
        grid=(depth, nb),
        in_specs=[tblk, tblk,
                  pl.BlockSpec((1, 2, NSA_DH, NSA_DH), lambda l, b: (l, 0, 0, 0)),
                  pl.BlockSpec((1, 2, NSA_DH), lambda l, b: (l, 0, 0)),
                  pl.BlockSpec((1, 2, NSA_DH, NSA_DH), lambda l, b: (l, 0, 0, 0))],
        out_specs=blk,
        out_shape=jax.ShapeDtypeStruct((depth, nb, n, 2 * N_KV), BF16),
        compiler_params=_cparams(("arbitrary", "arbitrary")),
        name="cmp_mlp_cache",
    )(first, second, w1, b1, w2)


def _block_scores(imp, qpos, n_slc):
    blk = lax.broadcasted_iota(I32, imp.shape, 1)
    cur = qpos // SEL_BLOCK
    forced = (blk == 0) | (blk == cur) | (blk == cur - 1)
    valid = blk * SEL_BLOCK <= qpos
    score = jnp.where(valid, imp + jnp.where(forced, FORCE_BONUS, 0.0), -1.0)
    return jnp.where(blk < n_slc, score, -2.0)


def _select_blocks_ranked(imp, qpos, n_slc):
    nb8 = (n_slc + SUBLANE - 1) // SUBLANE * SUBLANE
    st = _block_scores(imp, qpos, n_slc).T[:nb8]
    ridx = lax.broadcasted_iota(I32, st.shape, 0)
    cnt = jnp.zeros(st.shape, F32)
    for i in range(n_slc):
        ri = st[i:i + 1]
        cnt = cnt + jnp.where(ri > st, 1.0, jnp.where(ri == st, jnp.where(ridx > i, 1.0, 0.0), 0.0))
    sel_t = jnp.where((cnt < min(N_SEL, n_slc)) & (ridx < n_slc), 1.0, 0.0)
    if nb8 < LANE:
        sel_t = jnp.concatenate([sel_t, jnp.zeros((LANE - nb8, st.shape[1]), F32)], axis=0)
    return sel_t.T


def _select_blocks(imp, qpos, n_slc):
    blk = lax.broadcasted_iota(I32, imp.shape, 1)
    score = _block_scores(imp, qpos, n_slc)
    sel = jnp.zeros(imp.shape, F32)
    for _ in range(min(N_SEL, n_slc)):
        mx = jnp.max(score, axis=-1, keepdims=True)
        idx = jnp.min(jnp.where(score == mx, blk, 1 << 30), axis=-1, keepdims=True)
        hit = blk == idx
        sel = jnp.where(hit, 1.0, sel)
        score = jnp.where(hit, -3.0, score)
    return sel


def _cmp_branch(q, kck, kcv, qpos, n_cmp):
    s = _dot_nt(q, kck) * (NSA_DH ** -0.5)
    col = lax.broadcasted_iota(I32, s.shape, 1)
    mask = (col * CMP_STRIDE + (2 * CMP_STRIDE - 1) <= qpos) & (col < n_cmp)
    p = _masked_softmax(s, mask)
    return p, _dot(p.astype(BF16), kcv)


NSA_KC = 1024


def _nsa_prompt_kernel(q_ref, kc_ref, ks_ref, vs_ref, kw_ref, vw_ref, g_ref, o_ref,
                       bias_scr, m_scr, l_scr, acc_scr, *, tq, seq, slab):
    q0 = pl.program_id(1) * tq
    rows = NSA_REP * tq
    ncp = seq // CMP_STRIDE
    n_slc = seq // SEL_BLOCK
    scale = NSA_DH ** -0.5
    gates = g_ref[...]
    q4s = [jnp.concatenate([q_ref[:, (g * NSA_REP + r) * NSA_DH:(g * NSA_REP + r + 1) * NSA_DH]
                            for r in range(NSA_REP)], axis=0) for g in range(NSA_KVH)]
    qpos_c = q0 + (lax.broadcasted_iota(I32, (rows, ncp), 0) & (tq - 1))
    o_cs, psums = [], []
    for g in range(NSA_KVH):
        p_c, o_c = _cmp_branch(q4s[g], kc_ref[0, :, g * NSA_DH:(g + 1) * NSA_DH],
                               kc_ref[0, :, N_KV + g * NSA_DH:N_KV + (g + 1) * NSA_DH], qpos_c, ncp - 1)
        psum = p_c[0:tq]
        for r in range(1, NSA_REP):
            psum = psum + p_c[r * tq:(r + 1) * tq]
        o_cs.append(o_c)
        psums.append(psum)

    def ranked():
        gmat = (lax.broadcasted_iota(I32, (ncp, LANE), 0) // (SEL_BLOCK // CMP_STRIDE)
                == lax.broadcasted_iota(I32, (ncp, LANE), 1)).astype(BF16)
        imp = _dot2(jnp.concatenate(psums, axis=0), gmat)
        qp = q0 + (lax.broadcasted_iota(I32, (NSA_KVH * tq, LANE), 0) & (tq - 1))
        return _select_blocks_ranked(imp, qp, n_slc)

    sel = lax.cond(q0 + tq > N_SEL * SEL_BLOCK, ranked,
                   lambda: jnp.ones((NSA_KVH * tq, LANE), F32))
    emat = (lax.broadcasted_iota(I32, (LANE, seq), 1) // SEL_BLOCK
            == lax.broadcasted_iota(I32, (LANE, seq), 0)).astype(BF16)
    nk = (q0 + tq + NSA_KC - 1) // NSA_KC
    s0 = pl.multiple_of(jnp.maximum(q0 + tq - slab, 0), SUBLANE)
    key = lax.broadcasted_iota(I32, (tq, seq), 1)
    causal = key <= q0 + lax.broadcasted_iota(I32, (tq, seq), 0)
    wp = s0 + lax.broadcasted_iota(I32, (tq, slab), 1)
    qpw = q0 + lax.broadcasted_iota(I32, (tq, slab), 0)
    wb1 = jnp.where((wp <= qpw) & (wp > qpw - WINDOW), 0.0, NEG_INF)

    def add_bias(s, bias):
        return jnp.concatenate([s[r * tq:(r + 1) * tq] + bias for r in range(NSA_REP)], axis=0)

    for g in range(NSA_KVH):
        keep = (_dot(sel[g * tq:(g + 1) * tq].astype(BF16), emat) > 0.5) & causal
        bias = jnp.where(keep, 0.0, NEG_INF)
        for c in range(seq // NSA_KC):
            bias_scr[g, c] = bias[:, c * NSA_KC:(c + 1) * NSA_KC]
    m_scr[...] = jnp.full(m_scr.shape, NEG_INF, F32)
    l_scr[...] = jnp.zeros(l_scr.shape, F32)
    acc_scr[...] = jnp.zeros(acc_scr.shape, F32)

    def body(c, carry):
        k0 = pl.multiple_of(c * NSA_KC, NSA_KC)
        for g in range(NSA_KVH):
            gs = slice(g * NSA_DH, (g + 1) * NSA_DH)
            s = add_bias(_dot_nt(q4s[g], ks_ref[pl.ds(k0, NSA_KC), gs]) * scale, bias_scr[g, c])
            m_old = m_scr[g]
            m_new = jnp.maximum(m_old, _row_reduce(s, jnp.maximum, jnp.max))
            p = jnp.exp(s - m_new)
            alpha = jnp.exp(m_old - m_new)
            l_scr[g] = alpha * l_scr[g] + _row_reduce(p, jnp.add, jnp.sum)
            acc_scr[g] = alpha * acc_scr[g] + _dot(p.astype(BF16), vs_ref[pl.ds(k0, NSA_KC), gs])
            m_scr[g] = m_new
        return carry

    lax.fori_loop(0, nk, body, 0)
    for g in range(NSA_KVH):
        gs = slice(g * NSA_DH, (g + 1) * NSA_DH)
        q4 = q4s[g]
        o_s = acc_scr[g] / l_scr[g]
        s_w = add_bias(_dot_nt(q4, kw_ref[pl.ds(s0, slab), gs]) * scale, wb1)
        p_w = jnp.exp(s_w - _row_reduce(s_w, jnp.maximum, jnp.max))
        o_w = (_dot(p_w.astype(BF16), vw_ref[pl.ds(s0, slab), gs])
               / _row_reduce(p_w, jnp.add, jnp.sum))
        o_c = o_cs[g]
        for r in range(NSA_REP):
            h = g * NSA_REP + r
            rs = slice(r * tq, (r + 1) * tq)
            o = (gates[:, 3 * h:3 * h + 1] * o_c[rs] + gates[:, 3 * h + 1:3 * h + 2] * o_s[rs]
                 + gates[:, 3 * h + 2:3 * h + 3] * o_w[rs])
            o_ref[:, h * NSA_DH:(h + 1) * NSA_DH] = o.astype(BF16)


def _nsa_prompt_call(qn, kc, bs, bw, gates, nb, seq):
    tq = 128
    nq = seq // tq
    slab = min(seq, WINDOW + tq)
    kv = lambda col: pl.BlockSpec((seq, N_KV), lambda b, i: (b, col))
    kern = functools.partial(_nsa_prompt_kernel, tq=tq, seq=seq, slab=slab)
    rows = NSA_REP * tq
    return pl.pallas_call(
        kern,
        grid=(nb, nq),
        in_specs=[pl.BlockSpec((tq, D_C), lambda b, i: (b * nq + i, 0)),
                  pl.BlockSpec((1, seq // CMP_STRIDE, 2 * N_KV), lambda b, i: (b, 0, 0)),
                  kv(0), kv(1), kv(0), kv(1),
                  pl.BlockSpec((tq, LANE), lambda b, i: (b * nq + i, 0))],
        out_specs=pl.BlockSpec((tq, D_C), lambda b, i: (b * nq + i, 0)),
        out_shape=jax.ShapeDtypeStruct((nb * seq, D_C), BF16),
        scratch_shapes=[pltpu.VMEM((NSA_KVH, seq // NSA_KC, tq, NSA_KC), F32),
                        pltpu.VMEM((NSA_KVH, rows, 1), F32), pltpu.VMEM((NSA_KVH, rows, 1), F32),
                        pltpu.VMEM((NSA_KVH, rows, NSA_DH), F32)],
        compiler_params=_cparams(("arbitrary", "arbitrary")),
        name="nsa_prompt",
    )(qn, kc, bs, bs, bw, bw, gates)


def _nsa_s_sel_kernel(q_ref, kc_ref, oc_ref, sel_ref, *, past, dl, n_tiles):
    rows = NSA_REP * dl
    ncp = past // CMP_STRIDE
    n_slc = (past + dl + SEL_BLOCK - 1) // SEL_BLOCK
    nbp = (n_slc + LANE - 1) // LANE * LANE
    per_tile = PG * PAGE // SEL_BLOCK
    gmat = (lax.broadcasted_iota(I32, (ncp, nbp), 0) // (SEL_BLOCK // CMP_STRIDE)
            == lax.broadcasted_iota(I32, (ncp, nbp), 1)).astype(BF16)
    rmat = (lax.broadcasted_iota(I32, (rows, rows), 0) // NSA_REP
            == lax.broadcasted_iota(I32, (rows, rows), 1) // NSA_REP).astype(BF16)

    def qpos_of(shape):
        return past + lax.broadcasted_iota(I32, shape, 0) // NSA_REP

    lane = lax.broadcasted_iota(I32, (rows, LANE), 1)
    for g in range(NSA_KVH):
        gs = slice(g * NSA_DH, (g + 1) * NSA_DH)
        p_c, o_c = _cmp_branch(q_ref[0, g], kc_ref[0, 0, :, gs],
                               kc_ref[0, 0, :, N_KV + g * NSA_DH:N_KV + (g + 1) * NSA_DH],
                               qpos_of((rows, ncp)), ncp - 1)
        oc_ref[0, g] = o_c
        psum = _dot2g(rmat, p_c)
        imp = _dot2(psum, gmat)
        sel = _select_blocks(imp, qpos_of((rows, nbp)), n_slc)
        for tl in range(n_tiles):
            lo = tl * per_tile
            chunk = sel[:, (lo // LANE) * LANE:(lo // LANE + 1) * LANE]
            sh = (LANE - lo % LANE) % LANE
            if sh:
                chunk = pltpu.roll(chunk, sh, 1)
            sel_ref[0, g, tl] = jnp.where(lane < per_tile, chunk, 0.0)


def _nsa_s_sel_call(q16, kc, layer, past, dl):
    nb = q16.shape[0]
    rows = NSA_REP * dl
    n_tiles = past // (PG * PAGE) + 1
    kern = functools.partial(_nsa_s_sel_kernel, past=past, dl=dl, n_tiles=n_tiles)
    return pl.pallas_call(
        kern,
        grid=(nb,),
        in_specs=[pl.BlockSpec((1, NSA_KVH, rows, NSA_DH), lambda b: (b, 0, 0, 0)),
                  pl.BlockSpec((1, 1, past // CMP_STRIDE, 2 * N_KV), lambda b: (layer, b, 0, 0))],
        out_specs=[pl.BlockSpec((1, NSA_KVH, rows, NSA_DH), lambda b: (b, 0, 0, 0)),
                   pl.BlockSpec((1, NSA_KVH, n_tiles, rows, LANE), lambda b: (b, 0, 0, 0, 0))],
        out_shape=[jax.ShapeDtypeStruct((nb, NSA_KVH, rows, NSA_DH), F32),
                   jax.ShapeDtypeStruct((nb, NSA_KVH, n_tiles, rows, LANE), F32)],
        compiler_params=_cparams(("arbitrary",)),
        name="nsa_sample_select",
    )(q16, kc)


def _nsa_s_attn_kernel(pt_ref, *refs, past, dl, npg):
    pages = refs[:PG]
    (q_ref, selt_ref, sell_ref, emat_ref, oc_ref, ns_ref, nw_ref, wp_ref, g_ref,
     o_ref, m_scr, l_scr, a_scr) = refs[PG:]
    pg = pl.program_id(1)
    rows = NSA_REP * dl
    scale = NSA_DH ** -0.5
    wb = wp_ref.shape[2] // KV_TILES

    def tok_rows(ref, c, n):
        return ref[0, 0, pl.ds(c, n, stride=KV_TILES), :].astype(BF16)

    @pl.when(pg == 0)
    def _():
        m_scr[...] = jnp.full(m_scr.shape, NEG_INF, F32)
        l_scr[...] = jnp.zeros(l_scr.shape, F32)
        a_scr[...] = jnp.zeros(a_scr.shape, F32)

    for g in range(NSA_KVH):
        q = q_ref[0, g]
        kmask = _dot(selt_ref[0, g, 0].astype(BF16), emat_ref[...]) > 0.5
        s = jnp.concatenate(
            [_dot_nt(q, tok_rows(pages[k], g, PAGE)) for k in range(PG)], axis=1) * scale
        s = jnp.where(kmask, s, NEG_INF)
        m_old = m_scr[g]
        m_new = jnp.maximum(m_old, jnp.max(s, axis=-1, keepdims=True))
        p = jnp.where(kmask, jnp.exp(s - m_new), 0.0)
        alpha = jnp.exp(m_old - m_new)
        l_scr[g] = alpha * l_scr[g] + jnp.sum(p, axis=-1, keepdims=True)
        acc = alpha * a_scr[g]
        pb = p.astype(BF16)
        for k in range(PG):
            acc = acc + _dot(pb[:, k * PAGE:(k + 1) * PAGE], tok_rows(pages[k], NSA_KVH + g, PAGE))
        a_scr[g] = acc
        m_scr[g] = m_new

    @pl.when(pg == npg - 1)
    def _():
        gates = g_ref[0]
        tok = lax.broadcasted_iota(I32, (rows, SUBLANE), 0) // NSA_REP
        ncol = lax.broadcasted_iota(I32, (rows, SUBLANE), 1)
        for g in range(NSA_KVH):
            gs = slice(g * NSA_DH, (g + 1) * NSA_DH)
            vsl = slice(N_KV + g * NSA_DH, N_KV + (g + 1) * NSA_DH)
            q = q_ref[0, g]
            nmask = (sell_ref[0, g, 0][:, 0:1] > 0.5) & (ncol <= tok) & (ncol < dl)
            s_n = jnp.where(nmask, _dot_nt(q, ns_ref[0, :, gs].astype(BF16)) * scale, NEG_INF)
            m_old = m_scr[g]
            m_new = jnp.maximum(m_old, jnp.max(s_n, axis=-1, keepdims=True))
            p_n = jnp.where(nmask, jnp.exp(s_n - m_new), 0.0)
            alpha = jnp.exp(m_old - m_new)
            lsum = alpha * l_scr[g] + jnp.sum(p_n, axis=-1, keepdims=True)
            acc = alpha * a_scr[g] + _dot(p_n.astype(BF16), ns_ref[0, :, vsl].astype(BF16))
            o_s = acc / jnp.maximum(lsum, 1e-30)
            wrow = lax.broadcasted_iota(I32, (rows, wb), 1)
            wtok = lax.broadcasted_iota(I32, (rows, wb), 0) // NSA_REP
            pmask = (wrow > wtok + (wb - WINDOW)) & (wrow + (past - wb) >= 0)
            s_p = jnp.where(pmask, _dot_nt(q, tok_rows(wp_ref, g, wb)) * scale, NEG_INF)
            wmask = (ncol <= tok) & (ncol < dl)
            s_q = jnp.where(wmask, _dot_nt(q, nw_ref[0, :, gs].astype(BF16)) * scale, NEG_INF)
            mw = jnp.maximum(jnp.max(s_p, axis=-1, keepdims=True), jnp.max(s_q, axis=-1, keepdims=True))
            p_p = jnp.where(pmask, jnp.exp(s_p - mw), 0.0)
            p_q = jnp.where(wmask, jnp.exp(s_q - mw), 0.0)
            den = jnp.maximum(jnp.sum(p_p, axis=-1, keepdims=True) + jnp.sum(p_q, axis=-1, keepdims=True),
                              1e-30)
            o_w = (_dot(p_p.astype(BF16), tok_rows(wp_ref, NSA_KVH + g, wb))
                   + _dot(p_q.astype(BF16), nw_ref[0, :, vsl].astype(BF16))) / den
            gg = gates[g]
            o = gg[:, 0:1] * oc_ref[0, g] + gg[:, 1:2] * o_s + gg[:, 2:3] * o_w
            o_ref[0, g] = o.astype(BF16)


def _nsa_s_attn_call(cache, pt_flat, layer, q16, selt, emat, o_c, new_slc, new_win, win_prev, gates16,
                     past, dl):
    nb = q16.shape[0]
    rows = NSA_REP * dl
    n_pages = past // PAGE
    npg = n_pages // PG
    wrows = win_prev.shape[2]
    b4 = lambda shape: pl.BlockSpec((1,) + shape, lambda b, g, pt: (b,) + (0,) * len(shape))
    grid_spec = pltpu.PrefetchScalarGridSpec(
        num_scalar_prefetch=1,
        grid=(nb, npg),
        in_specs=_page_specs(n_pages, lambda b, g, pt: layer, lambda b, g, pt: b, lambda b, g, pt: g,
                             (PAGE * KV_TILES, LANE))
        + [b4((NSA_KVH, rows, NSA_DH)),
           pl.BlockSpec((1, NSA_KVH, 1, rows, LANE), lambda b, g, pt: (b, 0, g, 0, 0)),
           pl.BlockSpec((1, NSA_KVH, 1, rows, LANE), lambda b, g, pt: (b, 0, npg, 0, 0)),
           pl.BlockSpec((LANE, PG * PAGE), lambda b, g, pt: (0, 0)),
           b4((NSA_KVH, rows, NSA_DH)),
           b4((SUBLANE, 2 * N_KV)), b4((SUBLANE, 2 * N_KV)),
           pl.BlockSpec((1, 1, wrows, LANE), lambda b, g, pt: (layer, b, 0, 0)),
           b4((NSA_KVH, rows, LANE))],
        out_specs=b4((NSA_KVH, rows, NSA_DH)),
        scratch_shapes=[pltpu.VMEM((NSA_KVH, rows, 1), F32), pltpu.VMEM((NSA_KVH, rows, 1), F32),
                        pltpu.VMEM((NSA_KVH, rows, NSA_DH), F32)],
    )
    kern = functools.partial(_nsa_s_attn_kernel, past=past, dl=dl, npg=npg)
    return pl.pallas_call(
        kern,
        grid_spec=grid_spec,
        out_shape=jax.ShapeDtypeStruct((nb, NSA_KVH, rows, NSA_DH), BF16),
        compiler_params=_cparams(("arbitrary", "arbitrary")),
        name="nsa_sample_attn",
    )(pt_flat, *([cache] * PG), q16, selt, selt, emat, o_c, new_slc, new_win, win_prev, gates16)


def _merge_kernel(ya_ref, yb_ref, yc_ref, ga_ref, gb_ref, gc_ref, wa_ref, wb_ref, wc_ref, o_ref):
    m = (jax.nn.sigmoid(ga_ref[...]) * _dot(ya_ref[...], wa_ref[0])
         + jax.nn.sigmoid(gb_ref[...]) * _dot(yb_ref[...], wb_ref[0])
         + jax.nn.sigmoid(gc_ref[...]) * _dot(yc_ref[...], wc_ref[0]))
    o_ref[...] = m.astype(BF16)


def _merge_call(ya, yb, yc, hin, wa, wb, wc, layer):
    m = ya.shape[0]
    tm = min(256, m)
    const = lambda shape: pl.BlockSpec((1,) + shape, lambda i: (layer, 0, 0))
    gate = lambda col: pl.BlockSpec((tm, D_MODEL), lambda i: (i, col))
    return pl.pallas_call(
        _merge_kernel,
        grid=(m // tm,),
        in_specs=[pl.BlockSpec((tm, D_A), lambda i: (i, 0)), pl.BlockSpec((tm, D_B), lambda i: (i, 0)),
                  pl.BlockSpec((tm, D_C), lambda i: (i, 0)), gate(0), gate(1), gate(2),
                  const((D_A, D_MODEL)), const((D_B, D_MODEL)), const((D_C, D_MODEL))],
        out_specs=pl.BlockSpec((tm, D_MODEL), lambda i: (i, 0)),
        out_shape=jax.ShapeDtypeStruct((m, D_MODEL), BF16),
        compiler_params=_cparams(("arbitrary",)),
        name="merge",
    )(ya, yb, yc, hin, hin, hin, wa, wb, wc)


def _outproj_kernel(m_ref, x_ref, g_ref, w_ref, o_ref):
    o_ref[...] = x_ref[...] + g_ref[0] * _dot(m_ref[...], w_ref[0])


def _outproj_call(merged, x, modx, rows_per_group, w, layer):
    m = x.shape[0]
    tm = min(512, m)
    return pl.pallas_call(
        _outproj_kernel,
        grid=(m // tm,),
        in_specs=[pl.BlockSpec((tm, D_MODEL), lambda i: (i, 0)),
                  pl.BlockSpec((tm, D_MODEL), lambda i: (i, 0)),
                  _mod_spec(modx, tm, rows_per_group, 2),
                  pl.BlockSpec((1, D_MODEL, D_MODEL), lambda i: (layer, 0, 0))],
        out_specs=pl.BlockSpec((tm, D_MODEL), lambda i: (i, 0)),
        out_shape=jax.ShapeDtypeStruct((m, D_MODEL), F32),
        compiler_params=_cparams(("arbitrary",)),
        name="outproj",
    )(merged, x, modx, w)


def _ffn_tail(ua, ua1, ua2, ub, cw_ref, cb_ref, wd_ref):
    cw = cw_ref[0]
    conv = cb_ref[0] + cw[0:1] * ua2 + cw[1:2] * ua1 + cw[2:3] * ua
    return _dot((jax.nn.gelu(conv) * ub).astype(BF16), wd_ref[0])


def _ffn_prompt_kernel(x_ref, xh_ref, sc_ref, sh_ref, g_ref, nw_ref, cp_ref, wua_ref, wub_ref, cw_ref,
                       cb_ref, wd_ref, o_ref, tail_ref, h_scr, hh_scr, acc, *, tm, seq):
    f = pl.program_id(1)

    @pl.when(f == 0)
    def _():
        nw, sc, sh = nw_ref[...], sc_ref[0], sh_ref[0]
        h_scr[...] = (_rms(x_ref[...], nw) * (1.0 + sc) + sh).astype(BF16)
        hh_scr[...] = (_rms(xh_ref[...], nw) * (1.0 + sc) + sh).astype(BF16)
        acc[...] = jnp.zeros(acc.shape, F32)

    wua = wua_ref[0]
    ua = _dot(h_scr[...], wua)
    ub = _dot(h_scr[...], wub_ref[0])
    at_start = (pl.program_id(0) * tm) % seq == 0
    prev = jnp.where(at_start, cp_ref[0], _dot(hh_scr[...], wua))
    row = lax.broadcasted_iota(I32, ua.shape, 0)
    p1, p2 = prev[SUBLANE - 1:SUBLANE], prev[SUBLANE - 2:SUBLANE - 1]
    ua1 = jnp.where(row == 0, p1, pltpu.roll(ua, 1, 0))
    ua2 = jnp.where(row == 0, p2, jnp.where(row == 1, p1, pltpu.roll(ua, 2, 0)))
    tail_ref[0] = ua[tm - SUBLANE:tm]
    acc[...] += _ffn_tail(ua, ua1, ua2, ub, cw_ref, cb_ref, wd_ref)

    @pl.when(f == pl.num_programs(1) - 1)
    def _():
        o_ref[...] = x_ref[...] + g_ref[0] * acc[...]


def _ffn_prompt_call(x, modx, seq, nw, cprev, fw, layer):
    m = x.shape[0]
    tm = min(512, m)
    hb = tm // SUBLANE
    kern = functools.partial(_ffn_prompt_kernel, tm=tm, seq=seq)
    return pl.pallas_call(
        kern,
        grid=(m // tm, D_FFP // TF),
        in_specs=[pl.BlockSpec((tm, D_MODEL), lambda i, f: (i, 0)),
                  pl.BlockSpec((SUBLANE, D_MODEL), lambda i, f: (jnp.maximum(i * hb - 1, 0), 0)),
                  _mod_spec(modx, tm, seq, 4), _mod_spec(modx, tm, seq, 3), _mod_spec(modx, tm, seq, 5),
                  pl.BlockSpec((1, D_MODEL), lambda i, f: (0, 0)),
                  pl.BlockSpec((1, SUBLANE, TF), lambda i, f: ((i * tm) // seq, 0, f)),
                  pl.BlockSpec((1, D_MODEL, TF), lambda i, f: (layer, 0, f)),
                  pl.BlockSpec((1, D_MODEL, TF), lambda i, f: (layer, 0, D_FFP // TF + f)),
                  pl.BlockSpec((1, CONV_W, TF), lambda i, f: (layer, 0, f)),
                  pl.BlockSpec((1, 1, TF), lambda i, f: (layer, 0, f)),
                  pl.BlockSpec((1, TF, D_MODEL), lambda i, f: (layer, f, 0))],
        out_specs=[pl.BlockSpec((tm, D_MODEL), lambda i, f: (i, 0)),
                   pl.BlockSpec((1, SUBLANE, TF), lambda i, f: (i, 0, f))],
        out_shape=[jax.ShapeDtypeStruct((m, D_MODEL), F32),
                   jax.ShapeDtypeStruct((m // tm, SUBLANE, D_FFP), F32)],
        scratch_shapes=[pltpu.VMEM((tm, D_MODEL), BF16), pltpu.VMEM((SUBLANE, D_MODEL), BF16),
                        pltpu.VMEM((tm, D_MODEL), F32)],
        compiler_params=_cparams(("arbitrary", "arbitrary")),
        name="ffn_prompt",
    )(x, x, modx, modx, modx, nw, cprev, fw["wu"], fw["wu"], fw["cw"], fw["cb"], fw["wd"])


def _ffn_sample_kernel(x_ref, sc_ref, sh_ref, g_ref, nw_ref, p1_ref, p2_ref, wua_ref, wub_ref, cw_ref,
                       cb_ref, wd_ref, o_ref, ua_ref, h_scr, acc, *, dl):
    f = pl.program_id(0)

    @pl.when(f == 0)
    def _():
        h_scr[...] = (_rms(x_ref[...], nw_ref[...]) * (1.0 + sc_ref[0]) + sh_ref[0]).astype(BF16)
        acc[...] = jnp.zeros(acc.shape, F32)

    ua = _dot(h_scr[...], wua_ref[0])
    ub = _dot(h_scr[...], wub_ref[0])
    tin = lax.broadcasted_iota(I32, ua.shape, 0) % dl
    ua1 = jnp.where(tin >= 1, pltpu.roll(ua, 1, 0), p1_ref[...])
    ua2 = jnp.where(tin >= 2, pltpu.roll(ua, 2, 0), p2_ref[...])
    ua_ref[...] = ua
    acc[...] += _ffn_tail(ua, ua1, ua2, ub, cw_ref, cb_ref, wd_ref)

    @pl.when(f == pl.num_programs(0) - 1)
    def _():
        o_ref[...] = x_ref[...] + g_ref[0] * acc[...]


def _ffn_sample_call(x, modx, dl, nw, p1, p2, fw, layer):
    m = x.shape[0]
    kern = functools.partial(_ffn_sample_kernel, dl=dl)
    full = lambda i: pl.BlockSpec((m, D_MODEL), lambda f: (0, 0))
    return pl.pallas_call(
        kern,
        grid=(D_FFP // TF,),
        in_specs=[full(0),
                  pl.BlockSpec((1, m, D_MODEL), lambda f: (0, 0, 4)),
                  pl.BlockSpec((1, m, D_MODEL), lambda f: (0, 0, 3)),
                  pl.BlockSpec((1, m, D_MODEL), lambda f: (0, 0, 5)),
                  pl.BlockSpec((1, D_MODEL), lambda f: (0, 0)),
                  pl.BlockSpec((m, TF), lambda f: (0, f)), pl.BlockSpec((m, TF), lambda f: (0, f)),
                  pl.BlockSpec((1, D_MODEL, TF), lambda f: (layer, 0, f)),
                  pl.BlockSpec((1, D_MODEL, TF), lambda f: (layer, 0, D_FFP // TF + f)),
                  pl.BlockSpec((1, CONV_W, TF), lambda f: (layer, 0, f)),
                  pl.BlockSpec((1, 1, TF), lambda f: (layer, 0, f)),
                  pl.BlockSpec((1, TF, D_MODEL), lambda f: (layer, f, 0))],
        out_specs=[full(0), pl.BlockSpec((m, TF), lambda f: (0, f))],
        out_shape=[jax.ShapeDtypeStruct((m, D_MODEL), F32),
                   jax.ShapeDtypeStruct((m, D_FFP), F32)],
        scratch_shapes=[pltpu.VMEM((m, D_MODEL), BF16), pltpu.VMEM((m, D_MODEL), F32)],
        compiler_params=_cparams(("arbitrary",)),
        name="ffn_sample",
    )(x, modx, modx, modx, nw, p1, p2, fw["wu"], fw["wu"], fw["cw"], fw["cb"], fw["wd"])


def _rope_tables(pos):
    half = ROPE_DIM // 2
    inv = jnp.exp(jnp.arange(half, dtype=F32) * (-math.log(ROPE_THETA) / half))
    ang = pos.astype(F32)[:, None] * inv[None, :]
    cos, sin = jnp.cos(ang), jnp.sin(ang)
    n = pos.shape[0]
    rest = NSA_DH - ROPE_DIM
    cos_t = jnp.concatenate([cos, cos, jnp.ones((n, rest), F32)], axis=1)
    sin_a = jnp.concatenate([-sin, jnp.zeros((n, NSA_DH - half), F32)], axis=1)
    sin_b = jnp.concatenate([jnp.zeros((n, half), F32), sin, jnp.zeros((n, rest), F32)], axis=1)
    return cos_t, sin_a, sin_b


_W_IN_SIZES = (D_A, D_B, D_B, D_B, D_B, D_C, 6 * N_KV, 3 * NSA_HEADS, 3 * D_MODEL)
_W_IN_DST = (OFF_UA, OFF_HQ, OFF_HF, OFF_HI, OFF_HGT, OFF_NQ, OFF_NKV, OFF_NGT, OFF_MGT)
N_IN = sum(_W_IN_SIZES)


W_IN_RB = 256


def _w_in_src_row(i):
    src = jnp.int32(0)
    start = 0
    for size, dst in zip(_W_IN_SIZES, _W_IN_DST):
        inside = (i * W_IN_RB >= dst) & (i * W_IN_RB < dst + max(size, W_IN_RB))
        src = jnp.where(inside, start + i * W_IN_RB - dst, src)
        start += size
    return src


def _w_in_prep_kernel(x_ref, o_ref):
    i = pl.program_id(1)
    x = x_ref[0].astype(BF16)
    row = lax.broadcasted_iota(I32, x.shape, 0)
    o_ref[0] = jnp.where((i * W_IN_RB == OFF_NGT) & (row >= 3 * NSA_HEADS), jnp.zeros_like(x), x)


def _permute_w_in(w_in):
    depth, d, _ = w_in.shape
    w_t = jnp.swapaxes(w_in, 1, 2)
    return pl.pallas_call(
        _w_in_prep_kernel,
        grid=(depth, N_INP // W_IN_RB),
        in_specs=[pl.BlockSpec((pl.Element(1), pl.Element(W_IN_RB), pl.Element(d)),
                               lambda l, i: (l, pl.multiple_of(_w_in_src_row(i), SUBLANE), 0))],
        out_specs=pl.BlockSpec((1, W_IN_RB, d), lambda l, i: (l, i, 0)),
        out_shape=jax.ShapeDtypeStruct((depth, N_INP, d), BF16),
        compiler_params=_cparams(("arbitrary", "arbitrary")),
        name="w_in_prep",
    )(w_t)


def _w_up_prep_kernel(x_ref, o_ref):
    o_ref[0, :, :D_FF] = x_ref[0].astype(BF16)
    o_ref[0, :, D_FF:] = jnp.zeros((o_ref.shape[1], D_FFP - D_FF), BF16)


def _pad_w_up(w_up):
    depth, d, _ = w_up.shape
    r = 512
    return pl.pallas_call(
        _w_up_prep_kernel,
        grid=(depth, d // r, 2),
        in_specs=[pl.BlockSpec((1, r, D_FF), lambda l, i, h: (l, i, h))],
        out_specs=pl.BlockSpec((1, r, D_FFP), lambda l, i, h: (l, i, h)),
        out_shape=jax.ShapeDtypeStruct((depth, d, 2 * D_FFP), BF16),
        compiler_params=_cparams(("arbitrary", "arbitrary", "arbitrary")),
        name="w_up_prep",
    )(w_up)


def _s5_params(a_re, a_im, log_dt, b_re, b_im, c_re, c_im, d, w_glu, b_glu):
    dt = jnp.exp(log_dt)[:, None]
    mag = jnp.exp(a_re * dt)
    ab_re, ab_im = mag * jnp.cos(a_im * dt), mag * jnp.sin(a_im * dt)
    den = a_re * a_re + a_im * a_im
    cf_re = ((ab_re - 1.0) * a_re + ab_im * a_im) / den
    cf_im = (ab_im * a_re - (ab_re - 1.0) * a_im) / den
    bb_re = cf_re[..., None] * b_re - cf_im[..., None] * b_im
    bb_im = cf_re[..., None] * b_im + cf_im[..., None] * b_re
    eye = jnp.eye(S5_G, dtype=F32)
    bdiag = lambda bb: jnp.einsum("gnc,gh->gchn", bb, eye).reshape(D_A, S5_W).astype(BF16)
    cdiag = lambda cc: jnp.einsum("gcn,gh->gnhc", cc, eye).reshape(S5_W, D_A).astype(BF16)
    pr, pi = ab_re.reshape(1, S5_W), ab_im.reshape(1, S5_W)
    prs, pis = [pr], [pi]
    for _ in range(SUBLANE - 1):
        nr = prs[-1] * pr - pis[-1] * pi
        ni = prs[-1] * pi + pis[-1] * pr
        prs.append(nr)
        pis.append(ni)
    return dict(bbr=bdiag(bb_re), bbi=bdiag(bb_im), ccr=cdiag(c_re), cci=cdiag(c_im),
                apr=jnp.concatenate(prs, axis=0), api=jnp.concatenate(pis, axis=0),
                d=d.reshape(1, D_A), wg=w_glu.astype(BF16), bg=b_glu.reshape(1, D_A))


def _hgrn_params(lb):
    pos = lb > 0
    lb_safe = jnp.where(pos, lb, 1.0)
    z = jnp.zeros_like(lb)
    return jnp.stack([jnp.log(lb_safe), jnp.log1p(-lb), 1.0 - lb, pos.astype(F32), z, z, z, z], axis=0)


def _cmp_params(cmp_a, w1, b1, w2):
    def lanes(a):
        return jnp.concatenate([a[0], a[0], a[1], a[1]], axis=-1)
    return dict(a1=lanes(cmp_a[:, :CMP_STRIDE]), a2=lanes(cmp_a[:, CMP_STRIDE:]),
                w1=w1.astype(BF16), b1=b1, w2=w2.astype(BF16))


def _ffn_params(w_up, conv_w, conv_b, w_down):
    padc = lambda a: jnp.pad(a, [(0, 0)] * (a.ndim - 1) + [(0, D_FFP - D_FF)])
    wu = _pad_w_up(w_up)
    wd = jnp.pad(w_down.astype(BF16), ((0, 0), (0, D_FFP - D_FF), (0, 0)))
    return dict(wu=wu, cw=padc(conv_w), cb=padc(conv_b)[:, None, :], wd=wd)


def kernel(x_prompt, x_sample, cache_cmp, cache_slc, state_win, state_s5, state_hgrn, state_conv,
           page_table, c_prompt, c_sample, w_mod, b_mod, norm1_w, norm2_w, w_in,
           s5_a_re, s5_a_im, s5_log_dt, s5_b_re, s5_b_im, s5_c_re, s5_c_im, s5_d, s5_w_glu, s5_b_glu,
           hg_lb_logits, hg_norm_w, nsa_q_norm, nsa_k_norm, cmp_a, cmp_w1, cmp_b1, cmp_w2,
           w_branch_a, w_branch_b, w_branch_c, w_out, w_up, conv_w, conv_b, w_down):
    bsz, seq, _ = x_prompt.shape
    dbsz, dl, _ = x_sample.shape
    depth = w_in.shape[0]
    n_pages = page_table.shape[1]
    past = n_pages * PAGE
    wb = state_win.shape[2]
    mp, ms = bsz * seq, dbsz * dl
    rows = NSA_REP * dl
    assert seq % 512 == 0 and seq // SEL_BLOCK <= LANE and n_pages % PG == 0 and dl <= SUBLANE

    probs = jax.nn.softmax(hg_lb_logits.astype(F32), axis=0)
    lower_bounds = jnp.cumsum(probs, axis=0) - probs[0:1]
    w_in_p = _permute_w_in(w_in)
    wa, wbr, wc, wo = (w.astype(BF16) for w in (w_branch_a, w_branch_b, w_branch_c, w_out))
    fw = _ffn_params(w_up, conv_w, conv_b, w_down)
    rope_p = _rope_tables(jnp.arange(seq, dtype=I32))
    rope_s = tuple(jnp.tile(t, (dbsz, 1)) for t in _rope_tables(past + jnp.arange(dl, dtype=I32)))
    pt_flat = page_table.reshape(-1).astype(I32)
    cache_cmp4 = cache_cmp.reshape(depth, -1, PAGE * KV_TILES, LANE)
    cache_slc4 = cache_slc.reshape(depth, -1, PAGE * KV_TILES, LANE)
    win_prev4 = state_win.reshape(depth, dbsz, wb * KV_TILES, LANE)
    emat = (jnp.arange(PG * PAGE)[None, :] // SEL_BLOCK == jnp.arange(LANE)[:, None]).astype(BF16)

    nr = -(-(bsz + dbsz) // 16) * 16
    c_all = jnp.concatenate([c_prompt, c_sample, jnp.zeros((nr - bsz - dbsz, D_MODEL), F32)], axis=0)
    mod = _mod_call(c_all, w_mod, b_mod)

    cmp_lanes = lambda a: jnp.concatenate([a[:, 0], a[:, 0], a[:, 1], a[:, 1]], axis=-1).reshape(
        depth, SUB_ROWS, LANE)
    first, second = _cmp_pool_cache_call(cache_cmp4, pt_flat, dbsz, n_pages,
                                         cmp_lanes(cmp_a[:, :, :CMP_STRIDE]),
                                         cmp_lanes(cmp_a[:, :, CMP_STRIDE:]))
    kc_past = _cmp_mlp_cache_call(first, second, cmp_w1.astype(BF16), cmp_b1, cmp_w2.astype(BF16))

    xp = x_prompt.reshape(mp, D_MODEL)
    xs = x_sample.reshape(ms, D_MODEL)
    outs = [[] for _ in range(12)]
    row_bufs = None
    for l in range(depth):
        s5p = _s5_params(s5_a_re[l], s5_a_im[l], s5_log_dt[l], s5_b_re[l], s5_b_im[l], s5_c_re[l],
                         s5_c_im[l], s5_d[l], s5_w_glu[l], s5_b_glu[l])
        hgp = _hgrn_params(lower_bounds[l])
        hgn = hg_norm_w[l].reshape(1, HG_DV)
        cp = _cmp_params(cmp_a[l], cmp_w1[l], cmp_b1[l], cmp_w2[l])
        n1, n2 = norm1_w[l].reshape(1, D_MODEL), norm2_w[l].reshape(1, D_MODEL)
        qn, kn = nsa_q_norm[l].reshape(1, NSA_DH), nsa_k_norm[l]
        modp = mod[l, :bsz].reshape(bsz, 1, 6 * D_MODEL)
        mods = jnp.repeat(mod[l, bsz:bsz + dbsz], dl, axis=0).reshape(1, ms, 6 * D_MODEL)

        hin = _inproj_call(xp, modp, seq, n1, w_in_p, l)
        ya, s5_new = _s5_call(hin, OFF_UA // D_A, bsz, seq, jnp.zeros((bsz, 2, S5_W), F32), s5p,
                              SUBLANE - 1)
        yb, hg_new = _hgrn_call(hin, tuple(o // D_B for o in (OFF_HQ, OFF_HF, OFF_HI, OFF_HGT)), bsz, seq,
                                seq, jnp.zeros((bsz, HG_HEADS, HG_DK, HG_DV), F32), hgp, hgn)
        qo, *row_bufs, bc, bs, bw, gates = _nsa_prep_call(hin, rope_p, seq, qn, kn, l, depth, row_bufs)
        kc = _cmp_prompt_call(row_bufs[0], bsz, seq, cp, l)
        yc = _nsa_prompt_call(qo, kc, bs, bw, gates, bsz, seq)
        merged = _merge_call(ya, yb, yc, hin, wa, wbr, wc, l)
        xp = _outproj_call(merged, xp, modp, seq, wo, l)
        xp, tails = _ffn_prompt_call(xp, modp, seq, n2, jnp.zeros((bsz, SUBLANE, D_FFP), F32), fw, l)
        conv_new = tails.reshape(bsz, -1, SUBLANE, D_FFP)[:, -1, SUBLANE - (CONV_W - 1):, :D_FF]
        for i, o in ((0, s5_new.reshape(bsz, 2, S5_G, S5_N)), (1, hg_new), (5, conv_new)):
            outs[2 * i].append(o)

        hin = _inproj_call(xs, mods, dl, n1, w_in_p, l)
        padt = lambda a, n: jnp.pad(a.reshape(dbsz, dl, -1), ((0, 0), (0, n - dl), (0, 0))).reshape(dbsz * n, -1)
        u_pad = padt(hin[:, OFF_UA:OFF_UA + D_A], SUBLANE)
        ya, s5_new = _s5_call(u_pad, 0, dbsz, SUBLANE, state_s5[l].reshape(dbsz, 2, S5_W), s5p, dl - 1)
        ya = ya.reshape(dbsz, SUBLANE, D_A)[:, :dl].reshape(ms, D_A)
        hg_pad = padt(hin[:, OFF_HQ:OFF_HQ + 4 * D_B], HG_C)
        yb, hg_new = _hgrn_call(hg_pad, (0, 1, 2, 3), dbsz, HG_C, dl, state_hgrn[l], hgp, hgn)
        yb = yb.reshape(dbsz, HG_C, D_B)[:, :dl].reshape(ms, D_B)
        qo, rc, rs, rw, bc, bs, bw, gates = _nsa_prep_call(hin, rope_s, dl, qn, kn)
        q16 = qo.reshape(dbsz, dl, NSA_KVH, NSA_REP, NSA_DH).transpose(0, 2, 1, 3, 4).reshape(
            dbsz, NSA_KVH, rows, NSA_DH)
        o_c, selt = _nsa_s_sel_call(q16, kc_past, l, past, dl)
        g16 = gates[:, :3 * NSA_HEADS].reshape(dbsz, dl, NSA_KVH, NSA_REP, 3).transpose(0, 2, 1, 3, 4)
        g16 = jnp.pad(g16.reshape(dbsz, NSA_KVH, rows, 3), ((0, 0), (0, 0), (0, 0), (0, LANE - 3)))
        pad8 = lambda r: jnp.pad(r.reshape(dbsz, dl, 2 * N_KV), ((0, 0), (0, SUBLANE - dl), (0, 0)))
        yc = _nsa_s_attn_call(cache_slc4, pt_flat, l, q16, selt, emat, o_c, pad8(rs), pad8(rw), win_prev4,
                              g16, past, dl)
        yc = yc.reshape(dbsz, NSA_KVH, dl, NSA_REP, NSA_DH).transpose(0, 2, 1, 3, 4).reshape(ms, D_C)
        merged = _merge_call(ya, yb, yc, hin, wa, wbr, wc, l)
        xs = _outproj_call(merged, xs, mods, dl, wo, l)
        st = jnp.pad(state_conv[l], ((0, 0), (0, 0), (0, D_FFP - D_FF)))
        tin = jnp.arange(dl)[None, :, None]
        p1 = jnp.where(tin == 0, st[:, 1:2], 0.0).reshape(ms, D_FFP)
        p2 = jnp.where(tin == 0, st[:, 0:1], jnp.where(tin == 1, st[:, 1:2], 0.0)).reshape(ms, D_FFP)
        xs, ua = _ffn_sample_call(xs, mods, dl, n2, p1, p2, fw, l)
        ext = jnp.concatenate([state_conv[l], ua[:, :D_FF].reshape(dbsz, dl, D_FF)], axis=1)
        kv5 = lambda r: r.reshape(dbsz, dl, 2, NSA_KVH, NSA_DH)
        win_new = jnp.concatenate([state_win[l], kv5(rw)], axis=1)[:, -wb:]
        for i, o in enumerate((s5_new.reshape(dbsz, 2, S5_G, S5_N), hg_new, kv5(rc), kv5(rs), win_new,
                               ext[:, dl:])):
            outs[2 * i + 1].append(o)

    st = [jnp.stack(o) if o else None for o in outs]
    rows6 = [r.reshape(depth, bsz, seq, 2, NSA_KVH, NSA_DH) for r in row_bufs]
    st[4], st[6], st[8] = rows6[0], rows6[1], rows6[2][:, :, -min(WINDOW, seq):]
    return (xp.reshape(bsz, seq, D_MODEL), xs.reshape(dbsz, dl, D_MODEL), *st)
```

```python
import functools
import math

import jax
import jax.numpy as jnp
from jax import lax
from jax.experimental import pallas as pl
from jax.experimental.pallas import tpu as pltpu

F32 = jnp.float32
BF16 = jnp.bfloat16
I32 = jnp.int32

D_MODEL = 2048
D_A = D_MODEL // 4
S5_GROUP = 16
S5_G = D_A // S5_GROUP
S5_N = 64
S5_W = S5_G * S5_N
D_B = D_MODEL // 4
HG_DK = 128
HG_DV = 128
HG_HEADS = D_B // HG_DV
D_C = D_MODEL // 2
NSA_DH = 128
NSA_HEADS = D_C // NSA_DH
NSA_KVH = 2
NSA_REP = NSA_HEADS // NSA_KVH
N_KV = NSA_KVH * NSA_DH
ROPE_DIM = NSA_DH // 4
ROPE_THETA = 500000.0
CMP_STRIDE = 16
SEL_BLOCK = 64
N_SEL = 16
WINDOW = 512
FORCE_BONUS = 1000.0
D_FF = ((8 * D_MODEL) // 3 + 127) // 128 * 128
CONV_W = 3
EPS = 1e-6
NEG_INF = -1e30
PAGE = 128

LANE = 128
SUBLANE = 8
VMEM_LIMIT = 56 * 1024 * 1024

OFF_MGT = 0
OFF_NQ = 3 * D_MODEL
OFF_NKV = OFF_NQ + D_C
OFF_UA = OFF_NKV + 6 * N_KV
OFF_HQ = OFF_UA + D_A
OFF_HF = OFF_HQ + D_B
OFF_HI = OFF_HF + D_B
OFF_HGT = OFF_HI + D_B
OFF_NGT = OFF_HGT + D_B
NGT_PAD = 256
N_INP = OFF_NGT + NGT_PAD
TN_IN = 1280
D_FFP = 5632
TF = 512
HG_C = 64
HG_SB = 16


def _cparams(sem):
    return pltpu.CompilerParams(dimension_semantics=sem, vmem_limit_bytes=VMEM_LIMIT)


def _dot(a, b):
    return jnp.dot(a, b, preferred_element_type=F32)


def _dot_nt(a, b):
    return lax.dot_general(a, b, (((1,), (1,)), ((), ())), preferred_element_type=F32)


def _dot_tn(a, b):
    return lax.dot_general(a, b, (((0,), (0,)), ((), ())), preferred_element_type=F32)


def _dot2(a, b):
    hi = a.astype(BF16)
    lo = (a - hi.astype(F32)).astype(BF16)
    return _dot(hi, b) + _dot(lo, b)


def _dot3(a, b):
    hi = a.astype(BF16)
    r1 = a - hi.astype(F32)
    mid = r1.astype(BF16)
    lo = (r1 - mid.astype(F32)).astype(BF16)
    return _dot(b, hi) + _dot(b, mid) + _dot(b, lo)


def _rms(x, w):
    return x * lax.rsqrt(jnp.mean(x * x, axis=-1, keepdims=True) + EPS) * w


def _masked_softmax(s, mask):
    s = jnp.where(mask, s, NEG_INF)
    p = jnp.where(mask, jnp.exp(s - jnp.max(s, axis=-1, keepdims=True)), 0.0)
    return p / jnp.maximum(jnp.sum(p, axis=-1, keepdims=True), 1e-30)


def _row_reduce(x, combine, reduce):
    n = x.shape[-1]
    if n % LANE or n == LANE:
        return reduce(x, axis=-1, keepdims=True)
    acc = x[:, :LANE]
    for k in range(1, n // LANE):
        acc = combine(acc, x[:, k * LANE:(k + 1) * LANE])
    return reduce(acc, axis=-1, keepdims=True)


def _log_sigmoid(x):
    return jnp.minimum(x, 0.0) - jnp.log1p(jnp.exp(-jnp.abs(x)))


def _logaddexp(a, b):
    return jnp.maximum(a, b) + jnp.log1p(jnp.exp(-jnp.abs(a - b)))


def _mod_kernel(c_ref, w_ref, b_ref, o_ref):
    c = c_ref[...]
    a = (c * jax.nn.sigmoid(c)).astype(BF16)
    o_ref[0] = _dot(a, w_ref[0].astype(BF16)) + b_ref[0]


def _mod_call(c_all, w_mod, b_mod):
    depth, d, n = w_mod.shape
    r = c_all.shape[0]
    tn = 1024
    return pl.pallas_call(
        _mod_kernel,
        grid=(depth, n // tn),
        in_specs=[pl.BlockSpec((r, d), lambda l, j: (0, 0)),
                  pl.BlockSpec((1, d, tn), lambda l, j: (l, 0, j)),
                  pl.BlockSpec((1, 1, tn), lambda l, j: (l, 0, j))],
        out_specs=pl.BlockSpec((1, r, tn), lambda l, j: (l, 0, j)),
        out_shape=jax.ShapeDtypeStruct((depth, r, n), F32),
        compiler_params=_cparams(("arbitrary", "arbitrary")),
        name="mod",
    )(c_all, w_mod, b_mod.reshape(depth, 1, n))


def _inproj_kernel(x_ref, sc_ref, sh_ref, nw_ref, w_ref, o_ref, h_scr):
    @pl.when(pl.program_id(1) == 0)
    def _():
        y = _rms(x_ref[...], nw_ref[...])
        h_scr[...] = (y * (1.0 + sc_ref[0]) + sh_ref[0]).astype(BF16)

    o_ref[...] = _dot_nt(h_scr[...], w_ref[0])


def _mod_spec(modx, tm, rows_per_group, col):
    if modx.shape[1] == 1:
        return pl.BlockSpec((1, 1, D_MODEL), lambda i, *_: ((i * tm) // rows_per_group, 0, col))
    return pl.BlockSpec((1, tm, D_MODEL), lambda i, *_: (0, i, col))


def _inproj_call(x, modx, rows_per_group, nw, w, layer):
    m = x.shape[0]
    tm = min(1024, m)
    return pl.pallas_call(
        _inproj_kernel,
        grid=(m // tm, N_INP // TN_IN),
        in_specs=[pl.BlockSpec((tm, D_MODEL), lambda i, j: (i, 0)),
                  _mod_spec(modx, tm, rows_per_group, 1),
                  _mod_spec(modx, tm, rows_per_group, 0),
                  pl.BlockSpec((1, D_MODEL), lambda i, j: (0, 0)),
                  pl.BlockSpec((1, TN_IN, D_MODEL), lambda i, j: (layer, j, 0))],
        out_specs=pl.BlockSpec((tm, TN_IN), lambda i, j: (i, j)),
        out_shape=jax.ShapeDtypeStruct((m, N_INP), F32),
        scratch_shapes=[pltpu.VMEM((tm, D_MODEL), BF16)],
        compiler_params=_cparams(("arbitrary", "arbitrary")),
        name="inproj",
    )(x, modx, modx, nw, w)


S5_LC = 512


def _s5_kernel(u_ref, h0_ref, bbr_ref, bbi_ref, ccr_ref, cci_ref, apr_ref, api_ref, gpr_ref, gpi_ref,
               d_ref, wg_ref, bg_ref, y_ref, hl_ref, hre, him, car, *, t_rows, last_row):
    ts = t_rows // SUBLANE

    @pl.when(pl.program_id(1) == 0)
    def _():
        car[...] = h0_ref[0]

    ri = lax.broadcasted_iota(I32, (t_rows, t_rows), 0)
    ci = lax.broadcasted_iota(I32, (t_rows, t_rows), 1)
    perm = (ci == (ri % SUBLANE) * ts + ri // SUBLANE).astype(BF16)
    unperm = (ci == (ri % ts) * SUBLANE + ri // ts).astype(BF16)
    u = _dot3(u_ref[...], perm)
    ub = u.astype(BF16)
    nblk = D_A // LANE
    for j in range(nblk):
        us, ss = slice(j * LANE, (j + 1) * LANE), slice(j * S5_LC, (j + 1) * S5_LC)
        hre[:, ss] = _dot(ub[:, us], bbr_ref[us, ss])
        him[:, ss] = _dot(ub[:, us], bbi_ref[us, ss])
    row = lax.broadcasted_iota(I32, (SUBLANE, S5_LC), 0)
    zero = jnp.zeros((SUBLANE, S5_LC), F32)
    for c in range(S5_W // S5_LC):
        sl = slice(c * S5_LC, (c + 1) * S5_LC)
        ar, ai = apr_ref[0:SUBLANE, sl], api_ref[0:SUBLANE, sl]

        def local(t, carry, sl=sl, ar=ar, ai=ai):
            hr, hi = carry
            r0 = pl.multiple_of(t * SUBLANE, SUBLANE)
            nr = ar * hr - ai * hi + hre[pl.ds(r0, SUBLANE), sl]
            ni = ar * hi + ai * hr + him[pl.ds(r0, SUBLANE), sl]
            hre[pl.ds(r0, SUBLANE), sl] = nr
            him[pl.ds(r0, SUBLANE), sl] = ni
            return nr, ni

        er, ei = lax.fori_loop(0, ts, local, (zero, zero))
        yr = jnp.where(row == 0, car[0:1, sl], pltpu.roll(er, 1, 0))
        yi = jnp.where(row == 0, car[1:2, sl], pltpu.roll(ei, 1, 0))
        for k, s in enumerate((1, 2, 4)):
            gr, gi = gpr_ref[k:k + 1, sl], gpi_ref[k:k + 1, sl]
            sr = jnp.where(row >= s, pltpu.roll(yr, s, 0), 0.0)
            si = jnp.where(row >= s, pltpu.roll(yi, s, 0), 0.0)
            yr, yi = yr + (gr * sr - gi * si), yi + (gr * si + gi * sr)
        g1r, g1i = gpr_ref[0:1, sl], gpi_ref[0:1, sl]
        car[0:1, sl] = (g1r * yr - g1i * yi + er)[SUBLANE - 1:SUBLANE]
        car[1:2, sl] = (g1r * yi + g1i * yr + ei)[SUBLANE - 1:SUBLANE]

        def fix(t, carry, sl=sl, yr=yr, yi=yi):
            r0 = pl.multiple_of(t * SUBLANE, SUBLANE)
            pr, pi = apr_ref[pl.ds(r0, SUBLANE), sl], api_ref[pl.ds(r0, SUBLANE), sl]
            hre[pl.ds(r0, SUBLANE), sl] += pr * yr - pi * yi
            him[pl.ds(r0, SUBLANE), sl] += pr * yi + pi * yr
            return carry

        lax.fori_loop(0, ts, fix, 0)

    lr = t_rows - SUBLANE + last_row
    hl_ref[0, 0:1, :] = hre[lr:lr + 1, :]
    hl_ref[0, 1:2, :] = him[lr:lr + 1, :]
    y = jnp.concatenate(
        [_dot(hre[:, j * S5_LC:(j + 1) * S5_LC].astype(BF16),
              ccr_ref[j * S5_LC:(j + 1) * S5_LC, j * LANE:(j + 1) * LANE])
         - _dot(him[:, j * S5_LC:(j + 1) * S5_LC].astype(BF16),
                cci_ref[j * S5_LC:(j + 1) * S5_LC, j * LANE:(j + 1) * LANE])
         for j in range(nblk)], axis=1) + d_ref[...] * u
    y = jax.nn.gelu(y)
    z = _dot(y.astype(BF16), wg_ref[...]) + bg_ref[...]
    y_ref[...] = _dot(unperm, (y * jax.nn.sigmoid(z)).astype(BF16)).astype(BF16)


def _s5_call(u_arr, ucol, nb, seq, h0, p, last_row):
    t_rows = min(256, seq)
    nt = seq // t_rows
    const = lambda shape: pl.BlockSpec(shape, lambda b, t: (0,) * len(shape))
    kern = functools.partial(_s5_kernel, t_rows=t_rows, last_row=last_row)
    return pl.pallas_call(
        kern,
        grid=(nb, nt),
        in_specs=[pl.BlockSpec((t_rows, D_A), lambda b, t: (b * nt + t, ucol)),
                  pl.BlockSpec((1, 2, S5_W), lambda b, t: (b, 0, 0)),
                  const((D_A, S5_W)), const((D_A, S5_W)), const((S5_W, D_A)), const((S5_W, D_A)),
                  const((t_rows, S5_W)), const((t_rows, S5_W)),
                  const((SUBLANE, S5_W)), const((SUBLANE, S5_W)), const((1, D_A)),
                  const((D_A, D_A)), const((1, D_A))],
        out_specs=[pl.BlockSpec((t_rows, D_A), lambda b, t: (b * nt + t, 0)),
                   pl.BlockSpec((1, 2, S5_W), lambda b, t: (b, 0, 0))],
        out_shape=[jax.ShapeDtypeStruct((nb * seq, D_A), BF16),
                   jax.ShapeDtypeStruct((nb, 2, S5_W), F32)],
        scratch_shapes=[pltpu.VMEM((t_rows, S5_W), F32), pltpu.VMEM((t_rows, S5_W), F32),
                        pltpu.VMEM((2, S5_W), F32)],
        compiler_params=_cparams(("arbitrary", "arbitrary")),
        name="s5",
    )(u_arr, h0, p["bbr"], p["bbi"], p["ccr"], p["cci"], *p["pow"][t_rows // SUBLANE], p["d"],
      p["wg"], p["bg"])


def _hgrn_kernel(q_ref, f_ref, i_ref, g_ref, s0_ref, lp_ref, nw_ref, y_ref, sl_ref, st, pscr, rscr,
                 *, l_valid, chunks):
    t = pl.program_id(1)
    c = HG_C
    nsb = c // HG_SB
    nh = HG_HEADS

    @pl.when(t == 0)
    def _():
        for h in range(nh):
            st[h] = s0_ref[0, h].T

    row = lax.broadcasted_iota(I32, (c, D_B), 0)
    tri = (lax.broadcasted_iota(I32, (c, c), 0) >= lax.broadcasted_iota(I32, (c, c), 1)).astype(BF16)
    srow = lax.broadcasted_iota(I32, (HG_SB, D_B), 0)
    rows16 = lambda r: jnp.broadcast_to(r, (HG_SB, D_B))

    def chunk(ci, carry):
        c0 = pl.multiple_of(ci * c, c)
        crows = pl.ds(c0, c)
        valid = (t * (chunks * c) + c0 + row) < l_valid
        hf = f_ref[crows, :]
        ls = _log_sigmoid(hf)
        lf = jnp.where(lp_ref[3:4, :] > 0.5, _logaddexp(lp_ref[0:1, :], lp_ref[1:2, :] + ls), ls)
        kk = lp_ref[2:3, :] * jax.nn.sigmoid(-hf)
        lf = jnp.where(valid, lf, 0.0)
        kk = jnp.where(valid, kk, 0.0)
        hq = q_ref[crows, :]
        q = hq * jax.nn.sigmoid(hq)
        v = i_ref[crows, :]
        b = _dot3(lf, tri)
        zero_row = jnp.zeros((1, D_B), F32)
        bref = [zero_row] + [b[i * HG_SB - 1:i * HG_SB] for i in range(1, nsb)]
        bend = [b[(j + 1) * HG_SB - 1:(j + 1) * HG_SB] for j in range(nsb)]
        qq = q * jnp.exp(b - jnp.concatenate([rows16(r) for r in bref], axis=0))
        kks = kk * jnp.exp(jnp.concatenate([rows16(r) for r in bend], axis=0) - b)
        lhs = []
        for j in range(nsb - 1):
            dsel = jnp.concatenate(
                [rows16(jnp.exp(bref[i] - bend[j])) if i > j else jnp.zeros((HG_SB, D_B), F32)
                 for i in range(nsb)], axis=0)
            lhs.append((qq * dsel).astype(BF16))
        rhs = [jnp.where(row // HG_SB == j, kks, 0.0).astype(BF16) for j in range(nsb - 1)]
        for i in range(nsb):
            rs = slice(i * HG_SB, (i + 1) * HG_SB)
            bi, qi, ki = b[rs], q[rs], kk[rs]
            for s in range(HG_SB):
                m = srow >= s
                e = jnp.exp(jnp.where(m, bi - bi[s:s + 1], 0.0))
                piece = jnp.where(m, (qi * ki[s:s + 1]) * e, 0.0).astype(BF16)
                for h in range(nh):
                    r0 = ((h * nsb + i) * HG_SB + s) * HG_SB
                    pscr[r0:r0 + HG_SB, :] = piece[:, h * HG_DK:(h + 1) * HG_DK]
        rscr[...] = _dot(pscr[...], jnp.ones((HG_DK, HG_DK), BF16))
        qe = (q * jnp.exp(b)).astype(BF16)
        vb = v.astype(BF16)
        bl = b[c - 1:c]
        kdec = (kk * jnp.exp(bl - b)).astype(BF16)
        ebl = jnp.exp(bl)
        gt = g_ref[crows, :]
        gate = gt * jax.nn.sigmoid(gt)
        for h in range(nh):
            hs = slice(h * HG_DK, (h + 1) * HG_DK)
            s_t = st[h]
            att = _dot_nt(jnp.concatenate([x[:, hs] for x in lhs], axis=1),
                          jnp.concatenate([x[:, hs] for x in rhs], axis=1))
            o = _dot(att.astype(BF16), vb[:, hs]) + _dot_nt(qe[:, hs], s_t.astype(BF16))
            diag = []
            for i in range(nsb):
                od = jnp.zeros((HG_SB, HG_DV), F32)
                for s in range(HG_SB):
                    r0 = ((h * nsb + i) * HG_SB + s) * HG_SB
                    od = od + rscr[r0:r0 + HG_SB, :] * v[i * HG_SB + s:i * HG_SB + s + 1, hs]
                diag.append(od)
            o = o + jnp.concatenate(diag, axis=0)
            st[h] = s_t * ebl[:, hs] + _dot_tn(vb[:, hs], kdec[:, hs])
            y_ref[crows, hs] = (_rms(o, nw_ref[...]) * gate[:, hs]).astype(BF16)
        return carry

    lax.fori_loop(0, chunks, chunk, 0)

    @pl.when(t == pl.num_programs(1) - 1)
    def _():
        for h in range(nh):
            sl_ref[0, h] = st[h].T


def _dot2g(g, z):
    hi = z.astype(BF16)
    lo = (z - hi.astype(F32)).astype(BF16)
    return _dot(g, hi) + _dot(g, lo)


def _hgrn_call(arr, cols, nb, seq_pad, l_valid, s0, lp, nw):
    chunks = 4 if seq_pad % (4 * HG_C) == 0 else 1
    rows = chunks * HG_C
    nt = seq_pad // rows
    blk = lambda col: pl.BlockSpec((rows, D_B), lambda b, t: (b * nt + t, col))
    kern = functools.partial(_hgrn_kernel, l_valid=l_valid, chunks=chunks)
    return pl.pallas_call(
        kern,
        grid=(nb, nt),
        in_specs=[blk(cols[0]), blk(cols[1]), blk(cols[2]), blk(cols[3]),
                  pl.BlockSpec((1, HG_HEADS, HG_DK, HG_DV), lambda b, t: (b, 0, 0, 0)),
                  pl.BlockSpec((SUBLANE, D_B), lambda b, t: (0, 0)),
                  pl.BlockSpec((1, HG_DV), lambda b, t: (0, 0))],
        out_specs=[pl.BlockSpec((rows, D_B), lambda b, t: (b * nt + t, 0)),
                   pl.BlockSpec((1, HG_HEADS, HG_DK, HG_DV), lambda b, t: (b, 0, 0, 0))],
        out_shape=[jax.ShapeDtypeStruct((nb * seq_pad, D_B), BF16),
                   jax.ShapeDtypeStruct((nb, HG_HEADS, HG_DK, HG_DV), F32)],
        scratch_shapes=[pltpu.VMEM((HG_HEADS, HG_DV, HG_DK), F32),
                        pltpu.VMEM((HG_HEADS * HG_C * HG_SB, HG_DK), BF16),
                        pltpu.VMEM((HG_HEADS * HG_C * HG_SB, HG_DK), F32)],
        compiler_params=_cparams(("arbitrary", "arbitrary")),
        name="hgrn",
    )(arr, arr, arr, arr, s0, lp, nw)


def _rope(x, cos_t, sin_a, sin_b):
    return (x * cos_t + pltpu.roll(x, NSA_DH - ROPE_DIM // 2, 1) * sin_a
            + pltpu.roll(x, ROPE_DIM // 2, 1) * sin_b)


def _nsa_prep_kernel(q_ref, kc_ref, ks_ref, kw_ref, g_ref, cos_ref, sa_ref, sb_ref, qn_ref, kn_ref,
                     qo_ref, rc_ref, rs_ref, rw_ref, bc_ref, bs_ref, bw_ref, go_ref):
    cos_t, sin_a, sin_b = cos_ref[...], sa_ref[...], sb_ref[...]
    qn = qn_ref[...]
    for h in range(NSA_HEADS):
        hs = slice(h * NSA_DH, (h + 1) * NSA_DH)
        qo_ref[:, hs] = _rope(_rms(q_ref[:, hs], qn), cos_t, sin_a, sin_b).astype(BF16)
    for br, (src, dst, dstb) in enumerate(((kc_ref, rc_ref, bc_ref), (ks_ref, rs_ref, bs_ref),
                                           (kw_ref, rw_ref, bw_ref))):
        kn = kn_ref[br:br + 1, :]
        tm = src.shape[0]
        for g in range(NSA_KVH):
            gs = slice(g * NSA_DH, (g + 1) * NSA_DH)
            k = _rope(_rms(src[:, gs], kn), cos_t, sin_a, sin_b)
            dst[pl.ds(g, tm, stride=KV_TILES), :] = k
            dstb[:, gs] = k.astype(BF16)
            vs = slice(N_KV + g * NSA_DH, N_KV + (g + 1) * NSA_DH)
            v = src[:, vs]
            dst[pl.ds(NSA_KVH + g, tm, stride=KV_TILES), :] = v
            dstb[:, vs] = v.astype(BF16)
    go_ref[...] = jax.nn.sigmoid(g_ref[...])


def _nsa_prep_call(hin, rope_tabs, seq, qn, kn, layer=0, depth=1, row_bufs=None):
    m = hin.shape[0]
    tm = min(256, m)
    npos = rope_tabs[0].shape[0] // tm
    rspec = pl.BlockSpec((tm, NSA_DH), lambda i: (i % npos, 0))
    kvb = OFF_NKV // (2 * N_KV)
    n_in = 10
    extra = () if row_bufs is None else tuple(row_bufs)

    def kern(*refs):
        _nsa_prep_kernel(*refs[:n_in], *refs[n_in + len(extra):])

    return pl.pallas_call(
        kern,
        grid=(m // tm,),
        input_output_aliases={n_in + k: 1 + k for k in range(len(extra))},
        in_specs=[pl.BlockSpec((tm, D_C), lambda i: (i, OFF_NQ // D_C)),
                  pl.BlockSpec((tm, 2 * N_KV), lambda i: (i, kvb)),
                  pl.BlockSpec((tm, 2 * N_KV), lambda i: (i, kvb + 1)),
                  pl.BlockSpec((tm, 2 * N_KV), lambda i: (i, kvb + 2)),
                  pl.BlockSpec((tm, LANE), lambda i: (i, OFF_NGT // LANE)),
                  rspec, rspec, rspec,
                  pl.BlockSpec((1, NSA_DH), lambda i: (0, 0)),
                  pl.BlockSpec((3, NSA_DH), lambda i: (0, 0))]
        + [pl.BlockSpec(memory_space=pl.ANY)] * len(extra),
        out_specs=[pl.BlockSpec((tm, D_C), lambda i: (i, 0))]
        + [pl.BlockSpec((tm * KV_TILES, LANE), lambda i: (layer * (m // tm) + i, 0))] * 3
        + [pl.BlockSpec((tm, 2 * N_KV), lambda i: (i, 0))] * 3
        + [pl.BlockSpec((tm, LANE), lambda i: (i, 0))],
        out_shape=[jax.ShapeDtypeStruct((m, D_C), BF16)]
        + [jax.ShapeDtypeStruct((depth * m * KV_TILES, LANE), F32)] * 3
        + [jax.ShapeDtypeStruct((m, 2 * N_KV), BF16)] * 3
        + [jax.ShapeDtypeStruct((m, LANE), F32)],
        compiler_params=_cparams(("arbitrary",)),
        name="nsa_prep",
    )(hin, hin, hin, hin, hin, *rope_tabs, qn, kn, *extra)


KV_TILES = 2 * N_KV // LANE


def _pool_rows(load, n, a1_ref, a2_ref):
    firsts, seconds = [], []
    for c in range(KV_TILES):
        ls = slice(c * LANE, (c + 1) * LANE)
        first = jnp.zeros((n, LANE), F32)
        second = jnp.zeros((n, LANE), F32)
        for j in range(CMP_STRIDE):
            x = load(j, c)
            first = first + x * a1_ref[j:j + 1, ls]
            second = second + x * a2_ref[j:j + 1, ls]
        firsts.append(first)
        seconds.append(second)
    return firsts, seconds


def _tile_rows(j, c, n):
    return pl.ds(j * KV_TILES + c, n, stride=CMP_STRIDE * KV_TILES)


def _cmp_mlp(first, second, n, w1_ref, b1_ref, w2_ref, o_ref):
    for c in range(2):
        for g in range(NSA_KVH):
            tile = 2 * c + g
            pooled = first[tile] + pltpu.roll(second[tile], n - 1, 0)
            h = _dot(pooled.astype(BF16), w1_ref[c]) + b1_ref[c:c + 1, :]
            o_ref[0, :, tile * NSA_DH:(tile + 1) * NSA_DH] = _dot(
                jax.nn.gelu(h).astype(BF16), w2_ref[c]).astype(BF16)


def _cmp_prompt_kernel(x_ref, a1_ref, a2_ref, w1_ref, b1_ref, w2_ref, o_ref, *, n):
    first, second = _pool_rows(lambda j, c: x_ref[_tile_rows(j, c, n), :], n, a1_ref, a2_ref)
    _cmp_mlp(first, second, n, w1_ref, b1_ref, w2_ref, o_ref)


def _cmp_prompt_call(rows, nb, seq, cp, layer=0):
    n = seq // CMP_STRIDE
    const = lambda shape: pl.BlockSpec(shape, lambda b: (0,) * len(shape))
    return pl.pallas_call(
        functools.partial(_cmp_prompt_kernel, n=n),
        grid=(nb,),
        in_specs=[pl.BlockSpec((seq * KV_TILES, LANE), lambda b: (layer * nb + b, 0)),
                  const((CMP_STRIDE, 2 * N_KV)), const((CMP_STRIDE, 2 * N_KV)),
                  const((2, NSA_DH, NSA_DH)), const((2, NSA_DH)), const((2, NSA_DH, NSA_DH))],
        out_specs=pl.BlockSpec((1, n, 2 * N_KV), lambda b: (b, 0, 0)),
        out_shape=jax.ShapeDtypeStruct((nb, n, 2 * N_KV), BF16),
        compiler_params=_cparams(("arbitrary",)),
        name="cmp_prompt",
    )(rows.reshape(-1, LANE), cp["a1"], cp["a2"], cp["w1"], cp["b1"], cp["w2"])


PG = 16
SUB_PER_PAGE = PAGE // CMP_STRIDE


SUB_ROWS = CMP_STRIDE * KV_TILES


def _cmp_pool_cache_kernel(pt_ref, *refs):
    pages = refs[:PG]
    a1_ref, a2_ref, f_ref, s_ref = refs[PG:]
    nv = SUB_ROWS // SUBLANE
    w1 = [a1_ref[0, v * SUBLANE:(v + 1) * SUBLANE, :] for v in range(nv)]
    w2 = [a2_ref[0, v * SUBLANE:(v + 1) * SUBLANE, :] for v in range(nv)]
    low = lax.broadcasted_iota(I32, (SUBLANE, LANE), 0) < KV_TILES

    def fold(acc):
        return acc + pltpu.roll(acc, KV_TILES, 0)

    for k in range(PG):
        for n2 in range(SUB_PER_PAGE // 2):
            halves = []
            for n in (2 * n2, 2 * n2 + 1):
                a1 = a2 = None
                for v in range(nv):
                    x = pages[k][0, 0, n * SUB_ROWS + v * SUBLANE:n * SUB_ROWS + (v + 1) * SUBLANE, :]
                    a1 = x * w1[v] if a1 is None else a1 + x * w1[v]
                    a2 = x * w2[v] if a2 is None else a2 + x * w2[v]
                halves.append((fold(a1), fold(a2)))
            r0 = (k * SUB_PER_PAGE + 2 * n2) * KV_TILES
            f_ref[0, 0, r0:r0 + SUBLANE, :] = jnp.where(low, halves[0][0], halves[1][0])
            s_ref[0, 0, r0:r0 + SUBLANE, :] = jnp.where(low, halves[0][1], halves[1][1])


def _page_specs(n_pages, layer_of, batch_of, group_of, page_shape=(PAGE, 2 * N_KV)):
    def spec(k):
        return pl.BlockSpec(
            (1, 1) + page_shape,
            lambda *a, k=k: (layer_of(*a), a[-1][batch_of(*a) * n_pages + group_of(*a) * PG + k], 0, 0))
    return [spec(k) for k in range(PG)]


def _cmp_pool_cache_call(cache, pt_flat, nb, n_pages, a1, a2):
    depth = cache.shape[0]
    npg = n_pages // PG
    nsub = n_pages * SUB_PER_PAGE
    rows = PG * SUB_PER_PAGE
    grid_spec = pltpu.PrefetchScalarGridSpec(
        num_scalar_prefetch=1,
        grid=(depth, nb, npg),
        in_specs=_page_specs(n_pages, lambda l, b, g, pt: l, lambda l, b, g, pt: b,
                             lambda l, b, g, pt: g, (PAGE * KV_TILES, LANE))
        + [pl.BlockSpec((1, SUB_ROWS, LANE), lambda l, b, g, pt: (l, 0, 0))] * 2,
        out_specs=[pl.BlockSpec((1, 1, rows * KV_TILES, LANE), lambda l, b, g, pt: (l, b, g, 0))] * 2,
    )
    return pl.pallas_call(
        _cmp_pool_cache_kernel,
        grid_spec=grid_spec,
        out_shape=[jax.ShapeDtypeStruct((depth, nb, nsub * KV_TILES, LANE), F32)] * 2,
        compiler_params=_cparams(("arbitrary", "arbitrary", "arbitrary")),
        name="cmp_pool_cache",
    )(pt_flat, *([cache] * PG), a1, a2)


def _cmp_mlp_cache_kernel(f_ref, s_ref, w1_ref, b1_ref, w2_ref, o_ref, *, n):
    tiles = lambda ref: [ref[0, 0, pl.ds(c, n, stride=KV_TILES), :] for c in range(KV_TILES)]
    _cmp_mlp(tiles(f_ref), tiles(s_ref), n, w1_ref.at[0], b1_ref.at[0], w2_ref.at[0], o_ref.at[0])


def _cmp_mlp_cache_call(first, second, w1, b1, w2):
    depth, nb, rows, _ = first.shape
    n = rows // KV_TILES
    blk = pl.BlockSpec((1, 1, n, 2 * N_KV), lambda l, b: (l, b, 0, 0))
    tblk = pl.BlockSpec((1, 1, rows, LANE), lambda l, b: (l, b, 0, 0))
    return pl.pallas_call(
        functools.partial(_cmp_mlp_cache_kernel, n=n),
        grid=(depth, nb),
        in_specs=[tblk, tblk,
                  pl.BlockSpec((1, 2, NSA_DH, NSA_DH), lambda l, b: (l, 0, 0, 0)),
                  pl.BlockSpec((1, 2, NSA_DH), lambda l, b: (l, 0, 0)),
                  pl.BlockSpec((1, 2, NSA_DH, NSA_DH), lambda l, b: (l, 0, 0, 0))],
        out_specs=blk,
        out_shape=jax.ShapeDtypeStruct((depth, nb, n, 2 * N_KV), BF16),
        compiler_params=_cparams(("arbitrary", "arbitrary")),
        name="cmp_mlp_cache",
    )(first, second, w1, b1, w2)


def _block_scores(imp, qpos, n_slc):
    blk = lax.broadcasted_iota(I32, imp.shape, 1)
    cur = qpos // SEL_BLOCK
    forced = (blk == 0) | (blk == cur) | (blk == cur - 1)
    valid = blk * SEL_BLOCK <= qpos
    score = jnp.where(valid, imp + jnp.where(forced, FORCE_BONUS, 0.0), -1.0)
    return jnp.where(blk < n_slc, score, -2.0)


def _select_blocks_ranked(imp, qpos, n_slc):
    nb8 = (n_slc + SUBLANE - 1) // SUBLANE * SUBLANE
    st = _block_scores(imp, qpos, n_slc).T[:nb8]
    ridx = lax.broadcasted_iota(I32, st.shape, 0)
    cnt = jnp.zeros(st.shape, F32)
    for i in range(n_slc):
        ri = st[i:i + 1]
        cnt = cnt + jnp.where(ri > st, 1.0, jnp.where(ri == st, jnp.where(ridx > i, 1.0, 0.0), 0.0))
    sel_t = jnp.where((cnt < min(N_SEL, n_slc)) & (ridx < n_slc), 1.0, 0.0)
    if nb8 < LANE:
        sel_t = jnp.concatenate([sel_t, jnp.zeros((LANE - nb8, st.shape[1]), F32)], axis=0)
    return sel_t.T


def _select_blocks(imp, qpos, n_slc):
    blk = lax.broadcasted_iota(I32, imp.shape, 1)
    score = _block_scores(imp, qpos, n_slc)
    sel = jnp.zeros(imp.shape, F32)
    for _ in range(min(N_SEL, n_slc)):
        mx = jnp.max(score, axis=-1, keepdims=True)
        idx = jnp.min(jnp.where(score == mx, blk, 1 << 30), axis=-1, keepdims=True)
        hit = blk == idx
        sel = jnp.where(hit, 1.0, sel)
        score = jnp.where(hit, -3.0, score)
    return sel


def _cmp_branch(q, kck, kcv, qpos, n_cmp):
    s = _dot_nt(q, kck) * (NSA_DH ** -0.5)
    col = lax.broadcasted_iota(I32, s.shape, 1)
    mask = (col * CMP_STRIDE + (2 * CMP_STRIDE - 1) <= qpos) & (col < n_cmp)
    p = _masked_softmax(s, mask)
    return p, _dot(p.astype(BF16), kcv)


NSA_KC = 1024


def _nsa_prompt_kernel(q_ref, kc_ref, ks_ref, vs_ref, kw_ref, vw_ref, g_ref, o_ref,
                       bias_scr, m_scr, l_scr, acc_scr, *, tq, seq, slab):
    q0 = pl.program_id(1) * tq
    rows = NSA_REP * tq
    ncp = seq // CMP_STRIDE
    n_slc = seq // SEL_BLOCK
    scale = NSA_DH ** -0.5
    gates = g_ref[...]
    q4s = [jnp.concatenate([q_ref[:, (g * NSA_REP + r) * NSA_DH:(g * NSA_REP + r + 1) * NSA_DH]
                            for r in range(NSA_REP)], axis=0) for g in range(NSA_KVH)]
    qpos_c = q0 + (lax.broadcasted_iota(I32, (rows, ncp), 0) & (tq - 1))
    o_cs, psums = [], []
    for g in range(NSA_KVH):
        p_c, o_c = _cmp_branch(q4s[g], kc_ref[0, :, g * NSA_DH:(g + 1) * NSA_DH],
                               kc_ref[0, :, N_KV + g * NSA_DH:N_KV + (g + 1) * NSA_DH], qpos_c, ncp - 1)
        psum = p_c[0:tq]
        for r in range(1, NSA_REP):
            psum = psum + p_c[r * tq:(r + 1) * tq]
        o_cs.append(o_c)
        psums.append(psum)

    def ranked():
        gmat = (lax.broadcasted_iota(I32, (ncp, LANE), 0) // (SEL_BLOCK // CMP_STRIDE)
                == lax.broadcasted_iota(I32, (ncp, LANE), 1)).astype(BF16)
        imp = _dot2(jnp.concatenate(psums, axis=0), gmat)
        qp = q0 + (lax.broadcasted_iota(I32, (NSA_KVH * tq, LANE), 0) & (tq - 1))
        return _select_blocks_ranked(imp, qp, n_slc)

    sel = lax.cond(q0 + tq > N_SEL * SEL_BLOCK, ranked,
                   lambda: jnp.ones((NSA_KVH * tq, LANE), F32))
    emat = (lax.broadcasted_iota(I32, (LANE, seq), 1) // SEL_BLOCK
            == lax.broadcasted_iota(I32, (LANE, seq), 0)).astype(BF16)
    nk = (q0 + tq + NSA_KC - 1) // NSA_KC
    s0 = pl.multiple_of(jnp.maximum(q0 + tq - slab, 0), SUBLANE)
    key = lax.broadcasted_iota(I32, (tq, seq), 1)
    causal = key <= q0 + lax.broadcasted_iota(I32, (tq, seq), 0)
    wp = s0 + lax.broadcasted_iota(I32, (tq, slab), 1)
    qpw = q0 + lax.broadcasted_iota(I32, (tq, slab), 0)
    wb1 = jnp.where((wp <= qpw) & (wp > qpw - WINDOW), 0.0, NEG_INF)

    def add_bias(s, bias):
        return jnp.concatenate([s[r * tq:(r + 1) * tq] + bias for r in range(NSA_REP)], axis=0)

    for g in range(NSA_KVH):
        keep = (_dot(sel[g * tq:(g + 1) * tq].astype(BF16), emat) > 0.5) & causal
        bias = jnp.where(keep, 0.0, NEG_INF)
        for c in range(seq // NSA_KC):
            bias_scr[g, c] = bias[:, c * NSA_KC:(c + 1) * NSA_KC]
    m_scr[...] = jnp.full(m_scr.shape, NEG_INF, F32)
    l_scr[...] = jnp.zeros(l_scr.shape, F32)
    acc_scr[...] = jnp.zeros(acc_scr.shape, F32)

    def body(c, carry):
        k0 = pl.multiple_of(c * NSA_KC, NSA_KC)
        for g in range(NSA_KVH):
            gs = slice(g * NSA_DH, (g + 1) * NSA_DH)
            s = add_bias(_dot_nt(q4s[g], ks_ref[pl.ds(k0, NSA_KC), gs]) * scale, bias_scr[g, c])
            m_old = m_scr[g]
            m_new = jnp.maximum(m_old, _row_reduce(s, jnp.maximum, jnp.max))
            p = jnp.exp(s - m_new)
            alpha = jnp.exp(m_old - m_new)
            l_scr[g] = alpha * l_scr[g] + _row_reduce(p, jnp.add, jnp.sum)
            acc_scr[g] = alpha * acc_scr[g] + _dot(p.astype(BF16), vs_ref[pl.ds(k0, NSA_KC), gs])
            m_scr[g] = m_new
        return carry

    lax.fori_loop(0, nk, body, 0)
    for g in range(NSA_KVH):
        gs = slice(g * NSA_DH, (g + 1) * NSA_DH)
        q4 = q4s[g]
        o_s = acc_scr[g] / l_scr[g]
        s_w = add_bias(_dot_nt(q4, kw_ref[pl.ds(s0, slab), gs]) * scale, wb1)
        p_w = jnp.exp(s_w - _row_reduce(s_w, jnp.maximum, jnp.max))
        o_w = (_dot(p_w.astype(BF16), vw_ref[pl.ds(s0, slab), gs])
               / _row_reduce(p_w, jnp.add, jnp.sum))
        o_c = o_cs[g]
        for r in range(NSA_REP):
            h = g * NSA_REP + r
            rs = slice(r * tq, (r + 1) * tq)
            o = (gates[:, 3 * h:3 * h + 1] * o_c[rs] + gates[:, 3 * h + 1:3 * h + 2] * o_s[rs]
                 + gates[:, 3 * h + 2:3 * h + 3] * o_w[rs])
            o_ref[:, h * NSA_DH:(h + 1) * NSA_DH] = o.astype(BF16)


def _nsa_prompt_call(qn, kc, bs, bw, gates, nb, seq):
    tq = 128
    nq = seq // tq
    slab = min(seq, WINDOW + tq)
    kv = lambda col: pl.BlockSpec((seq, N_KV), lambda b, i: (b, col))
    kern = functools.partial(_nsa_prompt_kernel, tq=tq, seq=seq, slab=slab)
    rows = NSA_REP * tq
    return pl.pallas_call(
        kern,
        grid=(nb, nq),
        in_specs=[pl.BlockSpec((tq, D_C), lambda b, i: (b * nq + i, 0)),
                  pl.BlockSpec((1, seq // CMP_STRIDE, 2 * N_KV), lambda b, i: (b, 0, 0)),
                  kv(0), kv(1), kv(0), kv(1),
                  pl.BlockSpec((tq, LANE), lambda b, i: (b * nq + i, 0))],
        out_specs=pl.BlockSpec((tq, D_C), lambda b, i: (b * nq + i, 0)),
        out_shape=jax.ShapeDtypeStruct((nb * seq, D_C), BF16),
        scratch_shapes=[pltpu.VMEM((NSA_KVH, seq // NSA_KC, tq, NSA_KC), F32),
                        pltpu.VMEM((NSA_KVH, rows, 1), F32), pltpu.VMEM((NSA_KVH, rows, 1), F32),
                        pltpu.VMEM((NSA_KVH, rows, NSA_DH), F32)],
        compiler_params=_cparams(("arbitrary", "arbitrary")),
        name="nsa_prompt",
    )(qn, kc, bs, bs, bw, bw, gates)


def _nsa_s_sel_kernel(q_ref, kc_ref, oc_ref, sel_ref, *, past, dl, n_tiles):
    rows = NSA_REP * dl
    ncp = past // CMP_STRIDE
    n_slc = (past + dl + SEL_BLOCK - 1) // SEL_BLOCK
    nbp = (n_slc + LANE - 1) // LANE * LANE
    per_tile = PG * PAGE // SEL_BLOCK
    gmat = (lax.broadcasted_iota(I32, (ncp, nbp), 0) // (SEL_BLOCK // CMP_STRIDE)
            == lax.broadcasted_iota(I32, (ncp, nbp), 1)).astype(BF16)
    rmat = (lax.broadcasted_iota(I32, (rows, rows), 0) // NSA_REP
            == lax.broadcasted_iota(I32, (rows, rows), 1) // NSA_REP).astype(BF16)

    def qpos_of(shape):
        return past + lax.broadcasted_iota(I32, shape, 0) // NSA_REP

    lane = lax.broadcasted_iota(I32, (rows, LANE), 1)
    for g in range(NSA_KVH):
        gs = slice(g * NSA_DH, (g + 1) * NSA_DH)
        p_c, o_c = _cmp_branch(q_ref[0, g], kc_ref[0, 0, :, gs],
                               kc_ref[0, 0, :, N_KV + g * NSA_DH:N_KV + (g + 1) * NSA_DH],
                               qpos_of((rows, ncp)), ncp - 1)
        oc_ref[0, g] = o_c
        psum = _dot2g(rmat, p_c)
        imp = _dot2(psum, gmat)
        sel = _select_blocks(imp, qpos_of((rows, nbp)), n_slc)
        for tl in range(n_tiles):
            lo = tl * per_tile
            chunk = sel[:, (lo // LANE) * LANE:(lo // LANE + 1) * LANE]
            sh = (LANE - lo % LANE) % LANE
            if sh:
                chunk = pltpu.roll(chunk, sh, 1)
            sel_ref[0, g, tl] = jnp.where(lane < per_tile, chunk, 0.0)


def _nsa_s_sel_call(q16, kc, layer, past, dl):
    nb = q16.shape[0]
    rows = NSA_REP * dl
    n_tiles = past // (PG * PAGE) + 1
    kern = functools.partial(_nsa_s_sel_kernel, past=past, dl=dl, n_tiles=n_tiles)
    return pl.pallas_call(
        kern,
        grid=(nb,),
        in_specs=[pl.BlockSpec((1, NSA_KVH, rows, NSA_DH), lambda b: (b, 0, 0, 0)),
                  pl.BlockSpec((1, 1, past // CMP_STRIDE, 2 * N_KV), lambda b: (layer, b, 0, 0))],
        out_specs=[pl.BlockSpec((1, NSA_KVH, rows, NSA_DH), lambda b: (b, 0, 0, 0)),
                   pl.BlockSpec((1, NSA_KVH, n_tiles, rows, LANE), lambda b: (b, 0, 0, 0, 0))],
        out_shape=[jax.ShapeDtypeStruct((nb, NSA_KVH, rows, NSA_DH), F32),
                   jax.ShapeDtypeStruct((nb, NSA_KVH, n_tiles, rows, LANE), F32)],
        compiler_params=_cparams(("arbitrary",)),
        name="nsa_sample_select",
    )(q16, kc)


def _nsa_s_attn_kernel(pt_ref, *refs, past, dl, npg):
    pages = refs[:PG]
    (q_ref, selt_ref, sell_ref, emat_ref, oc_ref, ns_ref, nw_ref, wp_ref, g_ref,
     o_ref, m_scr, l_scr, a_scr) = refs[PG:]
    pg = pl.program_id(1)
    rows = NSA_REP * dl
    scale = NSA_DH ** -0.5
    wb = wp_ref.shape[2] // KV_TILES

    def tok_rows(ref, c, n):
        return ref[0, 0, pl.ds(c, n, stride=KV_TILES), :].astype(BF16)

    @pl.when(pg == 0)
    def _():
        m_scr[...] = jnp.full(m_scr.shape, NEG_INF, F32)
        l_scr[...] = jnp.zeros(l_scr.shape, F32)
        a_scr[...] = jnp.zeros(a_scr.shape, F32)

    for g in range(NSA_KVH):
        q = q_ref[0, g]
        kmask = _dot(selt_ref[0, g, 0].astype(BF16), emat_ref[...]) > 0.5
        s = jnp.concatenate(
            [_dot_nt(q, tok_rows(pages[k], g, PAGE)) for k in range(PG)], axis=1) * scale
        s = jnp.where(kmask, s, NEG_INF)
        m_old = m_scr[g]
        m_new = jnp.maximum(m_old, jnp.max(s, axis=-1, keepdims=True))
        p = jnp.where(kmask, jnp.exp(s - m_new), 0.0)
        alpha = jnp.exp(m_old - m_new)
        l_scr[g] = alpha * l_scr[g] + jnp.sum(p, axis=-1, keepdims=True)
        acc = alpha * a_scr[g]
        pb = p.astype(BF16)
        for k in range(PG):
            acc = acc + _dot(pb[:, k * PAGE:(k + 1) * PAGE], tok_rows(pages[k], NSA_KVH + g, PAGE))
        a_scr[g] = acc
        m_scr[g] = m_new

    @pl.when(pg == npg - 1)
    def _():
        gates = g_ref[0]
        tok = lax.broadcasted_iota(I32, (rows, SUBLANE), 0) // NSA_REP
        ncol = lax.broadcasted_iota(I32, (rows, SUBLANE), 1)
        for g in range(NSA_KVH):
            gs = slice(g * NSA_DH, (g + 1) * NSA_DH)
            vsl = slice(N_KV + g * NSA_DH, N_KV + (g + 1) * NSA_DH)
            q = q_ref[0, g]
            nmask = (sell_ref[0, g, 0][:, 0:1] > 0.5) & (ncol <= tok) & (ncol < dl)
            s_n = jnp.where(nmask, _dot_nt(q, ns_ref[0, :, gs].astype(BF16)) * scale, NEG_INF)
            m_old = m_scr[g]
            m_new = jnp.maximum(m_old, jnp.max(s_n, axis=-1, keepdims=True))
            p_n = jnp.where(nmask, jnp.exp(s_n - m_new), 0.0)
            alpha = jnp.exp(m_old - m_new)
            lsum = alpha * l_scr[g] + jnp.sum(p_n, axis=-1, keepdims=True)
            acc = alpha * a_scr[g] + _dot(p_n.astype(BF16), ns_ref[0, :, vsl].astype(BF16))
            o_s = acc / jnp.maximum(lsum, 1e-30)
            wrow = lax.broadcasted_iota(I32, (rows, wb), 1)
            wtok = lax.broadcasted_iota(I32, (rows, wb), 0) // NSA_REP
            pmask = (wrow > wtok + (wb - WINDOW)) & (wrow + (past - wb) >= 0)
            s_p = jnp.where(pmask, _dot_nt(q, tok_rows(wp_ref, g, wb)) * scale, NEG_INF)
            wmask = (ncol <= tok) & (ncol < dl)
            s_q = jnp.where(wmask, _dot_nt(q, nw_ref[0, :, gs].astype(BF16)) * scale, NEG_INF)
            mw = jnp.maximum(jnp.max(s_p, axis=-1, keepdims=True), jnp.max(s_q, axis=-1, keepdims=True))
            p_p = jnp.where(pmask, jnp.exp(s_p - mw), 0.0)
            p_q = jnp.where(wmask, jnp.exp(s_q - mw), 0.0)
            den = jnp.maximum(jnp.sum(p_p, axis=-1, keepdims=True) + jnp.sum(p_q, axis=-1, keepdims=True),
                              1e-30)
            o_w = (_dot(p_p.astype(BF16), tok_rows(wp_ref, NSA_KVH + g, wb))
                   + _dot(p_q.astype(BF16), nw_ref[0, :, vsl].astype(BF16))) / den
            gg = gates[g]
            o = gg[:, 0:1] * oc_ref[0, g] + gg[:, 1:2] * o_s + gg[:, 2:3] * o_w
            o_ref[0, g] = o.astype(BF16)


def _nsa_s_attn_call(cache, pt_flat, layer, q16, selt, emat, o_c, new_slc, new_win, win_prev, gates16,
                     past, dl):
    nb = q16.shape[0]
    rows = NSA_REP * dl
    n_pages = past // PAGE
    npg = n_pages // PG
    wrows = win_prev.shape[2]
    b4 = lambda shape: pl.BlockSpec((1,) + shape, lambda b, g, pt: (b,) + (0,) * len(shape))
    grid_spec = pltpu.PrefetchScalarGridSpec(
        num_scalar_prefetch=1,
        grid=(nb, npg),
        in_specs=_page_specs(n_pages, lambda b, g, pt: layer, lambda b, g, pt: b, lambda b, g, pt: g,
                             (PAGE * KV_TILES, LANE))
        + [b4((NSA_KVH, rows, NSA_DH)),
           pl.BlockSpec((1, NSA_KVH, 1, rows, LANE), lambda b, g, pt: (b, 0, g, 0, 0)),
           pl.BlockSpec((1, NSA_KVH, 1, rows, LANE), lambda b, g, pt: (b, 0, npg, 0, 0)),
           pl.BlockSpec((LANE, PG * PAGE), lambda b, g, pt: (0, 0)),
           b4((NSA_KVH, rows, NSA_DH)),
           b4((SUBLANE, 2 * N_KV)), b4((SUBLANE, 2 * N_KV)),
           pl.BlockSpec((1, 1, wrows, LANE), lambda b, g, pt: (layer, b, 0, 0)),
           b4((NSA_KVH, rows, LANE))],
        out_specs=b4((NSA_KVH, rows, NSA_DH)),
        scratch_shapes=[pltpu.VMEM((NSA_KVH, rows, 1), F32), pltpu.VMEM((NSA_KVH, rows, 1), F32),
                        pltpu.VMEM((NSA_KVH, rows, NSA_DH), F32)],
    )
    kern = functools.partial(_nsa_s_attn_kernel, past=past, dl=dl, npg=npg)
    return pl.pallas_call(
        kern,
        grid_spec=grid_spec,
        out_shape=jax.ShapeDtypeStruct((nb, NSA_KVH, rows, NSA_DH), BF16),
        compiler_params=_cparams(("arbitrary", "arbitrary")),
        name="nsa_sample_attn",
    )(pt_flat, *([cache] * PG), q16, selt, selt, emat, o_c, new_slc, new_win, win_prev, gates16)


def _merge_kernel(ya_ref, yb_ref, yc_ref, ga_ref, gb_ref, gc_ref, wa_ref, wb_ref, wc_ref, o_ref):
    m = (jax.nn.sigmoid(ga_ref[...]) * _dot(ya_ref[...], wa_ref[0])
         + jax.nn.sigmoid(gb_ref[...]) * _dot(yb_ref[...], wb_ref[0])
         + jax.nn.sigmoid(gc_ref[...]) * _dot(yc_ref[...], wc_ref[0]))
    o_ref[...] = m.astype(BF16)


def _merge_call(ya, yb, yc, hin, wa, wb, wc, layer):
    m = ya.shape[0]
    tm = min(256, m)
    const = lambda shape: pl.BlockSpec((1,) + shape, lambda i: (layer, 0, 0))
    gate = lambda col: pl.BlockSpec((tm, D_MODEL), lambda i: (i, col))
    return pl.pallas_call(
        _merge_kernel,
        grid=(m // tm,),
        in_specs=[pl.BlockSpec((tm, D_A), lambda i: (i, 0)), pl.BlockSpec((tm, D_B), lambda i: (i, 0)),
                  pl.BlockSpec((tm, D_C), lambda i: (i, 0)), gate(0), gate(1), gate(2),
                  const((D_A, D_MODEL)), const((D_B, D_MODEL)), const((D_C, D_MODEL))],
        out_specs=pl.BlockSpec((tm, D_MODEL), lambda i: (i, 0)),
        out_shape=jax.ShapeDtypeStruct((m, D_MODEL), BF16),
        compiler_params=_cparams(("arbitrary",)),
        name="merge",
    )(ya, yb, yc, hin, hin, hin, wa, wb, wc)


def _outproj_kernel(m_ref, x_ref, g_ref, w_ref, o_ref):
    o_ref[...] = x_ref[...] + g_ref[0] * _dot(m_ref[...], w_ref[0])


def _outproj_call(merged, x, modx, rows_per_group, w, layer):
    m = x.shape[0]
    tm = min(512, m)
    return pl.pallas_call(
        _outproj_kernel,
        grid=(m // tm,),
        in_specs=[pl.BlockSpec((tm, D_MODEL), lambda i: (i, 0)),
                  pl.BlockSpec((tm, D_MODEL), lambda i: (i, 0)),
                  _mod_spec(modx, tm, rows_per_group, 2),
                  pl.BlockSpec((1, D_MODEL, D_MODEL), lambda i: (layer, 0, 0))],
        out_specs=pl.BlockSpec((tm, D_MODEL), lambda i: (i, 0)),
        out_shape=jax.ShapeDtypeStruct((m, D_MODEL), F32),
        compiler_params=_cparams(("arbitrary",)),
        name="outproj",
    )(merged, x, modx, w)


def _ffn_tail(ua, ua1, ua2, ub, cw_ref, cb_ref, wd_ref):
    cw = cw_ref[0]
    conv = cb_ref[0] + cw[0:1] * ua2 + cw[1:2] * ua1 + cw[2:3] * ua
    return _dot((jax.nn.gelu(conv) * ub).astype(BF16), wd_ref[0])


FFN_HALO = 16


def _ffn_prompt_kernel(x_ref, xh_ref, sc_ref, sh_ref, g_ref, nw_ref, cp_ref, wua_ref, wub_ref, cw_ref,
                       cb_ref, wd_ref, o_ref, tail_ref, h_scr, acc, *, tm, seq):
    f = pl.program_id(1)

    @pl.when(f == 0)
    def _():
        nw, sc, sh = nw_ref[...], sc_ref[0], sh_ref[0]
        h_scr[0:FFN_HALO, :] = (_rms(xh_ref[...], nw) * (1.0 + sc) + sh).astype(BF16)
        h_scr[FFN_HALO:, :] = (_rms(x_ref[...], nw) * (1.0 + sc) + sh).astype(BF16)
        acc[...] = jnp.zeros(acc.shape, F32)

    ua_all = _dot(h_scr[...], wua_ref[0])
    ua = ua_all[FFN_HALO:]
    ub = _dot(h_scr[FFN_HALO:, :], wub_ref[0])
    at_start = (pl.program_id(0) * tm) % seq == 0
    prev = jnp.where(at_start, cp_ref[0], ua_all[FFN_HALO - SUBLANE:FFN_HALO])
    row = lax.broadcasted_iota(I32, ua.shape, 0)
    p1, p2 = prev[SUBLANE - 1:SUBLANE], prev[SUBLANE - 2:SUBLANE - 1]
    ua1 = jnp.where(row == 0, p1, pltpu.roll(ua, 1, 0))
    ua2 = jnp.where(row == 0, p2, jnp.where(row == 1, p1, pltpu.roll(ua, 2, 0)))
    tail_ref[0] = ua[tm - SUBLANE:tm]
    acc[...] += _ffn_tail(ua, ua1, ua2, ub, cw_ref, cb_ref, wd_ref)

    @pl.when(f == pl.num_programs(1) - 1)
    def _():
        o_ref[...] = x_ref[...] + g_ref[0] * acc[...]


def _ffn_prompt_call(x, modx, seq, nw, cprev, fw, layer):
    m = x.shape[0]
    tm = min(512, m)
    hb = tm // FFN_HALO
    kern = functools.partial(_ffn_prompt_kernel, tm=tm, seq=seq)
    return pl.pallas_call(
        kern,
        grid=(m // tm, D_FFP // TF),
        in_specs=[pl.BlockSpec((tm, D_MODEL), lambda i, f: (i, 0)),
                  pl.BlockSpec((FFN_HALO, D_MODEL), lambda i, f: (jnp.maximum(i * hb - 1, 0), 0)),
                  _mod_spec(modx, tm, seq, 4), _mod_spec(modx, tm, seq, 3), _mod_spec(modx, tm, seq, 5),
                  pl.BlockSpec((1, D_MODEL), lambda i, f: (0, 0)),
                  pl.BlockSpec((1, SUBLANE, TF), lambda i, f: ((i * tm) // seq, 0, f)),
                  pl.BlockSpec((1, D_MODEL, TF), lambda i, f: (layer, 0, f)),
                  pl.BlockSpec((1, D_MODEL, TF), lambda i, f: (layer, 0, D_FFP // TF + f)),
                  pl.BlockSpec((1, CONV_W, TF), lambda i, f: (layer, 0, f)),
                  pl.BlockSpec((1, 1, TF), lambda i, f: (layer, 0, f)),
                  pl.BlockSpec((1, TF, D_MODEL), lambda i, f: (layer, f, 0))],
        out_specs=[pl.BlockSpec((tm, D_MODEL), lambda i, f: (i, 0)),
                   pl.BlockSpec((1, SUBLANE, TF), lambda i, f: (i, 0, f))],
        out_shape=[jax.ShapeDtypeStruct((m, D_MODEL), F32),
                   jax.ShapeDtypeStruct((m // tm, SUBLANE, D_FFP), F32)],
        scratch_shapes=[pltpu.VMEM((tm + FFN_HALO, D_MODEL), BF16), pltpu.VMEM((tm, D_MODEL), F32)],
        compiler_params=_cparams(("arbitrary", "arbitrary")),
        name="ffn_prompt",
    )(x, x, modx, modx, modx, nw, cprev, fw["wu"], fw["wu"], fw["cw"], fw["cb"], fw["wd"])


def _ffn_sample_kernel(x_ref, sc_ref, sh_ref, g_ref, nw_ref, p1_ref, p2_ref, wua_ref, wub_ref, cw_ref,
                       cb_ref, wd_ref, o_ref, ua_ref, h_scr, acc, *, dl):
    f = pl.program_id(0)

    @pl.when(f == 0)
    def _():
        h_scr[...] = (_rms(x_ref[...], nw_ref[...]) * (1.0 + sc_ref[0]) + sh_ref[0]).astype(BF16)
        acc[...] = jnp.zeros(acc.shape, F32)

    ua = _dot(h_scr[...], wua_ref[0])
    ub = _dot(h_scr[...], wub_ref[0])
    tin = lax.broadcasted_iota(I32, ua.shape, 0) % dl
    ua1 = jnp.where(tin >= 1, pltpu.roll(ua, 1, 0), p1_ref[...])
    ua2 = jnp.where(tin >= 2, pltpu.roll(ua, 2, 0), p2_ref[...])
    ua_ref[...] = ua
    acc[...] += _ffn_tail(ua, ua1, ua2, ub, cw_ref, cb_ref, wd_ref)

    @pl.when(f == pl.num_programs(0) - 1)
    def _():
        o_ref[...] = x_ref[...] + g_ref[0] * acc[...]


def _ffn_sample_call(x, modx, dl, nw, p1, p2, fw, layer):
    m = x.shape[0]
    kern = functools.partial(_ffn_sample_kernel, dl=dl)
    full = lambda i: pl.BlockSpec((m, D_MODEL), lambda f: (0, 0))
    return pl.pallas_call(
        kern,
        grid=(D_FFP // TF,),
        in_specs=[full(0),
                  pl.BlockSpec((1, m, D_MODEL), lambda f: (0, 0, 4)),
                  pl.BlockSpec((1, m, D_MODEL), lambda f: (0, 0, 3)),
                  pl.BlockSpec((1, m, D_MODEL), lambda f: (0, 0, 5)),
                  pl.BlockSpec((1, D_MODEL), lambda f: (0, 0)),
                  pl.BlockSpec((m, TF), lambda f: (0, f)), pl.BlockSpec((m, TF), lambda f: (0, f)),
                  pl.BlockSpec((1, D_MODEL, TF), lambda f: (layer, 0, f)),
                  pl.BlockSpec((1, D_MODEL, TF), lambda f: (layer, 0, D_FFP // TF + f)),
                  pl.BlockSpec((1, CONV_W, TF), lambda f: (layer, 0, f)),
                  pl.BlockSpec((1, 1, TF), lambda f: (layer, 0, f)),
                  pl.BlockSpec((1, TF, D_MODEL), lambda f: (layer, f, 0))],
        out_specs=[full(0), pl.BlockSpec((m, TF), lambda f: (0, f))],
        out_shape=[jax.ShapeDtypeStruct((m, D_MODEL), F32),
                   jax.ShapeDtypeStruct((m, D_FFP), F32)],
        scratch_shapes=[pltpu.VMEM((m, D_MODEL), BF16), pltpu.VMEM((m, D_MODEL), F32)],
        compiler_params=_cparams(("arbitrary",)),
        name="ffn_sample",
    )(x, modx, modx, modx, nw, p1, p2, fw["wu"], fw["wu"], fw["cw"], fw["cb"], fw["wd"])


def _rope_tables(pos):
    half = ROPE_DIM // 2
    inv = jnp.exp(jnp.arange(half, dtype=F32) * (-math.log(ROPE_THETA) / half))
    ang = pos.astype(F32)[:, None] * inv[None, :]
    cos, sin = jnp.cos(ang), jnp.sin(ang)
    n = pos.shape[0]
    rest = NSA_DH - ROPE_DIM
    cos_t = jnp.concatenate([cos, cos, jnp.ones((n, rest), F32)], axis=1)
    sin_a = jnp.concatenate([-sin, jnp.zeros((n, NSA_DH - half), F32)], axis=1)
    sin_b = jnp.concatenate([jnp.zeros((n, half), F32), sin, jnp.zeros((n, rest), F32)], axis=1)
    return cos_t, sin_a, sin_b


_W_IN_SIZES = (D_A, D_B, D_B, D_B, D_B, D_C, 6 * N_KV, 3 * NSA_HEADS, 3 * D_MODEL)
_W_IN_DST = (OFF_UA, OFF_HQ, OFF_HF, OFF_HI, OFF_HGT, OFF_NQ, OFF_NKV, OFF_NGT, OFF_MGT)
N_IN = sum(_W_IN_SIZES)


W_IN_RB = 256


def _w_in_src_row(i):
    src = jnp.int32(0)
    start = 0
    for size, dst in zip(_W_IN_SIZES, _W_IN_DST):
        inside = (i * W_IN_RB >= dst) & (i * W_IN_RB < dst + max(size, W_IN_RB))
        src = jnp.where(inside, start + i * W_IN_RB - dst, src)
        start += size
    return src


def _w_in_prep_kernel(x_ref, o_ref):
    i = pl.program_id(1)
    x = x_ref[0].astype(BF16)
    row = lax.broadcasted_iota(I32, x.shape, 0)
    o_ref[0] = jnp.where((i * W_IN_RB == OFF_NGT) & (row >= 3 * NSA_HEADS), jnp.zeros_like(x), x)


def _permute_w_in(w_in):
    depth, d, _ = w_in.shape
    w_t = jnp.swapaxes(w_in, 1, 2)
    return pl.pallas_call(
        _w_in_prep_kernel,
        grid=(depth, N_INP // W_IN_RB),
        in_specs=[pl.BlockSpec((pl.Element(1), pl.Element(W_IN_RB), pl.Element(d)),
                               lambda l, i: (l, pl.multiple_of(_w_in_src_row(i), SUBLANE), 0))],
        out_specs=pl.BlockSpec((1, W_IN_RB, d), lambda l, i: (l, i, 0)),
        out_shape=jax.ShapeDtypeStruct((depth, N_INP, d), BF16),
        compiler_params=_cparams(("arbitrary", "arbitrary")),
        name="w_in_prep",
    )(w_t)


def _w_up_prep_kernel(x_ref, o_ref):
    o_ref[0, :, :D_FF] = x_ref[0].astype(BF16)
    o_ref[0, :, D_FF:] = jnp.zeros((o_ref.shape[1], D_FFP - D_FF), BF16)


def _pad_w_up(w_up):
    depth, d, _ = w_up.shape
    r = 512
    return pl.pallas_call(
        _w_up_prep_kernel,
        grid=(depth, d // r, 2),
        in_specs=[pl.BlockSpec((1, r, D_FF), lambda l, i, h: (l, i, h))],
        out_specs=pl.BlockSpec((1, r, D_FFP), lambda l, i, h: (l, i, h)),
        out_shape=jax.ShapeDtypeStruct((depth, d, 2 * D_FFP), BF16),
        compiler_params=_cparams(("arbitrary", "arbitrary", "arbitrary")),
        name="w_up_prep",
    )(w_up)


def _s5_params(a_re, a_im, log_dt, b_re, b_im, c_re, c_im, d, w_glu, b_glu, seg_lens):
    dt = jnp.exp(log_dt)[:, None]
    mag = jnp.exp(a_re * dt)
    ab_re, ab_im = mag * jnp.cos(a_im * dt), mag * jnp.sin(a_im * dt)
    den = a_re * a_re + a_im * a_im
    cf_re = ((ab_re - 1.0) * a_re + ab_im * a_im) / den
    cf_im = (ab_im * a_re - (ab_re - 1.0) * a_im) / den
    bb_re = cf_re[..., None] * b_re - cf_im[..., None] * b_im
    bb_im = cf_re[..., None] * b_im + cf_im[..., None] * b_re
    eye = jnp.eye(S5_G, dtype=F32)
    bdiag = lambda bb: jnp.einsum("gnc,gh->gchn", bb, eye).reshape(D_A, S5_W).astype(BF16)
    cdiag = lambda cc: jnp.einsum("gcn,gh->gnhc", cc, eye).reshape(S5_W, D_A).astype(BF16)
    cmul = lambda a, b: (a[0] * b[0] - a[1] * b[1], a[0] * b[1] + a[1] * b[0])
    abar = (ab_re.reshape(1, S5_W), ab_im.reshape(1, S5_W))
    pw = [abar]
    for _ in range(max(seg_lens) - 1):
        pw.append(cmul(pw[-1], abar))

    def tables(ts):
        rep = lambda k: jnp.concatenate([jnp.broadcast_to(p[k], (SUBLANE, S5_W)) for p in pw[:ts]], axis=0)
        g1 = pw[ts - 1]
        g2 = cmul(g1, g1)
        g4 = cmul(g2, g2)
        pad = jnp.zeros((SUBLANE - 3, S5_W), F32)
        gp = lambda k: jnp.concatenate([g1[k], g2[k], g4[k], pad], axis=0)
        return rep(0), rep(1), gp(0), gp(1)

    return dict(bbr=bdiag(bb_re), bbi=bdiag(bb_im), ccr=cdiag(c_re), cci=cdiag(c_im),
                pow={ts: tables(ts) for ts in seg_lens},
                d=d.reshape(1, D_A), wg=w_glu.astype(BF16), bg=b_glu.reshape(1, D_A))


def _hgrn_params(lb):
    pos = lb > 0
    lb_safe = jnp.where(pos, lb, 1.0)
    z = jnp.zeros_like(lb)
    return jnp.stack([jnp.log(lb_safe), jnp.log1p(-lb), 1.0 - lb, pos.astype(F32), z, z, z, z], axis=0)


def _cmp_params(cmp_a, w1, b1, w2):
    def lanes(a):
        return jnp.concatenate([a[0], a[0], a[1], a[1]], axis=-1)
    return dict(a1=lanes(cmp_a[:, :CMP_STRIDE]), a2=lanes(cmp_a[:, CMP_STRIDE:]),
                w1=w1.astype(BF16), b1=b1, w2=w2.astype(BF16))


def _ffn_params(w_up, conv_w, conv_b, w_down):
    padc = lambda a: jnp.pad(a, [(0, 0)] * (a.ndim - 1) + [(0, D_FFP - D_FF)])
    wu = _pad_w_up(w_up)
    wd = jnp.pad(w_down.astype(BF16), ((0, 0), (0, D_FFP - D_FF), (0, 0)))
    return dict(wu=wu, cw=padc(conv_w), cb=padc(conv_b)[:, None, :], wd=wd)


def kernel(x_prompt, x_sample, cache_cmp, cache_slc, state_win, state_s5, state_hgrn, state_conv,
           page_table, c_prompt, c_sample, w_mod, b_mod, norm1_w, norm2_w, w_in,
           s5_a_re, s5_a_im, s5_log_dt, s5_b_re, s5_b_im, s5_c_re, s5_c_im, s5_d, s5_w_glu, s5_b_glu,
           hg_lb_logits, hg_norm_w, nsa_q_norm, nsa_k_norm, cmp_a, cmp_w1, cmp_b1, cmp_w2,
           w_branch_a, w_branch_b, w_branch_c, w_out, w_up, conv_w, conv_b, w_down):
    bsz, seq, _ = x_prompt.shape
    dbsz, dl, _ = x_sample.shape
    depth = w_in.shape[0]
    n_pages = page_table.shape[1]
    past = n_pages * PAGE
    wb = state_win.shape[2]
    mp, ms = bsz * seq, dbsz * dl
    rows = NSA_REP * dl
    assert seq % 512 == 0 and seq // SEL_BLOCK <= LANE and n_pages % PG == 0 and dl <= SUBLANE

    probs = jax.nn.softmax(hg_lb_logits.astype(F32), axis=0)
    lower_bounds = jnp.cumsum(probs, axis=0) - probs[0:1]
    w_in_p = _permute_w_in(w_in)
    wa, wbr, wc, wo = (w.astype(BF16) for w in (w_branch_a, w_branch_b, w_branch_c, w_out))
    fw = _ffn_params(w_up, conv_w, conv_b, w_down)
    rope_p = _rope_tables(jnp.arange(seq, dtype=I32))
    rope_s = tuple(jnp.tile(t, (dbsz, 1)) for t in _rope_tables(past + jnp.arange(dl, dtype=I32)))
    pt_flat = page_table.reshape(-1).astype(I32)
    cache_cmp4 = cache_cmp.reshape(depth, -1, PAGE * KV_TILES, LANE)
    cache_slc4 = cache_slc.reshape(depth, -1, PAGE * KV_TILES, LANE)
    win_prev4 = state_win.reshape(depth, dbsz, wb * KV_TILES, LANE)
    emat = (jnp.arange(PG * PAGE)[None, :] // SEL_BLOCK == jnp.arange(LANE)[:, None]).astype(BF16)

    nr = -(-(bsz + dbsz) // 16) * 16
    c_all = jnp.concatenate([c_prompt, c_sample, jnp.zeros((nr - bsz - dbsz, D_MODEL), F32)], axis=0)
    mod = _mod_call(c_all, w_mod, b_mod)

    cmp_lanes = lambda a: jnp.concatenate([a[:, 0], a[:, 0], a[:, 1], a[:, 1]], axis=-1).reshape(
        depth, SUB_ROWS, LANE)
    first, second = _cmp_pool_cache_call(cache_cmp4, pt_flat, dbsz, n_pages,
                                         cmp_lanes(cmp_a[:, :, :CMP_STRIDE]),
                                         cmp_lanes(cmp_a[:, :, CMP_STRIDE:]))
    kc_past = _cmp_mlp_cache_call(first, second, cmp_w1.astype(BF16), cmp_b1, cmp_w2.astype(BF16))

    xp = x_prompt.reshape(mp, D_MODEL)
    xs = x_sample.reshape(ms, D_MODEL)
    outs = [[] for _ in range(12)]
    row_bufs = None
    for l in range(depth):
        s5p = _s5_params(s5_a_re[l], s5_a_im[l], s5_log_dt[l], s5_b_re[l], s5_b_im[l], s5_c_re[l],
                         s5_c_im[l], s5_d[l], s5_w_glu[l], s5_b_glu[l],
                         (min(256, seq) // SUBLANE, 1))
        hgp = _hgrn_params(lower_bounds[l])
        hgn = hg_norm_w[l].reshape(1, HG_DV)
        cp = _cmp_params(cmp_a[l], cmp_w1[l], cmp_b1[l], cmp_w2[l])
        n1, n2 = norm1_w[l].reshape(1, D_MODEL), norm2_w[l].reshape(1, D_MODEL)
        qn, kn = nsa_q_norm[l].reshape(1, NSA_DH), nsa_k_norm[l]
        modp = mod[l, :bsz].reshape(bsz, 1, 6 * D_MODEL)
        mods = jnp.repeat(mod[l, bsz:bsz + dbsz], dl, axis=0).reshape(1, ms, 6 * D_MODEL)

        hin = _inproj_call(xp, modp, seq, n1, w_in_p, l)
        ya, s5_new = _s5_call(hin, OFF_UA // D_A, bsz, seq, jnp.zeros((bsz, 2, S5_W), F32), s5p,
                              SUBLANE - 1)
        yb, hg_new = _hgrn_call(hin, tuple(o // D_B for o in (OFF_HQ, OFF_HF, OFF_HI, OFF_HGT)), bsz, seq,
                                seq, jnp.zeros((bsz, HG_HEADS, HG_DK, HG_DV), F32), hgp, hgn)
        qo, *row_bufs, bc, bs, bw, gates = _nsa_prep_call(hin, rope_p, seq, qn, kn, l, depth, row_bufs)
        kc = _cmp_prompt_call(row_bufs[0], bsz, seq, cp, l)
        yc = _nsa_prompt_call(qo, kc, bs, bw, gates, bsz, seq)
        merged = _merge_call(ya, yb, yc, hin, wa, wbr, wc, l)
        xp = _outproj_call(merged, xp, modp, seq, wo, l)
        xp, tails = _ffn_prompt_call(xp, modp, seq, n2, jnp.zeros((bsz, SUBLANE, D_FFP), F32), fw, l)
        conv_new = tails.reshape(bsz, -1, SUBLANE, D_FFP)[:, -1, SUBLANE - (CONV_W - 1):, :D_FF]
        for i, o in ((0, s5_new.reshape(bsz, 2, S5_G, S5_N)), (1, hg_new), (5, conv_new)):
            outs[2 * i].append(o)

        hin = _inproj_call(xs, mods, dl, n1, w_in_p, l)
        padt = lambda a, n: jnp.pad(a.reshape(dbsz, dl, -1), ((0, 0), (0, n - dl), (0, 0))).reshape(dbsz * n, -1)
        u_pad = padt(hin[:, OFF_UA:OFF_UA + D_A], SUBLANE)
        ya, s5_new = _s5_call(u_pad, 0, dbsz, SUBLANE, state_s5[l].reshape(dbsz, 2, S5_W), s5p, dl - 1)
        ya = ya.reshape(dbsz, SUBLANE, D_A)[:, :dl].reshape(ms, D_A)
        hg_pad = padt(hin[:, OFF_HQ:OFF_HQ + 4 * D_B], HG_C)
        yb, hg_new = _hgrn_call(hg_pad, (0, 1, 2, 3), dbsz, HG_C, dl, state_hgrn[l], hgp, hgn)
        yb = yb.reshape(dbsz, HG_C, D_B)[:, :dl].reshape(ms, D_B)
        qo, rc, rs, rw, bc, bs, bw, gates = _nsa_prep_call(hin, rope_s, dl, qn, kn)
        q16 = qo.reshape(dbsz, dl, NSA_KVH, NSA_REP, NSA_DH).transpose(0, 2, 1, 3, 4).reshape(
            dbsz, NSA_KVH, rows, NSA_DH)
        o_c, selt = _nsa_s_sel_call(q16, kc_past, l, past, dl)
        g16 = gates[:, :3 * NSA_HEADS].reshape(dbsz, dl, NSA_KVH, NSA_REP, 3).transpose(0, 2, 1, 3, 4)
        g16 = jnp.pad(g16.reshape(dbsz, NSA_KVH, rows, 3), ((0, 0), (0, 0), (0, 0), (0, LANE - 3)))
        pad8 = lambda r: jnp.pad(r.reshape(dbsz, dl, 2 * N_KV), ((0, 0), (0, SUBLANE - dl), (0, 0)))
        yc = _nsa_s_attn_call(cache_slc4, pt_flat, l, q16, selt, emat, o_c, pad8(rs), pad8(rw), win_prev4,
                              g16, past, dl)
        yc = yc.reshape(dbsz, NSA_KVH, dl, NSA_REP, NSA_DH).transpose(0, 2, 1, 3, 4).reshape(ms, D_C)
        merged = _merge_call(ya, yb, yc, hin, wa, wbr, wc, l)
        xs = _outproj_call(merged, xs, mods, dl, wo, l)
        st = jnp.pad(state_conv[l], ((0, 0), (0, 0), (0, D_FFP - D_FF)))
        tin = jnp.arange(dl)[None, :, None]
        p1 = jnp.where(tin == 0, st[:, 1:2], 0.0).reshape(ms, D_FFP)
        p2 = jnp.where(tin == 0, st[:, 0:1], jnp.where(tin == 1, st[:, 1:2], 0.0)).reshape(ms, D_FFP)
        xs, ua = _ffn_sample_call(xs, mods, dl, n2, p1, p2, fw, l)
        ext = jnp.concatenate([state_conv[l], ua[:, :D_FF].reshape(dbsz, dl, D_FF)], axis=1)
        kv5 = lambda r: r.reshape(dbsz, dl, 2, NSA_KVH, NSA_DH)
        win_new = jnp.concatenate([state_win[l], kv5(rw)], axis=1)[:, -wb:]
        for i, o in enumerate((s5_new.reshape(dbsz, 2, S5_G, S5_N), hg_new, kv5(rc), kv5(rs), win_new,
                               ext[:, dl:])):
            outs[2 * i + 1].append(o)

    st = [jnp.stack(o) if o else None for o in outs]
    rows6 = [r.reshape(depth, bsz, seq, 2, NSA_KVH, NSA_DH) for r in row_bufs]
    st[4], st[6], st[8] = rows6[0], rows6[1], rows6[2][:, :, -min(WINDOW, seq):]
    return (xp.reshape(bsz, seq, D_MODEL), xs.reshape(dbsz, dl, D_MODEL), *st)
```

```python
import functools
import math

import jax
import jax.numpy as jnp
from jax import lax
from jax.experimental import pallas as pl
from jax.experimental.pallas import tpu as pltpu

F32 = jnp.float32
BF16 = jnp.bfloat16
I32 = jnp.int32

D_MODEL = 2048
D_A = D_MODEL // 4
S5_GROUP = 16
S5_G = D_A // S5_GROUP
S5_N = 64
S5_W = S5_G * S5_N
D_B = D_MODEL // 4
HG_DK = 128
HG_DV = 128
HG_HEADS = D_B // HG_DV
D_C = D_MODEL // 2
NSA_DH = 128
NSA_HEADS = D_C // NSA_DH
NSA_KVH = 2
NSA_REP = NSA_HEADS // NSA_KVH
N_KV = NSA_KVH * NSA_DH
ROPE_DIM = NSA_DH // 4
ROPE_THETA = 500000.0
CMP_STRIDE = 16
SEL_BLOCK = 64
N_SEL = 16
WINDOW = 512
FORCE_BONUS = 1000.0
D_FF = ((8 * D_MODEL) // 3 + 127) // 128 * 128
CONV_W = 3
EPS = 1e-6
NEG_INF = -1e30
PAGE = 128

LANE = 128
SUBLANE = 8
VMEM_LIMIT = 56 * 1024 * 1024

OFF_MGT = 0
OFF_NQ = 3 * D_MODEL
OFF_NKV = OFF_NQ + D_C
OFF_UA = OFF_NKV + 6 * N_KV
OFF_HQ = OFF_UA + D_A
OFF_HF = OFF_HQ + D_B
OFF_HI = OFF_HF + D_B
OFF_HGT = OFF_HI + D_B
OFF_NGT = OFF_HGT + D_B
NGT_PAD = 256
N_INP = OFF_NGT + NGT_PAD
TN_IN = 1280
D_FFP = 5632
TF = 512
HG_C = 64
HG_SB = 8


def _cparams(sem):
    return pltpu.CompilerParams(dimension_semantics=sem, vmem_limit_bytes=VMEM_LIMIT)


def _dot(a, b):
    return jnp.dot(a, b, preferred_element_type=F32)


def _dot_nt(a, b):
    return lax.dot_general(a, b, (((1,), (1,)), ((), ())), preferred_element_type=F32)


def _dot_tn(a, b):
    return lax.dot_general(a, b, (((0,), (0,)), ((), ())), preferred_element_type=F32)


def _dot2(a, b):
    hi = a.astype(BF16)
    lo = (a - hi.astype(F32)).astype(BF16)
    return _dot(hi, b) + _dot(lo, b)


def _dot3(a, b):
    hi = a.astype(BF16)
    r1 = a - hi.astype(F32)
    mid = r1.astype(BF16)
    lo = (r1 - mid.astype(F32)).astype(BF16)
    return _dot(b, hi) + _dot(b, mid) + _dot(b, lo)


def _rms(x, w):
    return x * lax.rsqrt(jnp.mean(x * x, axis=-1, keepdims=True) + EPS) * w


def _masked_softmax(s, mask):
    s = jnp.where(mask, s, NEG_INF)
    p = jnp.where(mask, jnp.exp(s - jnp.max(s, axis=-1, keepdims=True)), 0.0)
    return p / jnp.maximum(jnp.sum(p, axis=-1, keepdims=True), 1e-30)


def _row_reduce(x, combine, reduce):
    n = x.shape[-1]
    if n % LANE or n == LANE:
        return reduce(x, axis=-1, keepdims=True)
    acc = x[:, :LANE]
    for k in range(1, n // LANE):
        acc = combine(acc, x[:, k * LANE:(k + 1) * LANE])
    return reduce(acc, axis=-1, keepdims=True)


def _log_sigmoid(x):
    return jnp.minimum(x, 0.0) - jnp.log1p(jnp.exp(-jnp.abs(x)))


def _logaddexp(a, b):
    return jnp.maximum(a, b) + jnp.log1p(jnp.exp(-jnp.abs(a - b)))


def _mod_kernel(c_ref, w_ref, b_ref, o_ref):
    c = c_ref[...]
    a = (c * jax.nn.sigmoid(c)).astype(BF16)
    o_ref[0] = _dot(a, w_ref[0].astype(BF16)) + b_ref[0]


def _mod_call(c_all, w_mod, b_mod):
    depth, d, n = w_mod.shape
    r = c_all.shape[0]
    tn = 1024
    return pl.pallas_call(
        _mod_kernel,
        grid=(depth, n // tn),
        in_specs=[pl.BlockSpec((r, d), lambda l, j: (0, 0)),
                  pl.BlockSpec((1, d, tn), lambda l, j: (l, 0, j)),
                  pl.BlockSpec((1, 1, tn), lambda l, j: (l, 0, j))],
        out_specs=pl.BlockSpec((1, r, tn), lambda l, j: (l, 0, j)),
        out_shape=jax.ShapeDtypeStruct((depth, r, n), F32),
        compiler_params=_cparams(("arbitrary", "arbitrary")),
        name="mod",
    )(c_all, w_mod, b_mod.reshape(depth, 1, n))


def _inproj_kernel(x_ref, sc_ref, sh_ref, nw_ref, w_ref, o_ref, h_scr):
    @pl.when(pl.program_id(1) == 0)
    def _():
        y = _rms(x_ref[...], nw_ref[...])
        h_scr[...] = (y * (1.0 + sc_ref[0]) + sh_ref[0]).astype(BF16)

    o_ref[...] = _dot_nt(h_scr[...], w_ref[0])


def _mod_spec(modx, tm, rows_per_group, col):
    if modx.shape[1] == 1:
        return pl.BlockSpec((1, 1, D_MODEL), lambda i, *_: ((i * tm) // rows_per_group, 0, col))
    return pl.BlockSpec((1, tm, D_MODEL), lambda i, *_: (0, i, col))


def _inproj_call(x, modx, rows_per_group, nw, w, layer):
    m = x.shape[0]
    tm = min(1024, m)
    return pl.pallas_call(
        _inproj_kernel,
        grid=(m // tm, N_INP // TN_IN),
        in_specs=[pl.BlockSpec((tm, D_MODEL), lambda i, j: (i, 0)),
                  _mod_spec(modx, tm, rows_per_group, 1),
                  _mod_spec(modx, tm, rows_per_group, 0),
                  pl.BlockSpec((1, D_MODEL), lambda i, j: (0, 0)),
                  pl.BlockSpec((1, TN_IN, D_MODEL), lambda i, j: (layer, j, 0))],
        out_specs=pl.BlockSpec((tm, TN_IN), lambda i, j: (i, j)),
        out_shape=jax.ShapeDtypeStruct((m, N_INP), F32),
        scratch_shapes=[pltpu.VMEM((tm, D_MODEL), BF16)],
        compiler_params=_cparams(("arbitrary", "arbitrary")),
        name="inproj",
    )(x, modx, modx, nw, w)


S5_LC = 512


def _s5_kernel(u_ref, h0_ref, bbr_ref, bbi_ref, ccr_ref, cci_ref, apr_ref, api_ref, gpr_ref, gpi_ref,
               d_ref, wg_ref, bg_ref, y_ref, hl_ref, hre, him, car, *, t_rows, last_row):
    ts = t_rows // SUBLANE

    @pl.when(pl.program_id(1) == 0)
    def _():
        car[...] = h0_ref[0]

    ri = lax.broadcasted_iota(I32, (t_rows, t_rows), 0)
    ci = lax.broadcasted_iota(I32, (t_rows, t_rows), 1)
    perm = (ci == (ri % SUBLANE) * ts + ri // SUBLANE).astype(BF16)
    unperm = (ci == (ri % ts) * SUBLANE + ri // ts).astype(BF16)
    u = _dot3(u_ref[...], perm)
    ub = u.astype(BF16)
    nblk = D_A // LANE
    for j in range(nblk):
        us, ss = slice(j * LANE, (j + 1) * LANE), slice(j * S5_LC, (j + 1) * S5_LC)
        hre[:, ss] = _dot(ub[:, us], bbr_ref[us, ss])
        him[:, ss] = _dot(ub[:, us], bbi_ref[us, ss])
    row = lax.broadcasted_iota(I32, (SUBLANE, S5_LC), 0)
    zero = jnp.zeros((SUBLANE, S5_LC), F32)
    for c in range(S5_W // S5_LC):
        sl = slice(c * S5_LC, (c + 1) * S5_LC)
        ar, ai = apr_ref[0:SUBLANE, sl], api_ref[0:SUBLANE, sl]

        def local(t, carry, sl=sl, ar=ar, ai=ai):
            hr, hi = carry
            r0 = pl.multiple_of(t * SUBLANE, SUBLANE)
            nr = ar * hr - ai * hi + hre[pl.ds(r0, SUBLANE), sl]
            ni = ar * hi + ai * hr + him[pl.ds(r0, SUBLANE), sl]
            hre[pl.ds(r0, SUBLANE), sl] = nr
            him[pl.ds(r0, SUBLANE), sl] = ni
            return nr, ni

        er, ei = lax.fori_loop(0, ts, local, (zero, zero))
        yr = jnp.where(row == 0, car[0:1, sl], pltpu.roll(er, 1, 0))
        yi = jnp.where(row == 0, car[1:2, sl], pltpu.roll(ei, 1, 0))
        for k, s in enumerate((1, 2, 4)):
            gr, gi = gpr_ref[k:k + 1, sl], gpi_ref[k:k + 1, sl]
            sr = jnp.where(row >= s, pltpu.roll(yr, s, 0), 0.0)
            si = jnp.where(row >= s, pltpu.roll(yi, s, 0), 0.0)
            yr, yi = yr + (gr * sr - gi * si), yi + (gr * si + gi * sr)
        g1r, g1i = gpr_ref[0:1, sl], gpi_ref[0:1, sl]
        car[0:1, sl] = (g1r * yr - g1i * yi + er)[SUBLANE - 1:SUBLANE]
        car[1:2, sl] = (g1r * yi + g1i * yr + ei)[SUBLANE - 1:SUBLANE]

        def fix(t, carry, sl=sl, yr=yr, yi=yi):
            r0 = pl.multiple_of(t * SUBLANE, SUBLANE)
            pr, pi = apr_ref[pl.ds(r0, SUBLANE), sl], api_ref[pl.ds(r0, SUBLANE), sl]
            hre[pl.ds(r0, SUBLANE), sl] += pr * yr - pi * yi
            him[pl.ds(r0, SUBLANE), sl] += pr * yi + pi * yr
            return carry

        lax.fori_loop(0, ts, fix, 0)

    lr = t_rows - SUBLANE + last_row
    hl_ref[0, 0:1, :] = hre[lr:lr + 1, :]
    hl_ref[0, 1:2, :] = him[lr:lr + 1, :]
    y = jnp.concatenate(
        [_dot(hre[:, j * S5_LC:(j + 1) * S5_LC].astype(BF16),
              ccr_ref[j * S5_LC:(j + 1) * S5_LC, j * LANE:(j + 1) * LANE])
         - _dot(him[:, j * S5_LC:(j + 1) * S5_LC].astype(BF16),
                cci_ref[j * S5_LC:(j + 1) * S5_LC, j * LANE:(j + 1) * LANE])
         for j in range(nblk)], axis=1) + d_ref[...] * u
    y = jax.nn.gelu(y)
    z = _dot(y.astype(BF16), wg_ref[...]) + bg_ref[...]
    y_ref[...] = _dot(unperm, (y * jax.nn.sigmoid(z)).astype(BF16)).astype(BF16)


def _s5_call(u_arr, ucol, nb, seq, h0, p, last_row):
    t_rows = min(256, seq)
    nt = seq // t_rows
    const = lambda shape: pl.BlockSpec(shape, lambda b, t: (0,) * len(shape))
    kern = functools.partial(_s5_kernel, t_rows=t_rows, last_row=last_row)
    return pl.pallas_call(
        kern,
        grid=(nb, nt),
        in_specs=[pl.BlockSpec((t_rows, D_A), lambda b, t: (b * nt + t, ucol)),
                  pl.BlockSpec((1, 2, S5_W), lambda b, t: (b, 0, 0)),
                  const((D_A, S5_W)), const((D_A, S5_W)), const((S5_W, D_A)), const((S5_W, D_A)),
                  const((t_rows, S5_W)), const((t_rows, S5_W)),
                  const((SUBLANE, S5_W)), const((SUBLANE, S5_W)), const((1, D_A)),
                  const((D_A, D_A)), const((1, D_A))],
        out_specs=[pl.BlockSpec((t_rows, D_A), lambda b, t: (b * nt + t, 0)),
                   pl.BlockSpec((1, 2, S5_W), lambda b, t: (b, 0, 0))],
        out_shape=[jax.ShapeDtypeStruct((nb * seq, D_A), BF16),
                   jax.ShapeDtypeStruct((nb, 2, S5_W), F32)],
        scratch_shapes=[pltpu.VMEM((t_rows, S5_W), F32), pltpu.VMEM((t_rows, S5_W), F32),
                        pltpu.VMEM((2, S5_W), F32)],
        compiler_params=_cparams(("arbitrary", "arbitrary")),
        name="s5",
    )(u_arr, h0, p["bbr"], p["bbi"], p["ccr"], p["cci"], *p["pow"][t_rows // SUBLANE], p["d"],
      p["wg"], p["bg"])


def _hgrn_kernel(q_ref, f_ref, i_ref, g_ref, s0_ref, lp_ref, nw_ref, y_ref, sl_ref, st, pscr, rscr,
                 *, l_valid, chunks):
    t = pl.program_id(1)
    c = HG_C
    nsb = c // HG_SB
    nh = HG_HEADS

    @pl.when(t == 0)
    def _():
        for h in range(nh):
            st[h] = s0_ref[0, h].T

    row = lax.broadcasted_iota(I32, (c, D_B), 0)
    tri = (lax.broadcasted_iota(I32, (c, c), 0) >= lax.broadcasted_iota(I32, (c, c), 1)).astype(BF16)
    srow = lax.broadcasted_iota(I32, (HG_SB, D_B), 0)
    rows16 = lambda r: jnp.broadcast_to(r, (HG_SB, D_B))

    def chunk(ci, carry):
        c0 = pl.multiple_of(ci * c, c)
        crows = pl.ds(c0, c)
        valid = (t * (chunks * c) + c0 + row) < l_valid
        hf = f_ref[crows, :]
        ls = _log_sigmoid(hf)
        lf = jnp.where(lp_ref[3:4, :] > 0.5, _logaddexp(lp_ref[0:1, :], lp_ref[1:2, :] + ls), ls)
        kk = lp_ref[2:3, :] * jax.nn.sigmoid(-hf)
        lf = jnp.where(valid, lf, 0.0)
        kk = jnp.where(valid, kk, 0.0)
        hq = q_ref[crows, :]
        q = hq * jax.nn.sigmoid(hq)
        v = i_ref[crows, :]
        b = _dot3(lf, tri)
        zero_row = jnp.zeros((1, D_B), F32)
        bref = [zero_row] + [b[i * HG_SB - 1:i * HG_SB] for i in range(1, nsb)]
        bend = [b[(j + 1) * HG_SB - 1:(j + 1) * HG_SB] for j in range(nsb)]
        qq = q * jnp.exp(b - jnp.concatenate([rows16(r) for r in bref], axis=0))
        kks = kk * jnp.exp(jnp.concatenate([rows16(r) for r in bend], axis=0) - b)
        lhs = []
        for j in range(nsb - 1):
            dsel = jnp.concatenate(
                [rows16(jnp.exp(bref[i] - bend[j])) if i > j else jnp.zeros((HG_SB, D_B), F32)
                 for i in range(nsb)], axis=0)
            lhs.append((qq * dsel).astype(BF16))
        rhs = [jnp.where(row // HG_SB == j, kks, 0.0).astype(BF16) for j in range(nsb - 1)]
        for i in range(nsb):
            rs = slice(i * HG_SB, (i + 1) * HG_SB)
            bi, qi, ki = b[rs], q[rs], kk[rs]
            for s0 in range(0, HG_SB, 2):
                pair = []
                for s in (s0, s0 + 1):
                    e = jnp.exp(jnp.where(srow >= s, bi - bi[s:s + 1], NEG_INF))
                    pair.append((qi * ki[s:s + 1]) * e)
                piece = jnp.concatenate(pair, axis=0).astype(BF16)
                for h in range(nh):
                    r0 = ((h * nsb + i) * HG_SB + s0) * HG_SB
                    pscr[r0:r0 + 2 * HG_SB, :] = piece[:, h * HG_DK:(h + 1) * HG_DK]
        rscr[...] = _dot(pscr[...], jnp.ones((HG_DK, HG_DK), BF16))
        qe = (q * jnp.exp(b)).astype(BF16)
        vb = v.astype(BF16)
        bl = b[c - 1:c]
        kdec = (kk * jnp.exp(bl - b)).astype(BF16)
        ebl = jnp.exp(bl)
        gt = g_ref[crows, :]
        gate = gt * jax.nn.sigmoid(gt)
        for h in range(nh):
            hs = slice(h * HG_DK, (h + 1) * HG_DK)
            s_t = st[h]
            att = _dot_nt(jnp.concatenate([x[:, hs] for x in lhs], axis=1),
                          jnp.concatenate([x[:, hs] for x in rhs], axis=1))
            o = _dot(att.astype(BF16), vb[:, hs]) + _dot_nt(qe[:, hs], s_t.astype(BF16))
            diag = []
            for i in range(nsb):
                od = jnp.zeros((HG_SB, HG_DV), F32)
                for s in range(HG_SB):
                    r0 = ((h * nsb + i) * HG_SB + s) * HG_SB
                    od = od + rscr[r0:r0 + HG_SB, :] * v[i * HG_SB + s:i * HG_SB + s + 1, hs]
                diag.append(od)
            o = o + jnp.concatenate(diag, axis=0)
            st[h] = s_t * ebl[:, hs] + _dot_tn(vb[:, hs], kdec[:, hs])
            y_ref[crows, hs] = (_rms(o, nw_ref[...]) * gate[:, hs]).astype(BF16)
        return carry

    lax.fori_loop(0, chunks, chunk, 0)

    @pl.when(t == pl.num_programs(1) - 1)
    def _():
        for h in range(nh):
            sl_ref[0, h] = st[h].T


def _dot2g(g, z):
    hi = z.astype(BF16)
    lo = (z - hi.astype(F32)).astype(BF16)
    return _dot(g, hi) + _dot(g, lo)


def _hgrn_call(arr, cols, nb, seq_pad, l_valid, s0, lp, nw):
    chunks = 4 if seq_pad % (4 * HG_C) == 0 else 1
    rows = chunks * HG_C
    nt = seq_pad // rows
    blk = lambda col: pl.BlockSpec((rows, D_B), lambda b, t: (b * nt + t, col))
    kern = functools.partial(_hgrn_kernel, l_valid=l_valid, chunks=chunks)
    return pl.pallas_call(
        kern,
        grid=(nb, nt),
        in_specs=[blk(cols[0]), blk(cols[1]), blk(cols[2]), blk(cols[3]),
                  pl.BlockSpec((1, HG_HEADS, HG_DK, HG_DV), lambda b, t: (b, 0, 0, 0)),
                  pl.BlockSpec((SUBLANE, D_B), lambda b, t: (0, 0)),
                  pl.BlockSpec((1, HG_DV), lambda b, t: (0, 0))],
        out_specs=[pl.BlockSpec((rows, D_B), lambda b, t: (b * nt + t, 0)),
                   pl.BlockSpec((1, HG_HEADS, HG_DK, HG_DV), lambda b, t: (b, 0, 0, 0))],
        out_shape=[jax.ShapeDtypeStruct((nb * seq_pad, D_B), BF16),
                   jax.ShapeDtypeStruct((nb, HG_HEADS, HG_DK, HG_DV), F32)],
        scratch_shapes=[pltpu.VMEM((HG_HEADS, HG_DV, HG_DK), F32),
                        pltpu.VMEM((HG_HEADS * HG_C * HG_SB, HG_DK), BF16),
                        pltpu.VMEM((HG_HEADS * HG_C * HG_SB, HG_DK), F32)],
        compiler_params=_cparams(("arbitrary", "arbitrary")),
        name="hgrn",
    )(arr, arr, arr, arr, s0, lp, nw)


def _rope(x, cos_t, sin_a, sin_b):
    return (x * cos_t + pltpu.roll(x, NSA_DH - ROPE_DIM // 2, 1) * sin_a
            + pltpu.roll(x, ROPE_DIM // 2, 1) * sin_b)


def _nsa_prep_kernel(q_ref, kc_ref, ks_ref, kw_ref, g_ref, cos_ref, sa_ref, sb_ref, qn_ref, kn_ref,
                     qo_ref, rc_ref, rs_ref, rw_ref, bc_ref, bs_ref, bw_ref, go_ref):
    cos_t, sin_a, sin_b = cos_ref[...], sa_ref[...], sb_ref[...]
    qn = qn_ref[...]
    for h in range(NSA_HEADS):
        hs = slice(h * NSA_DH, (h + 1) * NSA_DH)
        qo_ref[:, hs] = _rope(_rms(q_ref[:, hs], qn), cos_t, sin_a, sin_b).astype(BF16)
    for br, (src, dst, dstb) in enumerate(((kc_ref, rc_ref, bc_ref), (ks_ref, rs_ref, bs_ref),
                                           (kw_ref, rw_ref, bw_ref))):
        kn = kn_ref[br:br + 1, :]
        tm = src.shape[0]
        for g in range(NSA_KVH):
            gs = slice(g * NSA_DH, (g + 1) * NSA_DH)
            k = _rope(_rms(src[:, gs], kn), cos_t, sin_a, sin_b)
            dst[pl.ds(g, tm, stride=KV_TILES), :] = k
            dstb[:, gs] = k.astype(BF16)
            vs = slice(N_KV + g * NSA_DH, N_KV + (g + 1) * NSA_DH)
            v = src[:, vs]
            dst[pl.ds(NSA_KVH + g, tm, stride=KV_TILES), :] = v
            dstb[:, vs] = v.astype(BF16)
    go_ref[...] = jax.nn.sigmoid(g_ref[...])


def _nsa_prep_call(hin, rope_tabs, seq, qn, kn, layer=0, depth=1, row_bufs=None):
    m = hin.shape[0]
    tm = min(256, m)
    npos = rope_tabs[0].shape[0] // tm
    rspec = pl.BlockSpec((tm, NSA_DH), lambda i: (i % npos, 0))
    kvb = OFF_NKV // (2 * N_KV)
    n_in = 10
    extra = () if row_bufs is None else tuple(row_bufs)

    def kern(*refs):
        _nsa_prep_kernel(*refs[:n_in], *refs[n_in + len(extra):])

    return pl.pallas_call(
        kern,
        grid=(m // tm,),
        input_output_aliases={n_in + k: 1 + k for k in range(len(extra))},
        in_specs=[pl.BlockSpec((tm, D_C), lambda i: (i, OFF_NQ // D_C)),
                  pl.BlockSpec((tm, 2 * N_KV), lambda i: (i, kvb)),
                  pl.BlockSpec((tm, 2 * N_KV), lambda i: (i, kvb + 1)),
                  pl.BlockSpec((tm, 2 * N_KV), lambda i: (i, kvb + 2)),
                  pl.BlockSpec((tm, LANE), lambda i: (i, OFF_NGT // LANE)),
                  rspec, rspec, rspec,
                  pl.BlockSpec((1, NSA_DH), lambda i: (0, 0)),
                  pl.BlockSpec((3, NSA_DH), lambda i: (0, 0))]
        + [pl.BlockSpec(memory_space=pl.ANY)] * len(extra),
        out_specs=[pl.BlockSpec((tm, D_C), lambda i: (i, 0))]
        + [pl.BlockSpec((tm * KV_TILES, LANE), lambda i: (layer * (m // tm) + i, 0))] * 3
        + [pl.BlockSpec((tm, 2 * N_KV), lambda i: (i, 0))] * 3
        + [pl.BlockSpec((tm, LANE), lambda i: (i, 0))],
        out_shape=[jax.ShapeDtypeStruct((m, D_C), BF16)]
        + [jax.ShapeDtypeStruct((depth * m * KV_TILES, LANE), F32)] * 3
        + [jax.ShapeDtypeStruct((m, 2 * N_KV), BF16)] * 3
        + [jax.ShapeDtypeStruct((m, LANE), F32)],
        compiler_params=_cparams(("arbitrary",)),
        name="nsa_prep",
    )(hin, hin, hin, hin, hin, *rope_tabs, qn, kn, *extra)


KV_TILES = 2 * N_KV // LANE


def _pool_rows(load, n, a1_ref, a2_ref):
    firsts, seconds = [], []
    for c in range(KV_TILES):
        ls = slice(c * LANE, (c + 1) * LANE)
        first = jnp.zeros((n, LANE), F32)
        second = jnp.zeros((n, LANE), F32)
        for j in range(CMP_STRIDE):
            x = load(j, c)
            first = first + x * a1_ref[j:j + 1, ls]
            second = second + x * a2_ref[j:j + 1, ls]
        firsts.append(first)
        seconds.append(second)
    return firsts, seconds


def _tile_rows(j, c, n):
    return pl.ds(j * KV_TILES + c, n, stride=CMP_STRIDE * KV_TILES)


def _cmp_mlp(first, second, n, w1_ref, b1_ref, w2_ref, o_ref):
    for c in range(2):
        for g in range(NSA_KVH):
            tile = 2 * c + g
            pooled = first[tile] + pltpu.roll(second[tile], n - 1, 0)
            h = _dot(pooled.astype(BF16), w1_ref[c]) + b1_ref[c:c + 1, :]
            o_ref[0, :, tile * NSA_DH:(tile + 1) * NSA_DH] = _dot(
                jax.nn.gelu(h).astype(BF16), w2_ref[c]).astype(BF16)


def _cmp_prompt_kernel(x_ref, a1_ref, a2_ref, w1_ref, b1_ref, w2_ref, o_ref, *, n):
    first, second = _pool_rows(lambda j, c: x_ref[_tile_rows(j, c, n), :], n, a1_ref, a2_ref)
    _cmp_mlp(first, second, n, w1_ref, b1_ref, w2_ref, o_ref)


def _cmp_prompt_call(rows, nb, seq, cp, layer=0):
    n = seq // CMP_STRIDE
    const = lambda shape: pl.BlockSpec(shape, lambda b: (0,) * len(shape))
    return pl.pallas_call(
        functools.partial(_cmp_prompt_kernel, n=n),
        grid=(nb,),
        in_specs=[pl.BlockSpec((seq * KV_TILES, LANE), lambda b: (layer * nb + b, 0)),
                  const((CMP_STRIDE, 2 * N_KV)), const((CMP_STRIDE, 2 * N_KV)),
                  const((2, NSA_DH, NSA_DH)), const((2, NSA_DH)), const((2, NSA_DH, NSA_DH))],
        out_specs=pl.BlockSpec((1, n, 2 * N_KV), lambda b: (b, 0, 0)),
        out_shape=jax.ShapeDtypeStruct((nb, n, 2 * N_KV), BF16),
        compiler_params=_cparams(("arbitrary",)),
        name="cmp_prompt",
    )(rows.reshape(-1, LANE), cp["a1"], cp["a2"], cp["w1"], cp["b1"], cp["w2"])


PG = 16
SUB_PER_PAGE = PAGE // CMP_STRIDE


SUB_ROWS = CMP_STRIDE * KV_TILES


def _cmp_pool_cache_kernel(pt_ref, *refs):
    pages = refs[:PG]
    a1_ref, a2_ref, f_ref, s_ref = refs[PG:]
    nv = SUB_ROWS // SUBLANE
    w1 = [a1_ref[0, v * SUBLANE:(v + 1) * SUBLANE, :] for v in range(nv)]
    w2 = [a2_ref[0, v * SUBLANE:(v + 1) * SUBLANE, :] for v in range(nv)]
    low = lax.broadcasted_iota(I32, (SUBLANE, LANE), 0) < KV_TILES

    def fold(acc):
        return acc + pltpu.roll(acc, KV_TILES, 0)

    for k in range(PG):
        for n2 in range(SUB_PER_PAGE // 2):
            halves = []
            for n in (2 * n2, 2 * n2 + 1):
                a1 = a2 = None
                for v in range(nv):
                    x = pages[k][0, 0, n * SUB_ROWS + v * SUBLANE:n * SUB_ROWS + (v + 1) * SUBLANE, :]
                    a1 = x * w1[v] if a1 is None else a1 + x * w1[v]
                    a2 = x * w2[v] if a2 is None else a2 + x * w2[v]
                halves.append((fold(a1), fold(a2)))
            r0 = (k * SUB_PER_PAGE + 2 * n2) * KV_TILES
            f_ref[0, 0, r0:r0 + SUBLANE, :] = jnp.where(low, halves[0][0], halves[1][0])
            s_ref[0, 0, r0:r0 + SUBLANE, :] = jnp.where(low, halves[0][1], halves[1][1])


def _page_specs(n_pages, layer_of, batch_of, group_of, page_shape=(PAGE, 2 * N_KV)):
    def spec(k):
        return pl.BlockSpec(
            (1, 1) + page_shape,
            lambda *a, k=k: (layer_of(*a), a[-1][batch_of(*a) * n_pages + group_of(*a) * PG + k], 0, 0))
    return [spec(k) for k in range(PG)]


def _cmp_pool_cache_call(cache, pt_flat, nb, n_pages, a1, a2):
    depth = cache.shape[0]
    npg = n_pages // PG
    nsub = n_pages * SUB_PER_PAGE
    rows = PG * SUB_PER_PAGE
    grid_spec = pltpu.PrefetchScalarGridSpec(
        num_scalar_prefetch=1,
        grid=(depth, nb, npg),
        in_specs=_page_specs(n_pages, lambda l, b, g, pt: l, lambda l, b, g, pt: b,
                             lambda l, b, g, pt: g, (PAGE * KV_TILES, LANE))
        + [pl.BlockSpec((1, SUB_ROWS, LANE), lambda l, b, g, pt: (l, 0, 0))] * 2,
        out_specs=[pl.BlockSpec((1, 1, rows * KV_TILES, LANE), lambda l, b, g, pt: (l, b, g, 0))] * 2,
    )
    return pl.pallas_call(
        _cmp_pool_cache_kernel,
        grid_spec=grid_spec,
        out_shape=[jax.ShapeDtypeStruct((depth, nb, nsub * KV_TILES, LANE), F32)] * 2,
        compiler_params=_cparams(("arbitrary", "arbitrary", "arbitrary")),
        name="cmp_pool_cache",
    )(pt_flat, *([cache] * PG), a1, a2)


def _cmp_mlp_cache_kernel(f_ref, s_ref, w1_ref, b1_ref, w2_ref, o_ref, *, n):
    tiles = lambda ref: [ref[0, 0, pl.ds(c, n, stride=KV_TILES), :] for c in range(KV_TILES)]
    _cmp_mlp(tiles(f_ref), tiles(s_ref), n, w1_ref.at[0], b1_ref.at[0], w2_ref.at[0], o_ref.at[0])


def _cmp_mlp_cache_call(first, second, w1, b1, w2):
    depth, nb, rows, _ = first.shape
    n = rows // KV_TILES
    blk = pl.BlockSpec((1, 1, n, 2 * N_KV), lambda l, b: (l, b, 0, 0))
    tblk = pl.BlockSpec((1, 1, rows, LANE), lambda l, b: (l, b, 0, 0))
    return pl.pallas_call(
        functools.partial(_cmp_mlp_cache_kernel, n=n),
        grid=(depth, nb),
        in_specs=[tblk, tblk,
                  pl.BlockSpec((1, 2, NSA_DH, NSA_DH), lambda l, b: (l, 0, 0, 0)),
                  pl.BlockSpec((1, 2, NSA_DH), lambda l, b: (l, 0, 0)),
                  pl.BlockSpec((1, 2, NSA_DH, NSA_DH), lambda l, b: (l, 0, 0, 0))],
        out_specs=blk,
        out_shape=jax.ShapeDtypeStruct((depth, nb, n, 2 * N_KV), BF16),
        compiler_params=_cparams(("arbitrary", "arbitrary")),
        name="cmp_mlp_cache",
    )(first, second, w1, b1, w2)


def _block_scores(imp, qpos, n_slc):
    blk = lax.broadcasted_iota(I32, imp.shape, 1)
    cur = qpos // SEL_BLOCK
    forced = (blk == 0) | (blk == cur) | (blk == cur - 1)
    valid = blk * SEL_BLOCK <= qpos
    score = jnp.where(valid, imp + jnp.where(forced, FORCE_BONUS, 0.0), -1.0)
    return jnp.where(blk < n_slc, score, -2.0)


def _select_blocks_ranked(imp, qpos, n_slc):
    nb8 = (n_slc + SUBLANE - 1) // SUBLANE * SUBLANE
    st = _block_scores(imp, qpos, n_slc).T[:nb8]
    ridx = lax.broadcasted_iota(I32, st.shape, 0)
    cnt = jnp.zeros(st.shape, F32)
    for i in range(n_slc):
        ri = st[i:i + 1]
        cnt = cnt + jnp.where(ri > st, 1.0, jnp.where(ri == st, jnp.where(ridx > i, 1.0, 0.0), 0.0))
    sel_t = jnp.where((cnt < min(N_SEL, n_slc)) & (ridx < n_slc), 1.0, 0.0)
    if nb8 < LANE:
        sel_t = jnp.concatenate([sel_t, jnp.zeros((LANE - nb8, st.shape[1]), F32)], axis=0)
    return sel_t.T


def _select_blocks(imp, qpos, n_slc):
    blk = lax.broadcasted_iota(I32, imp.shape, 1)
    score = _block_scores(imp, qpos, n_slc)
    sel = jnp.zeros(imp.shape, F32)
    for _ in range(min(N_SEL, n_slc)):
        mx = jnp.max(score, axis=-1, keepdims=True)
        idx = jnp.min(jnp.where(score == mx, blk, 1 << 30), axis=-1, keepdims=True)
        hit = blk == idx
        sel = jnp.where(hit, 1.0, sel)
        score = jnp.where(hit, -3.0, score)
    return sel


def _cmp_branch(q, kck, kcv, qpos, n_cmp):
    s = _dot_nt(q, kck) * (NSA_DH ** -0.5)
    col = lax.broadcasted_iota(I32, s.shape, 1)
    mask = (col * CMP_STRIDE + (2 * CMP_STRIDE - 1) <= qpos) & (col < n_cmp)
    p = _masked_softmax(s, mask)
    return p, _dot(p.astype(BF16), kcv)


NSA_KC = 1024


def _nsa_prompt_kernel(q_ref, kc_ref, ks_ref, vs_ref, kw_ref, vw_ref, g_ref, o_ref,
                       bias_scr, m_scr, l_scr, acc_scr, *, tq, seq, slab):
    q0 = pl.program_id(1) * tq
    rows = NSA_REP * tq
    ncp = seq // CMP_STRIDE
    n_slc = seq // SEL_BLOCK
    scale = NSA_DH ** -0.5
    gates = g_ref[...]
    q4s = [jnp.concatenate([q_ref[:, (g * NSA_REP + r) * NSA_DH:(g * NSA_REP + r + 1) * NSA_DH]
                            for r in range(NSA_REP)], axis=0) for g in range(NSA_KVH)]
    qpos_c = q0 + (lax.broadcasted_iota(I32, (rows, ncp), 0) & (tq - 1))
    o_cs, psums = [], []
    for g in range(NSA_KVH):
        p_c, o_c = _cmp_branch(q4s[g], kc_ref[0, :, g * NSA_DH:(g + 1) * NSA_DH],
                               kc_ref[0, :, N_KV + g * NSA_DH:N_KV + (g + 1) * NSA_DH], qpos_c, ncp - 1)
        psum = p_c[0:tq]
        for r in range(1, NSA_REP):
            psum = psum + p_c[r * tq:(r + 1) * tq]
        o_cs.append(o_c)
        psums.append(psum)

    def ranked():
        gmat = (lax.broadcasted_iota(I32, (ncp, LANE), 0) // (SEL_BLOCK // CMP_STRIDE)
                == lax.broadcasted_iota(I32, (ncp, LANE), 1)).astype(BF16)
        imp = _dot2(jnp.concatenate(psums, axis=0), gmat)
        qp = q0 + (lax.broadcasted_iota(I32, (NSA_KVH * tq, LANE), 0) & (tq - 1))
        return _select_blocks_ranked(imp, qp, n_slc)

    sel = lax.cond(q0 + tq > N_SEL * SEL_BLOCK, ranked,
                   lambda: jnp.ones((NSA_KVH * tq, LANE), F32))
    emat = (lax.broadcasted_iota(I32, (LANE, seq), 1) // SEL_BLOCK
            == lax.broadcasted_iota(I32, (LANE, seq), 0)).astype(BF16)
    nk = (q0 + tq + NSA_KC - 1) // NSA_KC
    s0 = pl.multiple_of(jnp.maximum(q0 + tq - slab, 0), SUBLANE)
    key = lax.broadcasted_iota(I32, (tq, seq), 1)
    causal = key <= q0 + lax.broadcasted_iota(I32, (tq, seq), 0)
    wp = s0 + lax.broadcasted_iota(I32, (tq, slab), 1)
    qpw = q0 + lax.broadcasted_iota(I32, (tq, slab), 0)
    wb1 = jnp.where((wp <= qpw) & (wp > qpw - WINDOW), 0.0, NEG_INF)

    def add_bias(s, bias):
        return jnp.concatenate([s[r * tq:(r + 1) * tq] + bias for r in range(NSA_REP)], axis=0)

    for g in range(NSA_KVH):
        keep = (_dot(sel[g * tq:(g + 1) * tq].astype(BF16), emat) > 0.5) & causal
        bias = jnp.where(keep, 0.0, NEG_INF)
        for c in range(seq // NSA_KC):
            bias_scr[g, c] = bias[:, c * NSA_KC:(c + 1) * NSA_KC]
    m_scr[...] = jnp.full(m_scr.shape, NEG_INF, F32)
    l_scr[...] = jnp.zeros(l_scr.shape, F32)
    acc_scr[...] = jnp.zeros(acc_scr.shape, F32)

    def body(c, carry):
        k0 = pl.multiple_of(c * NSA_KC, NSA_KC)
        for g in range(NSA_KVH):
            gs = slice(g * NSA_DH, (g + 1) * NSA_DH)
            s = add_bias(_dot_nt(q4s[g], ks_ref[pl.ds(k0, NSA_KC), gs]) * scale, bias_scr[g, c])
            m_old = m_scr[g]
            m_new = jnp.maximum(m_old, _row_reduce(s, jnp.maximum, jnp.max))
            p = jnp.exp(s - m_new)
            alpha = jnp.exp(m_old - m_new)
            l_scr[g] = alpha * l_scr[g] + _row_reduce(p, jnp.add, jnp.sum)
            acc_scr[g] = alpha * acc_scr[g] + _dot(p.astype(BF16), vs_ref[pl.ds(k0, NSA_KC), gs])
            m_scr[g] = m_new
        return carry

    lax.fori_loop(0, nk, body, 0)
    for g in range(NSA_KVH):
        gs = slice(g * NSA_DH, (g + 1) * NSA_DH)
        q4 = q4s[g]
        o_s = acc_scr[g] / l_scr[g]
        s_w = add_bias(_dot_nt(q4, kw_ref[pl.ds(s0, slab), gs]) * scale, wb1)
        p_w = jnp.exp(s_w - _row_reduce(s_w, jnp.maximum, jnp.max))
        o_w = (_dot(p_w.astype(BF16), vw_ref[pl.ds(s0, slab), gs])
               / _row_reduce(p_w, jnp.add, jnp.sum))
        o_c = o_cs[g]
        for r in range(NSA_REP):
            h = g * NSA_REP + r
            rs = slice(r * tq, (r + 1) * tq)
            o = (gates[:, 3 * h:3 * h + 1] * o_c[rs] + gates[:, 3 * h + 1:3 * h + 2] * o_s[rs]
                 + gates[:, 3 * h + 2:3 * h + 3] * o_w[rs])
            o_ref[:, h * NSA_DH:(h + 1) * NSA_DH] = o.astype(BF16)


def _nsa_prompt_call(qn, kc, bs, bw, gates, nb, seq):
    tq = 128
    nq = seq // tq
    slab = min(seq, WINDOW + tq)
    kv = lambda col: pl.BlockSpec((seq, N_KV), lambda b, i: (b, col))
    kern = functools.partial(_nsa_prompt_kernel, tq=tq, seq=seq, slab=slab)
    rows = NSA_REP * tq
    return pl.pallas_call(
        kern,
        grid=(nb, nq),
        in_specs=[pl.BlockSpec((tq, D_C), lambda b, i: (b * nq + i, 0)),
                  pl.BlockSpec((1, seq // CMP_STRIDE, 2 * N_KV), lambda b, i: (b, 0, 0)),
                  kv(0), kv(1), kv(0), kv(1),
                  pl.BlockSpec((tq, LANE), lambda b, i: (b * nq + i, 0))],
        out_specs=pl.BlockSpec((tq, D_C), lambda b, i: (b * nq + i, 0)),
        out_shape=jax.ShapeDtypeStruct((nb * seq, D_C), BF16),
        scratch_shapes=[pltpu.VMEM((NSA_KVH, seq // NSA_KC, tq, NSA_KC), F32),
                        pltpu.VMEM((NSA_KVH, rows, 1), F32), pltpu.VMEM((NSA_KVH, rows, 1), F32),
                        pltpu.VMEM((NSA_KVH, rows, NSA_DH), F32)],
        compiler_params=_cparams(("arbitrary", "arbitrary")),
        name="nsa_prompt",
    )(qn, kc, bs, bs, bw, bw, gates)


def _nsa_s_sel_kernel(q_ref, kc_ref, oc_ref, sel_ref, *, past, dl, n_tiles):
    rows = NSA_REP * dl
    ncp = past // CMP_STRIDE
    n_slc = (past + dl + SEL_BLOCK - 1) // SEL_BLOCK
    nbp = (n_slc + LANE - 1) // LANE * LANE
    per_tile = PG * PAGE // SEL_BLOCK
    gmat = (lax.broadcasted_iota(I32, (ncp, nbp), 0) // (SEL_BLOCK // CMP_STRIDE)
            == lax.broadcasted_iota(I32, (ncp, nbp), 1)).astype(BF16)
    rmat = (lax.broadcasted_iota(I32, (rows, rows), 0) // NSA_REP
            == lax.broadcasted_iota(I32, (rows, rows), 1) // NSA_REP).astype(BF16)

    def qpos_of(shape):
        return past + lax.broadcasted_iota(I32, shape, 0) // NSA_REP

    lane = lax.broadcasted_iota(I32, (rows, LANE), 1)
    imps = []
    for g in range(NSA_KVH):
        gs = slice(g * NSA_DH, (g + 1) * NSA_DH)
        p_c, o_c = _cmp_branch(q_ref[0, g], kc_ref[0, 0, :, gs],
                               kc_ref[0, 0, :, N_KV + g * NSA_DH:N_KV + (g + 1) * NSA_DH],
                               qpos_of((rows, ncp)), ncp - 1)
        oc_ref[0, g] = o_c
        psum = _dot2g(rmat, p_c)
        imps.append(_dot2(psum, gmat))
    qp = past + (lax.broadcasted_iota(I32, (NSA_KVH * rows, nbp), 0) % rows) // NSA_REP
    sel_all = _select_blocks(jnp.concatenate(imps, axis=0), qp, n_slc)
    for g in range(NSA_KVH):
        sel = sel_all[g * rows:(g + 1) * rows]
        for tl in range(n_tiles):
            lo = tl * per_tile
            chunk = sel[:, (lo // LANE) * LANE:(lo // LANE + 1) * LANE]
            sh = (LANE - lo % LANE) % LANE
            if sh:
                chunk = pltpu.roll(chunk, sh, 1)
            sel_ref[0, g, tl] = jnp.where(lane < per_tile, chunk, 0.0)


def _nsa_s_sel_call(q16, kc, layer, past, dl):
    nb = q16.shape[0]
    rows = NSA_REP * dl
    n_tiles = past // (PG * PAGE) + 1
    kern = functools.partial(_nsa_s_sel_kernel, past=past, dl=dl, n_tiles=n_tiles)
    return pl.pallas_call(
        kern,
        grid=(nb,),
        in_specs=[pl.BlockSpec((1, NSA_KVH, rows, NSA_DH), lambda b: (b, 0, 0, 0)),
                  pl.BlockSpec((1, 1, past // CMP_STRIDE, 2 * N_KV), lambda b: (layer, b, 0, 0))],
        out_specs=[pl.BlockSpec((1, NSA_KVH, rows, NSA_DH), lambda b: (b, 0, 0, 0)),
                   pl.BlockSpec((1, NSA_KVH, n_tiles, rows, LANE), lambda b: (b, 0, 0, 0, 0))],
        out_shape=[jax.ShapeDtypeStruct((nb, NSA_KVH, rows, NSA_DH), F32),
                   jax.ShapeDtypeStruct((nb, NSA_KVH, n_tiles, rows, LANE), F32)],
        compiler_params=_cparams(("arbitrary",)),
        name="nsa_sample_select",
    )(q16, kc)


def _nsa_s_attn_kernel(pt_ref, *refs, past, dl, npg):
    pages = refs[:PG]
    (q_ref, selt_ref, sell_ref, emat_ref, oc_ref, ns_ref, nw_ref, wp_ref, g_ref,
     o_ref, m_scr, l_scr, a_scr) = refs[PG:]
    pg = pl.program_id(1)
    rows = NSA_REP * dl
    scale = NSA_DH ** -0.5
    wb = wp_ref.shape[2] // KV_TILES

    def tok_rows(ref, c, n):
        return ref[0, 0, pl.ds(c, n, stride=KV_TILES), :].astype(BF16)

    @pl.when(pg == 0)
    def _():
        m_scr[...] = jnp.full(m_scr.shape, NEG_INF, F32)
        l_scr[...] = jnp.zeros(l_scr.shape, F32)
        a_scr[...] = jnp.zeros(a_scr.shape, F32)

    for g in range(NSA_KVH):
        q = q_ref[0, g]
        kmask = _dot(selt_ref[0, g, 0].astype(BF16), emat_ref[...]) > 0.5
        k_all = jnp.concatenate([tok_rows(pages[k], g, PAGE) for k in range(PG)], axis=0)
        v_all = jnp.concatenate([tok_rows(pages[k], NSA_KVH + g, PAGE) for k in range(PG)], axis=0)
        s = jnp.where(kmask, _dot_nt(q, k_all) * scale, NEG_INF)
        m_old = m_scr[g]
        m_new = jnp.maximum(m_old, jnp.max(s, axis=-1, keepdims=True))
        p = jnp.where(kmask, jnp.exp(s - m_new), 0.0)
        alpha = jnp.exp(m_old - m_new)
        l_scr[g] = alpha * l_scr[g] + jnp.sum(p, axis=-1, keepdims=True)
        a_scr[g] = alpha * a_scr[g] + _dot(p.astype(BF16), v_all)
        m_scr[g] = m_new

    @pl.when(pg == npg - 1)
    def _():
        gates = g_ref[0]
        tok = lax.broadcasted_iota(I32, (rows, SUBLANE), 0) // NSA_REP
        ncol = lax.broadcasted_iota(I32, (rows, SUBLANE), 1)
        for g in range(NSA_KVH):
            gs = slice(g * NSA_DH, (g + 1) * NSA_DH)
            vsl = slice(N_KV + g * NSA_DH, N_KV + (g + 1) * NSA_DH)
            q = q_ref[0, g]
            nmask = (sell_ref[0, g, 0][:, 0:1] > 0.5) & (ncol <= tok) & (ncol < dl)
            s_n = jnp.where(nmask, _dot_nt(q, ns_ref[0, :, gs].astype(BF16)) * scale, NEG_INF)
            m_old = m_scr[g]
            m_new = jnp.maximum(m_old, jnp.max(s_n, axis=-1, keepdims=True))
            p_n = jnp.where(nmask, jnp.exp(s_n - m_new), 0.0)
            alpha = jnp.exp(m_old - m_new)
            lsum = alpha * l_scr[g] + jnp.sum(p_n, axis=-1, keepdims=True)
            acc = alpha * a_scr[g] + _dot(p_n.astype(BF16), ns_ref[0, :, vsl].astype(BF16))
            o_s = acc / jnp.maximum(lsum, 1e-30)
            wrow = lax.broadcasted_iota(I32, (rows, wb), 1)
            wtok = lax.broadcasted_iota(I32, (rows, wb), 0) // NSA_REP
            pmask = (wrow > wtok + (wb - WINDOW)) & (wrow + (past - wb) >= 0)
            s_p = jnp.where(pmask, _dot_nt(q, tok_rows(wp_ref, g, wb)) * scale, NEG_INF)
            wmask = (ncol <= tok) & (ncol < dl)
            s_q = jnp.where(wmask, _dot_nt(q, nw_ref[0, :, gs].astype(BF16)) * scale, NEG_INF)
            mw = jnp.maximum(jnp.max(s_p, axis=-1, keepdims=True), jnp.max(s_q, axis=-1, keepdims=True))
            p_p = jnp.where(pmask, jnp.exp(s_p - mw), 0.0)
            p_q = jnp.where(wmask, jnp.exp(s_q - mw), 0.0)
            den = jnp.maximum(jnp.sum(p_p, axis=-1, keepdims=True) + jnp.sum(p_q, axis=-1, keepdims=True),
                              1e-30)
            o_w = (_dot(p_p.astype(BF16), tok_rows(wp_ref, NSA_KVH + g, wb))
                   + _dot(p_q.astype(BF16), nw_ref[0, :, vsl].astype(BF16))) / den
            gg = gates[g]
            o = gg[:, 0:1] * oc_ref[0, g] + gg[:, 1:2] * o_s + gg[:, 2:3] * o_w
            o_ref[0, g] = o.astype(BF16)


def _nsa_s_attn_call(cache, pt_flat, layer, q16, selt, emat, o_c, new_slc, new_win, win_prev, gates16,
                     past, dl):
    nb = q16.shape[0]
    rows = NSA_REP * dl
    n_pages = past // PAGE
    npg = n_pages // PG
    wrows = win_prev.shape[2]
    b4 = lambda shape: pl.BlockSpec((1,) + shape, lambda b, g, pt: (b,) + (0,) * len(shape))
    grid_spec = pltpu.PrefetchScalarGridSpec(
        num_scalar_prefetch=1,
        grid=(nb, npg),
        in_specs=_page_specs(n_pages, lambda b, g, pt: layer, lambda b, g, pt: b, lambda b, g, pt: g,
                             (PAGE * KV_TILES, LANE))
        + [b4((NSA_KVH, rows, NSA_DH)),
           pl.BlockSpec((1, NSA_KVH, 1, rows, LANE), lambda b, g, pt: (b, 0, g, 0, 0)),
           pl.BlockSpec((1, NSA_KVH, 1, rows, LANE), lambda b, g, pt: (b, 0, npg, 0, 0)),
           pl.BlockSpec((LANE, PG * PAGE), lambda b, g, pt: (0, 0)),
           b4((NSA_KVH, rows, NSA_DH)),
           b4((SUBLANE, 2 * N_KV)), b4((SUBLANE, 2 * N_KV)),
           pl.BlockSpec((1, 1, wrows, LANE), lambda b, g, pt: (layer, b, 0, 0)),
           b4((NSA_KVH, rows, LANE))],
        out_specs=b4((NSA_KVH, rows, NSA_DH)),
        scratch_shapes=[pltpu.VMEM((NSA_KVH, rows, 1), F32), pltpu.VMEM((NSA_KVH, rows, 1), F32),
                        pltpu.VMEM((NSA_KVH, rows, NSA_DH), F32)],
    )
    kern = functools.partial(_nsa_s_attn_kernel, past=past, dl=dl, npg=npg)
    return pl.pallas_call(
        kern,
        grid_spec=grid_spec,
        out_shape=jax.ShapeDtypeStruct((nb, NSA_KVH, rows, NSA_DH), BF16),
        compiler_params=_cparams(("arbitrary", "arbitrary")),
        name="nsa_sample_attn",
    )(pt_flat, *([cache] * PG), q16, selt, selt, emat, o_c, new_slc, new_win, win_prev, gates16)


def _merge_kernel(ya_ref, yb_ref, yc_ref, ga_ref, gb_ref, gc_ref, wa_ref, wb_ref, wc_ref, o_ref):
    m = (jax.nn.sigmoid(ga_ref[...]) * _dot(ya_ref[...], wa_ref[0])
         + jax.nn.sigmoid(gb_ref[...]) * _dot(yb_ref[...], wb_ref[0])
         + jax.nn.sigmoid(gc_ref[...]) * _dot(yc_ref[...], wc_ref[0]))
    o_ref[...] = m.astype(BF16)


def _merge_call(ya, yb, yc, hin, wa, wb, wc, layer):
    m = ya.shape[0]
    tm = min(256, m)
    const = lambda shape: pl.BlockSpec((1,) + shape, lambda i: (layer, 0, 0))
    gate = lambda col: pl.BlockSpec((tm, D_MODEL), lambda i: (i, col))
    return pl.pallas_call(
        _merge_kernel,
        grid=(m // tm,),
        in_specs=[pl.BlockSpec((tm, D_A), lambda i: (i, 0)), pl.BlockSpec((tm, D_B), lambda i: (i, 0)),
                  pl.BlockSpec((tm, D_C), lambda i: (i, 0)), gate(0), gate(1), gate(2),
                  const((D_A, D_MODEL)), const((D_B, D_MODEL)), const((D_C, D_MODEL))],
        out_specs=pl.BlockSpec((tm, D_MODEL), lambda i: (i, 0)),
        out_shape=jax.ShapeDtypeStruct((m, D_MODEL), BF16),
        compiler_params=_cparams(("arbitrary",)),
        name="merge",
    )(ya, yb, yc, hin, hin, hin, wa, wb, wc)


def _outproj_kernel(m_ref, x_ref, g_ref, w_ref, o_ref):
    o_ref[...] = x_ref[...] + g_ref[0] * _dot(m_ref[...], w_ref[0])


def _outproj_call(merged, x, modx, rows_per_group, w, layer):
    m = x.shape[0]
    tm = min(512, m)
    return pl.pallas_call(
        _outproj_kernel,
        grid=(m // tm,),
        in_specs=[pl.BlockSpec((tm, D_MODEL), lambda i: (i, 0)),
                  pl.BlockSpec((tm, D_MODEL), lambda i: (i, 0)),
                  _mod_spec(modx, tm, rows_per_group, 2),
                  pl.BlockSpec((1, D_MODEL, D_MODEL), lambda i: (layer, 0, 0))],
        out_specs=pl.BlockSpec((tm, D_MODEL), lambda i: (i, 0)),
        out_shape=jax.ShapeDtypeStruct((m, D_MODEL), F32),
        compiler_params=_cparams(("arbitrary",)),
        name="outproj",
    )(merged, x, modx, w)


def _ffn_tail(ua, ua1, ua2, ub, cw_ref, cb_ref, wd_ref):
    cw = cw_ref[0]
    conv = cb_ref[0] + cw[0:1] * ua2 + cw[1:2] * ua1 + cw[2:3] * ua
    return _dot((jax.nn.gelu(conv) * ub).astype(BF16), wd_ref[0])


FFN_HALO = 16


def _ffn_prompt_kernel(x_ref, xh_ref, sc_ref, sh_ref, g_ref, nw_ref, cp_ref, wua_ref, wub_ref, cw_ref,
                       cb_ref, wd_ref, o_ref, tail_ref, h_scr, acc, *, tm, seq):
    f = pl.program_id(1)

    @pl.when(f == 0)
    def _():
        nw, sc, sh = nw_ref[...], sc_ref[0], sh_ref[0]
        h_scr[0:FFN_HALO, :] = (_rms(xh_ref[...], nw) * (1.0 + sc) + sh).astype(BF16)
        h_scr[FFN_HALO:, :] = (_rms(x_ref[...], nw) * (1.0 + sc) + sh).astype(BF16)
        acc[...] = jnp.zeros(acc.shape, F32)

    ua_all = _dot(h_scr[...], wua_ref[0])
    ua = ua_all[FFN_HALO:]
    ub = _dot(h_scr[FFN_HALO:, :], wub_ref[0])
    at_start = (pl.program_id(0) * tm) % seq == 0
    prev = jnp.where(at_start, cp_ref[0], ua_all[FFN_HALO - SUBLANE:FFN_HALO])
    row = lax.broadcasted_iota(I32, ua.shape, 0)
    p1, p2 = prev[SUBLANE - 1:SUBLANE], prev[SUBLANE - 2:SUBLANE - 1]
    ua1 = jnp.where(row == 0, p1, pltpu.roll(ua, 1, 0))
    ua2 = jnp.where(row == 0, p2, jnp.where(row == 1, p1, pltpu.roll(ua, 2, 0)))
    tail_ref[0] = ua[tm - SUBLANE:tm]
    acc[...] += _ffn_tail(ua, ua1, ua2, ub, cw_ref, cb_ref, wd_ref)

    @pl.when(f == pl.num_programs(1) - 1)
    def _():
        o_ref[...] = x_ref[...] + g_ref[0] * acc[...]


def _ffn_prompt_call(x, modx, seq, nw, cprev, fw, layer):
    m = x.shape[0]
    tm = min(512, m)
    hb = tm // FFN_HALO
    kern = functools.partial(_ffn_prompt_kernel, tm=tm, seq=seq)
    return pl.pallas_call(
        kern,
        grid=(m // tm, D_FFP // TF),
        in_specs=[pl.BlockSpec((tm, D_MODEL), lambda i, f: (i, 0)),
                  pl.BlockSpec((FFN_HALO, D_MODEL), lambda i, f: (jnp.maximum(i * hb - 1, 0), 0)),
                  _mod_spec(modx, tm, seq, 4), _mod_spec(modx, tm, seq, 3), _mod_spec(modx, tm, seq, 5),
                  pl.BlockSpec((1, D_MODEL), lambda i, f: (0, 0)),
                  pl.BlockSpec((1, SUBLANE, TF), lambda i, f: ((i * tm) // seq, 0, f)),
                  pl.BlockSpec((1, D_MODEL, TF), lambda i, f: (layer, 0, f)),
                  pl.BlockSpec((1, D_MODEL, TF), lambda i, f: (layer, 0, D_FFP // TF + f)),
                  pl.BlockSpec((1, CONV_W, TF), lambda i, f: (layer, 0, f)),
                  pl.BlockSpec((1, 1, TF), lambda i, f: (layer, 0, f)),
                  pl.BlockSpec((1, TF, D_MODEL), lambda i, f: (layer, f, 0))],
        out_specs=[pl.BlockSpec((tm, D_MODEL), lambda i, f: (i, 0)),
                   pl.BlockSpec((1, SUBLANE, TF), lambda i, f: (i, 0, f))],
        out_shape=[jax.ShapeDtypeStruct((m, D_MODEL), F32),
                   jax.ShapeDtypeStruct((m // tm, SUBLANE, D_FFP), F32)],
        scratch_shapes=[pltpu.VMEM((tm + FFN_HALO, D_MODEL), BF16), pltpu.VMEM((tm, D_MODEL), F32)],
        compiler_params=_cparams(("arbitrary", "arbitrary")),
        name="ffn_prompt",
    )(x, x, modx, modx, modx, nw, cprev, fw["wu"], fw["wu"], fw["cw"], fw["cb"], fw["wd"])


def _ffn_sample_kernel(x_ref, sc_ref, sh_ref, g_ref, nw_ref, p1_ref, p2_ref, wua_ref, wub_ref, cw_ref,
                       cb_ref, wd_ref, o_ref, ua_ref, h_scr, acc, *, dl):
    f = pl.program_id(0)

    @pl.when(f == 0)
    def _():
        h_scr[...] = (_rms(x_ref[...], nw_ref[...]) * (1.0 + sc_ref[0]) + sh_ref[0]).astype(BF16)
        acc[...] = jnp.zeros(acc.shape, F32)

    ua = _dot(h_scr[...], wua_ref[0])
    ub = _dot(h_scr[...], wub_ref[0])
    tin = lax.broadcasted_iota(I32, ua.shape, 0) % dl
    ua1 = jnp.where(tin >= 1, pltpu.roll(ua, 1, 0), p1_ref[...])
    ua2 = jnp.where(tin >= 2, pltpu.roll(ua, 2, 0), p2_ref[...])
    ua_ref[...] = ua
    acc[...] += _ffn_tail(ua, ua1, ua2, ub, cw_ref, cb_ref, wd_ref)

    @pl.when(f == pl.num_programs(0) - 1)
    def _():
        o_ref[...] = x_ref[...] + g_ref[0] * acc[...]


def _ffn_sample_call(x, modx, dl, nw, p1, p2, fw, layer):
    m = x.shape[0]
    kern = functools.partial(_ffn_sample_kernel, dl=dl)
    full = lambda i: pl.BlockSpec((m, D_MODEL), lambda f: (0, 0))
    return pl.pallas_call(
        kern,
        grid=(D_FFP // TF,),
        in_specs=[full(0),
                  pl.BlockSpec((1, m, D_MODEL), lambda f: (0, 0, 4)),
                  pl.BlockSpec((1, m, D_MODEL), lambda f: (0, 0, 3)),
                  pl.BlockSpec((1, m, D_MODEL), lambda f: (0, 0, 5)),
                  pl.BlockSpec((1, D_MODEL), lambda f: (0, 0)),
                  pl.BlockSpec((m, TF), lambda f: (0, f)), pl.BlockSpec((m, TF), lambda f: (0, f)),
                  pl.BlockSpec((1, D_MODEL, TF), lambda f: (layer, 0, f)),
                  pl.BlockSpec((1, D_MODEL, TF), lambda f: (layer, 0, D_FFP // TF + f)),
                  pl.BlockSpec((1, CONV_W, TF), lambda f: (layer, 0, f)),
                  pl.BlockSpec((1, 1, TF), lambda f: (layer, 0, f)),
                  pl.BlockSpec((1, TF, D_MODEL), lambda f: (layer, f, 0))],
        out_specs=[full(0), pl.BlockSpec((m, TF), lambda f: (0, f))],
        out_shape=[jax.ShapeDtypeStruct((m, D_MODEL), F32),
                   jax.ShapeDtypeStruct((m, D_FFP), F32)],
        scratch_shapes=[pltpu.VMEM((m, D_MODEL), BF16), pltpu.VMEM((m, D_MODEL), F32)],
        compiler_params=_cparams(("arbitrary",)),
        name="ffn_sample",
    )(x, modx, modx, modx, nw, p1, p2, fw["wu"], fw["wu"], fw["cw"], fw["cb"], fw["wd"])


def _rope_tables(pos):
    half = ROPE_DIM // 2
    inv = jnp.exp(jnp.arange(half, dtype=F32) * (-math.log(ROPE_THETA) / half))
    ang = pos.astype(F32)[:, None] * inv[None, :]
    cos, sin = jnp.cos(ang), jnp.sin(ang)
    n = pos.shape[0]
    rest = NSA_DH - ROPE_DIM
    cos_t = jnp.concatenate([cos, cos, jnp.ones((n, rest), F32)], axis=1)
    sin_a = jnp.concatenate([-sin, jnp.zeros((n, NSA_DH - half), F32)], axis=1)
    sin_b = jnp.concatenate([jnp.zeros((n, half), F32), sin, jnp.zeros((n, rest), F32)], axis=1)
    return cos_t, sin_a, sin_b


_W_IN_SIZES = (D_A, D_B, D_B, D_B, D_B, D_C, 6 * N_KV, 3 * NSA_HEADS, 3 * D_MODEL)
_W_IN_DST = (OFF_UA, OFF_HQ, OFF_HF, OFF_HI, OFF_HGT, OFF_NQ, OFF_NKV, OFF_NGT, OFF_MGT)
N_IN = sum(_W_IN_SIZES)


W_IN_RB = 256


def _w_in_src_row(i):
    src = jnp.int32(0)
    start = 0
    for size, dst in zip(_W_IN_SIZES, _W_IN_DST):
        inside = (i * W_IN_RB >= dst) & (i * W_IN_RB < dst + max(size, W_IN_RB))
        src = jnp.where(inside, start + i * W_IN_RB - dst, src)
        start += size
    return src


def _w_in_prep_kernel(x_ref, o_ref):
    i = pl.program_id(1)
    x = x_ref[0].astype(BF16)
    row = lax.broadcasted_iota(I32, x.shape, 0)
    o_ref[0] = jnp.where((i * W_IN_RB == OFF_NGT) & (row >= 3 * NSA_HEADS), jnp.zeros_like(x), x)


def _permute_w_in(w_in):
    depth, d, _ = w_in.shape
    w_t = jnp.swapaxes(w_in, 1, 2)
    return pl.pallas_call(
        _w_in_prep_kernel,
        grid=(depth, N_INP // W_IN_RB),
        in_specs=[pl.BlockSpec((pl.Element(1), pl.Element(W_IN_RB), pl.Element(d)),
                               lambda l, i: (l, pl.multiple_of(_w_in_src_row(i), SUBLANE), 0))],
        out_specs=pl.BlockSpec((1, W_IN_RB, d), lambda l, i: (l, i, 0)),
        out_shape=jax.ShapeDtypeStruct((depth, N_INP, d), BF16),
        compiler_params=_cparams(("arbitrary", "arbitrary")),
        name="w_in_prep",
    )(w_t)


def _w_up_prep_kernel(x_ref, o_ref):
    o_ref[0, :, :D_FF] = x_ref[0].astype(BF16)
    o_ref[0, :, D_FF:] = jnp.zeros((o_ref.shape[1], D_FFP - D_FF), BF16)


def _pad_w_up(w_up):
    depth, d, _ = w_up.shape
    r = 512
    return pl.pallas_call(
        _w_up_prep_kernel,
        grid=(depth, d // r, 2),
        in_specs=[pl.BlockSpec((1, r, D_FF), lambda l, i, h: (l, i, h))],
        out_specs=pl.BlockSpec((1, r, D_FFP), lambda l, i, h: (l, i, h)),
        out_shape=jax.ShapeDtypeStruct((depth, d, 2 * D_FFP), BF16),
        compiler_params=_cparams(("arbitrary", "arbitrary", "arbitrary")),
        name="w_up_prep",
    )(w_up)


def _s5_params(a_re, a_im, log_dt, b_re, b_im, c_re, c_im, d, w_glu, b_glu, seg_lens):
    dt = jnp.exp(log_dt)[:, None]
    mag = jnp.exp(a_re * dt)
    ab_re, ab_im = mag * jnp.cos(a_im * dt), mag * jnp.sin(a_im * dt)
    den = a_re * a_re + a_im * a_im
    cf_re = ((ab_re - 1.0) * a_re + ab_im * a_im) / den
    cf_im = (ab_im * a_re - (ab_re - 1.0) * a_im) / den
    bb_re = cf_re[..., None] * b_re - cf_im[..., None] * b_im
    bb_im = cf_re[..., None] * b_im + cf_im[..., None] * b_re
    eye = jnp.eye(S5_G, dtype=F32)
    bdiag = lambda bb: jnp.einsum("gnc,gh->gchn", bb, eye).reshape(D_A, S5_W).astype(BF16)
    cdiag = lambda cc: jnp.einsum("gcn,gh->gnhc", cc, eye).reshape(S5_W, D_A).astype(BF16)
    cmul = lambda a, b: (a[0] * b[0] - a[1] * b[1], a[0] * b[1] + a[1] * b[0])
    abar = (ab_re.reshape(1, S5_W), ab_im.reshape(1, S5_W))
    pw = [abar]
    for _ in range(max(seg_lens) - 1):
        pw.append(cmul(pw[-1], abar))

    def tables(ts):
        rep = lambda k: jnp.concatenate([jnp.broadcast_to(p[k], (SUBLANE, S5_W)) for p in pw[:ts]], axis=0)
        g1 = pw[ts - 1]
        g2 = cmul(g1, g1)
        g4 = cmul(g2, g2)
        pad = jnp.zeros((SUBLANE - 3, S5_W), F32)
        gp = lambda k: jnp.concatenate([g1[k], g2[k], g4[k], pad], axis=0)
        return rep(0), rep(1), gp(0), gp(1)

    return dict(bbr=bdiag(bb_re), bbi=bdiag(bb_im), ccr=cdiag(c_re), cci=cdiag(c_im),
                pow={ts: tables(ts) for ts in seg_lens},
                d=d.reshape(1, D_A), wg=w_glu.astype(BF16), bg=b_glu.reshape(1, D_A))


def _hgrn_params(lb):
    pos = lb > 0
    lb_safe = jnp.where(pos, lb, 1.0)
    z = jnp.zeros_like(lb)
    return jnp.stack([jnp.log(lb_safe), jnp.log1p(-lb), 1.0 - lb, pos.astype(F32), z, z, z, z], axis=0)


def _cmp_params(cmp_a, w1, b1, w2):
    def lanes(a):
        return jnp.concatenate([a[0], a[0], a[1], a[1]], axis=-1)
    return dict(a1=lanes(cmp_a[:, :CMP_STRIDE]), a2=lanes(cmp_a[:, CMP_STRIDE:]),
                w1=w1.astype(BF16), b1=b1, w2=w2.astype(BF16))


def _ffn_params(w_up, conv_w, conv_b, w_down):
    padc = lambda a: jnp.pad(a, [(0, 0)] * (a.ndim - 1) + [(0, D_FFP - D_FF)])
    wu = _pad_w_up(w_up)
    wd = jnp.pad(w_down.astype(BF16), ((0, 0), (0, D_FFP - D_FF), (0, 0)))
    return dict(wu=wu, cw=padc(conv_w), cb=padc(conv_b)[:, None, :], wd=wd)


def kernel(x_prompt, x_sample, cache_cmp, cache_slc, state_win, state_s5, state_hgrn, state_conv,
           page_table, c_prompt, c_sample, w_mod, b_mod, norm1_w, norm2_w, w_in,
           s5_a_re, s5_a_im, s5_log_dt, s5_b_re, s5_b_im, s5_c_re, s5_c_im, s5_d, s5_w_glu, s5_b_glu,
           hg_lb_logits, hg_norm_w, nsa_q_norm, nsa_k_norm, cmp_a, cmp_w1, cmp_b1, cmp_w2,
           w_branch_a, w_branch_b, w_branch_c, w_out, w_up, conv_w, conv_b, w_down):
    bsz, seq, _ = x_prompt.shape
    dbsz, dl, _ = x_sample.shape
    depth = w_in.shape[0]
    n_pages = page_table.shape[1]
    past = n_pages * PAGE
    wb = state_win.shape[2]
    mp, ms = bsz * seq, dbsz * dl
    rows = NSA_REP * dl
    assert seq % 512 == 0 and seq // SEL_BLOCK <= LANE and n_pages % PG == 0 and dl <= SUBLANE

    probs = jax.nn.softmax(hg_lb_logits.astype(F32), axis=0)
    lower_bounds = jnp.cumsum(probs, axis=0) - probs[0:1]
    w_in_p = _permute_w_in(w_in)
    wa, wbr, wc, wo = (w.astype(BF16) for w in (w_branch_a, w_branch_b, w_branch_c, w_out))
    fw = _ffn_params(w_up, conv_w, conv_b, w_down)
    rope_p = _rope_tables(jnp.arange(seq, dtype=I32))
    rope_s = tuple(jnp.tile(t, (dbsz, 1)) for t in _rope_tables(past + jnp.arange(dl, dtype=I32)))
    pt_flat = page_table.reshape(-1).astype(I32)
    cache_cmp4 = cache_cmp.reshape(depth, -1, PAGE * KV_TILES, LANE)
    cache_slc4 = cache_slc.reshape(depth, -1, PAGE * KV_TILES, LANE)
    win_prev4 = state_win.reshape(depth, dbsz, wb * KV_TILES, LANE)
    emat = (jnp.arange(PG * PAGE)[None, :] // SEL_BLOCK == jnp.arange(LANE)[:, None]).astype(BF16)

    nr = -(-(bsz + dbsz) // 16) * 16
    c_all = jnp.concatenate([c_prompt, c_sample, jnp.zeros((nr - bsz - dbsz, D_MODEL), F32)], axis=0)
    mod = _mod_call(c_all, w_mod, b_mod)

    cmp_lanes = lambda a: jnp.concatenate([a[:, 0], a[:, 0], a[:, 1], a[:, 1]], axis=-1).reshape(
        depth, SUB_ROWS, LANE)
    first, second = _cmp_pool_cache_call(cache_cmp4, pt_flat, dbsz, n_pages,
                                         cmp_lanes(cmp_a[:, :, :CMP_STRIDE]),
                                         cmp_lanes(cmp_a[:, :, CMP_STRIDE:]))
    kc_past = _cmp_mlp_cache_call(first, second, cmp_w1.astype(BF16), cmp_b1, cmp_w2.astype(BF16))

    xp = x_prompt.reshape(mp, D_MODEL)
    xs = x_sample.reshape(ms, D_MODEL)
    outs = [[] for _ in range(12)]
    row_bufs = None
    for l in range(depth):
        s5p = _s5_params(s5_a_re[l], s5_a_im[l], s5_log_dt[l], s5_b_re[l], s5_b_im[l], s5_c_re[l],
                         s5_c_im[l], s5_d[l], s5_w_glu[l], s5_b_glu[l],
                         (min(256, seq) // SUBLANE, 1))
        hgp = _hgrn_params(lower_bounds[l])
        hgn = hg_norm_w[l].reshape(1, HG_DV)
        cp = _cmp_params(cmp_a[l], cmp_w1[l], cmp_b1[l], cmp_w2[l])
        n1, n2 = norm1_w[l].reshape(1, D_MODEL), norm2_w[l].reshape(1, D_MODEL)
        qn, kn = nsa_q_norm[l].reshape(1, NSA_DH), nsa_k_norm[l]
        modp = mod[l, :bsz].reshape(bsz, 1, 6 * D_MODEL)
        mods = jnp.repeat(mod[l, bsz:bsz + dbsz], dl, axis=0).reshape(1, ms, 6 * D_MODEL)

        hin = _inproj_call(xp, modp, seq, n1, w_in_p, l)
        ya, s5_new = _s5_call(hin, OFF_UA // D_A, bsz, seq, jnp.zeros((bsz, 2, S5_W), F32), s5p,
                              SUBLANE - 1)
        yb, hg_new = _hgrn_call(hin, tuple(o // D_B for o in (OFF_HQ, OFF_HF, OFF_HI, OFF_HGT)), bsz, seq,
                                seq, jnp.zeros((bsz, HG_HEADS, HG_DK, HG_DV), F32), hgp, hgn)
        qo, *row_bufs, bc, bs, bw, gates = _nsa_prep_call(hin, rope_p, seq, qn, kn, l, depth, row_bufs)
        kc = _cmp_prompt_call(row_bufs[0], bsz, seq, cp, l)
        yc = _nsa_prompt_call(qo, kc, bs, bw, gates, bsz, seq)
        merged = _merge_call(ya, yb, yc, hin, wa, wbr, wc, l)
        xp = _outproj_call(merged, xp, modp, seq, wo, l)
        xp, tails = _ffn_prompt_call(xp, modp, seq, n2, jnp.zeros((bsz, SUBLANE, D_FFP), F32), fw, l)
        conv_new = tails.reshape(bsz, -1, SUBLANE, D_FFP)[:, -1, SUBLANE - (CONV_W - 1):, :D_FF]
        for i, o in ((0, s5_new.reshape(bsz, 2, S5_G, S5_N)), (1, hg_new), (5, conv_new)):
            outs[2 * i].append(o)

        hin = _inproj_call(xs, mods, dl, n1, w_in_p, l)
        padt = lambda a, n: jnp.pad(a.reshape(dbsz, dl, -1), ((0, 0), (0, n - dl), (0, 0))).reshape(dbsz * n, -1)
        u_pad = padt(hin[:, OFF_UA:OFF_UA + D_A], SUBLANE)
        ya, s5_new = _s5_call(u_pad, 0, dbsz, SUBLANE, state_s5[l].reshape(dbsz, 2, S5_W), s5p, dl - 1)
        ya = ya.reshape(dbsz, SUBLANE, D_A)[:, :dl].reshape(ms, D_A)
        hg_pad = padt(hin[:, OFF_HQ:OFF_HQ + 4 * D_B], HG_C)
        yb, hg_new = _hgrn_call(hg_pad, (0, 1, 2, 3), dbsz, HG_C, dl, state_hgrn[l], hgp, hgn)
        yb = yb.reshape(dbsz, HG_C, D_B)[:, :dl].reshape(ms, D_B)
        qo, rc, rs, rw, bc, bs, bw, gates = _nsa_prep_call(hin, rope_s, dl, qn, kn)
        q16 = qo.reshape(dbsz, dl, NSA_KVH, NSA_REP, NSA_DH).transpose(0, 2, 1, 3, 4).reshape(
            dbsz, NSA_KVH, rows, NSA_DH)
        o_c, selt = _nsa_s_sel_call(q16, kc_past, l, past, dl)
        g16 = gates[:, :3 * NSA_HEADS].reshape(dbsz, dl, NSA_KVH, NSA_REP, 3).transpose(0, 2, 1, 3, 4)
        g16 = jnp.pad(g16.reshape(dbsz, NSA_KVH, rows, 3), ((0, 0), (0, 0), (0, 0), (0, LANE - 3)))
        pad8 = lambda r: jnp.pad(r.reshape(dbsz, dl, 2 * N_KV), ((0, 0), (0, SUBLANE - dl), (0, 0)))
        yc = _nsa_s_attn_call(cache_slc4, pt_flat, l, q16, selt, emat, o_c, pad8(rs), pad8(rw), win_prev4,
                              g16, past, dl)
        yc = yc.reshape(dbsz, NSA_KVH, dl, NSA_REP, NSA_DH).transpose(0, 2, 1, 3, 4).reshape(ms, D_C)
        merged = _merge_call(ya, yb, yc, hin, wa, wbr, wc, l)
        xs = _outproj_call(merged, xs, mods, dl, wo, l)
        st = jnp.pad(state_conv[l], ((0, 0), (0, 0), (0, D_FFP - D_FF)))
        tin = jnp.arange(dl)[None, :, None]
        p1 = jnp.where(tin == 0, st[:, 1:2], 0.0).reshape(ms, D_FFP)
        p2 = jnp.where(tin == 0, st[:, 0:1], jnp.where(tin == 1, st[:, 1:2], 0.0)).reshape(ms, D_FFP)
        xs, ua = _ffn_sample_call(xs, mods, dl, n2, p1, p2, fw, l)
        ext = jnp.concatenate([state_conv[l], ua[:, :D_FF].reshape(dbsz, dl, D_FF)], axis=1)
        kv5 = lambda r: r.reshape(dbsz, dl, 2, NSA_KVH, NSA_DH)
        win_new = jnp.concatenate([state_win[l], kv5(rw)], axis=1)[:, -wb:]
        for i, o in enumerate((s5_new.reshape(dbsz, 2, S5_G, S5_N), hg_new, kv5(rc), kv5(rs), win_new,
                               ext[:, dl:])):
            outs[2 * i + 1].append(o)

    st = [jnp.stack(o) if o else None for o in outs]
    rows6 = [r.reshape(depth, bsz, seq, 2, NSA_KVH, NSA_DH) for r in row_bufs]
    st[4], st[6], st[8] = rows6[0], rows6[1], rows6[2][:, :, -min(WINDOW, seq):]
    return (xp.reshape(bsz, seq, D_MODEL), xs.reshape(dbsz, dl, D_MODEL), *st)
```

```python
import functools
import math

import jax
import jax.numpy as jnp
from jax import lax
from jax.experimental import pallas as pl
from jax.experimental.pallas import tpu as pltpu

F32 = jnp.float32
BF16 = jnp.bfloat16
I32 = jnp.int32

D_MODEL = 2048
D_A = D_MODEL // 4
S5_GROUP = 16
S5_G = D_A // S5_GROUP
S5_N = 64
S5_W = S5_G * S5_N
D_B = D_MODEL // 4
HG_DK = 128
HG_DV = 128
HG_HEADS = D_B // HG_DV
D_C = D_MODEL // 2
NSA_DH = 128
NSA_HEADS = D_C // NSA_DH
NSA_KVH = 2
NSA_REP = NSA_HEADS // NSA_KVH
N_KV = NSA_KVH * NSA_DH
ROPE_DIM = NSA_DH // 4
ROPE_THETA = 500000.0
CMP_STRIDE = 16
SEL_BLOCK = 64
N_SEL = 16
WINDOW = 512
FORCE_BONUS = 1000.0
D_FF = ((8 * D_MODEL) // 3 + 127) // 128 * 128
CONV_W = 3
EPS = 1e-6
NEG_INF = -1e30
PAGE = 128

LANE = 128
SUBLANE = 8
VMEM_LIMIT = 56 * 1024 * 1024

OFF_MGT = 0
OFF_NQ = 3 * D_MODEL
OFF_NKV = OFF_NQ + D_C
OFF_UA = OFF_NKV + 6 * N_KV
OFF_HQ = OFF_UA + D_A
OFF_HF = OFF_HQ + D_B
OFF_HI = OFF_HF + D_B
OFF_HGT = OFF_HI + D_B
OFF_NGT = OFF_HGT + D_B
NGT_PAD = 256
N_INP = OFF_NGT + NGT_PAD
TN_IN = 1280
D_FFP = 5632
TF = 512
HG_C = 64
HG_SB = 8


def _cparams(sem):
    return pltpu.CompilerParams(dimension_semantics=sem, vmem_limit_bytes=VMEM_LIMIT)


def _dot(a, b):
    return jnp.dot(a, b, preferred_element_type=F32)


def _dot_nt(a, b):
    return lax.dot_general(a, b, (((1,), (1,)), ((), ())), preferred_element_type=F32)


def _dot_tn(a, b):
    return lax.dot_general(a, b, (((0,), (0,)), ((), ())), preferred_element_type=F32)


def _dot2(a, b):
    hi = a.astype(BF16)
    lo = (a - hi.astype(F32)).astype(BF16)
    return _dot(hi, b) + _dot(lo, b)


def _dot3(a, b):
    hi = a.astype(BF16)
    r1 = a - hi.astype(F32)
    mid = r1.astype(BF16)
    lo = (r1 - mid.astype(F32)).astype(BF16)
    return _dot(b, hi) + _dot(b, mid) + _dot(b, lo)


def _rms(x, w):
    return x * lax.rsqrt(jnp.mean(x * x, axis=-1, keepdims=True) + EPS) * w


def _masked_softmax(s, mask):
    s = jnp.where(mask, s, NEG_INF)
    p = jnp.where(mask, jnp.exp(s - jnp.max(s, axis=-1, keepdims=True)), 0.0)
    return p / jnp.maximum(jnp.sum(p, axis=-1, keepdims=True), 1e-30)


def _row_reduce(x, combine, reduce):
    n = x.shape[-1]
    if n % LANE or n == LANE:
        return reduce(x, axis=-1, keepdims=True)
    acc = x[:, :LANE]
    for k in range(1, n // LANE):
        acc = combine(acc, x[:, k * LANE:(k + 1) * LANE])
    return reduce(acc, axis=-1, keepdims=True)


def _log_sigmoid(x):
    return jnp.minimum(x, 0.0) - jnp.log1p(jnp.exp(-jnp.abs(x)))


def _logaddexp(a, b):
    return jnp.maximum(a, b) + jnp.log1p(jnp.exp(-jnp.abs(a - b)))


def _mod_kernel(c_ref, w_ref, b_ref, o_ref):
    c = c_ref[...]
    a = (c * jax.nn.sigmoid(c)).astype(BF16)
    o_ref[0] = _dot(a, w_ref[0].astype(BF16)) + b_ref[0]


def _mod_call(c_all, w_mod, b_mod):
    depth, d, n = w_mod.shape
    r = c_all.shape[0]
    tn = 1024
    return pl.pallas_call(
        _mod_kernel,
        grid=(depth, n // tn),
        in_specs=[pl.BlockSpec((r, d), lambda l, j: (0, 0)),
                  pl.BlockSpec((1, d, tn), lambda l, j: (l, 0, j)),
                  pl.BlockSpec((1, 1, tn), lambda l, j: (l, 0, j))],
        out_specs=pl.BlockSpec((1, r, tn), lambda l, j: (l, 0, j)),
        out_shape=jax.ShapeDtypeStruct((depth, r, n), F32),
        compiler_params=_cparams(("arbitrary", "arbitrary")),
        name="mod",
    )(c_all, w_mod, b_mod.reshape(depth, 1, n))


def _inproj_kernel(x_ref, sc_ref, sh_ref, nw_ref, w_ref, o_ref, h_scr):
    @pl.when(pl.program_id(1) == 0)
    def _():
        y = _rms(x_ref[...], nw_ref[...])
        h_scr[...] = (y * (1.0 + sc_ref[0]) + sh_ref[0]).astype(BF16)

    o_ref[...] = _dot_nt(h_scr[...], w_ref[0])


def _mod_spec(modx, tm, rows_per_group, col):
    if modx.shape[1] == 1:
        return pl.BlockSpec((1, 1, D_MODEL), lambda i, *_: ((i * tm) // rows_per_group, 0, col))
    return pl.BlockSpec((1, tm, D_MODEL), lambda i, *_: (0, i, col))


def _inproj_call(x, modx, rows_per_group, nw, w, layer):
    m = x.shape[0]
    tm = min(1024, m)
    return pl.pallas_call(
        _inproj_kernel,
        grid=(m // tm, N_INP // TN_IN),
        in_specs=[pl.BlockSpec((tm, D_MODEL), lambda i, j: (i, 0)),
                  _mod_spec(modx, tm, rows_per_group, 1),
                  _mod_spec(modx, tm, rows_per_group, 0),
                  pl.BlockSpec((1, D_MODEL), lambda i, j: (0, 0)),
                  pl.BlockSpec((1, TN_IN, D_MODEL), lambda i, j: (layer, j, 0))],
        out_specs=pl.BlockSpec((tm, TN_IN), lambda i, j: (i, j)),
        out_shape=jax.ShapeDtypeStruct((m, N_INP), F32),
        scratch_shapes=[pltpu.VMEM((tm, D_MODEL), BF16)],
        compiler_params=_cparams(("arbitrary", "arbitrary")),
        name="inproj",
    )(x, modx, modx, nw, w)


S5_LC = 512


def _s5_kernel(u_ref, h0_ref, bbr_ref, bbi_ref, ccr_ref, cci_ref, apr_ref, api_ref, gpr_ref, gpi_ref,
               d_ref, wg_ref, bg_ref, y_ref, hl_ref, hre, him, car, *, t_rows, last_row):
    ts = t_rows // SUBLANE

    @pl.when(pl.program_id(1) == 0)
    def _():
        car[...] = h0_ref[0]

    ri = lax.broadcasted_iota(I32, (t_rows, t_rows), 0)
    ci = lax.broadcasted_iota(I32, (t_rows, t_rows), 1)
    perm = (ci == (ri % SUBLANE) * ts + ri // SUBLANE).astype(BF16)
    unperm = (ci == (ri % ts) * SUBLANE + ri // ts).astype(BF16)
    u = _dot3(u_ref[...], perm)
    ub = u.astype(BF16)
    nblk = D_A // LANE
    for j in range(nblk):
        us, ss = slice(j * LANE, (j + 1) * LANE), slice(j * S5_LC, (j + 1) * S5_LC)
        hre[:, ss] = _dot(ub[:, us], bbr_ref[us, ss])
        him[:, ss] = _dot(ub[:, us], bbi_ref[us, ss])
    row = lax.broadcasted_iota(I32, (SUBLANE, S5_LC), 0)
    zero = jnp.zeros((SUBLANE, S5_LC), F32)
    for c in range(S5_W // S5_LC):
        sl = slice(c * S5_LC, (c + 1) * S5_LC)
        ar, ai = apr_ref[0:SUBLANE, sl], api_ref[0:SUBLANE, sl]

        def local(t, carry, sl=sl, ar=ar, ai=ai):
            hr, hi = carry
            r0 = pl.multiple_of(t * SUBLANE, SUBLANE)
            nr = ar * hr - ai * hi + hre[pl.ds(r0, SUBLANE), sl]
            ni = ar * hi + ai * hr + him[pl.ds(r0, SUBLANE), sl]
            hre[pl.ds(r0, SUBLANE), sl] = nr
            him[pl.ds(r0, SUBLANE), sl] = ni
            return nr, ni

        er, ei = lax.fori_loop(0, ts, local, (zero, zero))
        yr = jnp.where(row == 0, car[0:1, sl], pltpu.roll(er, 1, 0))
        yi = jnp.where(row == 0, car[1:2, sl], pltpu.roll(ei, 1, 0))
        for k, s in enumerate((1, 2, 4)):
            gr, gi = gpr_ref[k:k + 1, sl], gpi_ref[k:k + 1, sl]
            sr = jnp.where(row >= s, pltpu.roll(yr, s, 0), 0.0)
            si = jnp.where(row >= s, pltpu.roll(yi, s, 0), 0.0)
            yr, yi = yr + (gr * sr - gi * si), yi + (gr * si + gi * sr)
        g1r, g1i = gpr_ref[0:1, sl], gpi_ref[0:1, sl]
        car[0:1, sl] = (g1r * yr - g1i * yi + er)[SUBLANE - 1:SUBLANE]
        car[1:2, sl] = (g1r * yi + g1i * yr + ei)[SUBLANE - 1:SUBLANE]

        def fix(t, carry, sl=sl, yr=yr, yi=yi):
            r0 = pl.multiple_of(t * SUBLANE, SUBLANE)
            pr, pi = apr_ref[pl.ds(r0, SUBLANE), sl], api_ref[pl.ds(r0, SUBLANE), sl]
            hre[pl.ds(r0, SUBLANE), sl] += pr * yr - pi * yi
            him[pl.ds(r0, SUBLANE), sl] += pr * yi + pi * yr
            return carry

        lax.fori_loop(0, ts, fix, 0)

    lr = t_rows - SUBLANE + last_row
    hl_ref[0, 0:1, :] = hre[lr:lr + 1, :]
    hl_ref[0, 1:2, :] = him[lr:lr + 1, :]
    y = jnp.concatenate(
        [_dot(hre[:, j * S5_LC:(j + 1) * S5_LC].astype(BF16),
              ccr_ref[j * S5_LC:(j + 1) * S5_LC, j * LANE:(j + 1) * LANE])
         - _dot(him[:, j * S5_LC:(j + 1) * S5_LC].astype(BF16),
                cci_ref[j * S5_LC:(j + 1) * S5_LC, j * LANE:(j + 1) * LANE])
         for j in range(nblk)], axis=1) + d_ref[...] * u
    y = jax.nn.gelu(y)
    z = _dot(y.astype(BF16), wg_ref[...]) + bg_ref[...]
    y_ref[...] = _dot(unperm, (y * jax.nn.sigmoid(z)).astype(BF16)).astype(BF16)


def _s5_call(u_arr, ucol, nb, seq, h0, p, last_row):
    t_rows = min(256, seq)
    nt = seq // t_rows
    const = lambda shape: pl.BlockSpec(shape, lambda b, t: (0,) * len(shape))
    kern = functools.partial(_s5_kernel, t_rows=t_rows, last_row=last_row)
    return pl.pallas_call(
        kern,
        grid=(nb, nt),
        in_specs=[pl.BlockSpec((t_rows, D_A), lambda b, t: (b * nt + t, ucol)),
                  pl.BlockSpec((1, 2, S5_W), lambda b, t: (b, 0, 0)),
                  const((D_A, S5_W)), const((D_A, S5_W)), const((S5_W, D_A)), const((S5_W, D_A)),
                  const((t_rows, S5_W)), const((t_rows, S5_W)),
                  const((SUBLANE, S5_W)), const((SUBLANE, S5_W)), const((1, D_A)),
                  const((D_A, D_A)), const((1, D_A))],
        out_specs=[pl.BlockSpec((t_rows, D_A), lambda b, t: (b * nt + t, 0)),
                   pl.BlockSpec((1, 2, S5_W), lambda b, t: (b, 0, 0))],
        out_shape=[jax.ShapeDtypeStruct((nb * seq, D_A), BF16),
                   jax.ShapeDtypeStruct((nb, 2, S5_W), F32)],
        scratch_shapes=[pltpu.VMEM((t_rows, S5_W), F32), pltpu.VMEM((t_rows, S5_W), F32),
                        pltpu.VMEM((2, S5_W), F32)],
        compiler_params=_cparams(("arbitrary", "arbitrary")),
        name="s5",
    )(u_arr, h0, p["bbr"], p["bbi"], p["ccr"], p["cci"], *p["pow"][t_rows // SUBLANE], p["d"],
      p["wg"], p["bg"])


def _hgrn_kernel(q_ref, f_ref, i_ref, g_ref, s0_ref, lp_ref, nw_ref, y_ref, sl_ref, st, pscr, rscr,
                 *, l_valid, chunks):
    t = pl.program_id(1)
    c = HG_C
    nsb = c // HG_SB
    nh = HG_HEADS

    @pl.when(t == 0)
    def _():
        for h in range(nh):
            st[h] = s0_ref[0, h].T

    row = lax.broadcasted_iota(I32, (c, D_B), 0)
    tri = (lax.broadcasted_iota(I32, (c, c), 0) >= lax.broadcasted_iota(I32, (c, c), 1)).astype(BF16)
    srow = lax.broadcasted_iota(I32, (HG_SB, D_B), 0)
    rows16 = lambda r: jnp.broadcast_to(r, (HG_SB, D_B))

    def chunk(ci, carry):
        c0 = pl.multiple_of(ci * c, c)
        crows = pl.ds(c0, c)
        valid = (t * (chunks * c) + c0 + row) < l_valid
        hf = f_ref[crows, :]
        ls = _log_sigmoid(hf)
        lf = jnp.where(lp_ref[3:4, :] > 0.5, _logaddexp(lp_ref[0:1, :], lp_ref[1:2, :] + ls), ls)
        kk = lp_ref[2:3, :] * jax.nn.sigmoid(-hf)
        lf = jnp.where(valid, lf, 0.0)
        kk = jnp.where(valid, kk, 0.0)
        hq = q_ref[crows, :]
        q = hq * jax.nn.sigmoid(hq)
        v = i_ref[crows, :]
        b = _dot3(lf, tri)
        zero_row = jnp.zeros((1, D_B), F32)
        bref = [zero_row] + [b[i * HG_SB - 1:i * HG_SB] for i in range(1, nsb)]
        bend = [b[(j + 1) * HG_SB - 1:(j + 1) * HG_SB] for j in range(nsb)]
        qq = q * jnp.exp(b - jnp.concatenate([rows16(r) for r in bref], axis=0))
        kks = kk * jnp.exp(jnp.concatenate([rows16(r) for r in bend], axis=0) - b)
        lhs = []
        for j in range(nsb - 1):
            dsel = jnp.concatenate(
                [rows16(jnp.exp(bref[i] - bend[j])) if i > j else jnp.zeros((HG_SB, D_B), F32)
                 for i in range(nsb)], axis=0)
            lhs.append((qq * dsel).astype(BF16))
        rhs = [jnp.where(row // HG_SB == j, kks, 0.0).astype(BF16) for j in range(nsb - 1)]
        for i in range(nsb):
            rs = slice(i * HG_SB, (i + 1) * HG_SB)
            bi, qi, ki = b[rs], q[rs], kk[rs]
            for s0 in range(0, HG_SB, 2):
                pair = []
                for s in (s0, s0 + 1):
                    e = jnp.exp(jnp.where(srow >= s, bi - bi[s:s + 1], NEG_INF))
                    pair.append((qi * ki[s:s + 1]) * e)
                piece = jnp.concatenate(pair, axis=0).astype(BF16)
                for h in range(nh):
                    r0 = ((h * nsb + i) * HG_SB + s0) * HG_SB
                    pscr[r0:r0 + 2 * HG_SB, :] = piece[:, h * HG_DK:(h + 1) * HG_DK]
        rscr[...] = _dot(pscr[...], jnp.ones((HG_DK, HG_DK), BF16))
        qe = (q * jnp.exp(b)).astype(BF16)
        vb = v.astype(BF16)
        bl = b[c - 1:c]
        kdec = (kk * jnp.exp(bl - b)).astype(BF16)
        ebl = jnp.exp(bl)
        gt = g_ref[crows, :]
        gate = gt * jax.nn.sigmoid(gt)
        for h in range(nh):
            hs = slice(h * HG_DK, (h + 1) * HG_DK)
            s_t = st[h]
            att = _dot_nt(jnp.concatenate([x[:, hs] for x in lhs], axis=1),
                          jnp.concatenate([x[:, hs] for x in rhs], axis=1))
            o = _dot(att.astype(BF16), vb[:, hs]) + _dot_nt(qe[:, hs], s_t.astype(BF16))
            diag = []
            for i in range(nsb):
                od = jnp.zeros((HG_SB, HG_DV), F32)
                for s in range(HG_SB):
                    r0 = ((h * nsb + i) * HG_SB + s) * HG_SB
                    od = od + rscr[r0:r0 + HG_SB, :] * v[i * HG_SB + s:i * HG_SB + s + 1, hs]
                diag.append(od)
            o = o + jnp.concatenate(diag, axis=0)
            st[h] = s_t * ebl[:, hs] + _dot_tn(vb[:, hs], kdec[:, hs])
            y_ref[crows, hs] = (_rms(o, nw_ref[...]) * gate[:, hs]).astype(BF16)
        return carry

    lax.fori_loop(0, chunks, chunk, 0)

    @pl.when(t == pl.num_programs(1) - 1)
    def _():
        for h in range(nh):
            sl_ref[0, h] = st[h].T


def _dot2g(g, z):
    hi = z.astype(BF16)
    lo = (z - hi.astype(F32)).astype(BF16)
    return _dot(g, hi) + _dot(g, lo)


def _hgrn_call(arr, cols, nb, seq_pad, l_valid, s0, lp, nw):
    chunks = 4 if seq_pad % (4 * HG_C) == 0 else 1
    rows = chunks * HG_C
    nt = seq_pad // rows
    blk = lambda col: pl.BlockSpec((rows, D_B), lambda b, t: (b * nt + t, col))
    kern = functools.partial(_hgrn_kernel, l_valid=l_valid, chunks=chunks)
    return pl.pallas_call(
        kern,
        grid=(nb, nt),
        in_specs=[blk(cols[0]), blk(cols[1]), blk(cols[2]), blk(cols[3]),
                  pl.BlockSpec((1, HG_HEADS, HG_DK, HG_DV), lambda b, t: (b, 0, 0, 0)),
                  pl.BlockSpec((SUBLANE, D_B), lambda b, t: (0, 0)),
                  pl.BlockSpec((1, HG_DV), lambda b, t: (0, 0))],
        out_specs=[pl.BlockSpec((rows, D_B), lambda b, t: (b * nt + t, 0)),
                   pl.BlockSpec((1, HG_HEADS, HG_DK, HG_DV), lambda b, t: (b, 0, 0, 0))],
        out_shape=[jax.ShapeDtypeStruct((nb * seq_pad, D_B), BF16),
                   jax.ShapeDtypeStruct((nb, HG_HEADS, HG_DK, HG_DV), F32)],
        scratch_shapes=[pltpu.VMEM((HG_HEADS, HG_DV, HG_DK), F32),
                        pltpu.VMEM((HG_HEADS * HG_C * HG_SB, HG_DK), BF16),
                        pltpu.VMEM((HG_HEADS * HG_C * HG_SB, HG_DK), F32)],
        compiler_params=_cparams(("arbitrary", "arbitrary")),
        name="hgrn",
    )(arr, arr, arr, arr, s0, lp, nw)


def _rope(x, cos_t, sin_a, sin_b):
    return (x * cos_t + pltpu.roll(x, NSA_DH - ROPE_DIM // 2, 1) * sin_a
            + pltpu.roll(x, ROPE_DIM // 2, 1) * sin_b)


def _nsa_prep_kernel(q_ref, kc_ref, ks_ref, kw_ref, g_ref, cos_ref, sa_ref, sb_ref, qn_ref, kn_ref,
                     qo_ref, rc_ref, rs_ref, rw_ref, bc_ref, bs_ref, bw_ref, go_ref):
    cos_t, sin_a, sin_b = cos_ref[...], sa_ref[...], sb_ref[...]
    qn = qn_ref[...]
    for h in range(NSA_HEADS):
        hs = slice(h * NSA_DH, (h + 1) * NSA_DH)
        qo_ref[:, hs] = _rope(_rms(q_ref[:, hs], qn), cos_t, sin_a, sin_b).astype(BF16)
    for br, (src, dst, dstb) in enumerate(((kc_ref, rc_ref, bc_ref), (ks_ref, rs_ref, bs_ref),
                                           (kw_ref, rw_ref, bw_ref))):
        kn = kn_ref[br:br + 1, :]
        tm = src.shape[0]
        for g in range(NSA_KVH):
            gs = slice(g * NSA_DH, (g + 1) * NSA_DH)
            k = _rope(_rms(src[:, gs], kn), cos_t, sin_a, sin_b)
            dst[pl.ds(g, tm, stride=KV_TILES), :] = k
            dstb[:, gs] = k.astype(BF16)
            vs = slice(N_KV + g * NSA_DH, N_KV + (g + 1) * NSA_DH)
            v = src[:, vs]
            dst[pl.ds(NSA_KVH + g, tm, stride=KV_TILES), :] = v
            dstb[:, vs] = v.astype(BF16)
    go_ref[...] = jax.nn.sigmoid(g_ref[...])


def _nsa_prep_call(hin, rope_tabs, seq, qn, kn, layer=0, depth=1, row_bufs=None):
    m = hin.shape[0]
    tm = min(256, m)
    npos = rope_tabs[0].shape[0] // tm
    rspec = pl.BlockSpec((tm, NSA_DH), lambda i: (i % npos, 0))
    kvb = OFF_NKV // (2 * N_KV)
    n_in = 10
    extra = () if row_bufs is None else tuple(row_bufs)

    def kern(*refs):
        _nsa_prep_kernel(*refs[:n_in], *refs[n_in + len(extra):])

    return pl.pallas_call(
        kern,
        grid=(m // tm,),
        input_output_aliases={n_in + k: 1 + k for k in range(len(extra))},
        in_specs=[pl.BlockSpec((tm, D_C), lambda i: (i, OFF_NQ // D_C)),
                  pl.BlockSpec((tm, 2 * N_KV), lambda i: (i, kvb)),
                  pl.BlockSpec((tm, 2 * N_KV), lambda i: (i, kvb + 1)),
                  pl.BlockSpec((tm, 2 * N_KV), lambda i: (i, kvb + 2)),
                  pl.BlockSpec((tm, LANE), lambda i: (i, OFF_NGT // LANE)),
                  rspec, rspec, rspec,
                  pl.BlockSpec((1, NSA_DH), lambda i: (0, 0)),
                  pl.BlockSpec((3, NSA_DH), lambda i: (0, 0))]
        + [pl.BlockSpec(memory_space=pl.ANY)] * len(extra),
        out_specs=[pl.BlockSpec((tm, D_C), lambda i: (i, 0))]
        + [pl.BlockSpec((tm * KV_TILES, LANE), lambda i: (layer * (m // tm) + i, 0))] * 3
        + [pl.BlockSpec((tm, 2 * N_KV), lambda i: (i, 0))] * 3
        + [pl.BlockSpec((tm, LANE), lambda i: (i, 0))],
        out_shape=[jax.ShapeDtypeStruct((m, D_C), BF16)]
        + [jax.ShapeDtypeStruct((depth * m * KV_TILES, LANE), F32)] * 3
        + [jax.ShapeDtypeStruct((m, 2 * N_KV), BF16)] * 3
        + [jax.ShapeDtypeStruct((m, LANE), F32)],
        compiler_params=_cparams(("arbitrary",)),
        name="nsa_prep",
    )(hin, hin, hin, hin, hin, *rope_tabs, qn, kn, *extra)


KV_TILES = 2 * N_KV // LANE


def _pool_rows(load, n, a1_ref, a2_ref):
    firsts, seconds = [], []
    for c in range(KV_TILES):
        ls = slice(c * LANE, (c + 1) * LANE)
        first = jnp.zeros((n, LANE), F32)
        second = jnp.zeros((n, LANE), F32)
        for j in range(CMP_STRIDE):
            x = load(j, c)
            first = first + x * a1_ref[j:j + 1, ls]
            second = second + x * a2_ref[j:j + 1, ls]
        firsts.append(first)
        seconds.append(second)
    return firsts, seconds


def _tile_rows(j, c, n):
    return pl.ds(j * KV_TILES + c, n, stride=CMP_STRIDE * KV_TILES)


def _cmp_mlp(first, second, n, w1_ref, b1_ref, w2_ref, o_ref):
    for c in range(2):
        for g in range(NSA_KVH):
            tile = 2 * c + g
            pooled = first[tile] + pltpu.roll(second[tile], n - 1, 0)
            h = _dot(pooled.astype(BF16), w1_ref[c]) + b1_ref[c:c + 1, :]
            o_ref[0, :, tile * NSA_DH:(tile + 1) * NSA_DH] = _dot(
                jax.nn.gelu(h).astype(BF16), w2_ref[c]).astype(BF16)


def _cmp_prompt_kernel(x_ref, a1_ref, a2_ref, w1_ref, b1_ref, w2_ref, o_ref, *, n):
    first, second = _pool_rows(lambda j, c: x_ref[_tile_rows(j, c, n), :], n, a1_ref, a2_ref)
    _cmp_mlp(first, second, n, w1_ref, b1_ref, w2_ref, o_ref)


def _cmp_prompt_call(rows, nb, seq, cp, layer=0):
    n = seq // CMP_STRIDE
    const = lambda shape: pl.BlockSpec(shape, lambda b: (0,) * len(shape))
    return pl.pallas_call(
        functools.partial(_cmp_prompt_kernel, n=n),
        grid=(nb,),
        in_specs=[pl.BlockSpec((seq * KV_TILES, LANE), lambda b: (layer * nb + b, 0)),
                  const((CMP_STRIDE, 2 * N_KV)), const((CMP_STRIDE, 2 * N_KV)),
                  const((2, NSA_DH, NSA_DH)), const((2, NSA_DH)), const((2, NSA_DH, NSA_DH))],
        out_specs=pl.BlockSpec((1, n, 2 * N_KV), lambda b: (b, 0, 0)),
        out_shape=jax.ShapeDtypeStruct((nb, n, 2 * N_KV), BF16),
        compiler_params=_cparams(("arbitrary",)),
        name="cmp_prompt",
    )(rows.reshape(-1, LANE), cp["a1"], cp["a2"], cp["w1"], cp["b1"], cp["w2"])


PG = 16
SUB_PER_PAGE = PAGE // CMP_STRIDE


SUB_ROWS = CMP_STRIDE * KV_TILES


def _cmp_pool_cache_kernel(pt_ref, *refs):
    pages = refs[:PG]
    a1_ref, a2_ref, f_ref, s_ref = refs[PG:]
    nv = SUB_ROWS // SUBLANE
    w1 = [a1_ref[0, v * SUBLANE:(v + 1) * SUBLANE, :] for v in range(nv)]
    w2 = [a2_ref[0, v * SUBLANE:(v + 1) * SUBLANE, :] for v in range(nv)]
    low = lax.broadcasted_iota(I32, (SUBLANE, LANE), 0) < KV_TILES

    def fold(acc):
        return acc + pltpu.roll(acc, KV_TILES, 0)

    for k in range(PG):
        for n2 in range(SUB_PER_PAGE // 2):
            halves = []
            for n in (2 * n2, 2 * n2 + 1):
                a1 = a2 = None
                for v in range(nv):
                    x = pages[k][0, 0, n * SUB_ROWS + v * SUBLANE:n * SUB_ROWS + (v + 1) * SUBLANE, :]
                    a1 = x * w1[v] if a1 is None else a1 + x * w1[v]
                    a2 = x * w2[v] if a2 is None else a2 + x * w2[v]
                halves.append((fold(a1), fold(a2)))
            r0 = (k * SUB_PER_PAGE + 2 * n2) * KV_TILES
            f_ref[0, 0, r0:r0 + SUBLANE, :] = jnp.where(low, halves[0][0], halves[1][0])
            s_ref[0, 0, r0:r0 + SUBLANE, :] = jnp.where(low, halves[0][1], halves[1][1])


def _page_specs(n_pages, layer_of, batch_of, group_of, page_shape=(PAGE, 2 * N_KV)):
    def spec(k):
        return pl.BlockSpec(
            (1, 1) + page_shape,
            lambda *a, k=k: (layer_of(*a), a[-1][batch_of(*a) * n_pages + group_of(*a) * PG + k], 0, 0))
    return [spec(k) for k in range(PG)]


def _cmp_pool_cache_call(cache, pt_flat, nb, n_pages, a1, a2):
    depth = cache.shape[0]
    npg = n_pages // PG
    nsub = n_pages * SUB_PER_PAGE
    rows = PG * SUB_PER_PAGE
    grid_spec = pltpu.PrefetchScalarGridSpec(
        num_scalar_prefetch=1,
        grid=(depth, nb, npg),
        in_specs=_page_specs(n_pages, lambda l, b, g, pt: l, lambda l, b, g, pt: b,
                             lambda l, b, g, pt: g, (PAGE * KV_TILES, LANE))
        + [pl.BlockSpec((1, SUB_ROWS, LANE), lambda l, b, g, pt: (l, 0, 0))] * 2,
        out_specs=[pl.BlockSpec((1, 1, rows * KV_TILES, LANE), lambda l, b, g, pt: (l, b, g, 0))] * 2,
    )
    return pl.pallas_call(
        _cmp_pool_cache_kernel,
        grid_spec=grid_spec,
        out_shape=[jax.ShapeDtypeStruct((depth, nb, nsub * KV_TILES, LANE), F32)] * 2,
        compiler_params=_cparams(("arbitrary", "arbitrary", "arbitrary")),
        name="cmp_pool_cache",
    )(pt_flat, *([cache] * PG), a1, a2)


def _cmp_mlp_cache_kernel(f_ref, s_ref, w1_ref, b1_ref, w2_ref, o_ref, *, n):
    tiles = lambda ref: [ref[0, 0, pl.ds(c, n, stride=KV_TILES), :] for c in range(KV_TILES)]
    _cmp_mlp(tiles(f_ref), tiles(s_ref), n, w1_ref.at[0], b1_ref.at[0], w2_ref.at[0], o_ref.at[0])


def _cmp_mlp_cache_call(first, second, w1, b1, w2):
    depth, nb, rows, _ = first.shape
    n = rows // KV_TILES
    blk = pl.BlockSpec((1, 1, n, 2 * N_KV), lambda l, b: (l, b, 0, 0))
    tblk = pl.BlockSpec((1, 1, rows, LANE), lambda l, b: (l, b, 0, 0))
    return pl.pallas_call(
        functools.partial(_cmp_mlp_cache_kernel, n=n),
        grid=(depth, nb),
        in_specs=[tblk, tblk,
                  pl.BlockSpec((1, 2, NSA_DH, NSA_DH), lambda l, b: (l, 0, 0, 0)),
                  pl.BlockSpec((1, 2, NSA_DH), lambda l, b: (l, 0, 0)),
                  pl.BlockSpec((1, 2, NSA_DH, NSA_DH), lambda l, b: (l, 0, 0, 0))],
        out_specs=blk,
        out_shape=jax.ShapeDtypeStruct((depth, nb, n, 2 * N_KV), BF16),
        compiler_params=_cparams(("arbitrary", "arbitrary")),
        name="cmp_mlp_cache",
    )(first, second, w1, b1, w2)


def _block_scores(imp, qpos, n_slc):
    blk = lax.broadcasted_iota(I32, imp.shape, 1)
    cur = qpos // SEL_BLOCK
    forced = (blk == 0) | (blk == cur) | (blk == cur - 1)
    valid = blk * SEL_BLOCK <= qpos
    score = jnp.where(valid, imp + jnp.where(forced, FORCE_BONUS, 0.0), -1.0)
    return jnp.where(blk < n_slc, score, -2.0)


def _select_blocks_ranked(imp, qpos, n_slc):
    nb8 = (n_slc + SUBLANE - 1) // SUBLANE * SUBLANE
    st = _block_scores(imp, qpos, n_slc).T[:nb8]
    ridx = lax.broadcasted_iota(I32, st.shape, 0)
    cnt = jnp.zeros(st.shape, F32)
    for i in range(n_slc):
        ri = st[i:i + 1]
        cnt = cnt + jnp.where(ri > st, 1.0, jnp.where(ri == st, jnp.where(ridx > i, 1.0, 0.0), 0.0))
    sel_t = jnp.where((cnt < min(N_SEL, n_slc)) & (ridx < n_slc), 1.0, 0.0)
    if nb8 < LANE:
        sel_t = jnp.concatenate([sel_t, jnp.zeros((LANE - nb8, st.shape[1]), F32)], axis=0)
    return sel_t.T


def _select_blocks(imp, qpos, n_slc):
    blk = lax.broadcasted_iota(I32, imp.shape, 1)
    score = _block_scores(imp, qpos, n_slc)
    sel = jnp.zeros(imp.shape, F32)
    for _ in range(min(N_SEL, n_slc)):
        mx = jnp.max(score, axis=-1, keepdims=True)
        idx = jnp.min(jnp.where(score == mx, blk, 1 << 30), axis=-1, keepdims=True)
        hit = blk == idx
        sel = jnp.where(hit, 1.0, sel)
        score = jnp.where(hit, -3.0, score)
    return sel


def _cmp_branch(q, kck, kcv, qpos, n_cmp):
    s = _dot_nt(q, kck) * (NSA_DH ** -0.5)
    col = lax.broadcasted_iota(I32, s.shape, 1)
    mask = (col * CMP_STRIDE + (2 * CMP_STRIDE - 1) <= qpos) & (col < n_cmp)
    p = _masked_softmax(s, mask)
    return p, _dot(p.astype(BF16), kcv)


NSA_KC = 1024


def _nsa_prompt_kernel(q_ref, kc_ref, ks_ref, vs_ref, kw_ref, vw_ref, g_ref, o_ref,
                       bias_scr, m_scr, l_scr, acc_scr, *, tq, seq, slab):
    q0 = pl.program_id(1) * tq
    rows = NSA_REP * tq
    ncp = seq // CMP_STRIDE
    n_slc = seq // SEL_BLOCK
    scale = NSA_DH ** -0.5
    gates = g_ref[...]
    q4s = [jnp.concatenate([q_ref[:, (g * NSA_REP + r) * NSA_DH:(g * NSA_REP + r + 1) * NSA_DH]
                            for r in range(NSA_REP)], axis=0) for g in range(NSA_KVH)]
    qpos_c = q0 + (lax.broadcasted_iota(I32, (rows, ncp), 0) & (tq - 1))
    o_cs, psums = [], []
    for g in range(NSA_KVH):
        p_c, o_c = _cmp_branch(q4s[g], kc_ref[0, :, g * NSA_DH:(g + 1) * NSA_DH],
                               kc_ref[0, :, N_KV + g * NSA_DH:N_KV + (g + 1) * NSA_DH], qpos_c, ncp - 1)
        psum = p_c[0:tq]
        for r in range(1, NSA_REP):
            psum = psum + p_c[r * tq:(r + 1) * tq]
        o_cs.append(o_c)
        psums.append(psum)

    def ranked():
        gmat = (lax.broadcasted_iota(I32, (ncp, LANE), 0) // (SEL_BLOCK // CMP_STRIDE)
                == lax.broadcasted_iota(I32, (ncp, LANE), 1)).astype(BF16)
        imp = _dot2(jnp.concatenate(psums, axis=0), gmat)
        qp = q0 + (lax.broadcasted_iota(I32, (NSA_KVH * tq, LANE), 0) & (tq - 1))
        return _select_blocks_ranked(imp, qp, n_slc)

    sel = lax.cond(q0 + tq > N_SEL * SEL_BLOCK, ranked,
                   lambda: jnp.ones((NSA_KVH * tq, LANE), F32))
    emat = (lax.broadcasted_iota(I32, (LANE, seq), 1) // SEL_BLOCK
            == lax.broadcasted_iota(I32, (LANE, seq), 0)).astype(BF16)
    nk = (q0 + tq + NSA_KC - 1) // NSA_KC
    s0 = pl.multiple_of(jnp.maximum(q0 + tq - slab, 0), SUBLANE)
    key = lax.broadcasted_iota(I32, (tq, seq), 1)
    causal = key <= q0 + lax.broadcasted_iota(I32, (tq, seq), 0)
    wp = s0 + lax.broadcasted_iota(I32, (tq, slab), 1)
    qpw = q0 + lax.broadcasted_iota(I32, (tq, slab), 0)
    wb1 = jnp.where((wp <= qpw) & (wp > qpw - WINDOW), 0.0, NEG_INF)

    def add_bias(s, bias):
        return jnp.concatenate([s[r * tq:(r + 1) * tq] + bias for r in range(NSA_REP)], axis=0)

    for g in range(NSA_KVH):
        keep = (_dot(sel[g * tq:(g + 1) * tq].astype(BF16), emat) > 0.5) & causal
        bias = jnp.where(keep, 0.0, NEG_INF)
        for c in range(seq // NSA_KC):
            bias_scr[g, c] = bias[:, c * NSA_KC:(c + 1) * NSA_KC]
    m_scr[...] = jnp.full(m_scr.shape, NEG_INF, F32)
    l_scr[...] = jnp.zeros(l_scr.shape, F32)
    acc_scr[...] = jnp.zeros(acc_scr.shape, F32)

    def body(c, carry):
        k0 = pl.multiple_of(c * NSA_KC, NSA_KC)
        for g in range(NSA_KVH):
            gs = slice(g * NSA_DH, (g + 1) * NSA_DH)
            s = add_bias(_dot_nt(q4s[g], ks_ref[pl.ds(k0, NSA_KC), gs]) * scale, bias_scr[g, c])
            m_old = m_scr[g]
            m_new = jnp.maximum(m_old, _row_reduce(s, jnp.maximum, jnp.max))
            p = jnp.exp(s - m_new)
            alpha = jnp.exp(m_old - m_new)
            l_scr[g] = alpha * l_scr[g] + _row_reduce(p, jnp.add, jnp.sum)
            acc_scr[g] = alpha * acc_scr[g] + _dot(p.astype(BF16), vs_ref[pl.ds(k0, NSA_KC), gs])
            m_scr[g] = m_new
        return carry

    lax.fori_loop(0, nk, body, 0)
    for g in range(NSA_KVH):
        gs = slice(g * NSA_DH, (g + 1) * NSA_DH)
        q4 = q4s[g]
        o_s = acc_scr[g] / l_scr[g]
        s_w = add_bias(_dot_nt(q4, kw_ref[pl.ds(s0, slab), gs]) * scale, wb1)
        p_w = jnp.exp(s_w - _row_reduce(s_w, jnp.maximum, jnp.max))
        o_w = (_dot(p_w.astype(BF16), vw_ref[pl.ds(s0, slab), gs])
               / _row_reduce(p_w, jnp.add, jnp.sum))
        o_c = o_cs[g]
        for r in range(NSA_REP):
            h = g * NSA_REP + r
            rs = slice(r * tq, (r + 1) * tq)
            o = (gates[:, 3 * h:3 * h + 1] * o_c[rs] + gates[:, 3 * h + 1:3 * h + 2] * o_s[rs]
                 + gates[:, 3 * h + 2:3 * h + 3] * o_w[rs])
            o_ref[:, h * NSA_DH:(h + 1) * NSA_DH] = o.astype(BF16)


def _nsa_prompt_call(qn, kc, bs, bw, gates, nb, seq):
    tq = 128
    nq = seq // tq
    slab = min(seq, WINDOW + tq)
    kv = lambda col: pl.BlockSpec((seq, N_KV), lambda b, i: (b, col))
    kern = functools.partial(_nsa_prompt_kernel, tq=tq, seq=seq, slab=slab)
    rows = NSA_REP * tq
    return pl.pallas_call(
        kern,
        grid=(nb, nq),
        in_specs=[pl.BlockSpec((tq, D_C), lambda b, i: (b * nq + i, 0)),
                  pl.BlockSpec((1, seq // CMP_STRIDE, 2 * N_KV), lambda b, i: (b, 0, 0)),
                  kv(0), kv(1), kv(0), kv(1),
                  pl.BlockSpec((tq, LANE), lambda b, i: (b * nq + i, 0))],
        out_specs=pl.BlockSpec((tq, D_C), lambda b, i: (b * nq + i, 0)),
        out_shape=jax.ShapeDtypeStruct((nb * seq, D_C), BF16),
        scratch_shapes=[pltpu.VMEM((NSA_KVH, seq // NSA_KC, tq, NSA_KC), F32),
                        pltpu.VMEM((NSA_KVH, rows, 1), F32), pltpu.VMEM((NSA_KVH, rows, 1), F32),
                        pltpu.VMEM((NSA_KVH, rows, NSA_DH), F32)],
        compiler_params=_cparams(("arbitrary", "arbitrary")),
        name="nsa_prompt",
    )(qn, kc, bs, bs, bw, bw, gates)


def _nsa_s_sel_kernel(q_ref, kc_ref, oc_ref, sel_ref, *, past, dl, n_tiles):
    rows = NSA_REP * dl
    ncp = past // CMP_STRIDE
    n_slc = (past + dl + SEL_BLOCK - 1) // SEL_BLOCK
    nbp = (n_slc + LANE - 1) // LANE * LANE
    per_tile = PG * PAGE // SEL_BLOCK
    gmat = (lax.broadcasted_iota(I32, (ncp, nbp), 0) // (SEL_BLOCK // CMP_STRIDE)
            == lax.broadcasted_iota(I32, (ncp, nbp), 1)).astype(BF16)
    rmat = (lax.broadcasted_iota(I32, (rows, rows), 0) // NSA_REP
            == lax.broadcasted_iota(I32, (rows, rows), 1) // NSA_REP).astype(BF16)

    def qpos_of(shape):
        return past + lax.broadcasted_iota(I32, shape, 0) // NSA_REP

    lane = lax.broadcasted_iota(I32, (rows, LANE), 1)
    imps = []
    for g in range(NSA_KVH):
        gs = slice(g * NSA_DH, (g + 1) * NSA_DH)
        p_c, o_c = _cmp_branch(q_ref[0, g], kc_ref[0, 0, :, gs],
                               kc_ref[0, 0, :, N_KV + g * NSA_DH:N_KV + (g + 1) * NSA_DH],
                               qpos_of((rows, ncp)), ncp - 1)
        oc_ref[0, g] = o_c
        psum = _dot2g(rmat, p_c)
        imps.append(_dot2(psum, gmat))
    qp = past + (lax.broadcasted_iota(I32, (NSA_KVH * rows, nbp), 0) % rows) // NSA_REP
    sel_all = _select_blocks(jnp.concatenate(imps, axis=0), qp, n_slc)
    for g in range(NSA_KVH):
        sel = sel_all[g * rows:(g + 1) * rows]
        for tl in range(n_tiles):
            lo = tl * per_tile
            chunk = sel[:, (lo // LANE) * LANE:(lo // LANE + 1) * LANE]
            sh = (LANE - lo % LANE) % LANE
            if sh:
                chunk = pltpu.roll(chunk, sh, 1)
            sel_ref[0, g, tl] = jnp.where(lane < per_tile, chunk, 0.0)


def _nsa_s_sel_call(q16, kc, layer, past, dl):
    nb = q16.shape[0]
    rows = NSA_REP * dl
    n_tiles = past // (PG * PAGE) + 1
    kern = functools.partial(_nsa_s_sel_kernel, past=past, dl=dl, n_tiles=n_tiles)
    return pl.pallas_call(
        kern,
        grid=(nb,),
        in_specs=[pl.BlockSpec((1, NSA_KVH, rows, NSA_DH), lambda b: (b, 0, 0, 0)),
                  pl.BlockSpec((1, 1, past // CMP_STRIDE, 2 * N_KV), lambda b: (layer, b, 0, 0))],
        out_specs=[pl.BlockSpec((1, NSA_KVH, rows, NSA_DH), lambda b: (b, 0, 0, 0)),
                   pl.BlockSpec((1, NSA_KVH, n_tiles, rows, LANE), lambda b: (b, 0, 0, 0, 0))],
        out_shape=[jax.ShapeDtypeStruct((nb, NSA_KVH, rows, NSA_DH), F32),
                   jax.ShapeDtypeStruct((nb, NSA_KVH, n_tiles, rows, LANE), F32)],
        compiler_params=_cparams(("arbitrary",)),
        name="nsa_sample_select",
    )(q16, kc)


def _nsa_s_attn_kernel(pt_ref, *refs, past, dl, npg):
    pages = refs[:PG]
    (q_ref, selt_ref, sell_ref, emat_ref, oc_ref, ns_ref, nw_ref, wp_ref, g_ref,
     o_ref, m_scr, l_scr, a_scr) = refs[PG:]
    pg = pl.program_id(1)
    rows = NSA_REP * dl
    scale = NSA_DH ** -0.5
    wb = wp_ref.shape[2] // KV_TILES

    def tok_rows(ref, c, n):
        return ref[0, 0, pl.ds(c, n, stride=KV_TILES), :].astype(BF16)

    @pl.when(pg == 0)
    def _():
        m_scr[...] = jnp.full(m_scr.shape, NEG_INF, F32)
        l_scr[...] = jnp.zeros(l_scr.shape, F32)
        a_scr[...] = jnp.zeros(a_scr.shape, F32)

    for g in range(NSA_KVH):
        q = q_ref[0, g]
        kmask = _dot(selt_ref[0, g, 0].astype(BF16), emat_ref[...]) > 0.5
        k_all = jnp.concatenate([tok_rows(pages[k], g, PAGE) for k in range(PG)], axis=0)
        v_all = jnp.concatenate([tok_rows(pages[k], NSA_KVH + g, PAGE) for k in range(PG)], axis=0)
        s = jnp.where(kmask, _dot_nt(q, k_all) * scale, NEG_INF)
        m_old = m_scr[g]
        m_new = jnp.maximum(m_old, jnp.max(s, axis=-1, keepdims=True))
        p = jnp.where(kmask, jnp.exp(s - m_new), 0.0)
        alpha = jnp.exp(m_old - m_new)
        l_scr[g] = alpha * l_scr[g] + jnp.sum(p, axis=-1, keepdims=True)
        a_scr[g] = alpha * a_scr[g] + _dot(p.astype(BF16), v_all)
        m_scr[g] = m_new

    @pl.when(pg == npg - 1)
    def _():
        gates = g_ref[0]
        tok = lax.broadcasted_iota(I32, (rows, SUBLANE), 0) // NSA_REP
        ncol = lax.broadcasted_iota(I32, (rows, SUBLANE), 1)
        for g in range(NSA_KVH):
            gs = slice(g * NSA_DH, (g + 1) * NSA_DH)
            vsl = slice(N_KV + g * NSA_DH, N_KV + (g + 1) * NSA_DH)
            q = q_ref[0, g]
            nmask = (sell_ref[0, g, 0][:, 0:1] > 0.5) & (ncol <= tok) & (ncol < dl)
            s_n = jnp.where(nmask, _dot_nt(q, ns_ref[0, :, gs].astype(BF16)) * scale, NEG_INF)
            m_old = m_scr[g]
            m_new = jnp.maximum(m_old, jnp.max(s_n, axis=-1, keepdims=True))
            p_n = jnp.where(nmask, jnp.exp(s_n - m_new), 0.0)
            alpha = jnp.exp(m_old - m_new)
            lsum = alpha * l_scr[g] + jnp.sum(p_n, axis=-1, keepdims=True)
            acc = alpha * a_scr[g] + _dot(p_n.astype(BF16), ns_ref[0, :, vsl].astype(BF16))
            o_s = acc / jnp.maximum(lsum, 1e-30)
            wrow = lax.broadcasted_iota(I32, (rows, wb), 1)
            wtok = lax.broadcasted_iota(I32, (rows, wb), 0) // NSA_REP
            pmask = (wrow > wtok + (wb - WINDOW)) & (wrow + (past - wb) >= 0)
            s_p = jnp.where(pmask, _dot_nt(q, tok_rows(wp_ref, g, wb)) * scale, NEG_INF)
            wmask = (ncol <= tok) & (ncol < dl)
            s_q = jnp.where(wmask, _dot_nt(q, nw_ref[0, :, gs].astype(BF16)) * scale, NEG_INF)
            mw = jnp.maximum(jnp.max(s_p, axis=-1, keepdims=True), jnp.max(s_q, axis=-1, keepdims=True))
            p_p = jnp.where(pmask, jnp.exp(s_p - mw), 0.0)
            p_q = jnp.where(wmask, jnp.exp(s_q - mw), 0.0)
            den = jnp.maximum(jnp.sum(p_p, axis=-1, keepdims=True) + jnp.sum(p_q, axis=-1, keepdims=True),
                              1e-30)
            o_w = (_dot(p_p.astype(BF16), tok_rows(wp_ref, NSA_KVH + g, wb))
                   + _dot(p_q.astype(BF16), nw_ref[0, :, vsl].astype(BF16))) / den
            gg = gates[g]
            o = gg[:, 0:1] * oc_ref[0, g] + gg[:, 1:2] * o_s + gg[:, 2:3] * o_w
            o_ref[0, g] = o.astype(BF16)


def _nsa_s_attn_call(cache, pt_flat, layer, q16, selt, emat, o_c, new_slc, new_win, win_prev, gates16,
                     past, dl):
    nb = q16.shape[0]
    rows = NSA_REP * dl
    n_pages = past // PAGE
    npg = n_pages // PG
    wrows = win_prev.shape[2]
    b4 = lambda shape: pl.BlockSpec((1,) + shape, lambda b, g, pt: (b,) + (0,) * len(shape))
    grid_spec = pltpu.PrefetchScalarGridSpec(
        num_scalar_prefetch=1,
        grid=(nb, npg),
        in_specs=_page_specs(n_pages, lambda b, g, pt: layer, lambda b, g, pt: b, lambda b, g, pt: g,
                             (PAGE * KV_TILES, LANE))
        + [b4((NSA_KVH, rows, NSA_DH)),
           pl.BlockSpec((1, NSA_KVH, 1, rows, LANE), lambda b, g, pt: (b, 0, g, 0, 0)),
           pl.BlockSpec((1, NSA_KVH, 1, rows, LANE), lambda b, g, pt: (b, 0, npg, 0, 0)),
           pl.BlockSpec((LANE, PG * PAGE), lambda b, g, pt: (0, 0)),
           b4((NSA_KVH, rows, NSA_DH)),
           b4((SUBLANE, 2 * N_KV)), b4((SUBLANE, 2 * N_KV)),
           pl.BlockSpec((1, 1, wrows, LANE), lambda b, g, pt: (layer, b, 0, 0)),
           b4((NSA_KVH, rows, LANE))],
        out_specs=b4((NSA_KVH, rows, NSA_DH)),
        scratch_shapes=[pltpu.VMEM((NSA_KVH, rows, 1), F32), pltpu.VMEM((NSA_KVH, rows, 1), F32),
                        pltpu.VMEM((NSA_KVH, rows, NSA_DH), F32)],
    )
    kern = functools.partial(_nsa_s_attn_kernel, past=past, dl=dl, npg=npg)
    return pl.pallas_call(
        kern,
        grid_spec=grid_spec,
        out_shape=jax.ShapeDtypeStruct((nb, NSA_KVH, rows, NSA_DH), BF16),
        compiler_params=_cparams(("arbitrary", "arbitrary")),
        name="nsa_sample_attn",
    )(pt_flat, *([cache] * PG), q16, selt, selt, emat, o_c, new_slc, new_win, win_prev, gates16)


def _merge_kernel(ya_ref, yb_ref, yc_ref, ga_ref, gb_ref, gc_ref, wa_ref, wb_ref, wc_ref, o_ref):
    m = (jax.nn.sigmoid(ga_ref[...]) * _dot(ya_ref[...], wa_ref[0])
         + jax.nn.sigmoid(gb_ref[...]) * _dot(yb_ref[...], wb_ref[0])
         + jax.nn.sigmoid(gc_ref[...]) * _dot(yc_ref[...], wc_ref[0]))
    o_ref[...] = m.astype(BF16)


def _merge_call(ya, yb, yc, hin, wa, wb, wc, layer):
    m = ya.shape[0]
    tm = min(256, m)
    const = lambda shape: pl.BlockSpec((1,) + shape, lambda i: (layer, 0, 0))
    gate = lambda col: pl.BlockSpec((tm, D_MODEL), lambda i: (i, col))
    return pl.pallas_call(
        _merge_kernel,
        grid=(m // tm,),
        in_specs=[pl.BlockSpec((tm, D_A), lambda i: (i, 0)), pl.BlockSpec((tm, D_B), lambda i: (i, 0)),
                  pl.BlockSpec((tm, D_C), lambda i: (i, 0)), gate(0), gate(1), gate(2),
                  const((D_A, D_MODEL)), const((D_B, D_MODEL)), const((D_C, D_MODEL))],
        out_specs=pl.BlockSpec((tm, D_MODEL), lambda i: (i, 0)),
        out_shape=jax.ShapeDtypeStruct((m, D_MODEL), BF16),
        compiler_params=_cparams(("arbitrary",)),
        name="merge",
    )(ya, yb, yc, hin, hin, hin, wa, wb, wc)


def _outproj_kernel(m_ref, x_ref, g_ref, w_ref, o_ref):
    o_ref[...] = x_ref[...] + g_ref[0] * _dot(m_ref[...], w_ref[0])


def _outproj_call(merged, x, modx, rows_per_group, w, layer):
    m = x.shape[0]
    tm = min(512, m)
    return pl.pallas_call(
        _outproj_kernel,
        grid=(m // tm,),
        in_specs=[pl.BlockSpec((tm, D_MODEL), lambda i: (i, 0)),
                  pl.BlockSpec((tm, D_MODEL), lambda i: (i, 0)),
                  _mod_spec(modx, tm, rows_per_group, 2),
                  pl.BlockSpec((1, D_MODEL, D_MODEL), lambda i: (layer, 0, 0))],
        out_specs=pl.BlockSpec((tm, D_MODEL), lambda i: (i, 0)),
        out_shape=jax.ShapeDtypeStruct((m, D_MODEL), F32),
        compiler_params=_cparams(("arbitrary",)),
        name="outproj",
    )(merged, x, modx, w)


def _ffn_tail(ua, ua1, ua2, ub, cw_ref, cb_ref, wd_ref):
    cw = cw_ref[0]
    conv = cb_ref[0] + cw[0:1] * ua2 + cw[1:2] * ua1 + cw[2:3] * ua
    return _dot((jax.nn.gelu(conv) * ub).astype(BF16), wd_ref[0])


FFN_HALO = 16


def _ffn_prompt_kernel(x_ref, xh_ref, sc_ref, sh_ref, g_ref, nw_ref, cp_ref, wua_ref, wub_ref, cw_ref,
                       cb_ref, wd_ref, o_ref, tail_ref, h_scr, *, tm, seq):
    f = pl.program_id(1)

    @pl.when(f == 0)
    def _():
        nw, sc, sh = nw_ref[...], sc_ref[0], sh_ref[0]
        h_scr[0:FFN_HALO, :] = (_rms(xh_ref[...], nw) * (1.0 + sc) + sh).astype(BF16)
        h_scr[FFN_HALO:, :] = (_rms(x_ref[...], nw) * (1.0 + sc) + sh).astype(BF16)
        o_ref[...] = jnp.zeros(o_ref.shape, F32)

    ua_all = _dot(h_scr[...], wua_ref[0])
    ua = ua_all[FFN_HALO:]
    ub = _dot(h_scr[FFN_HALO:, :], wub_ref[0])
    at_start = (pl.program_id(0) * tm) % seq == 0
    prev = jnp.where(at_start, cp_ref[0], ua_all[FFN_HALO - SUBLANE:FFN_HALO])
    row = lax.broadcasted_iota(I32, ua.shape, 0)
    p1, p2 = prev[SUBLANE - 1:SUBLANE], prev[SUBLANE - 2:SUBLANE - 1]
    ua1 = jnp.where(row == 0, p1, pltpu.roll(ua, 1, 0))
    ua2 = jnp.where(row == 0, p2, jnp.where(row == 1, p1, pltpu.roll(ua, 2, 0)))
    tail_ref[0] = ua[tm - SUBLANE:tm]
    o_ref[...] += _ffn_tail(ua, ua1, ua2, ub, cw_ref, cb_ref, wd_ref)

    @pl.when(f == pl.num_programs(1) - 1)
    def _():
        o_ref[...] = x_ref[...] + g_ref[0] * o_ref[...]


def _ffn_prompt_call(x, modx, seq, nw, cprev, fw, layer):
    m = x.shape[0]
    tm = min(1024, m)
    hb = tm // FFN_HALO
    kern = functools.partial(_ffn_prompt_kernel, tm=tm, seq=seq)
    return pl.pallas_call(
        kern,
        grid=(m // tm, D_FFP // TF),
        in_specs=[pl.BlockSpec((tm, D_MODEL), lambda i, f: (i, 0), pipeline_mode=pl.Buffered(1)),
                  pl.BlockSpec((FFN_HALO, D_MODEL), lambda i, f: (jnp.maximum(i * hb - 1, 0), 0)),
                  _mod_spec(modx, tm, seq, 4), _mod_spec(modx, tm, seq, 3), _mod_spec(modx, tm, seq, 5),
                  pl.BlockSpec((1, D_MODEL), lambda i, f: (0, 0)),
                  pl.BlockSpec((1, SUBLANE, TF), lambda i, f: ((i * tm) // seq, 0, f)),
                  pl.BlockSpec((1, D_MODEL, TF), lambda i, f: (layer, 0, f)),
                  pl.BlockSpec((1, D_MODEL, TF), lambda i, f: (layer, 0, D_FFP // TF + f)),
                  pl.BlockSpec((1, CONV_W, TF), lambda i, f: (layer, 0, f)),
                  pl.BlockSpec((1, 1, TF), lambda i, f: (layer, 0, f)),
                  pl.BlockSpec((1, TF, D_MODEL), lambda i, f: (layer, f, 0))],
        out_specs=[pl.BlockSpec((tm, D_MODEL), lambda i, f: (i, 0)),
                   pl.BlockSpec((1, SUBLANE, TF), lambda i, f: (i, 0, f))],
        out_shape=[jax.ShapeDtypeStruct((m, D_MODEL), F32),
                   jax.ShapeDtypeStruct((m // tm, SUBLANE, D_FFP), F32)],
        scratch_shapes=[pltpu.VMEM((tm + FFN_HALO, D_MODEL), BF16)],
        compiler_params=_cparams(("arbitrary", "arbitrary")),
        name="ffn_prompt",
    )(x, x, modx, modx, modx, nw, cprev, fw["wu"], fw["wu"], fw["cw"], fw["cb"], fw["wd"])


def _ffn_sample_kernel(x_ref, sc_ref, sh_ref, g_ref, nw_ref, p1_ref, p2_ref, wua_ref, wub_ref, cw_ref,
                       cb_ref, wd_ref, o_ref, ua_ref, h_scr, acc, *, dl):
    f = pl.program_id(0)

    @pl.when(f == 0)
    def _():
        h_scr[...] = (_rms(x_ref[...], nw_ref[...]) * (1.0 + sc_ref[0]) + sh_ref[0]).astype(BF16)
        acc[...] = jnp.zeros(acc.shape, F32)

    ua = _dot(h_scr[...], wua_ref[0])
    ub = _dot(h_scr[...], wub_ref[0])
    tin = lax.broadcasted_iota(I32, ua.shape, 0) % dl
    ua1 = jnp.where(tin >= 1, pltpu.roll(ua, 1, 0), p1_ref[...])
    ua2 = jnp.where(tin >= 2, pltpu.roll(ua, 2, 0), p2_ref[...])
    ua_ref[...] = ua
    acc[...] += _ffn_tail(ua, ua1, ua2, ub, cw_ref, cb_ref, wd_ref)

    @pl.when(f == pl.num_programs(0) - 1)
    def _():
        o_ref[...] = x_ref[...] + g_ref[0] * acc[...]


def _ffn_sample_call(x, modx, dl, nw, p1, p2, fw, layer):
    m = x.shape[0]
    kern = functools.partial(_ffn_sample_kernel, dl=dl)
    full = lambda i: pl.BlockSpec((m, D_MODEL), lambda f: (0, 0))
    return pl.pallas_call(
        kern,
        grid=(D_FFP // TF,),
        in_specs=[full(0),
                  pl.BlockSpec((1, m, D_MODEL), lambda f: (0, 0, 4)),
                  pl.BlockSpec((1, m, D_MODEL), lambda f: (0, 0, 3)),
                  pl.BlockSpec((1, m, D_MODEL), lambda f: (0, 0, 5)),
                  pl.BlockSpec((1, D_MODEL), lambda f: (0, 0)),
                  pl.BlockSpec((m, TF), lambda f: (0, f)), pl.BlockSpec((m, TF), lambda f: (0, f)),
                  pl.BlockSpec((1, D_MODEL, TF), lambda f: (layer, 0, f)),
                  pl.BlockSpec((1, D_MODEL, TF), lambda f: (layer, 0, D_FFP // TF + f)),
                  pl.BlockSpec((1, CONV_W, TF), lambda f: (layer, 0, f)),
                  pl.BlockSpec((1, 1, TF), lambda f: (layer, 0, f)),
                  pl.BlockSpec((1, TF, D_MODEL), lambda f: (layer, f, 0))],
        out_specs=[full(0), pl.BlockSpec((m, TF), lambda f: (0, f))],
        out_shape=[jax.ShapeDtypeStruct((m, D_MODEL), F32),
                   jax.ShapeDtypeStruct((m, D_FFP), F32)],
        scratch_shapes=[pltpu.VMEM((m, D_MODEL), BF16), pltpu.VMEM((m, D_MODEL), F32)],
        compiler_params=_cparams(("arbitrary",)),
        name="ffn_sample",
    )(x, modx, modx, modx, nw, p1, p2, fw["wu"], fw["wu"], fw["cw"], fw["cb"], fw["wd"])


def _rope_tables(pos):
    half = ROPE_DIM // 2
    inv = jnp.exp(jnp.arange(half, dtype=F32) * (-math.log(ROPE_THETA) / half))
    ang = pos.astype(F32)[:, None] * inv[None, :]
    cos, sin = jnp.cos(ang), jnp.sin(ang)
    n = pos.shape[0]
    rest = NSA_DH - ROPE_DIM
    cos_t = jnp.concatenate([cos, cos, jnp.ones((n, rest), F32)], axis=1)
    sin_a = jnp.concatenate([-sin, jnp.zeros((n, NSA_DH - half), F32)], axis=1)
    sin_b = jnp.concatenate([jnp.zeros((n, half), F32), sin, jnp.zeros((n, rest), F32)], axis=1)
    return cos_t, sin_a, sin_b


_W_IN_SIZES = (D_A, D_B, D_B, D_B, D_B, D_C, 6 * N_KV, 3 * NSA_HEADS, 3 * D_MODEL)
_W_IN_DST = (OFF_UA, OFF_HQ, OFF_HF, OFF_HI, OFF_HGT, OFF_NQ, OFF_NKV, OFF_NGT, OFF_MGT)
N_IN = sum(_W_IN_SIZES)


W_IN_RB = 256


def _w_in_src_row(i):
    src = jnp.int32(0)
    start = 0
    for size, dst in zip(_W_IN_SIZES, _W_IN_DST):
        inside = (i * W_IN_RB >= dst) & (i * W_IN_RB < dst + max(size, W_IN_RB))
        src = jnp.where(inside, start + i * W_IN_RB - dst, src)
        start += size
    return src


def _w_in_prep_kernel(x_ref, o_ref):
    i = pl.program_id(1)
    x = x_ref[0].astype(BF16)
    row = lax.broadcasted_iota(I32, x.shape, 0)
    o_ref[0] = jnp.where((i * W_IN_RB == OFF_NGT) & (row >= 3 * NSA_HEADS), jnp.zeros_like(x), x)


def _permute_w_in(w_in):
    depth, d, _ = w_in.shape
    w_t = jnp.swapaxes(w_in, 1, 2)
    return pl.pallas_call(
        _w_in_prep_kernel,
        grid=(depth, N_INP // W_IN_RB),
        in_specs=[pl.BlockSpec((pl.Element(1), pl.Element(W_IN_RB), pl.Element(d)),
                               lambda l, i: (l, pl.multiple_of(_w_in_src_row(i), SUBLANE), 0))],
        out_specs=pl.BlockSpec((1, W_IN_RB, d), lambda l, i: (l, i, 0)),
        out_shape=jax.ShapeDtypeStruct((depth, N_INP, d), BF16),
        compiler_params=_cparams(("arbitrary", "arbitrary")),
        name="w_in_prep",
    )(w_t)


def _w_up_prep_kernel(x_ref, o_ref):
    o_ref[0, :, :D_FF] = x_ref[0].astype(BF16)
    o_ref[0, :, D_FF:] = jnp.zeros((o_ref.shape[1], D_FFP - D_FF), BF16)


def _pad_w_up(w_up):
    depth, d, _ = w_up.shape
    r = 512
    return pl.pallas_call(
        _w_up_prep_kernel,
        grid=(depth, d // r, 2),
        in_specs=[pl.BlockSpec((1, r, D_FF), lambda l, i, h: (l, i, h))],
        out_specs=pl.BlockSpec((1, r, D_FFP), lambda l, i, h: (l, i, h)),
        out_shape=jax.ShapeDtypeStruct((depth, d, 2 * D_FFP), BF16),
        compiler_params=_cparams(("arbitrary", "arbitrary", "arbitrary")),
        name="w_up_prep",
    )(w_up)


def _s5_params(a_re, a_im, log_dt, b_re, b_im, c_re, c_im, d, w_glu, b_glu, seg_lens):
    dt = jnp.exp(log_dt)[:, None]
    mag = jnp.exp(a_re * dt)
    ab_re, ab_im = mag * jnp.cos(a_im * dt), mag * jnp.sin(a_im * dt)
    den = a_re * a_re + a_im * a_im
    cf_re = ((ab_re - 1.0) * a_re + ab_im * a_im) / den
    cf_im = (ab_im * a_re - (ab_re - 1.0) * a_im) / den
    bb_re = cf_re[..., None] * b_re - cf_im[..., None] * b_im
    bb_im = cf_re[..., None] * b_im + cf_im[..., None] * b_re
    eye = jnp.eye(S5_G, dtype=F32)
    bdiag = lambda bb: jnp.einsum("gnc,gh->gchn", bb, eye).reshape(D_A, S5_W).astype(BF16)
    cdiag = lambda cc: jnp.einsum("gcn,gh->gnhc", cc, eye).reshape(S5_W, D_A).astype(BF16)
    cmul = lambda a, b: (a[0] * b[0] - a[1] * b[1], a[0] * b[1] + a[1] * b[0])
    abar = (ab_re.reshape(1, S5_W), ab_im.reshape(1, S5_W))
    pw = [abar]
    for _ in range(max(seg_lens) - 1):
        pw.append(cmul(pw[-1], abar))

    def tables(ts):
        rep = lambda k: jnp.concatenate([jnp.broadcast_to(p[k], (SUBLANE, S5_W)) for p in pw[:ts]], axis=0)
        g1 = pw[ts - 1]
        g2 = cmul(g1, g1)
        g4 = cmul(g2, g2)
        pad = jnp.zeros((SUBLANE - 3, S5_W), F32)
        gp = lambda k: jnp.concatenate([g1[k], g2[k], g4[k], pad], axis=0)
        return rep(0), rep(1), gp(0), gp(1)

    return dict(bbr=bdiag(bb_re), bbi=bdiag(bb_im), ccr=cdiag(c_re), cci=cdiag(c_im),
                pow={ts: tables(ts) for ts in seg_lens},
                d=d.reshape(1, D_A), wg=w_glu.astype(BF16), bg=b_glu.reshape(1, D_A))


def _hgrn_params(lb):
    pos = lb > 0
    lb_safe = jnp.where(pos, lb, 1.0)
    z = jnp.zeros_like(lb)
    return jnp.stack([jnp.log(lb_safe), jnp.log1p(-lb), 1.0 - lb, pos.astype(F32), z, z, z, z], axis=0)


def _cmp_params(cmp_a, w1, b1, w2):
    def lanes(a):
        return jnp.concatenate([a[0], a[0], a[1], a[1]], axis=-1)
    return dict(a1=lanes(cmp_a[:, :CMP_STRIDE]), a2=lanes(cmp_a[:, CMP_STRIDE:]),
                w1=w1.astype(BF16), b1=b1, w2=w2.astype(BF16))


def _ffn_params(w_up, conv_w, conv_b, w_down):
    padc = lambda a: jnp.pad(a, [(0, 0)] * (a.ndim - 1) + [(0, D_FFP - D_FF)])
    wu = _pad_w_up(w_up)
    wd = jnp.pad(w_down.astype(BF16), ((0, 0), (0, D_FFP - D_FF), (0, 0)))
    return dict(wu=wu, cw=padc(conv_w), cb=padc(conv_b)[:, None, :], wd=wd)


def kernel(x_prompt, x_sample, cache_cmp, cache_slc, state_win, state_s5, state_hgrn, state_conv,
           page_table, c_prompt, c_sample, w_mod, b_mod, norm1_w, norm2_w, w_in,
           s5_a_re, s5_a_im, s5_log_dt, s5_b_re, s5_b_im, s5_c_re, s5_c_im, s5_d, s5_w_glu, s5_b_glu,
           hg_lb_logits, hg_norm_w, nsa_q_norm, nsa_k_norm, cmp_a, cmp_w1, cmp_b1, cmp_w2,
           w_branch_a, w_branch_b, w_branch_c, w_out, w_up, conv_w, conv_b, w_down):
    bsz, seq, _ = x_prompt.shape
    dbsz, dl, _ = x_sample.shape
    depth = w_in.shape[0]
    n_pages = page_table.shape[1]
    past = n_pages * PAGE
    wb = state_win.shape[2]
    mp, ms = bsz * seq, dbsz * dl
    rows = NSA_REP * dl
    assert seq % 512 == 0 and seq // SEL_BLOCK <= LANE and n_pages % PG == 0 and dl <= SUBLANE

    probs = jax.nn.softmax(hg_lb_logits.astype(F32), axis=0)
    lower_bounds = jnp.cumsum(probs, axis=0) - probs[0:1]
    w_in_p = _permute_w_in(w_in)
    wa, wbr, wc, wo = (w.astype(BF16) for w in (w_branch_a, w_branch_b, w_branch_c, w_out))
    fw = _ffn_params(w_up, conv_w, conv_b, w_down)
    rope_p = _rope_tables(jnp.arange(seq, dtype=I32))
    rope_s = tuple(jnp.tile(t, (dbsz, 1)) for t in _rope_tables(past + jnp.arange(dl, dtype=I32)))
    pt_flat = page_table.reshape(-1).astype(I32)
    cache_cmp4 = cache_cmp.reshape(depth, -1, PAGE * KV_TILES, LANE)
    cache_slc4 = cache_slc.reshape(depth, -1, PAGE * KV_TILES, LANE)
    win_prev4 = state_win.reshape(depth, dbsz, wb * KV_TILES, LANE)
    emat = (jnp.arange(PG * PAGE)[None, :] // SEL_BLOCK == jnp.arange(LANE)[:, None]).astype(BF16)

    nr = -(-(bsz + dbsz) // 16) * 16
    c_all = jnp.concatenate([c_prompt, c_sample, jnp.zeros((nr - bsz - dbsz, D_MODEL), F32)], axis=0)
    mod = _mod_call(c_all, w_mod, b_mod)

    cmp_lanes = lambda a: jnp.concatenate([a[:, 0], a[:, 0], a[:, 1], a[:, 1]], axis=-1).reshape(
        depth, SUB_ROWS, LANE)
    first, second = _cmp_pool_cache_call(cache_cmp4, pt_flat, dbsz, n_pages,
                                         cmp_lanes(cmp_a[:, :, :CMP_STRIDE]),
                                         cmp_lanes(cmp_a[:, :, CMP_STRIDE:]))
    kc_past = _cmp_mlp_cache_call(first, second, cmp_w1.astype(BF16), cmp_b1, cmp_w2.astype(BF16))

    xp = x_prompt.reshape(mp, D_MODEL)
    xs = x_sample.reshape(ms, D_MODEL)
    outs = [[] for _ in range(12)]
    row_bufs = None
    for l in range(depth):
        s5p = _s5_params(s5_a_re[l], s5_a_im[l], s5_log_dt[l], s5_b_re[l], s5_b_im[l], s5_c_re[l],
                         s5_c_im[l], s5_d[l], s5_w_glu[l], s5_b_glu[l],
                         (min(256, seq) // SUBLANE, 1))
        hgp = _hgrn_params(lower_bounds[l])
        hgn = hg_norm_w[l].reshape(1, HG_DV)
        cp = _cmp_params(cmp_a[l], cmp_w1[l], cmp_b1[l], cmp_w2[l])
        n1, n2 = norm1_w[l].reshape(1, D_MODEL), norm2_w[l].reshape(1, D_MODEL)
        qn, kn = nsa_q_norm[l].reshape(1, NSA_DH), nsa_k_norm[l]
        modp = mod[l, :bsz].reshape(bsz, 1, 6 * D_MODEL)
        mods = jnp.repeat(mod[l, bsz:bsz + dbsz], dl, axis=0).reshape(1, ms, 6 * D_MODEL)

        hin = _inproj_call(xp, modp, seq, n1, w_in_p, l)
        ya, s5_new = _s5_call(hin, OFF_UA // D_A, bsz, seq, jnp.zeros((bsz, 2, S5_W), F32), s5p,
                              SUBLANE - 1)
        yb, hg_new = _hgrn_call(hin, tuple(o // D_B for o in (OFF_HQ, OFF_HF, OFF_HI, OFF_HGT)), bsz, seq,
                                seq, jnp.zeros((bsz, HG_HEADS, HG_DK, HG_DV), F32), hgp, hgn)
        qo, *row_bufs, bc, bs, bw, gates = _nsa_prep_call(hin, rope_p, seq, qn, kn, l, depth, row_bufs)
        kc = _cmp_prompt_call(row_bufs[0], bsz, seq, cp, l)
        yc = _nsa_prompt_call(qo, kc, bs, bw, gates, bsz, seq)
        merged = _merge_call(ya, yb, yc, hin, wa, wbr, wc, l)
        xp = _outproj_call(merged, xp, modp, seq, wo, l)
        xp, tails = _ffn_prompt_call(xp, modp, seq, n2, jnp.zeros((bsz, SUBLANE, D_FFP), F32), fw, l)
        conv_new = tails.reshape(bsz, -1, SUBLANE, D_FFP)[:, -1, SUBLANE - (CONV_W - 1):, :D_FF]
        for i, o in ((0, s5_new.reshape(bsz, 2, S5_G, S5_N)), (1, hg_new), (5, conv_new)):
            outs[2 * i].append(o)

        hin = _inproj_call(xs, mods, dl, n1, w_in_p, l)
        padt = lambda a, n: jnp.pad(a.reshape(dbsz, dl, -1), ((0, 0), (0, n - dl), (0, 0))).reshape(dbsz * n, -1)
        u_pad = padt(hin[:, OFF_UA:OFF_UA + D_A], SUBLANE)
        ya, s5_new = _s5_call(u_pad, 0, dbsz, SUBLANE, state_s5[l].reshape(dbsz, 2, S5_W), s5p, dl - 1)
        ya = ya.reshape(dbsz, SUBLANE, D_A)[:, :dl].reshape(ms, D_A)
        hg_pad = padt(hin[:, OFF_HQ:OFF_HQ + 4 * D_B], HG_C)
        yb, hg_new = _hgrn_call(hg_pad, (0, 1, 2, 3), dbsz, HG_C, dl, state_hgrn[l], hgp, hgn)
        yb = yb.reshape(dbsz, HG_C, D_B)[:, :dl].reshape(ms, D_B)
        qo, rc, rs, rw, bc, bs, bw, gates = _nsa_prep_call(hin, rope_s, dl, qn, kn)
        q16 = qo.reshape(dbsz, dl, NSA_KVH, NSA_REP, NSA_DH).transpose(0, 2, 1, 3, 4).reshape(
            dbsz, NSA_KVH, rows, NSA_DH)
        o_c, selt = _nsa_s_sel_call(q16, kc_past, l, past, dl)
        g16 = gates[:, :3 * NSA_HEADS].reshape(dbsz, dl, NSA_KVH, NSA_REP, 3).transpose(0, 2, 1, 3, 4)
        g16 = jnp.pad(g16.reshape(dbsz, NSA_KVH, rows, 3), ((0, 0), (0, 0), (0, 0), (0, LANE - 3)))
        pad8 = lambda r: jnp.pad(r.reshape(dbsz, dl, 2 * N_KV), ((0, 0), (0, SUBLANE - dl), (0, 0)))
        yc = _nsa_s_attn_call(cache_slc4, pt_flat, l, q16, selt, emat, o_c, pad8(rs), pad8(rw), win_prev4,
                              g16, past, dl)
        yc = yc.reshape(dbsz, NSA_KVH, dl, NSA_REP, NSA_DH).transpose(0, 2, 1, 3, 4).reshape(ms, D_C)
        merged = _merge_call(ya, yb, yc, hin, wa, wbr, wc, l)
        xs = _outproj_call(merged, xs, mods, dl, wo, l)
        st = jnp.pad(state_conv[l], ((0, 0), (0, 0), (0, D_FFP - D_FF)))
        tin = jnp.arange(dl)[None, :, None]
        p1 = jnp.where(tin == 0, st[:, 1:2], 0.0).reshape(ms, D_FFP)
        p2 = jnp.where(tin == 0, st[:, 0:1], jnp.where(tin == 1, st[:, 1:2], 0.0)).reshape(ms, D_FFP)
        xs, ua = _ffn_sample_call(xs, mods, dl, n2, p1, p2, fw, l)
        ext = jnp.concatenate([state_conv[l], ua[:, :D_FF].reshape(dbsz, dl, D_FF)], axis=1)
        kv5 = lambda r: r.reshape(dbsz, dl, 2, NSA_KVH, NSA_DH)
        win_new = jnp.concatenate([state_win[l], kv5(rw)], axis=1)[:, -wb:]
        for i, o in enumerate((s5_new.reshape(dbsz, 2, S5_G, S5_N), hg_new, kv5(rc), kv5(rs), win_new,
                               ext[:, dl:])):
            outs[2 * i + 1].append(o)

    st = [jnp.stack(o) if o else None for o in outs]
    rows6 = [r.reshape(depth, bsz, seq, 2, NSA_KVH, NSA_DH) for r in row_bufs]
    st[4], st[6], st[8] = rows6[0], rows6[1], rows6[2][:, :, -min(WINDOW, seq):]
    return (xp.reshape(bsz, seq, D_MODEL), xs.reshape(dbsz, dl, D_MODEL), *st)
```

```python
import functools
import math

import jax
import jax.numpy as jnp
from jax import lax
from jax.experimental import pallas as pl
from jax.experimental.pallas import tpu as pltpu

F32 = jnp.float32
BF16 = jnp.bfloat16
I32 = jnp.int32

D_MODEL = 2048
D_A = D_MODEL // 4
S5_GROUP = 16
S5_G = D_A // S5_GROUP
S5_N = 64
S5_W = S5_G * S5_N
D_B = D_MODEL // 4
HG_DK = 128
HG_DV = 128
HG_HEADS = D_B // HG_DV
D_C = D_MODEL // 2
NSA_DH = 128
NSA_HEADS = D_C // NSA_DH
NSA_KVH = 2
NSA_REP = NSA_HEADS // NSA_KVH
N_KV = NSA_KVH * NSA_DH
ROPE_DIM = NSA_DH // 4
ROPE_THETA = 500000.0
CMP_STRIDE = 16
SEL_BLOCK = 64
N_SEL = 16
WINDOW = 512
FORCE_BONUS = 1000.0
D_FF = ((8 * D_MODEL) // 3 + 127) // 128 * 128
CONV_W = 3
EPS = 1e-6
NEG_INF = -1e30
PAGE = 128

LANE = 128
SUBLANE = 8
VMEM_LIMIT = 56 * 1024 * 1024

OFF_MGT = 0
OFF_NQ = 3 * D_MODEL
OFF_NKV = OFF_NQ + D_C
OFF_UA = OFF_NKV + 6 * N_KV
OFF_HQ = OFF_UA + D_A
OFF_HF = OFF_HQ + D_B
OFF_HI = OFF_HF + D_B
OFF_HGT = OFF_HI + D_B
OFF_NGT = OFF_HGT + D_B
NGT_PAD = 256
N_INP = OFF_NGT + NGT_PAD
TN_IN = 1280
D_FFP = 5632
TF = 512
HG_C = 64
HG_SB = 8


def _cparams(sem):
    return pltpu.CompilerParams(dimension_semantics=sem, vmem_limit_bytes=VMEM_LIMIT)


def _dot(a, b):
    return jnp.dot(a, b, preferred_element_type=F32)


def _dot_nt(a, b):
    return lax.dot_general(a, b, (((1,), (1,)), ((), ())), preferred_element_type=F32)


def _dot_tn(a, b):
    return lax.dot_general(a, b, (((0,), (0,)), ((), ())), preferred_element_type=F32)


def _dot2(a, b):
    hi = a.astype(BF16)
    lo = (a - hi.astype(F32)).astype(BF16)
    return _dot(hi, b) + _dot(lo, b)


def _dot3(a, b):
    hi = a.astype(BF16)
    r1 = a - hi.astype(F32)
    mid = r1.astype(BF16)
    lo = (r1 - mid.astype(F32)).astype(BF16)
    return _dot(b, hi) + _dot(b, mid) + _dot(b, lo)


def _rms(x, w):
    return x * lax.rsqrt(jnp.mean(x * x, axis=-1, keepdims=True) + EPS) * w


def _masked_softmax(s, mask):
    s = jnp.where(mask, s, NEG_INF)
    p = jnp.where(mask, jnp.exp(s - jnp.max(s, axis=-1, keepdims=True)), 0.0)
    return p / jnp.maximum(jnp.sum(p, axis=-1, keepdims=True), 1e-30)


def _row_reduce(x, combine, reduce):
    n = x.shape[-1]
    if n % LANE or n == LANE:
        return reduce(x, axis=-1, keepdims=True)
    acc = x[:, :LANE]
    for k in range(1, n // LANE):
        acc = combine(acc, x[:, k * LANE:(k + 1) * LANE])
    return reduce(acc, axis=-1, keepdims=True)


def _log_sigmoid(x):
    return jnp.minimum(x, 0.0) - jnp.log1p(jnp.exp(-jnp.abs(x)))


def _logaddexp(a, b):
    return jnp.maximum(a, b) + jnp.log1p(jnp.exp(-jnp.abs(a - b)))


def _mod_kernel(c_ref, w_ref, b_ref, o_ref):
    c = c_ref[...]
    a = (c * jax.nn.sigmoid(c)).astype(BF16)
    o_ref[0] = _dot(a, w_ref[0].astype(BF16)) + b_ref[0]


def _mod_call(c_all, w_mod, b_mod):
    depth, d, n = w_mod.shape
    r = c_all.shape[0]
    tn = 1024
    return pl.pallas_call(
        _mod_kernel,
        grid=(depth, n // tn),
        in_specs=[pl.BlockSpec((r, d), lambda l, j: (0, 0)),
                  pl.BlockSpec((1, d, tn), lambda l, j: (l, 0, j)),
                  pl.BlockSpec((1, 1, tn), lambda l, j: (l, 0, j))],
        out_specs=pl.BlockSpec((1, r, tn), lambda l, j: (l, 0, j)),
        out_shape=jax.ShapeDtypeStruct((depth, r, n), F32),
        compiler_params=_cparams(("arbitrary", "arbitrary")),
        name="mod",
    )(c_all, w_mod, b_mod.reshape(depth, 1, n))


def _inproj_kernel(x_ref, sc_ref, sh_ref, nw_ref, w_ref, o_ref, h_scr):
    @pl.when(pl.program_id(1) == 0)
    def _():
        y = _rms(x_ref[...], nw_ref[...])
        h_scr[...] = (y * (1.0 + sc_ref[0]) + sh_ref[0]).astype(BF16)

    o_ref[...] = _dot_nt(h_scr[...], w_ref[0])


def _mod_spec(modx, tm, rows_per_group, col):
    if modx.shape[1] == 1:
        return pl.BlockSpec((1, 1, D_MODEL), lambda i, *_: ((i * tm) // rows_per_group, 0, col))
    return pl.BlockSpec((1, tm, D_MODEL), lambda i, *_: (0, i, col))


def _inproj_call(x, modx, rows_per_group, nw, w, layer):
    m = x.shape[0]
    tm = min(1024, m)
    return pl.pallas_call(
        _inproj_kernel,
        grid=(m // tm, N_INP // TN_IN),
        in_specs=[pl.BlockSpec((tm, D_MODEL), lambda i, j: (i, 0)),
                  _mod_spec(modx, tm, rows_per_group, 1),
                  _mod_spec(modx, tm, rows_per_group, 0),
                  pl.BlockSpec((1, D_MODEL), lambda i, j: (0, 0)),
                  pl.BlockSpec((1, TN_IN, D_MODEL), lambda i, j: (layer, j, 0))],
        out_specs=pl.BlockSpec((tm, TN_IN), lambda i, j: (i, j)),
        out_shape=jax.ShapeDtypeStruct((m, N_INP), F32),
        scratch_shapes=[pltpu.VMEM((tm, D_MODEL), BF16)],
        compiler_params=_cparams(("arbitrary", "arbitrary")),
        name="inproj",
    )(x, modx, modx, nw, w)


S5_LC = 512


def _s5_kernel(u_ref, h0_ref, bbr_ref, bbi_ref, ccr_ref, cci_ref, apr_ref, api_ref, gpr_ref, gpi_ref,
               d_ref, wg_ref, bg_ref, y_ref, hl_ref, hre, him, car, *, t_rows, last_row):
    ts = t_rows // SUBLANE

    @pl.when(pl.program_id(1) == 0)
    def _():
        car[...] = h0_ref[0]

    ri = lax.broadcasted_iota(I32, (t_rows, t_rows), 0)
    ci = lax.broadcasted_iota(I32, (t_rows, t_rows), 1)
    perm = (ci == (ri % SUBLANE) * ts + ri // SUBLANE).astype(BF16)
    unperm = (ci == (ri % ts) * SUBLANE + ri // ts).astype(BF16)
    u = _dot3(u_ref[...], perm)
    ub = u.astype(BF16)
    nblk = D_A // LANE
    for j in range(nblk):
        us, ss = slice(j * LANE, (j + 1) * LANE), slice(j * S5_LC, (j + 1) * S5_LC)
        hre[:, ss] = _dot(ub[:, us], bbr_ref[us, ss])
        him[:, ss] = _dot(ub[:, us], bbi_ref[us, ss])
    row = lax.broadcasted_iota(I32, (SUBLANE, S5_LC), 0)
    zero = jnp.zeros((SUBLANE, S5_LC), F32)
    for c in range(S5_W // S5_LC):
        sl = slice(c * S5_LC, (c + 1) * S5_LC)
        ar, ai = apr_ref[0:SUBLANE, sl], api_ref[0:SUBLANE, sl]

        def local(t, carry, sl=sl, ar=ar, ai=ai):
            hr, hi = carry
            r0 = pl.multiple_of(t * SUBLANE, SUBLANE)
            nr = ar * hr - ai * hi + hre[pl.ds(r0, SUBLANE), sl]
            ni = ar * hi + ai * hr + him[pl.ds(r0, SUBLANE), sl]
            hre[pl.ds(r0, SUBLANE), sl] = nr
            him[pl.ds(r0, SUBLANE), sl] = ni
            return nr, ni

        er, ei = lax.fori_loop(0, ts, local, (zero, zero))
        yr = jnp.where(row == 0, car[0:1, sl], pltpu.roll(er, 1, 0))
        yi = jnp.where(row == 0, car[1:2, sl], pltpu.roll(ei, 1, 0))
        for k, s in enumerate((1, 2, 4)):
            gr, gi = gpr_ref[k:k + 1, sl], gpi_ref[k:k + 1, sl]
            sr = jnp.where(row >= s, pltpu.roll(yr, s, 0), 0.0)
            si = jnp.where(row >= s, pltpu.roll(yi, s, 0), 0.0)
            yr, yi = yr + (gr * sr - gi * si), yi + (gr * si + gi * sr)
        g1r, g1i = gpr_ref[0:1, sl], gpi_ref[0:1, sl]
        car[0:1, sl] = (g1r * yr - g1i * yi + er)[SUBLANE - 1:SUBLANE]
        car[1:2, sl] = (g1r * yi + g1i * yr + ei)[SUBLANE - 1:SUBLANE]

        def fix(t, carry, sl=sl, yr=yr, yi=yi):
            r0 = pl.multiple_of(t * SUBLANE, SUBLANE)
            pr, pi = apr_ref[pl.ds(r0, SUBLANE), sl], api_ref[pl.ds(r0, SUBLANE), sl]
            hre[pl.ds(r0, SUBLANE), sl] += pr * yr - pi * yi
            him[pl.ds(r0, SUBLANE), sl] += pr * yi + pi * yr
            return carry

        lax.fori_loop(0, ts, fix, 0)

    lr = t_rows - SUBLANE + last_row
    hl_ref[0, 0:1, :] = hre[lr:lr + 1, :]
    hl_ref[0, 1:2, :] = him[lr:lr + 1, :]
    y = jnp.concatenate(
        [_dot(hre[:, j * S5_LC:(j + 1) * S5_LC].astype(BF16),
              ccr_ref[j * S5_LC:(j + 1) * S5_LC, j * LANE:(j + 1) * LANE])
         - _dot(him[:, j * S5_LC:(j + 1) * S5_LC].astype(BF16),
                cci_ref[j * S5_LC:(j + 1) * S5_LC, j * LANE:(j + 1) * LANE])
         for j in range(nblk)], axis=1) + d_ref[...] * u
    y = jax.nn.gelu(y)
    z = _dot(y.astype(BF16), wg_ref[...]) + bg_ref[...]
    y_ref[...] = _dot(unperm, (y * jax.nn.sigmoid(z)).astype(BF16)).astype(BF16)


def _s5_call(u_arr, ucol, nb, seq, h0, p, last_row):
    t_rows = min(256, seq)
    nt = seq // t_rows
    const = lambda shape: pl.BlockSpec(shape, lambda b, t: (0,) * len(shape))
    kern = functools.partial(_s5_kernel, t_rows=t_rows, last_row=last_row)
    return pl.pallas_call(
        kern,
        grid=(nb, nt),
        in_specs=[pl.BlockSpec((t_rows, D_A), lambda b, t: (b * nt + t, ucol)),
                  pl.BlockSpec((1, 2, S5_W), lambda b, t: (b, 0, 0)),
                  const((D_A, S5_W)), const((D_A, S5_W)), const((S5_W, D_A)), const((S5_W, D_A)),
                  const((t_rows, S5_W)), const((t_rows, S5_W)),
                  const((SUBLANE, S5_W)), const((SUBLANE, S5_W)), const((1, D_A)),
                  const((D_A, D_A)), const((1, D_A))],
        out_specs=[pl.BlockSpec((t_rows, D_A), lambda b, t: (b * nt + t, 0)),
                   pl.BlockSpec((1, 2, S5_W), lambda b, t: (b, 0, 0))],
        out_shape=[jax.ShapeDtypeStruct((nb * seq, D_A), BF16),
                   jax.ShapeDtypeStruct((nb, 2, S5_W), F32)],
        scratch_shapes=[pltpu.VMEM((t_rows, S5_W), F32), pltpu.VMEM((t_rows, S5_W), F32),
                        pltpu.VMEM((2, S5_W), F32)],
        compiler_params=_cparams(("arbitrary", "arbitrary")),
        name="s5",
    )(u_arr, h0, p["bbr"], p["bbi"], p["ccr"], p["cci"], *p["pow"][t_rows // SUBLANE], p["d"],
      p["wg"], p["bg"])


def _hgrn_kernel(q_ref, f_ref, i_ref, g_ref, s0_ref, lp_ref, nw_ref, y_ref, sl_ref, st, pscr, rscr,
                 *, l_valid, chunks):
    t = pl.program_id(1)
    c = HG_C
    nsb = c // HG_SB
    nh = HG_HEADS

    @pl.when(t == 0)
    def _():
        for h in range(nh):
            st[h] = s0_ref[0, h].T

    row = lax.broadcasted_iota(I32, (c, D_B), 0)
    tri = (lax.broadcasted_iota(I32, (c, c), 0) >= lax.broadcasted_iota(I32, (c, c), 1)).astype(BF16)
    srow = lax.broadcasted_iota(I32, (HG_SB, D_B), 0)
    rows16 = lambda r: jnp.broadcast_to(r, (HG_SB, D_B))

    def chunk(ci, carry):
        c0 = pl.multiple_of(ci * c, c)
        crows = pl.ds(c0, c)
        valid = (t * (chunks * c) + c0 + row) < l_valid
        hf = f_ref[crows, :]
        ls = _log_sigmoid(hf)
        lf = jnp.where(lp_ref[3:4, :] > 0.5, _logaddexp(lp_ref[0:1, :], lp_ref[1:2, :] + ls), ls)
        kk = lp_ref[2:3, :] * jax.nn.sigmoid(-hf)
        lf = jnp.where(valid, lf, 0.0)
        kk = jnp.where(valid, kk, 0.0)
        hq = q_ref[crows, :]
        q = hq * jax.nn.sigmoid(hq)
        v = i_ref[crows, :]
        b = _dot3(lf, tri)
        zero_row = jnp.zeros((1, D_B), F32)
        bref = [zero_row] + [b[i * HG_SB - 1:i * HG_SB] for i in range(1, nsb)]
        bend = [b[(j + 1) * HG_SB - 1:(j + 1) * HG_SB] for j in range(nsb)]
        qq = q * jnp.exp(b - jnp.concatenate([rows16(r) for r in bref], axis=0))
        kks = kk * jnp.exp(jnp.concatenate([rows16(r) for r in bend], axis=0) - b)
        lhs = []
        for j in range(nsb - 1):
            dsel = jnp.concatenate(
                [rows16(jnp.exp(bref[i] - bend[j])) if i > j else jnp.zeros((HG_SB, D_B), F32)
                 for i in range(nsb)], axis=0)
            lhs.append((qq * dsel).astype(BF16))
        rhs = [jnp.where(row // HG_SB == j, kks, 0.0).astype(BF16) for j in range(nsb - 1)]
        for i in range(nsb):
            rs = slice(i * HG_SB, (i + 1) * HG_SB)
            bi, qi, ki = b[rs], q[rs], kk[rs]
            for s0 in range(0, HG_SB, 2):
                pair = []
                for s in (s0, s0 + 1):
                    e = jnp.exp(jnp.where(srow >= s, bi - bi[s:s + 1], NEG_INF))
                    pair.append((qi * ki[s:s + 1]) * e)
                piece = jnp.concatenate(pair, axis=0).astype(BF16)
                for h in range(nh):
                    r0 = ((h * nsb + i) * HG_SB + s0) * HG_SB
                    pscr[r0:r0 + 2 * HG_SB, :] = piece[:, h * HG_DK:(h + 1) * HG_DK]
        rscr[...] = _dot(pscr[...], jnp.ones((HG_DK, HG_DK), BF16))
        qe = (q * jnp.exp(b)).astype(BF16)
        vb = v.astype(BF16)
        bl = b[c - 1:c]
        kdec = (kk * jnp.exp(bl - b)).astype(BF16)
        ebl = jnp.exp(bl)
        gt = g_ref[crows, :]
        gate = gt * jax.nn.sigmoid(gt)
        for h in range(nh):
            hs = slice(h * HG_DK, (h + 1) * HG_DK)
            s_t = st[h]
            att = _dot_nt(jnp.concatenate([x[:, hs] for x in lhs], axis=1),
                          jnp.concatenate([x[:, hs] for x in rhs], axis=1))
            o = _dot(att.astype(BF16), vb[:, hs]) + _dot_nt(qe[:, hs], s_t.astype(BF16))
            diag = []
            for i in range(nsb):
                od = jnp.zeros((HG_SB, HG_DV), F32)
                for s in range(HG_SB):
                    r0 = ((h * nsb + i) * HG_SB + s) * HG_SB
                    od = od + rscr[r0:r0 + HG_SB, :] * v[i * HG_SB + s:i * HG_SB + s + 1, hs]
                diag.append(od)
            o = o + jnp.concatenate(diag, axis=0)
            st[h] = s_t * ebl[:, hs] + _dot_tn(vb[:, hs], kdec[:, hs])
            y_ref[crows, hs] = (_rms(o, nw_ref[...]) * gate[:, hs]).astype(BF16)
        return carry

    lax.fori_loop(0, chunks, chunk, 0)

    @pl.when(t == pl.num_programs(1) - 1)
    def _():
        for h in range(nh):
            sl_ref[0, h] = st[h].T


def _dot2g(g, z):
    hi = z.astype(BF16)
    lo = (z - hi.astype(F32)).astype(BF16)
    return _dot(g, hi) + _dot(g, lo)


def _hgrn_call(arr, cols, nb, seq_pad, l_valid, s0, lp, nw):
    chunks = 4 if seq_pad % (4 * HG_C) == 0 else 1
    rows = chunks * HG_C
    nt = seq_pad // rows
    blk = lambda col: pl.BlockSpec((rows, D_B), lambda b, t: (b * nt + t, col))
    kern = functools.partial(_hgrn_kernel, l_valid=l_valid, chunks=chunks)
    return pl.pallas_call(
        kern,
        grid=(nb, nt),
        in_specs=[blk(cols[0]), blk(cols[1]), blk(cols[2]), blk(cols[3]),
                  pl.BlockSpec((1, HG_HEADS, HG_DK, HG_DV), lambda b, t: (b, 0, 0, 0)),
                  pl.BlockSpec((SUBLANE, D_B), lambda b, t: (0, 0)),
                  pl.BlockSpec((1, HG_DV), lambda b, t: (0, 0))],
        out_specs=[pl.BlockSpec((rows, D_B), lambda b, t: (b * nt + t, 0)),
                   pl.BlockSpec((1, HG_HEADS, HG_DK, HG_DV), lambda b, t: (b, 0, 0, 0))],
        out_shape=[jax.ShapeDtypeStruct((nb * seq_pad, D_B), BF16),
                   jax.ShapeDtypeStruct((nb, HG_HEADS, HG_DK, HG_DV), F32)],
        scratch_shapes=[pltpu.VMEM((HG_HEADS, HG_DV, HG_DK), F32),
                        pltpu.VMEM((HG_HEADS * HG_C * HG_SB, HG_DK), BF16),
                        pltpu.VMEM((HG_HEADS * HG_C * HG_SB, HG_DK), F32)],
        compiler_params=_cparams(("arbitrary", "arbitrary")),
        name="hgrn",
    )(arr, arr, arr, arr, s0, lp, nw)


def _rope(x, cos_t, sin_a, sin_b):
    return (x * cos_t + pltpu.roll(x, NSA_DH - ROPE_DIM // 2, 1) * sin_a
            + pltpu.roll(x, ROPE_DIM // 2, 1) * sin_b)


def _nsa_prep_kernel(q_ref, kc_ref, ks_ref, kw_ref, g_ref, cos_ref, sa_ref, sb_ref, qn_ref, kn_ref,
                     qo_ref, rc_ref, rs_ref, rw_ref, bc_ref, bs_ref, bw_ref, go_ref):
    cos_t, sin_a, sin_b = cos_ref[...], sa_ref[...], sb_ref[...]
    qn = qn_ref[...]
    for h in range(NSA_HEADS):
        hs = slice(h * NSA_DH, (h + 1) * NSA_DH)
        qo_ref[:, hs] = _rope(_rms(q_ref[:, hs], qn), cos_t, sin_a, sin_b).astype(BF16)
    for br, (src, dst, dstb) in enumerate(((kc_ref, rc_ref, bc_ref), (ks_ref, rs_ref, bs_ref),
                                           (kw_ref, rw_ref, bw_ref))):
        kn = kn_ref[br:br + 1, :]
        tm = src.shape[0]
        for g in range(NSA_KVH):
            gs = slice(g * NSA_DH, (g + 1) * NSA_DH)
            k = _rope(_rms(src[:, gs], kn), cos_t, sin_a, sin_b)
            dst[pl.ds(g, tm, stride=KV_TILES), :] = k
            dstb[:, gs] = k.astype(BF16)
            vs = slice(N_KV + g * NSA_DH, N_KV + (g + 1) * NSA_DH)
            v = src[:, vs]
            dst[pl.ds(NSA_KVH + g, tm, stride=KV_TILES), :] = v
            dstb[:, vs] = v.astype(BF16)
    go_ref[...] = jax.nn.sigmoid(g_ref[...])


def _nsa_prep_call(hin, rope_tabs, seq, qn, kn, layer=0, depth=1, row_bufs=None):
    m = hin.shape[0]
    tm = min(512, m)
    npos = rope_tabs[0].shape[0] // tm
    rspec = pl.BlockSpec((tm, NSA_DH), lambda i: (i % npos, 0))
    kvb = OFF_NKV // (2 * N_KV)
    n_in = 10
    extra = () if row_bufs is None else tuple(row_bufs)

    def kern(*refs):
        _nsa_prep_kernel(*refs[:n_in], *refs[n_in + len(extra):])

    return pl.pallas_call(
        kern,
        grid=(m // tm,),
        input_output_aliases={n_in + k: 1 + k for k in range(len(extra))},
        in_specs=[pl.BlockSpec((tm, D_C), lambda i: (i, OFF_NQ // D_C)),
                  pl.BlockSpec((tm, 2 * N_KV), lambda i: (i, kvb)),
                  pl.BlockSpec((tm, 2 * N_KV), lambda i: (i, kvb + 1)),
                  pl.BlockSpec((tm, 2 * N_KV), lambda i: (i, kvb + 2)),
                  pl.BlockSpec((tm, LANE), lambda i: (i, OFF_NGT // LANE)),
                  rspec, rspec, rspec,
                  pl.BlockSpec((1, NSA_DH), lambda i: (0, 0)),
                  pl.BlockSpec((3, NSA_DH), lambda i: (0, 0))]
        + [pl.BlockSpec(memory_space=pl.ANY)] * len(extra),
        out_specs=[pl.BlockSpec((tm, D_C), lambda i: (i, 0))]
        + [pl.BlockSpec((tm * KV_TILES, LANE), lambda i: (layer * (m // tm) + i, 0))] * 3
        + [pl.BlockSpec((tm, 2 * N_KV), lambda i: (i, 0))] * 3
        + [pl.BlockSpec((tm, LANE), lambda i: (i, 0))],
        out_shape=[jax.ShapeDtypeStruct((m, D_C), BF16)]
        + [jax.ShapeDtypeStruct((depth * m * KV_TILES, LANE), F32)] * 3
        + [jax.ShapeDtypeStruct((m, 2 * N_KV), BF16)] * 3
        + [jax.ShapeDtypeStruct((m, LANE), F32)],
        compiler_params=_cparams(("arbitrary",)),
        name="nsa_prep",
    )(hin, hin, hin, hin, hin, *rope_tabs, qn, kn, *extra)


KV_TILES = 2 * N_KV // LANE


def _pool_rows(load, n, a1_ref, a2_ref):
    firsts, seconds = [], []
    for c in range(KV_TILES):
        ls = slice(c * LANE, (c + 1) * LANE)
        first = jnp.zeros((n, LANE), F32)
        second = jnp.zeros((n, LANE), F32)
        for j in range(CMP_STRIDE):
            x = load(j, c)
            first = first + x * a1_ref[j:j + 1, ls]
            second = second + x * a2_ref[j:j + 1, ls]
        firsts.append(first)
        seconds.append(second)
    return firsts, seconds


def _tile_rows(j, c, n):
    return pl.ds(j * KV_TILES + c, n, stride=CMP_STRIDE * KV_TILES)


def _cmp_mlp(first, second, n, w1_ref, b1_ref, w2_ref, o_ref):
    for c in range(2):
        for g in range(NSA_KVH):
            tile = 2 * c + g
            pooled = first[tile] + pltpu.roll(second[tile], n - 1, 0)
            h = _dot(pooled.astype(BF16), w1_ref[c]) + b1_ref[c:c + 1, :]
            o_ref[0, :, tile * NSA_DH:(tile + 1) * NSA_DH] = _dot(
                jax.nn.gelu(h).astype(BF16), w2_ref[c]).astype(BF16)


def _cmp_prompt_kernel(x_ref, a1_ref, a2_ref, w1_ref, b1_ref, w2_ref, o_ref, *, n):
    first, second = _pool_rows(lambda j, c: x_ref[_tile_rows(j, c, n), :], n, a1_ref, a2_ref)
    _cmp_mlp(first, second, n, w1_ref, b1_ref, w2_ref, o_ref)


def _cmp_prompt_call(rows, nb, seq, cp, layer=0):
    n = seq // CMP_STRIDE
    const = lambda shape: pl.BlockSpec(shape, lambda b: (0,) * len(shape))
    return pl.pallas_call(
        functools.partial(_cmp_prompt_kernel, n=n),
        grid=(nb,),
        in_specs=[pl.BlockSpec((seq * KV_TILES, LANE), lambda b: (layer * nb + b, 0)),
                  const((CMP_STRIDE, 2 * N_KV)), const((CMP_STRIDE, 2 * N_KV)),
                  const((2, NSA_DH, NSA_DH)), const((2, NSA_DH)), const((2, NSA_DH, NSA_DH))],
        out_specs=pl.BlockSpec((1, n, 2 * N_KV), lambda b: (b, 0, 0)),
        out_shape=jax.ShapeDtypeStruct((nb, n, 2 * N_KV), BF16),
        compiler_params=_cparams(("arbitrary",)),
        name="cmp_prompt",
    )(rows.reshape(-1, LANE), cp["a1"], cp["a2"], cp["w1"], cp["b1"], cp["w2"])


PG = 16
SUB_PER_PAGE = PAGE // CMP_STRIDE


SUB_ROWS = CMP_STRIDE * KV_TILES


def _cmp_pool_cache_kernel(pt_ref, *refs):
    pages = refs[:PG]
    a1_ref, a2_ref, f_ref, s_ref = refs[PG:]
    nv = SUB_ROWS // SUBLANE
    w1 = [a1_ref[0, v * SUBLANE:(v + 1) * SUBLANE, :] for v in range(nv)]
    w2 = [a2_ref[0, v * SUBLANE:(v + 1) * SUBLANE, :] for v in range(nv)]
    low = lax.broadcasted_iota(I32, (SUBLANE, LANE), 0) < KV_TILES

    def fold(acc):
        return acc + pltpu.roll(acc, KV_TILES, 0)

    for k in range(PG):
        for n2 in range(SUB_PER_PAGE // 2):
            halves = []
            for n in (2 * n2, 2 * n2 + 1):
                a1 = a2 = None
                for v in range(nv):
                    x = pages[k][0, 0, n * SUB_ROWS + v * SUBLANE:n * SUB_ROWS + (v + 1) * SUBLANE, :]
                    a1 = x * w1[v] if a1 is None else a1 + x * w1[v]
                    a2 = x * w2[v] if a2 is None else a2 + x * w2[v]
                halves.append((fold(a1), fold(a2)))
            r0 = (k * SUB_PER_PAGE + 2 * n2) * KV_TILES
            f_ref[0, 0, r0:r0 + SUBLANE, :] = jnp.where(low, halves[0][0], halves[1][0])
            s_ref[0, 0, r0:r0 + SUBLANE, :] = jnp.where(low, halves[0][1], halves[1][1])


def _page_specs(n_pages, layer_of, batch_of, group_of, page_shape=(PAGE, 2 * N_KV)):
    def spec(k):
        return pl.BlockSpec(
            (1, 1) + page_shape,
            lambda *a, k=k: (layer_of(*a), a[-1][batch_of(*a) * n_pages + group_of(*a) * PG + k], 0, 0))
    return [spec(k) for k in range(PG)]


def _cmp_pool_cache_call(cache, pt_flat, nb, n_pages, a1, a2):
    depth = cache.shape[0]
    npg = n_pages // PG
    nsub = n_pages * SUB_PER_PAGE
    rows = PG * SUB_PER_PAGE
    grid_spec = pltpu.PrefetchScalarGridSpec(
        num_scalar_prefetch=1,
        grid=(depth, nb, npg),
        in_specs=_page_specs(n_pages, lambda l, b, g, pt: l, lambda l, b, g, pt: b,
                             lambda l, b, g, pt: g, (PAGE * KV_TILES, LANE))
        + [pl.BlockSpec((1, SUB_ROWS, LANE), lambda l, b, g, pt: (l, 0, 0))] * 2,
        out_specs=[pl.BlockSpec((1, 1, rows * KV_TILES, LANE), lambda l, b, g, pt: (l, b, g, 0))] * 2,
    )
    return pl.pallas_call(
        _cmp_pool_cache_kernel,
        grid_spec=grid_spec,
        out_shape=[jax.ShapeDtypeStruct((depth, nb, nsub * KV_TILES, LANE), F32)] * 2,
        compiler_params=_cparams(("arbitrary", "arbitrary", "arbitrary")),
        name="cmp_pool_cache",
    )(pt_flat, *([cache] * PG), a1, a2)


def _cmp_mlp_cache_kernel(f_ref, s_ref, w1_ref, b1_ref, w2_ref, o_ref, *, n):
    tiles = lambda ref: [ref[0, 0, pl.ds(c, n, stride=KV_TILES), :] for c in range(KV_TILES)]
    _cmp_mlp(tiles(f_ref), tiles(s_ref), n, w1_ref.at[0], b1_ref.at[0], w2_ref.at[0], o_ref.at[0])


def _cmp_mlp_cache_call(first, second, w1, b1, w2):
    depth, nb, rows, _ = first.shape
    n = rows // KV_TILES
    blk = pl.BlockSpec((1, 1, n, 2 * N_KV), lambda l, b: (l, b, 0, 0))
    tblk = pl.BlockSpec((1, 1, rows, LANE), lambda l, b: (l, b, 0, 0))
    return pl.pallas_call(
        functools.partial(_cmp_mlp_cache_kernel, n=n),
        grid=(depth, nb),
        in_specs=[tblk, tblk,
                  pl.BlockSpec((1, 2, NSA_DH, NSA_DH), lambda l, b: (l, 0, 0, 0)),
                  pl.BlockSpec((1, 2, NSA_DH), lambda l, b: (l, 0, 0)),
                  pl.BlockSpec((1, 2, NSA_DH, NSA_DH), lambda l, b: (l, 0, 0, 0))],
        out_specs=blk,
        out_shape=jax.ShapeDtypeStruct((depth, nb, n, 2 * N_KV), BF16),
        compiler_params=_cparams(("arbitrary", "arbitrary")),
        name="cmp_mlp_cache",
    )(first, second, w1, b1, w2)


def _block_scores(imp, qpos, n_slc):
    blk = lax.broadcasted_iota(I32, imp.shape, 1)
    cur = qpos // SEL_BLOCK
    forced = (blk == 0) | (blk == cur) | (blk == cur - 1)
    valid = blk * SEL_BLOCK <= qpos
    score = jnp.where(valid, imp + jnp.where(forced, FORCE_BONUS, 0.0), -1.0)
    return jnp.where(blk < n_slc, score, -2.0)


def _select_blocks_ranked(imp, qpos, n_slc):
    nb8 = (n_slc + SUBLANE - 1) // SUBLANE * SUBLANE
    st = _block_scores(imp, qpos, n_slc).T[:nb8]
    ridx = lax.broadcasted_iota(I32, st.shape, 0)
    cnt = jnp.zeros(st.shape, F32)
    for i in range(n_slc):
        ri = st[i:i + 1]
        cnt = cnt + jnp.where(ri > st, 1.0, jnp.where(ri == st, jnp.where(ridx > i, 1.0, 0.0), 0.0))
    sel_t = jnp.where((cnt < min(N_SEL, n_slc)) & (ridx < n_slc), 1.0, 0.0)
    if nb8 < LANE:
        sel_t = jnp.concatenate([sel_t, jnp.zeros((LANE - nb8, st.shape[1]), F32)], axis=0)
    return sel_t.T


def _select_blocks(imp, qpos, n_slc):
    blk = lax.broadcasted_iota(I32, imp.shape, 1)
    score = _block_scores(imp, qpos, n_slc)
    sel = jnp.zeros(imp.shape, F32)
    for _ in range(min(N_SEL, n_slc)):
        mx = jnp.max(score, axis=-1, keepdims=True)
        idx = jnp.min(jnp.where(score == mx, blk, 1 << 30), axis=-1, keepdims=True)
        hit = blk == idx
        sel = jnp.where(hit, 1.0, sel)
        score = jnp.where(hit, -3.0, score)
    return sel


def _cmp_branch(q, kck, kcv, qpos, n_cmp):
    s = _dot_nt(q, kck) * (NSA_DH ** -0.5)
    col = lax.broadcasted_iota(I32, s.shape, 1)
    mask = (col * CMP_STRIDE + (2 * CMP_STRIDE - 1) <= qpos) & (col < n_cmp)
    p = _masked_softmax(s, mask)
    return p, _dot(p.astype(BF16), kcv)


NSA_KC = 1024


def _nsa_prompt_kernel(q_ref, kc_ref, ks_ref, vs_ref, kw_ref, vw_ref, g_ref, o_ref,
                       bias_scr, m_scr, l_scr, acc_scr, *, tq, seq, slab):
    q0 = pl.program_id(1) * tq
    rows = NSA_REP * tq
    ncp = seq // CMP_STRIDE
    n_slc = seq // SEL_BLOCK
    scale = NSA_DH ** -0.5
    gates = g_ref[...]
    q4s = [jnp.concatenate([q_ref[:, (g * NSA_REP + r) * NSA_DH:(g * NSA_REP + r + 1) * NSA_DH]
                            for r in range(NSA_REP)], axis=0) for g in range(NSA_KVH)]
    qpos_c = q0 + (lax.broadcasted_iota(I32, (rows, ncp), 0) & (tq - 1))
    o_cs, psums = [], []
    for g in range(NSA_KVH):
        p_c, o_c = _cmp_branch(q4s[g], kc_ref[0, :, g * NSA_DH:(g + 1) * NSA_DH],
                               kc_ref[0, :, N_KV + g * NSA_DH:N_KV + (g + 1) * NSA_DH], qpos_c, ncp - 1)
        psum = p_c[0:tq]
        for r in range(1, NSA_REP):
            psum = psum + p_c[r * tq:(r + 1) * tq]
        o_cs.append(o_c)
        psums.append(psum)

    def ranked():
        gmat = (lax.broadcasted_iota(I32, (ncp, LANE), 0) // (SEL_BLOCK // CMP_STRIDE)
                == lax.broadcasted_iota(I32, (ncp, LANE), 1)).astype(BF16)
        imp = _dot2(jnp.concatenate(psums, axis=0), gmat)
        qp = q0 + (lax.broadcasted_iota(I32, (NSA_KVH * tq, LANE), 0) & (tq - 1))
        return _select_blocks_ranked(imp, qp, n_slc)

    sel = lax.cond(q0 + tq > N_SEL * SEL_BLOCK, ranked,
                   lambda: jnp.ones((NSA_KVH * tq, LANE), F32))
    emat = (lax.broadcasted_iota(I32, (LANE, seq), 1) // SEL_BLOCK
            == lax.broadcasted_iota(I32, (LANE, seq), 0)).astype(BF16)
    nk = (q0 + tq + NSA_KC - 1) // NSA_KC
    s0 = pl.multiple_of(jnp.maximum(q0 + tq - slab, 0), SUBLANE)
    key = lax.broadcasted_iota(I32, (tq, seq), 1)
    causal = key <= q0 + lax.broadcasted_iota(I32, (tq, seq), 0)
    wp = s0 + lax.broadcasted_iota(I32, (tq, slab), 1)
    qpw = q0 + lax.broadcasted_iota(I32, (tq, slab), 0)
    wb1 = jnp.where((wp <= qpw) & (wp > qpw - WINDOW), 0.0, NEG_INF)

    def add_bias(s, bias):
        return jnp.concatenate([s[r * tq:(r + 1) * tq] + bias for r in range(NSA_REP)], axis=0)

    for g in range(NSA_KVH):
        keep = (_dot(sel[g * tq:(g + 1) * tq].astype(BF16), emat) > 0.5) & causal
        bias = jnp.where(keep, 0.0, NEG_INF)
        for c in range(seq // NSA_KC):
            bias_scr[g, c] = bias[:, c * NSA_KC:(c + 1) * NSA_KC]
    m_scr[...] = jnp.full(m_scr.shape, NEG_INF, F32)
    l_scr[...] = jnp.zeros(l_scr.shape, F32)
    acc_scr[...] = jnp.zeros(acc_scr.shape, F32)

    def body(c, carry):
        k0 = pl.multiple_of(c * NSA_KC, NSA_KC)
        for g in range(NSA_KVH):
            gs = slice(g * NSA_DH, (g + 1) * NSA_DH)
            s = add_bias(_dot_nt(q4s[g], ks_ref[pl.ds(k0, NSA_KC), gs]) * scale, bias_scr[g, c])
            m_old = m_scr[g]
            m_new = jnp.maximum(m_old, _row_reduce(s, jnp.maximum, jnp.max))
            p = jnp.exp(s - m_new)
            alpha = jnp.exp(m_old - m_new)
            l_scr[g] = alpha * l_scr[g] + _row_reduce(p, jnp.add, jnp.sum)
            acc_scr[g] = alpha * acc_scr[g] + _dot(p.astype(BF16), vs_ref[pl.ds(k0, NSA_KC), gs])
            m_scr[g] = m_new
        return carry

    lax.fori_loop(0, nk, body, 0)
    for g in range(NSA_KVH):
        gs = slice(g * NSA_DH, (g + 1) * NSA_DH)
        q4 = q4s[g]
        o_s = acc_scr[g] / l_scr[g]
        s_w = add_bias(_dot_nt(q4, kw_ref[pl.ds(s0, slab), gs]) * scale, wb1)
        p_w = jnp.exp(s_w - _row_reduce(s_w, jnp.maximum, jnp.max))
        o_w = (_dot(p_w.astype(BF16), vw_ref[pl.ds(s0, slab), gs])
               / _row_reduce(p_w, jnp.add, jnp.sum))
        o_c = o_cs[g]
        for r in range(NSA_REP):
            h = g * NSA_REP + r
            rs = slice(r * tq, (r + 1) * tq)
            o = (gates[:, 3 * h:3 * h + 1] * o_c[rs] + gates[:, 3 * h + 1:3 * h + 2] * o_s[rs]
                 + gates[:, 3 * h + 2:3 * h + 3] * o_w[rs])
            o_ref[:, h * NSA_DH:(h + 1) * NSA_DH] = o.astype(BF16)


def _nsa_prompt_call(qn, kc, bs, bw, gates, nb, seq):
    tq = 128
    nq = seq // tq
    slab = min(seq, WINDOW + tq)
    kv = lambda col: pl.BlockSpec((seq, N_KV), lambda b, i: (b, col))
    kern = functools.partial(_nsa_prompt_kernel, tq=tq, seq=seq, slab=slab)
    rows = NSA_REP * tq
    return pl.pallas_call(
        kern,
        grid=(nb, nq),
        in_specs=[pl.BlockSpec((tq, D_C), lambda b, i: (b * nq + i, 0)),
                  pl.BlockSpec((1, seq // CMP_STRIDE, 2 * N_KV), lambda b, i: (b, 0, 0)),
                  kv(0), kv(1), kv(0), kv(1),
                  pl.BlockSpec((tq, LANE), lambda b, i: (b * nq + i, 0))],
        out_specs=pl.BlockSpec((tq, D_C), lambda b, i: (b * nq + i, 0)),
        out_shape=jax.ShapeDtypeStruct((nb * seq, D_C), BF16),
        scratch_shapes=[pltpu.VMEM((NSA_KVH, seq // NSA_KC, tq, NSA_KC), F32),
                        pltpu.VMEM((NSA_KVH, rows, 1), F32), pltpu.VMEM((NSA_KVH, rows, 1), F32),
                        pltpu.VMEM((NSA_KVH, rows, NSA_DH), F32)],
        compiler_params=_cparams(("arbitrary", "arbitrary")),
        name="nsa_prompt",
    )(qn, kc, bs, bs, bw, bw, gates)


def _nsa_s_sel_kernel(q_ref, kc_ref, oc_ref, sel_ref, *, past, dl, n_tiles):
    rows = NSA_REP * dl
    ncp = past // CMP_STRIDE
    n_slc = (past + dl + SEL_BLOCK - 1) // SEL_BLOCK
    nbp = (n_slc + LANE - 1) // LANE * LANE
    per_tile = PG * PAGE // SEL_BLOCK
    gmat = (lax.broadcasted_iota(I32, (ncp, nbp), 0) // (SEL_BLOCK // CMP_STRIDE)
            == lax.broadcasted_iota(I32, (ncp, nbp), 1)).astype(BF16)
    rmat = (lax.broadcasted_iota(I32, (rows, rows), 0) // NSA_REP
            == lax.broadcasted_iota(I32, (rows, rows), 1) // NSA_REP).astype(BF16)

    def qpos_of(shape):
        return past + lax.broadcasted_iota(I32, shape, 0) // NSA_REP

    lane = lax.broadcasted_iota(I32, (rows, LANE), 1)
    imps = []
    for g in range(NSA_KVH):
        gs = slice(g * NSA_DH, (g + 1) * NSA_DH)
        p_c, o_c = _cmp_branch(q_ref[0, g], kc_ref[0, 0, :, gs],
                               kc_ref[0, 0, :, N_KV + g * NSA_DH:N_KV + (g + 1) * NSA_DH],
                               qpos_of((rows, ncp)), ncp - 1)
        oc_ref[0, g] = o_c
        psum = _dot2g(rmat, p_c)
        imps.append(_dot2(psum, gmat))
    qp = past + (lax.broadcasted_iota(I32, (NSA_KVH * rows, nbp), 0) % rows) // NSA_REP
    sel_all = _select_blocks(jnp.concatenate(imps, axis=0), qp, n_slc)
    for g in range(NSA_KVH):
        sel = sel_all[g * rows:(g + 1) * rows]
        for tl in range(n_tiles):
            lo = tl * per_tile
            chunk = sel[:, (lo // LANE) * LANE:(lo // LANE + 1) * LANE]
            sh = (LANE - lo % LANE) % LANE
            if sh:
                chunk = pltpu.roll(chunk, sh, 1)
            sel_ref[0, g, tl] = jnp.where(lane < per_tile, chunk, 0.0)


def _nsa_s_sel_call(q16, kc, layer, past, dl):
    nb = q16.shape[0]
    rows = NSA_REP * dl
    n_tiles = past // (PG * PAGE) + 1
    kern = functools.partial(_nsa_s_sel_kernel, past=past, dl=dl, n_tiles=n_tiles)
    return pl.pallas_call(
        kern,
        grid=(nb,),
        in_specs=[pl.BlockSpec((1, NSA_KVH, rows, NSA_DH), lambda b: (b, 0, 0, 0)),
                  pl.BlockSpec((1, 1, past // CMP_STRIDE, 2 * N_KV), lambda b: (layer, b, 0, 0))],
        out_specs=[pl.BlockSpec((1, NSA_KVH, rows, NSA_DH), lambda b: (b, 0, 0, 0)),
                   pl.BlockSpec((1, NSA_KVH, n_tiles, rows, LANE), lambda b: (b, 0, 0, 0, 0))],
        out_shape=[jax.ShapeDtypeStruct((nb, NSA_KVH, rows, NSA_DH), F32),
                   jax.ShapeDtypeStruct((nb, NSA_KVH, n_tiles, rows, LANE), F32)],
        compiler_params=_cparams(("arbitrary",)),
        name="nsa_sample_select",
    )(q16, kc)


def _nsa_s_attn_kernel(pt_ref, *refs, past, dl, npg):
    pages = refs[:PG]
    (q_ref, selt_ref, sell_ref, emat_ref, oc_ref, ns_ref, nw_ref, wp_ref, g_ref,
     o_ref, m_scr, l_scr, a_scr) = refs[PG:]
    pg = pl.program_id(1)
    rows = NSA_REP * dl
    scale = NSA_DH ** -0.5
    wb = wp_ref.shape[2] // KV_TILES

    def tok_rows(ref, c, n):
        return ref[0, 0, pl.ds(c, n, stride=KV_TILES), :].astype(BF16)

    @pl.when(pg == 0)
    def _():
        m_scr[...] = jnp.full(m_scr.shape, NEG_INF, F32)
        l_scr[...] = jnp.zeros(l_scr.shape, F32)
        a_scr[...] = jnp.zeros(a_scr.shape, F32)

    for g in range(NSA_KVH):
        q = q_ref[0, g]
        kmask = _dot(selt_ref[0, g, 0].astype(BF16), emat_ref[...]) > 0.5
        k_all = jnp.concatenate([tok_rows(pages[k], g, PAGE) for k in range(PG)], axis=0)
        v_all = jnp.concatenate([tok_rows(pages[k], NSA_KVH + g, PAGE) for k in range(PG)], axis=0)
        s = jnp.where(kmask, _dot_nt(q, k_all) * scale, NEG_INF)
        m_old = m_scr[g]
        m_new = jnp.maximum(m_old, jnp.max(s, axis=-1, keepdims=True))
        p = jnp.where(kmask, jnp.exp(s - m_new), 0.0)
        alpha = jnp.exp(m_old - m_new)
        l_scr[g] = alpha * l_scr[g] + jnp.sum(p, axis=-1, keepdims=True)
        a_scr[g] = alpha * a_scr[g] + _dot(p.astype(BF16), v_all)
        m_scr[g] = m_new

    @pl.when(pg == npg - 1)
    def _():
        gates = g_ref[0]
        tok = lax.broadcasted_iota(I32, (rows, SUBLANE), 0) // NSA_REP
        ncol = lax.broadcasted_iota(I32, (rows, SUBLANE), 1)
        for g in range(NSA_KVH):
            gs = slice(g * NSA_DH, (g + 1) * NSA_DH)
            vsl = slice(N_KV + g * NSA_DH, N_KV + (g + 1) * NSA_DH)
            q = q_ref[0, g]
            nmask = (sell_ref[0, g, 0][:, 0:1] > 0.5) & (ncol <= tok) & (ncol < dl)
            s_n = jnp.where(nmask, _dot_nt(q, ns_ref[0, :, gs].astype(BF16)) * scale, NEG_INF)
            m_old = m_scr[g]
            m_new = jnp.maximum(m_old, jnp.max(s_n, axis=-1, keepdims=True))
            p_n = jnp.where(nmask, jnp.exp(s_n - m_new), 0.0)
            alpha = jnp.exp(m_old - m_new)
            lsum = alpha * l_scr[g] + jnp.sum(p_n, axis=-1, keepdims=True)
            acc = alpha * a_scr[g] + _dot(p_n.astype(BF16), ns_ref[0, :, vsl].astype(BF16))
            o_s = acc / jnp.maximum(lsum, 1e-30)
            wrow = lax.broadcasted_iota(I32, (rows, wb), 1)
            wtok = lax.broadcasted_iota(I32, (rows, wb), 0) // NSA_REP
            pmask = (wrow > wtok + (wb - WINDOW)) & (wrow + (past - wb) >= 0)
            s_p = jnp.where(pmask, _dot_nt(q, tok_rows(wp_ref, g, wb)) * scale, NEG_INF)
            wmask = (ncol <= tok) & (ncol < dl)
            s_q = jnp.where(wmask, _dot_nt(q, nw_ref[0, :, gs].astype(BF16)) * scale, NEG_INF)
            mw = jnp.maximum(jnp.max(s_p, axis=-1, keepdims=True), jnp.max(s_q, axis=-1, keepdims=True))
            p_p = jnp.where(pmask, jnp.exp(s_p - mw), 0.0)
            p_q = jnp.where(wmask, jnp.exp(s_q - mw), 0.0)
            den = jnp.maximum(jnp.sum(p_p, axis=-1, keepdims=True) + jnp.sum(p_q, axis=-1, keepdims=True),
                              1e-30)
            o_w = (_dot(p_p.astype(BF16), tok_rows(wp_ref, NSA_KVH + g, wb))
                   + _dot(p_q.astype(BF16), nw_ref[0, :, vsl].astype(BF16))) / den
            gg = gates[g]
            o = gg[:, 0:1] * oc_ref[0, g] + gg[:, 1:2] * o_s + gg[:, 2:3] * o_w
            o_ref[0, g] = o.astype(BF16)


def _nsa_s_attn_call(cache, pt_flat, layer, q16, selt, emat, o_c, new_slc, new_win, win_prev, gates16,
                     past, dl):
    nb = q16.shape[0]
    rows = NSA_REP * dl
    n_pages = past // PAGE
    npg = n_pages // PG
    wrows = win_prev.shape[2]
    b4 = lambda shape: pl.BlockSpec((1,) + shape, lambda b, g, pt: (b,) + (0,) * len(shape))
    grid_spec = pltpu.PrefetchScalarGridSpec(
        num_scalar_prefetch=1,
        grid=(nb, npg),
        in_specs=_page_specs(n_pages, lambda b, g, pt: layer, lambda b, g, pt: b, lambda b, g, pt: g,
                             (PAGE * KV_TILES, LANE))
        + [b4((NSA_KVH, rows, NSA_DH)),
           pl.BlockSpec((1, NSA_KVH, 1, rows, LANE), lambda b, g, pt: (b, 0, g, 0, 0)),
           pl.BlockSpec((1, NSA_KVH, 1, rows, LANE), lambda b, g, pt: (b, 0, npg, 0, 0)),
           pl.BlockSpec((LANE, PG * PAGE), lambda b, g, pt: (0, 0)),
           b4((NSA_KVH, rows, NSA_DH)),
           b4((SUBLANE, 2 * N_KV)), b4((SUBLANE, 2 * N_KV)),
           pl.BlockSpec((1, 1, wrows, LANE), lambda b, g, pt: (layer, b, 0, 0)),
           b4((NSA_KVH, rows, LANE))],
        out_specs=b4((NSA_KVH, rows, NSA_DH)),
        scratch_shapes=[pltpu.VMEM((NSA_KVH, rows, 1), F32), pltpu.VMEM((NSA_KVH, rows, 1), F32),
                        pltpu.VMEM((NSA_KVH, rows, NSA_DH), F32)],
    )
    kern = functools.partial(_nsa_s_attn_kernel, past=past, dl=dl, npg=npg)
    return pl.pallas_call(
        kern,
        grid_spec=grid_spec,
        out_shape=jax.ShapeDtypeStruct((nb, NSA_KVH, rows, NSA_DH), BF16),
        compiler_params=_cparams(("arbitrary", "arbitrary")),
        name="nsa_sample_attn",
    )(pt_flat, *([cache] * PG), q16, selt, selt, emat, o_c, new_slc, new_win, win_prev, gates16)


def _merge_kernel(ya_ref, yb_ref, yc_ref, ga_ref, gb_ref, gc_ref, wa_ref, wb_ref, wc_ref, o_ref):
    m = (jax.nn.sigmoid(ga_ref[...]) * _dot(ya_ref[...], wa_ref[0])
         + jax.nn.sigmoid(gb_ref[...]) * _dot(yb_ref[...], wb_ref[0])
         + jax.nn.sigmoid(gc_ref[...]) * _dot(yc_ref[...], wc_ref[0]))
    o_ref[...] = m.astype(BF16)


def _merge_call(ya, yb, yc, hin, wa, wb, wc, layer):
    m = ya.shape[0]
    tm = min(512, m)
    const = lambda shape: pl.BlockSpec((1,) + shape, lambda i: (layer, 0, 0))
    gate = lambda col: pl.BlockSpec((tm, D_MODEL), lambda i: (i, col))
    return pl.pallas_call(
        _merge_kernel,
        grid=(m // tm,),
        in_specs=[pl.BlockSpec((tm, D_A), lambda i: (i, 0)), pl.BlockSpec((tm, D_B), lambda i: (i, 0)),
                  pl.BlockSpec((tm, D_C), lambda i: (i, 0)), gate(0), gate(1), gate(2),
                  const((D_A, D_MODEL)), const((D_B, D_MODEL)), const((D_C, D_MODEL))],
        out_specs=pl.BlockSpec((tm, D_MODEL), lambda i: (i, 0)),
        out_shape=jax.ShapeDtypeStruct((m, D_MODEL), BF16),
        compiler_params=_cparams(("arbitrary",)),
        name="merge",
    )(ya, yb, yc, hin, hin, hin, wa, wb, wc)


def _outproj_kernel(m_ref, x_ref, g_ref, w_ref, o_ref):
    o_ref[...] = x_ref[...] + g_ref[0] * _dot(m_ref[...], w_ref[0])


def _outproj_call(merged, x, modx, rows_per_group, w, layer):
    m = x.shape[0]
    tm = min(512, m)
    return pl.pallas_call(
        _outproj_kernel,
        grid=(m // tm,),
        in_specs=[pl.BlockSpec((tm, D_MODEL), lambda i: (i, 0)),
                  pl.BlockSpec((tm, D_MODEL), lambda i: (i, 0)),
                  _mod_spec(modx, tm, rows_per_group, 2),
                  pl.BlockSpec((1, D_MODEL, D_MODEL), lambda i: (layer, 0, 0))],
        out_specs=pl.BlockSpec((tm, D_MODEL), lambda i: (i, 0)),
        out_shape=jax.ShapeDtypeStruct((m, D_MODEL), F32),
        compiler_params=_cparams(("arbitrary",)),
        name="outproj",
    )(merged, x, modx, w)


def _ffn_tail(ua, ua1, ua2, ub, cw_ref, cb_ref, wd_ref):
    cw = cw_ref[0]
    conv = cb_ref[0] + cw[0:1] * ua2 + cw[1:2] * ua1 + cw[2:3] * ua
    return _dot((jax.nn.gelu(conv) * ub).astype(BF16), wd_ref[0])


FFN_HALO = 16


def _ffn_prompt_kernel(x_ref, xh_ref, sc_ref, sh_ref, g_ref, nw_ref, cp_ref, wua_ref, wub_ref, cw_ref,
                       cb_ref, wd_ref, o_ref, tail_ref, h_scr, *, tm, seq):
    f = pl.program_id(1)

    @pl.when(f == 0)
    def _():
        nw, sc, sh = nw_ref[...], sc_ref[0], sh_ref[0]
        h_scr[0:FFN_HALO, :] = (_rms(xh_ref[...], nw) * (1.0 + sc) + sh).astype(BF16)
        h_scr[FFN_HALO:, :] = (_rms(x_ref[...], nw) * (1.0 + sc) + sh).astype(BF16)
        o_ref[...] = jnp.zeros(o_ref.shape, F32)

    ua_all = _dot(h_scr[...], wua_ref[0])
    ua = ua_all[FFN_HALO:]
    ub = _dot(h_scr[FFN_HALO:, :], wub_ref[0])
    at_start = (pl.program_id(0) * tm) % seq == 0
    prev = jnp.where(at_start, cp_ref[0], ua_all[FFN_HALO - SUBLANE:FFN_HALO])
    row = lax.broadcasted_iota(I32, ua.shape, 0)
    p1, p2 = prev[SUBLANE - 1:SUBLANE], prev[SUBLANE - 2:SUBLANE - 1]
    ua1 = jnp.where(row == 0, p1, pltpu.roll(ua, 1, 0))
    ua2 = jnp.where(row == 0, p2, jnp.where(row == 1, p1, pltpu.roll(ua, 2, 0)))
    tail_ref[0] = ua[tm - SUBLANE:tm]
    o_ref[...] += _ffn_tail(ua, ua1, ua2, ub, cw_ref, cb_ref, wd_ref)

    @pl.when(f == pl.num_programs(1) - 1)
    def _():
        o_ref[...] = x_ref[...] + g_ref[0] * o_ref[...]


def _ffn_prompt_call(x, modx, seq, nw, cprev, fw, layer):
    m = x.shape[0]
    tm = min(1024, m)
    hb = tm // FFN_HALO
    kern = functools.partial(_ffn_prompt_kernel, tm=tm, seq=seq)
    return pl.pallas_call(
        kern,
        grid=(m // tm, D_FFP // TF),
        in_specs=[pl.BlockSpec((tm, D_MODEL), lambda i, f: (i, 0), pipeline_mode=pl.Buffered(1)),
                  pl.BlockSpec((FFN_HALO, D_MODEL), lambda i, f: (jnp.maximum(i * hb - 1, 0), 0)),
                  _mod_spec(modx, tm, seq, 4), _mod_spec(modx, tm, seq, 3), _mod_spec(modx, tm, seq, 5),
                  pl.BlockSpec((1, D_MODEL), lambda i, f: (0, 0)),
                  pl.BlockSpec((1, SUBLANE, TF), lambda i, f: ((i * tm) // seq, 0, f)),
                  pl.BlockSpec((1, D_MODEL, TF), lambda i, f: (layer, 0, f)),
                  pl.BlockSpec((1, D_MODEL, TF), lambda i, f: (layer, 0, D_FFP // TF + f)),
                  pl.BlockSpec((1, CONV_W, TF), lambda i, f: (layer, 0, f)),
                  pl.BlockSpec((1, 1, TF), lambda i, f: (layer, 0, f)),
                  pl.BlockSpec((1, TF, D_MODEL), lambda i, f: (layer, f, 0))],
        out_specs=[pl.BlockSpec((tm, D_MODEL), lambda i, f: (i, 0)),
                   pl.BlockSpec((1, SUBLANE, TF), lambda i, f: (i, 0, f))],
        out_shape=[jax.ShapeDtypeStruct((m, D_MODEL), F32),
                   jax.ShapeDtypeStruct((m // tm, SUBLANE, D_FFP), F32)],
        scratch_shapes=[pltpu.VMEM((tm + FFN_HALO, D_MODEL), BF16)],
        compiler_params=_cparams(("arbitrary", "arbitrary")),
        name="ffn_prompt",
    )(x, x, modx, modx, modx, nw, cprev, fw["wu"], fw["wu"], fw["cw"], fw["cb"], fw["wd"])


def _ffn_sample_kernel(x_ref, sc_ref, sh_ref, g_ref, nw_ref, p1_ref, p2_ref, wua_ref, wub_ref, cw_ref,
                       cb_ref, wd_ref, o_ref, ua_ref, h_scr, acc, *, dl):
    f = pl.program_id(0)

    @pl.when(f == 0)
    def _():
        h_scr[...] = (_rms(x_ref[...], nw_ref[...]) * (1.0 + sc_ref[0]) + sh_ref[0]).astype(BF16)
        acc[...] = jnp.zeros(acc.shape, F32)

    ua = _dot(h_scr[...], wua_ref[0])
    ub = _dot(h_scr[...], wub_ref[0])
    tin = lax.broadcasted_iota(I32, ua.shape, 0) % dl
    ua1 = jnp.where(tin >= 1, pltpu.roll(ua, 1, 0), p1_ref[...])
    ua2 = jnp.where(tin >= 2, pltpu.roll(ua, 2, 0), p2_ref[...])
    ua_ref[...] = ua
    acc[...] += _ffn_tail(ua, ua1, ua2, ub, cw_ref, cb_ref, wd_ref)

    @pl.when(f == pl.num_programs(0) - 1)
    def _():
        o_ref[...] = x_ref[...] + g_ref[0] * acc[...]


def _ffn_sample_call(x, modx, dl, nw, p1, p2, fw, layer):
    m = x.shape[0]
    kern = functools.partial(_ffn_sample_kernel, dl=dl)
    full = lambda i: pl.BlockSpec((m, D_MODEL), lambda f: (0, 0))
    return pl.pallas_call(
        kern,
        grid=(D_FFP // TF,),
        in_specs=[full(0),
                  pl.BlockSpec((1, m, D_MODEL), lambda f: (0, 0, 4)),
                  pl.BlockSpec((1, m, D_MODEL), lambda f: (0, 0, 3)),
                  pl.BlockSpec((1, m, D_MODEL), lambda f: (0, 0, 5)),
                  pl.BlockSpec((1, D_MODEL), lambda f: (0, 0)),
                  pl.BlockSpec((m, TF), lambda f: (0, f)), pl.BlockSpec((m, TF), lambda f: (0, f)),
                  pl.BlockSpec((1, D_MODEL, TF), lambda f: (layer, 0, f)),
                  pl.BlockSpec((1, D_MODEL, TF), lambda f: (layer, 0, D_FFP // TF + f)),
                  pl.BlockSpec((1, CONV_W, TF), lambda f: (layer, 0, f)),
                  pl.BlockSpec((1, 1, TF), lambda f: (layer, 0, f)),
                  pl.BlockSpec((1, TF, D_MODEL), lambda f: (layer, f, 0))],
        out_specs=[full(0), pl.BlockSpec((m, TF), lambda f: (0, f))],
        out_shape=[jax.ShapeDtypeStruct((m, D_MODEL), F32),
                   jax.ShapeDtypeStruct((m, D_FFP), F32)],
        scratch_shapes=[pltpu.VMEM((m, D_MODEL), BF16), pltpu.VMEM((m, D_MODEL), F32)],
        compiler_params=_cparams(("arbitrary",)),
        name="ffn_sample",
    )(x, modx, modx, modx, nw, p1, p2, fw["wu"], fw["wu"], fw["cw"], fw["cb"], fw["wd"])


def _rope_tables(pos):
    half = ROPE_DIM // 2
    inv = jnp.exp(jnp.arange(half, dtype=F32) * (-math.log(ROPE_THETA) / half))
    ang = pos.astype(F32)[:, None] * inv[None, :]
    cos, sin = jnp.cos(ang), jnp.sin(ang)
    n = pos.shape[0]
    rest = NSA_DH - ROPE_DIM
    cos_t = jnp.concatenate([cos, cos, jnp.ones((n, rest), F32)], axis=1)
    sin_a = jnp.concatenate([-sin, jnp.zeros((n, NSA_DH - half), F32)], axis=1)
    sin_b = jnp.concatenate([jnp.zeros((n, half), F32), sin, jnp.zeros((n, rest), F32)], axis=1)
    return cos_t, sin_a, sin_b


_W_IN_SIZES = (D_A, D_B, D_B, D_B, D_B, D_C, 6 * N_KV, 3 * NSA_HEADS, 3 * D_MODEL)
_W_IN_DST = (OFF_UA, OFF_HQ, OFF_HF, OFF_HI, OFF_HGT, OFF_NQ, OFF_NKV, OFF_NGT, OFF_MGT)
N_IN = sum(_W_IN_SIZES)


W_IN_RB = 256


def _w_in_src_row(i):
    src = jnp.int32(0)
    start = 0
    for size, dst in zip(_W_IN_SIZES, _W_IN_DST):
        inside = (i * W_IN_RB >= dst) & (i * W_IN_RB < dst + max(size, W_IN_RB))
        src = jnp.where(inside, start + i * W_IN_RB - dst, src)
        start += size
    return src


def _w_in_prep_kernel(x_ref, o_ref):
    i = pl.program_id(1)
    x = x_ref[0].astype(BF16)
    row = lax.broadcasted_iota(I32, x.shape, 0)
    o_ref[0] = jnp.where((i * W_IN_RB == OFF_NGT) & (row >= 3 * NSA_HEADS), jnp.zeros_like(x), x)


def _permute_w_in(w_in):
    depth, d, _ = w_in.shape
    w_t = jnp.swapaxes(w_in, 1, 2)
    return pl.pallas_call(
        _w_in_prep_kernel,
        grid=(depth, N_INP // W_IN_RB),
        in_specs=[pl.BlockSpec((pl.Element(1), pl.Element(W_IN_RB), pl.Element(d)),
                               lambda l, i: (l, pl.multiple_of(_w_in_src_row(i), SUBLANE), 0))],
        out_specs=pl.BlockSpec((1, W_IN_RB, d), lambda l, i: (l, i, 0)),
        out_shape=jax.ShapeDtypeStruct((depth, N_INP, d), BF16),
        compiler_params=_cparams(("arbitrary", "arbitrary")),
        name="w_in_prep",
    )(w_t)


def _w_up_prep_kernel(x_ref, o_ref):
    o_ref[0, :, :D_FF] = x_ref[0].astype(BF16)
    o_ref[0, :, D_FF:] = jnp.zeros((o_ref.shape[1], D_FFP - D_FF), BF16)


def _pad_w_up(w_up):
    depth, d, _ = w_up.shape
    r = 512
    return pl.pallas_call(
        _w_up_prep_kernel,
        grid=(depth, d // r, 2),
        in_specs=[pl.BlockSpec((1, r, D_FF), lambda l, i, h: (l, i, h))],
        out_specs=pl.BlockSpec((1, r, D_FFP), lambda l, i, h: (l, i, h)),
        out_shape=jax.ShapeDtypeStruct((depth, d, 2 * D_FFP), BF16),
        compiler_params=_cparams(("arbitrary", "arbitrary", "arbitrary")),
        name="w_up_prep",
    )(w_up)


def _s5_params(a_re, a_im, log_dt, b_re, b_im, c_re, c_im, d, w_glu, b_glu, seg_lens):
    dt = jnp.exp(log_dt)[:, None]
    mag = jnp.exp(a_re * dt)
    ab_re, ab_im = mag * jnp.cos(a_im * dt), mag * jnp.sin(a_im * dt)
    den = a_re * a_re + a_im * a_im
    cf_re = ((ab_re - 1.0) * a_re + ab_im * a_im) / den
    cf_im = (ab_im * a_re - (ab_re - 1.0) * a_im) / den
    bb_re = cf_re[..., None] * b_re - cf_im[..., None] * b_im
    bb_im = cf_re[..., None] * b_im + cf_im[..., None] * b_re
    eye = jnp.eye(S5_G, dtype=F32)
    bdiag = lambda bb: jnp.einsum("gnc,gh->gchn", bb, eye).reshape(D_A, S5_W).astype(BF16)
    cdiag = lambda cc: jnp.einsum("gcn,gh->gnhc", cc, eye).reshape(S5_W, D_A).astype(BF16)
    cmul = lambda a, b: (a[0] * b[0] - a[1] * b[1], a[0] * b[1] + a[1] * b[0])
    abar = (ab_re.reshape(1, S5_W), ab_im.reshape(1, S5_W))
    pw = [abar]
    for _ in range(max(seg_lens) - 1):
        pw.append(cmul(pw[-1], abar))

    def tables(ts):
        rep = lambda k: jnp.concatenate([jnp.broadcast_to(p[k], (SUBLANE, S5_W)) for p in pw[:ts]], axis=0)
        g1 = pw[ts - 1]
        g2 = cmul(g1, g1)
        g4 = cmul(g2, g2)
        pad = jnp.zeros((SUBLANE - 3, S5_W), F32)
        gp = lambda k: jnp.concatenate([g1[k], g2[k], g4[k], pad], axis=0)
        return rep(0), rep(1), gp(0), gp(1)

    return dict(bbr=bdiag(bb_re), bbi=bdiag(bb_im), ccr=cdiag(c_re), cci=cdiag(c_im),
                pow={ts: tables(ts) for ts in seg_lens},
                d=d.reshape(1, D_A), wg=w_glu.astype(BF16), bg=b_glu.reshape(1, D_A))


def _hgrn_params(lb):
    pos = lb > 0
    lb_safe = jnp.where(pos, lb, 1.0)
    z = jnp.zeros_like(lb)
    return jnp.stack([jnp.log(lb_safe), jnp.log1p(-lb), 1.0 - lb, pos.astype(F32), z, z, z, z], axis=0)


def _cmp_params(cmp_a, w1, b1, w2):
    def lanes(a):
        return jnp.concatenate([a[0], a[0], a[1], a[1]], axis=-1)
    return dict(a1=lanes(cmp_a[:, :CMP_STRIDE]), a2=lanes(cmp_a[:, CMP_STRIDE:]),
                w1=w1.astype(BF16), b1=b1, w2=w2.astype(BF16))


def _ffn_params(w_up, conv_w, conv_b, w_down):
    padc = lambda a: jnp.pad(a, [(0, 0)] * (a.ndim - 1) + [(0, D_FFP - D_FF)])
    wu = _pad_w_up(w_up)
    wd = jnp.pad(w_down.astype(BF16), ((0, 0), (0, D_FFP - D_FF), (0, 0)))
    return dict(wu=wu, cw=padc(conv_w), cb=padc(conv_b)[:, None, :], wd=wd)


def kernel(x_prompt, x_sample, cache_cmp, cache_slc, state_win, state_s5, state_hgrn, state_conv,
           page_table, c_prompt, c_sample, w_mod, b_mod, norm1_w, norm2_w, w_in,
           s5_a_re, s5_a_im, s5_log_dt, s5_b_re, s5_b_im, s5_c_re, s5_c_im, s5_d, s5_w_glu, s5_b_glu,
           hg_lb_logits, hg_norm_w, nsa_q_norm, nsa_k_norm, cmp_a, cmp_w1, cmp_b1, cmp_w2,
           w_branch_a, w_branch_b, w_branch_c, w_out, w_up, conv_w, conv_b, w_down):
    bsz, seq, _ = x_prompt.shape
    dbsz, dl, _ = x_sample.shape
    depth = w_in.shape[0]
    n_pages = page_table.shape[1]
    past = n_pages * PAGE
    wb = state_win.shape[2]
    mp, ms = bsz * seq, dbsz * dl
    rows = NSA_REP * dl
    assert seq % 512 == 0 and seq // SEL_BLOCK <= LANE and n_pages % PG == 0 and dl <= SUBLANE

    probs = jax.nn.softmax(hg_lb_logits.astype(F32), axis=0)
    lower_bounds = jnp.cumsum(probs, axis=0) - probs[0:1]
    w_in_p = _permute_w_in(w_in)
    wa, wbr, wc, wo = (w.astype(BF16) for w in (w_branch_a, w_branch_b, w_branch_c, w_out))
    fw = _ffn_params(w_up, conv_w, conv_b, w_down)
    rope_p = _rope_tables(jnp.arange(seq, dtype=I32))
    rope_s = tuple(jnp.tile(t, (dbsz, 1)) for t in _rope_tables(past + jnp.arange(dl, dtype=I32)))
    pt_flat = page_table.reshape(-1).astype(I32)
    cache_cmp4 = cache_cmp.reshape(depth, -1, PAGE * KV_TILES, LANE)
    cache_slc4 = cache_slc.reshape(depth, -1, PAGE * KV_TILES, LANE)
    win_prev4 = state_win.reshape(depth, dbsz, wb * KV_TILES, LANE)
    emat = (jnp.arange(PG * PAGE)[None, :] // SEL_BLOCK == jnp.arange(LANE)[:, None]).astype(BF16)

    nr = -(-(bsz + dbsz) // 16) * 16
    c_all = jnp.concatenate([c_prompt, c_sample, jnp.zeros((nr - bsz - dbsz, D_MODEL), F32)], axis=0)
    mod = _mod_call(c_all, w_mod, b_mod)

    cmp_lanes = lambda a: jnp.concatenate([a[:, 0], a[:, 0], a[:, 1], a[:, 1]], axis=-1).reshape(
        depth, SUB_ROWS, LANE)
    first, second = _cmp_pool_cache_call(cache_cmp4, pt_flat, dbsz, n_pages,
                                         cmp_lanes(cmp_a[:, :, :CMP_STRIDE]),
                                         cmp_lanes(cmp_a[:, :, CMP_STRIDE:]))
    kc_past = _cmp_mlp_cache_call(first, second, cmp_w1.astype(BF16), cmp_b1, cmp_w2.astype(BF16))

    xp = x_prompt.reshape(mp, D_MODEL)
    xs = x_sample.reshape(ms, D_MODEL)
    outs = [[] for _ in range(12)]
    row_bufs = None
    for l in range(depth):
        s5p = _s5_params(s5_a_re[l], s5_a_im[l], s5_log_dt[l], s5_b_re[l], s5_b_im[l], s5_c_re[l],
                         s5_c_im[l], s5_d[l], s5_w_glu[l], s5_b_glu[l],
                         (min(256, seq) // SUBLANE, 1))
        hgp = _hgrn_params(lower_bounds[l])
        hgn = hg_norm_w[l].reshape(1, HG_DV)
        cp = _cmp_params(cmp_a[l], cmp_w1[l], cmp_b1[l], cmp_w2[l])
        n1, n2 = norm1_w[l].reshape(1, D_MODEL), norm2_w[l].reshape(1, D_MODEL)
        qn, kn = nsa_q_norm[l].reshape(1, NSA_DH), nsa_k_norm[l]
        modp = mod[l, :bsz].reshape(bsz, 1, 6 * D_MODEL)
        mods = jnp.repeat(mod[l, bsz:bsz + dbsz], dl, axis=0).reshape(1, ms, 6 * D_MODEL)

        hin = _inproj_call(xp, modp, seq, n1, w_in_p, l)
        ya, s5_new = _s5_call(hin, OFF_UA // D_A, bsz, seq, jnp.zeros((bsz, 2, S5_W), F32), s5p,
                              SUBLANE - 1)
        yb, hg_new = _hgrn_call(hin, tuple(o // D_B for o in (OFF_HQ, OFF_HF, OFF_HI, OFF_HGT)), bsz, seq,
                                seq, jnp.zeros((bsz, HG_HEADS, HG_DK, HG_DV), F32), hgp, hgn)
        qo, *row_bufs, bc, bs, bw, gates = _nsa_prep_call(hin, rope_p, seq, qn, kn, l, depth, row_bufs)
        kc = _cmp_prompt_call(row_bufs[0], bsz, seq, cp, l)
        yc = _nsa_prompt_call(qo, kc, bs, bw, gates, bsz, seq)
        merged = _merge_call(ya, yb, yc, hin, wa, wbr, wc, l)
        xp = _outproj_call(merged, xp, modp, seq, wo, l)
        xp, tails = _ffn_prompt_call(xp, modp, seq, n2, jnp.zeros((bsz, SUBLANE, D_FFP), F32), fw, l)
        conv_new = tails.reshape(bsz, -1, SUBLANE, D_FFP)[:, -1, SUBLANE - (CONV_W - 1):, :D_FF]
        for i, o in ((0, s5_new.reshape(bsz, 2, S5_G, S5_N)), (1, hg_new), (5, conv_new)):
            outs[2 * i].append(o)

        hin = _inproj_call(xs, mods, dl, n1, w_in_p, l)
        padt = lambda a, n: jnp.pad(a.reshape(dbsz, dl, -1), ((0, 0), (0, n - dl), (0, 0))).reshape(dbsz * n, -1)
        u_pad = padt(hin[:, OFF_UA:OFF_UA + D_A], SUBLANE)
        ya, s5_new = _s5_call(u_pad, 0, dbsz, SUBLANE, state_s5[l].reshape(dbsz, 2, S5_W), s5p, dl - 1)
        ya = ya.reshape(dbsz, SUBLANE, D_A)[:, :dl].reshape(ms, D_A)
        hg_pad = padt(hin[:, OFF_HQ:OFF_HQ + 4 * D_B], HG_C)
        yb, hg_new = _hgrn_call(hg_pad, (0, 1, 2, 3), dbsz, HG_C, dl, state_hgrn[l], hgp, hgn)
        yb = yb.reshape(dbsz, HG_C, D_B)[:, :dl].reshape(ms, D_B)
        qo, rc, rs, rw, bc, bs, bw, gates = _nsa_prep_call(hin, rope_s, dl, qn, kn)
        q16 = qo.reshape(dbsz, dl, NSA_KVH, NSA_REP, NSA_DH).transpose(0, 2, 1, 3, 4).reshape(
            dbsz, NSA_KVH, rows, NSA_DH)
        o_c, selt = _nsa_s_sel_call(q16, kc_past, l, past, dl)
        g16 = gates[:, :3 * NSA_HEADS].reshape(dbsz, dl, NSA_KVH, NSA_REP, 3).transpose(0, 2, 1, 3, 4)
        g16 = jnp.pad(g16.reshape(dbsz, NSA_KVH, rows, 3), ((0, 0), (0, 0), (0, 0), (0, LANE - 3)))
        pad8 = lambda r: jnp.pad(r.reshape(dbsz, dl, 2 * N_KV), ((0, 0), (0, SUBLANE - dl), (0, 0)))
        yc = _nsa_s_attn_call(cache_slc4, pt_flat, l, q16, selt, emat, o_c, pad8(rs), pad8(rw), win_prev4,
                              g16, past, dl)
        yc = yc.reshape(dbsz, NSA_KVH, dl, NSA_REP, NSA_DH).transpose(0, 2, 1, 3, 4).reshape(ms, D_C)
        merged = _merge_call(ya, yb, yc, hin, wa, wbr, wc, l)
        xs = _outproj_call(merged, xs, mods, dl, wo, l)
        st = jnp.pad(state_conv[l], ((0, 0), (0, 0), (0, D_FFP - D_FF)))
        tin = jnp.arange(dl)[None, :, None]
        p1 = jnp.where(tin == 0, st[:, 1:2], 0.0).reshape(ms, D_FFP)
        p2 = jnp.where(tin == 0, st[:, 0:1], jnp.where(tin == 1, st[:, 1:2], 0.0)).reshape(ms, D_FFP)
        xs, ua = _ffn_sample_call(xs, mods, dl, n2, p1, p2, fw, l)
        ext = jnp.concatenate([state_conv[l], ua[:, :D_FF].reshape(dbsz, dl, D_FF)], axis=1)
        kv5 = lambda r: r.reshape(dbsz, dl, 2, NSA_KVH, NSA_DH)
        win_new = jnp.concatenate([state_win[l], kv5(rw)], axis=1)[:, -wb:]
        for i, o in enumerate((s5_new.reshape(dbsz, 2, S5_G, S5_N), hg_new, kv5(rc), kv5(rs), win_new,
                               ext[:, dl:])):
            outs[2 * i + 1].append(o)

    st = [jnp.stack(o) if o else None for o in outs]
    rows6 = [r.reshape(depth, bsz, seq, 2, NSA_KVH, NSA_DH) for r in row_bufs]
    st[4], st[6], st[8] = rows6[0], rows6[1], rows6[2][:, :, -min(WINDOW, seq):]
    return (xp.reshape(bsz, seq, D_MODEL), xs.reshape(dbsz, dl, D_MODEL), *st)
```

```python
import functools
import math

import jax
import jax.numpy as jnp
from jax import lax
from jax.experimental import pallas as pl
from jax.experimental.pallas import tpu as pltpu

F32 = jnp.float32
BF16 = jnp.bfloat16
I32 = jnp.int32

D_MODEL = 2048
D_A = D_MODEL // 4
S5_GROUP = 16
S5_G = D_A // S5_GROUP
S5_N = 64
S5_W = S5_G * S5_N
D_B = D_MODEL // 4
HG_DK = 128
HG_DV = 128
HG_HEADS = D_B // HG_DV
D_C = D_MODEL // 2
NSA_DH = 128
NSA_HEADS = D_C // NSA_DH
NSA_KVH = 2
NSA_REP = NSA_HEADS // NSA_KVH
N_KV = NSA_KVH * NSA_DH
ROPE_DIM = NSA_DH // 4
ROPE_THETA = 500000.0
CMP_STRIDE = 16
SEL_BLOCK = 64
N_SEL = 16
WINDOW = 512
FORCE_BONUS = 1000.0
D_FF = ((8 * D_MODEL) // 3 + 127) // 128 * 128
CONV_W = 3
EPS = 1e-6
NEG_INF = -1e30
PAGE = 128

LANE = 128
SUBLANE = 8
VMEM_LIMIT = 56 * 1024 * 1024

OFF_MGT = 0
OFF_NQ = 3 * D_MODEL
OFF_NKV = OFF_NQ + D_C
OFF_UA = OFF_NKV + 6 * N_KV
OFF_HQ = OFF_UA + D_A
OFF_HF = OFF_HQ + D_B
OFF_HI = OFF_HF + D_B
OFF_HGT = OFF_HI + D_B
OFF_NGT = OFF_HGT + D_B
NGT_PAD = 256
N_INP = OFF_NGT + NGT_PAD
TN_IN = 1280
D_FFP = 5632
TF = 512
HG_C = 64
HG_SB = 8


def _cparams(sem):
    return pltpu.CompilerParams(dimension_semantics=sem, vmem_limit_bytes=VMEM_LIMIT)


def _dot(a, b):
    return jnp.dot(a, b, preferred_element_type=F32)


def _dot_nt(a, b):
    return lax.dot_general(a, b, (((1,), (1,)), ((), ())), preferred_element_type=F32)


def _dot_tn(a, b):
    return lax.dot_general(a, b, (((0,), (0,)), ((), ())), preferred_element_type=F32)


def _dot2(a, b):
    hi = a.astype(BF16)
    lo = (a - hi.astype(F32)).astype(BF16)
    return _dot(hi, b) + _dot(lo, b)


def _dot3(a, b):
    hi = a.astype(BF16)
    r1 = a - hi.astype(F32)
    mid = r1.astype(BF16)
    lo = (r1 - mid.astype(F32)).astype(BF16)
    return _dot(b, hi) + _dot(b, mid) + _dot(b, lo)


def _rms(x, w):
    return x * lax.rsqrt(jnp.mean(x * x, axis=-1, keepdims=True) + EPS) * w


def _masked_softmax(s, mask):
    s = jnp.where(mask, s, NEG_INF)
    p = jnp.where(mask, jnp.exp(s - jnp.max(s, axis=-1, keepdims=True)), 0.0)
    return p / jnp.maximum(jnp.sum(p, axis=-1, keepdims=True), 1e-30)


def _row_reduce(x, combine, reduce):
    n = x.shape[-1]
    if n % LANE or n == LANE:
        return reduce(x, axis=-1, keepdims=True)
    acc = x[:, :LANE]
    for k in range(1, n // LANE):
        acc = combine(acc, x[:, k * LANE:(k + 1) * LANE])
    return reduce(acc, axis=-1, keepdims=True)


def _log_sigmoid(x):
    return jnp.minimum(x, 0.0) - jnp.log1p(jnp.exp(-jnp.abs(x)))


def _logaddexp(a, b):
    return jnp.maximum(a, b) + jnp.log1p(jnp.exp(-jnp.abs(a - b)))


def _mod_kernel(c_ref, w_ref, b_ref, o_ref):
    c = c_ref[...]
    a = (c * jax.nn.sigmoid(c)).astype(BF16)
    o_ref[0] = _dot(a, w_ref[0].astype(BF16)) + b_ref[0]


def _mod_call(c_all, w_mod, b_mod):
    depth, d, n = w_mod.shape
    r = c_all.shape[0]
    tn = 1024
    return pl.pallas_call(
        _mod_kernel,
        grid=(depth, n // tn),
        in_specs=[pl.BlockSpec((r, d), lambda l, j: (0, 0)),
                  pl.BlockSpec((1, d, tn), lambda l, j: (l, 0, j)),
                  pl.BlockSpec((1, 1, tn), lambda l, j: (l, 0, j))],
        out_specs=pl.BlockSpec((1, r, tn), lambda l, j: (l, 0, j)),
        out_shape=jax.ShapeDtypeStruct((depth, r, n), F32),
        compiler_params=_cparams(("arbitrary", "arbitrary")),
        name="mod",
    )(c_all, w_mod, b_mod.reshape(depth, 1, n))


def _inproj_kernel(x_ref, sc_ref, sh_ref, nw_ref, w_ref, o_ref, h_scr):
    @pl.when(pl.program_id(1) == 0)
    def _():
        y = _rms(x_ref[...], nw_ref[...])
        h_scr[...] = (y * (1.0 + sc_ref[0]) + sh_ref[0]).astype(BF16)

    o_ref[...] = _dot_nt(h_scr[...], w_ref[0])


def _mod_spec(modx, tm, rows_per_group, col):
    if modx.shape[1] == 1:
        return pl.BlockSpec((1, 1, D_MODEL), lambda i, *_: ((i * tm) // rows_per_group, 0, col))
    return pl.BlockSpec((1, tm, D_MODEL), lambda i, *_: (0, i, col))


def _inproj_call(x, modx, rows_per_group, nw, w, layer):
    m = x.shape[0]
    tm = min(1024, m)
    return pl.pallas_call(
        _inproj_kernel,
        grid=(m // tm, N_INP // TN_IN),
        in_specs=[pl.BlockSpec((tm, D_MODEL), lambda i, j: (i, 0)),
                  _mod_spec(modx, tm, rows_per_group, 1),
                  _mod_spec(modx, tm, rows_per_group, 0),
                  pl.BlockSpec((1, D_MODEL), lambda i, j: (0, 0)),
                  pl.BlockSpec((1, TN_IN, D_MODEL), lambda i, j: (layer, j, 0))],
        out_specs=pl.BlockSpec((tm, TN_IN), lambda i, j: (i, j)),
        out_shape=jax.ShapeDtypeStruct((m, N_INP), F32),
        scratch_shapes=[pltpu.VMEM((tm, D_MODEL), BF16)],
        compiler_params=_cparams(("arbitrary", "arbitrary")),
        name="inproj",
    )(x, modx, modx, nw, w)


S5_LC = 512


def _s5_kernel(u_ref, h0_ref, bbr_ref, bbi_ref, ccr_ref, cci_ref, apr_ref, api_ref, gpr_ref, gpi_ref,
               d_ref, wg_ref, bg_ref, y_ref, hl_ref, hre, him, car, *, t_rows, last_row):
    ts = t_rows // SUBLANE

    @pl.when(pl.program_id(1) == 0)
    def _():
        car[...] = h0_ref[0]

    ri = lax.broadcasted_iota(I32, (t_rows, t_rows), 0)
    ci = lax.broadcasted_iota(I32, (t_rows, t_rows), 1)
    perm = (ci == (ri % SUBLANE) * ts + ri // SUBLANE).astype(BF16)
    unperm = (ci == (ri % ts) * SUBLANE + ri // ts).astype(BF16)
    u = _dot3(u_ref[...], perm)
    ub = u.astype(BF16)
    nblk = D_A // LANE
    for j in range(nblk):
        us, ss = slice(j * LANE, (j + 1) * LANE), slice(j * S5_LC, (j + 1) * S5_LC)
        hre[:, ss] = _dot(ub[:, us], bbr_ref[us, ss])
        him[:, ss] = _dot(ub[:, us], bbi_ref[us, ss])
    row = lax.broadcasted_iota(I32, (SUBLANE, S5_LC), 0)
    zero = jnp.zeros((SUBLANE, S5_LC), F32)
    for c in range(S5_W // S5_LC):
        sl = slice(c * S5_LC, (c + 1) * S5_LC)
        ar, ai = apr_ref[0:SUBLANE, sl], api_ref[0:SUBLANE, sl]

        def local(t, carry, sl=sl, ar=ar, ai=ai):
            hr, hi = carry
            r0 = pl.multiple_of(t * SUBLANE, SUBLANE)
            nr = ar * hr - ai * hi + hre[pl.ds(r0, SUBLANE), sl]
            ni = ar * hi + ai * hr + him[pl.ds(r0, SUBLANE), sl]
            hre[pl.ds(r0, SUBLANE), sl] = nr
            him[pl.ds(r0, SUBLANE), sl] = ni
            return nr, ni

        er, ei = lax.fori_loop(0, ts, local, (zero, zero))
        yr = jnp.where(row == 0, car[0:1, sl], pltpu.roll(er, 1, 0))
        yi = jnp.where(row == 0, car[1:2, sl], pltpu.roll(ei, 1, 0))
        for k, s in enumerate((1, 2, 4)):
            gr, gi = gpr_ref[k:k + 1, sl], gpi_ref[k:k + 1, sl]
            sr = jnp.where(row >= s, pltpu.roll(yr, s, 0), 0.0)
            si = jnp.where(row >= s, pltpu.roll(yi, s, 0), 0.0)
            yr, yi = yr + (gr * sr - gi * si), yi + (gr * si + gi * sr)
        g1r, g1i = gpr_ref[0:1, sl], gpi_ref[0:1, sl]
        car[0:1, sl] = (g1r * yr - g1i * yi + er)[SUBLANE - 1:SUBLANE]
        car[1:2, sl] = (g1r * yi + g1i * yr + ei)[SUBLANE - 1:SUBLANE]

        def fix(t, carry, sl=sl, yr=yr, yi=yi):
            r0 = pl.multiple_of(t * SUBLANE, SUBLANE)
            pr, pi = apr_ref[pl.ds(r0, SUBLANE), sl], api_ref[pl.ds(r0, SUBLANE), sl]
            hre[pl.ds(r0, SUBLANE), sl] += pr * yr - pi * yi
            him[pl.ds(r0, SUBLANE), sl] += pr * yi + pi * yr
            return carry

        lax.fori_loop(0, ts, fix, 0)

    lr = t_rows - SUBLANE + last_row
    hl_ref[0, 0:1, :] = hre[lr:lr + 1, :]
    hl_ref[0, 1:2, :] = him[lr:lr + 1, :]
    y = jnp.concatenate(
        [_dot(hre[:, j * S5_LC:(j + 1) * S5_LC].astype(BF16),
              ccr_ref[j * S5_LC:(j + 1) * S5_LC, j * LANE:(j + 1) * LANE])
         - _dot(him[:, j * S5_LC:(j + 1) * S5_LC].astype(BF16),
                cci_ref[j * S5_LC:(j + 1) * S5_LC, j * LANE:(j + 1) * LANE])
         for j in range(nblk)], axis=1) + d_ref[...] * u
    y = jax.nn.gelu(y)
    z = _dot(y.astype(BF16), wg_ref[...]) + bg_ref[...]
    y_ref[...] = _dot(unperm, (y * jax.nn.sigmoid(z)).astype(BF16)).astype(BF16)


def _s5_call(u_arr, ucol, nb, seq, h0, p, last_row):
    t_rows = min(256, seq)
    nt = seq // t_rows
    const = lambda shape: pl.BlockSpec(shape, lambda b, t: (0,) * len(shape))
    kern = functools.partial(_s5_kernel, t_rows=t_rows, last_row=last_row)
    return pl.pallas_call(
        kern,
        grid=(nb, nt),
        in_specs=[pl.BlockSpec((t_rows, D_A), lambda b, t: (b * nt + t, ucol)),
                  pl.BlockSpec((1, 2, S5_W), lambda b, t: (b, 0, 0)),
                  const((D_A, S5_W)), const((D_A, S5_W)), const((S5_W, D_A)), const((S5_W, D_A)),
                  const((t_rows, S5_W)), const((t_rows, S5_W)),
                  const((SUBLANE, S5_W)), const((SUBLANE, S5_W)), const((1, D_A)),
                  const((D_A, D_A)), const((1, D_A))],
        out_specs=[pl.BlockSpec((t_rows, D_A), lambda b, t: (b * nt + t, 0)),
                   pl.BlockSpec((1, 2, S5_W), lambda b, t: (b, 0, 0))],
        out_shape=[jax.ShapeDtypeStruct((nb * seq, D_A), BF16),
                   jax.ShapeDtypeStruct((nb, 2, S5_W), F32)],
        scratch_shapes=[pltpu.VMEM((t_rows, S5_W), F32), pltpu.VMEM((t_rows, S5_W), F32),
                        pltpu.VMEM((2, S5_W), F32)],
        compiler_params=_cparams(("arbitrary", "arbitrary")),
        name="s5",
    )(u_arr, h0, p["bbr"], p["bbi"], p["ccr"], p["cci"], *p["pow"][t_rows // SUBLANE], p["d"],
      p["wg"], p["bg"])


def _hgrn_kernel(q_ref, f_ref, i_ref, g_ref, s0_ref, lp_ref, nw_ref, y_ref, sl_ref, st, pscr, rscr,
                 *, l_valid, chunks):
    t = pl.program_id(1)
    c = HG_C
    nsb = c // HG_SB
    nh = HG_HEADS

    @pl.when(t == 0)
    def _():
        for h in range(nh):
            st[h] = s0_ref[0, h].T

    row = lax.broadcasted_iota(I32, (c, D_B), 0)
    tri = (lax.broadcasted_iota(I32, (c, c), 0) >= lax.broadcasted_iota(I32, (c, c), 1)).astype(BF16)
    srow = lax.broadcasted_iota(I32, (HG_SB, D_B), 0)
    rows16 = lambda r: jnp.broadcast_to(r, (HG_SB, D_B))

    def chunk(ci, carry):
        c0 = pl.multiple_of(ci * c, c)
        crows = pl.ds(c0, c)
        valid = (t * (chunks * c) + c0 + row) < l_valid
        hf = f_ref[crows, :]
        ls = _log_sigmoid(hf)
        lf = jnp.where(lp_ref[3:4, :] > 0.5, _logaddexp(lp_ref[0:1, :], lp_ref[1:2, :] + ls), ls)
        kk = lp_ref[2:3, :] * jax.nn.sigmoid(-hf)
        lf = jnp.where(valid, lf, 0.0)
        kk = jnp.where(valid, kk, 0.0)
        hq = q_ref[crows, :]
        q = hq * jax.nn.sigmoid(hq)
        v = i_ref[crows, :]
        b = _dot3(lf, tri)
        zero_row = jnp.zeros((1, D_B), F32)
        bref = [zero_row] + [b[i * HG_SB - 1:i * HG_SB] for i in range(1, nsb)]
        bend = [b[(j + 1) * HG_SB - 1:(j + 1) * HG_SB] for j in range(nsb)]
        qq = q * jnp.exp(b - jnp.concatenate([rows16(r) for r in bref], axis=0))
        kks = kk * jnp.exp(jnp.concatenate([rows16(r) for r in bend], axis=0) - b)
        lhs = []
        for j in range(nsb - 1):
            dsel = jnp.concatenate(
                [rows16(jnp.exp(bref[i] - bend[j])) if i > j else jnp.zeros((HG_SB, D_B), F32)
                 for i in range(nsb)], axis=0)
            lhs.append((qq * dsel).astype(BF16))
        rhs = [jnp.where(row // HG_SB == j, kks, 0.0).astype(BF16) for j in range(nsb - 1)]
        for i in range(nsb):
            rs = slice(i * HG_SB, (i + 1) * HG_SB)
            bi, qi, ki = b[rs], q[rs], kk[rs]
            for s0 in range(0, HG_SB, 2):
                pair = []
                for s in (s0, s0 + 1):
                    e = jnp.exp(jnp.where(srow >= s, bi - bi[s:s + 1], NEG_INF))
                    pair.append((qi * ki[s:s + 1]) * e)
                piece = jnp.concatenate(pair, axis=0).astype(BF16)
                for h in range(nh):
                    r0 = ((h * nsb + i) * HG_SB + s0) * HG_SB
                    pscr[r0:r0 + 2 * HG_SB, :] = piece[:, h * HG_DK:(h + 1) * HG_DK]
        rscr[...] = _dot(pscr[...], jnp.ones((HG_DK, HG_DK), BF16))
        qe = (q * jnp.exp(b)).astype(BF16)
        vb = v.astype(BF16)
        bl = b[c - 1:c]
        kdec = (kk * jnp.exp(bl - b)).astype(BF16)
        ebl = jnp.exp(bl)
        gt = g_ref[crows, :]
        gate = gt * jax.nn.sigmoid(gt)
        for h in range(nh):
            hs = slice(h * HG_DK, (h + 1) * HG_DK)
            s_t = st[h]
            att = _dot_nt(jnp.concatenate([x[:, hs] for x in lhs], axis=1),
                          jnp.concatenate([x[:, hs] for x in rhs], axis=1))
            o = _dot(att.astype(BF16), vb[:, hs]) + _dot_nt(qe[:, hs], s_t.astype(BF16))
            diag = []
            for i in range(nsb):
                od = jnp.zeros((HG_SB, HG_DV), F32)
                for s in range(HG_SB):
                    r0 = ((h * nsb + i) * HG_SB + s) * HG_SB
                    od = od + rscr[r0:r0 + HG_SB, :] * v[i * HG_SB + s:i * HG_SB + s + 1, hs]
                diag.append(od)
            o = o + jnp.concatenate(diag, axis=0)
            st[h] = s_t * ebl[:, hs] + _dot_tn(vb[:, hs], kdec[:, hs])
            y_ref[crows, hs] = (_rms(o, nw_ref[...]) * gate[:, hs]).astype(BF16)
        return carry

    lax.fori_loop(0, chunks, chunk, 0)

    @pl.when(t == pl.num_programs(1) - 1)
    def _():
        for h in range(nh):
            sl_ref[0, h] = st[h].T


def _dot2g(g, z):
    hi = z.astype(BF16)
    lo = (z - hi.astype(F32)).astype(BF16)
    return _dot(g, hi) + _dot(g, lo)


def _hgrn_call(arr, cols, nb, seq_pad, l_valid, s0, lp, nw):
    chunks = 4 if seq_pad % (4 * HG_C) == 0 else 1
    rows = chunks * HG_C
    nt = seq_pad // rows
    blk = lambda col: pl.BlockSpec((rows, D_B), lambda b, t: (b * nt + t, col))
    kern = functools.partial(_hgrn_kernel, l_valid=l_valid, chunks=chunks)
    return pl.pallas_call(
        kern,
        grid=(nb, nt),
        in_specs=[blk(cols[0]), blk(cols[1]), blk(cols[2]), blk(cols[3]),
                  pl.BlockSpec((1, HG_HEADS, HG_DK, HG_DV), lambda b, t: (b, 0, 0, 0)),
                  pl.BlockSpec((SUBLANE, D_B), lambda b, t: (0, 0)),
                  pl.BlockSpec((1, HG_DV), lambda b, t: (0, 0))],
        out_specs=[pl.BlockSpec((rows, D_B), lambda b, t: (b * nt + t, 0)),
                   pl.BlockSpec((1, HG_HEADS, HG_DK, HG_DV), lambda b, t: (b, 0, 0, 0))],
        out_shape=[jax.ShapeDtypeStruct((nb * seq_pad, D_B), BF16),
                   jax.ShapeDtypeStruct((nb, HG_HEADS, HG_DK, HG_DV), F32)],
        scratch_shapes=[pltpu.VMEM((HG_HEADS, HG_DV, HG_DK), F32),
                        pltpu.VMEM((HG_HEADS * HG_C * HG_SB, HG_DK), BF16),
                        pltpu.VMEM((HG_HEADS * HG_C * HG_SB, HG_DK), F32)],
        compiler_params=_cparams(("arbitrary", "arbitrary")),
        name="hgrn",
    )(arr, arr, arr, arr, s0, lp, nw)


def _rope(x, cos_t, sin_a, sin_b):
    return (x * cos_t + pltpu.roll(x, NSA_DH - ROPE_DIM // 2, 1) * sin_a
            + pltpu.roll(x, ROPE_DIM // 2, 1) * sin_b)


def _nsa_prep_kernel(q_ref, kc_ref, ks_ref, kw_ref, g_ref, cos_ref, sa_ref, sb_ref, qn_ref, kn_ref,
                     qo_ref, rc_ref, rs_ref, rw_ref, bc_ref, bs_ref, bw_ref, go_ref):
    cos_t, sin_a, sin_b = cos_ref[...], sa_ref[...], sb_ref[...]
    qn = qn_ref[...]
    for h in range(NSA_HEADS):
        hs = slice(h * NSA_DH, (h + 1) * NSA_DH)
        qo_ref[:, hs] = _rope(_rms(q_ref[:, hs], qn), cos_t, sin_a, sin_b).astype(BF16)
    for br, (src, dst, dstb) in enumerate(((kc_ref, rc_ref, bc_ref), (ks_ref, rs_ref, bs_ref),
                                           (kw_ref, rw_ref, bw_ref))):
        kn = kn_ref[br:br + 1, :]
        tm = src.shape[0]
        for g in range(NSA_KVH):
            gs = slice(g * NSA_DH, (g + 1) * NSA_DH)
            k = _rope(_rms(src[:, gs], kn), cos_t, sin_a, sin_b)
            dst[pl.ds(g, tm, stride=KV_TILES), :] = k
            dstb[:, gs] = k.astype(BF16)
            vs = slice(N_KV + g * NSA_DH, N_KV + (g + 1) * NSA_DH)
            v = src[:, vs]
            dst[pl.ds(NSA_KVH + g, tm, stride=KV_TILES), :] = v
            dstb[:, vs] = v.astype(BF16)
    go_ref[...] = jax.nn.sigmoid(g_ref[...])


def _nsa_prep_call(hin, rope_tabs, seq, qn, kn, layer=0, depth=1, row_bufs=None):
    m = hin.shape[0]
    tm = min(512, m)
    npos = rope_tabs[0].shape[0] // tm
    rspec = pl.BlockSpec((tm, NSA_DH), lambda i: (i % npos, 0))
    kvb = OFF_NKV // (2 * N_KV)
    n_in = 10
    extra = () if row_bufs is None else tuple(row_bufs)

    def kern(*refs):
        _nsa_prep_kernel(*refs[:n_in], *refs[n_in + len(extra):])

    return pl.pallas_call(
        kern,
        grid=(m // tm,),
        input_output_aliases={n_in + k: 1 + k for k in range(len(extra))},
        in_specs=[pl.BlockSpec((tm, D_C), lambda i: (i, OFF_NQ // D_C)),
                  pl.BlockSpec((tm, 2 * N_KV), lambda i: (i, kvb)),
                  pl.BlockSpec((tm, 2 * N_KV), lambda i: (i, kvb + 1)),
                  pl.BlockSpec((tm, 2 * N_KV), lambda i: (i, kvb + 2)),
                  pl.BlockSpec((tm, LANE), lambda i: (i, OFF_NGT // LANE)),
                  rspec, rspec, rspec,
                  pl.BlockSpec((1, NSA_DH), lambda i: (0, 0)),
                  pl.BlockSpec((3, NSA_DH), lambda i: (0, 0))]
        + [pl.BlockSpec(memory_space=pl.ANY)] * len(extra),
        out_specs=[pl.BlockSpec((tm, D_C), lambda i: (i, 0))]
        + [pl.BlockSpec((tm * KV_TILES, LANE), lambda i: (layer * (m // tm) + i, 0))] * 3
        + [pl.BlockSpec((tm, 2 * N_KV), lambda i: (i, 0))] * 3
        + [pl.BlockSpec((tm, LANE), lambda i: (i, 0))],
        out_shape=[jax.ShapeDtypeStruct((m, D_C), BF16)]
        + [jax.ShapeDtypeStruct((depth * m * KV_TILES, LANE), F32)] * 3
        + [jax.ShapeDtypeStruct((m, 2 * N_KV), BF16)] * 3
        + [jax.ShapeDtypeStruct((m, LANE), F32)],
        compiler_params=_cparams(("arbitrary",)),
        name="nsa_prep",
    )(hin, hin, hin, hin, hin, *rope_tabs, qn, kn, *extra)


KV_TILES = 2 * N_KV // LANE


def _pool_rows(load, n, a1_ref, a2_ref):
    firsts, seconds = [], []
    for c in range(KV_TILES):
        ls = slice(c * LANE, (c + 1) * LANE)
        first = jnp.zeros((n, LANE), F32)
        second = jnp.zeros((n, LANE), F32)
        for j in range(CMP_STRIDE):
            x = load(j, c)
            first = first + x * a1_ref[j:j + 1, ls]
            second = second + x * a2_ref[j:j + 1, ls]
        firsts.append(first)
        seconds.append(second)
    return firsts, seconds


def _tile_rows(j, c, n):
    return pl.ds(j * KV_TILES + c, n, stride=CMP_STRIDE * KV_TILES)


def _cmp_mlp(first, second, n, w1_ref, b1_ref, w2_ref, o_ref):
    for c in range(2):
        for g in range(NSA_KVH):
            tile = 2 * c + g
            pooled = first[tile] + pltpu.roll(second[tile], n - 1, 0)
            h = _dot(pooled.astype(BF16), w1_ref[c]) + b1_ref[c:c + 1, :]
            o_ref[0, :, tile * NSA_DH:(tile + 1) * NSA_DH] = _dot(
                jax.nn.gelu(h).astype(BF16), w2_ref[c]).astype(BF16)


def _cmp_prompt_kernel(x_ref, a1_ref, a2_ref, w1_ref, b1_ref, w2_ref, o_ref, *, n):
    first, second = _pool_rows(lambda j, c: x_ref[_tile_rows(j, c, n), :], n, a1_ref, a2_ref)
    _cmp_mlp(first, second, n, w1_ref, b1_ref, w2_ref, o_ref)


def _cmp_prompt_call(rows, nb, seq, cp, layer=0):
    n = seq // CMP_STRIDE
    const = lambda shape: pl.BlockSpec(shape, lambda b: (0,) * len(shape))
    return pl.pallas_call(
        functools.partial(_cmp_prompt_kernel, n=n),
        grid=(nb,),
        in_specs=[pl.BlockSpec((seq * KV_TILES, LANE), lambda b: (layer * nb + b, 0)),
                  const((CMP_STRIDE, 2 * N_KV)), const((CMP_STRIDE, 2 * N_KV)),
                  const((2, NSA_DH, NSA_DH)), const((2, NSA_DH)), const((2, NSA_DH, NSA_DH))],
        out_specs=pl.BlockSpec((1, n, 2 * N_KV), lambda b: (b, 0, 0)),
        out_shape=jax.ShapeDtypeStruct((nb, n, 2 * N_KV), BF16),
        compiler_params=_cparams(("arbitrary",)),
        name="cmp_prompt",
    )(rows.reshape(-1, LANE), cp["a1"], cp["a2"], cp["w1"], cp["b1"], cp["w2"])


PG = 32
SUB_PER_PAGE = PAGE // CMP_STRIDE


SUB_ROWS = CMP_STRIDE * KV_TILES


def _cmp_pool_cache_kernel(pt_ref, *refs):
    pages = refs[:PG]
    a1_ref, a2_ref, f_ref, s_ref = refs[PG:]
    nv = SUB_ROWS // SUBLANE
    w1 = [a1_ref[0, v * SUBLANE:(v + 1) * SUBLANE, :] for v in range(nv)]
    w2 = [a2_ref[0, v * SUBLANE:(v + 1) * SUBLANE, :] for v in range(nv)]
    low = lax.broadcasted_iota(I32, (SUBLANE, LANE), 0) < KV_TILES

    def fold(acc):
        return acc + pltpu.roll(acc, KV_TILES, 0)

    for k in range(PG):
        for n2 in range(SUB_PER_PAGE // 2):
            halves = []
            for n in (2 * n2, 2 * n2 + 1):
                a1 = a2 = None
                for v in range(nv):
                    x = pages[k][0, 0, n * SUB_ROWS + v * SUBLANE:n * SUB_ROWS + (v + 1) * SUBLANE, :]
                    a1 = x * w1[v] if a1 is None else a1 + x * w1[v]
                    a2 = x * w2[v] if a2 is None else a2 + x * w2[v]
                halves.append((fold(a1), fold(a2)))
            r0 = (k * SUB_PER_PAGE + 2 * n2) * KV_TILES
            f_ref[0, 0, r0:r0 + SUBLANE, :] = jnp.where(low, halves[0][0], halves[1][0])
            s_ref[0, 0, r0:r0 + SUBLANE, :] = jnp.where(low, halves[0][1], halves[1][1])


def _page_specs(n_pages, layer_of, batch_of, group_of, page_shape=(PAGE, 2 * N_KV)):
    def spec(k):
        return pl.BlockSpec(
            (1, 1) + page_shape,
            lambda *a, k=k: (layer_of(*a), a[-1][batch_of(*a) * n_pages + group_of(*a) * PG + k], 0, 0))
    return [spec(k) for k in range(PG)]


def _cmp_pool_cache_call(cache, pt_flat, nb, n_pages, a1, a2):
    depth = cache.shape[0]
    npg = n_pages // PG
    nsub = n_pages * SUB_PER_PAGE
    rows = PG * SUB_PER_PAGE
    grid_spec = pltpu.PrefetchScalarGridSpec(
        num_scalar_prefetch=1,
        grid=(depth, nb, npg),
        in_specs=_page_specs(n_pages, lambda l, b, g, pt: l, lambda l, b, g, pt: b,
                             lambda l, b, g, pt: g, (PAGE * KV_TILES, LANE))
        + [pl.BlockSpec((1, SUB_ROWS, LANE), lambda l, b, g, pt: (l, 0, 0))] * 2,
        out_specs=[pl.BlockSpec((1, 1, rows * KV_TILES, LANE), lambda l, b, g, pt: (l, b, g, 0))] * 2,
    )
    return pl.pallas_call(
        _cmp_pool_cache_kernel,
        grid_spec=grid_spec,
        out_shape=[jax.ShapeDtypeStruct((depth, nb, nsub * KV_TILES, LANE), F32)] * 2,
        compiler_params=_cparams(("arbitrary", "arbitrary", "arbitrary")),
        name="cmp_pool_cache",
    )(pt_flat, *([cache] * PG), a1, a2)


def _cmp_mlp_cache_kernel(f_ref, s_ref, w1_ref, b1_ref, w2_ref, o_ref, *, n):
    tiles = lambda ref: [ref[0, 0, pl.ds(c, n, stride=KV_TILES), :] for c in range(KV_TILES)]
    _cmp_mlp(tiles(f_ref), tiles(s_ref), n, w1_ref.at[0], b1_ref.at[0], w2_ref.at[0], o_ref.at[0])


def _cmp_mlp_cache_call(first, second, w1, b1, w2):
    depth, nb, rows, _ = first.shape
    n = rows // KV_TILES
    blk = pl.BlockSpec((1, 1, n, 2 * N_KV), lambda l, b: (l, b, 0, 0))
    tblk = pl.BlockSpec((1, 1, rows, LANE), lambda l, b: (l, b, 0, 0))
    return pl.pallas_call(
        functools.partial(_cmp_mlp_cache_kernel, n=n),
        grid=(depth, nb),
        in_specs=[tblk, tblk,
                  pl.BlockSpec((1, 2, NSA_DH, NSA_DH), lambda l, b: (l, 0, 0, 0)),
                  pl.BlockSpec((1, 2, NSA_DH), lambda l, b: (l, 0, 0)),
                  pl.BlockSpec((1, 2, NSA_DH, NSA_DH), lambda l, b: (l, 0, 0, 0))],
        out_specs=blk,
        out_shape=jax.ShapeDtypeStruct((depth, nb, n, 2 * N_KV), BF16),
        compiler_params=_cparams(("arbitrary", "arbitrary")),
        name="cmp_mlp_cache",
    )(first, second, w1, b1, w2)


def _block_scores(imp, qpos, n_slc):
    blk = lax.broadcasted_iota(I32, imp.shape, 1)
    cur = qpos // SEL_BLOCK
    forced = (blk == 0) | (blk == cur) | (blk == cur - 1)
    valid = blk * SEL_BLOCK <= qpos
    score = jnp.where(valid, imp + jnp.where(forced, FORCE_BONUS, 0.0), -1.0)
    return jnp.where(blk < n_slc, score, -2.0)


def _select_blocks_ranked(imp, qpos, n_slc):
    nb8 = (n_slc + SUBLANE - 1) // SUBLANE * SUBLANE
    st = _block_scores(imp, qpos, n_slc).T[:nb8]
    ridx = lax.broadcasted_iota(I32, st.shape, 0)
    cnt = jnp.zeros(st.shape, F32)
    for i in range(n_slc):
        ri = st[i:i + 1]
        cnt = cnt + jnp.where(ri > st, 1.0, jnp.where(ri == st, jnp.where(ridx > i, 1.0, 0.0), 0.0))
    sel_t = jnp.where((cnt < min(N_SEL, n_slc)) & (ridx < n_slc), 1.0, 0.0)
    if nb8 < LANE:
        sel_t = jnp.concatenate([sel_t, jnp.zeros((LANE - nb8, st.shape[1]), F32)], axis=0)
    return sel_t.T


def _select_blocks(imp, qpos, n_slc):
    blk = lax.broadcasted_iota(I32, imp.shape, 1)
    score = _block_scores(imp, qpos, n_slc)
    sel = jnp.zeros(imp.shape, F32)
    for _ in range(min(N_SEL, n_slc)):
        mx = jnp.max(score, axis=-1, keepdims=True)
        idx = jnp.min(jnp.where(score == mx, blk, 1 << 30), axis=-1, keepdims=True)
        hit = blk == idx
        sel = jnp.where(hit, 1.0, sel)
        score = jnp.where(hit, -3.0, score)
    return sel


def _cmp_branch(q, kck, kcv, qpos, n_cmp):
    s = _dot_nt(q, kck) * (NSA_DH ** -0.5)
    col = lax.broadcasted_iota(I32, s.shape, 1)
    mask = (col * CMP_STRIDE + (2 * CMP_STRIDE - 1) <= qpos) & (col < n_cmp)
    p = _masked_softmax(s, mask)
    return p, _dot(p.astype(BF16), kcv)


NSA_KC = 1024


def _nsa_prompt_kernel(q_ref, kc_ref, ks_ref, vs_ref, kw_ref, vw_ref, g_ref, o_ref,
                       bias_scr, m_scr, l_scr, acc_scr, *, tq, seq, slab):
    q0 = pl.program_id(1) * tq
    rows = NSA_REP * tq
    ncp = seq // CMP_STRIDE
    n_slc = seq // SEL_BLOCK
    scale = NSA_DH ** -0.5
    gates = g_ref[...]
    q4s = [jnp.concatenate([q_ref[:, (g * NSA_REP + r) * NSA_DH:(g * NSA_REP + r + 1) * NSA_DH]
                            for r in range(NSA_REP)], axis=0) for g in range(NSA_KVH)]
    qpos_c = q0 + (lax.broadcasted_iota(I32, (rows, ncp), 0) & (tq - 1))
    o_cs, psums = [], []
    for g in range(NSA_KVH):
        p_c, o_c = _cmp_branch(q4s[g], kc_ref[0, :, g * NSA_DH:(g + 1) * NSA_DH],
                               kc_ref[0, :, N_KV + g * NSA_DH:N_KV + (g + 1) * NSA_DH], qpos_c, ncp - 1)
        psum = p_c[0:tq]
        for r in range(1, NSA_REP):
            psum = psum + p_c[r * tq:(r + 1) * tq]
        o_cs.append(o_c)
        psums.append(psum)

    def ranked():
        gmat = (lax.broadcasted_iota(I32, (ncp, LANE), 0) // (SEL_BLOCK // CMP_STRIDE)
                == lax.broadcasted_iota(I32, (ncp, LANE), 1)).astype(BF16)
        imp = _dot2(jnp.concatenate(psums, axis=0), gmat)
        qp = q0 + (lax.broadcasted_iota(I32, (NSA_KVH * tq, LANE), 0) & (tq - 1))
        return _select_blocks_ranked(imp, qp, n_slc)

    sel = lax.cond(q0 + tq > N_SEL * SEL_BLOCK, ranked,
                   lambda: jnp.ones((NSA_KVH * tq, LANE), F32))
    emat = (lax.broadcasted_iota(I32, (LANE, seq), 1) // SEL_BLOCK
            == lax.broadcasted_iota(I32, (LANE, seq), 0)).astype(BF16)
    nk = (q0 + tq + NSA_KC - 1) // NSA_KC
    s0 = pl.multiple_of(jnp.maximum(q0 + tq - slab, 0), SUBLANE)
    key = lax.broadcasted_iota(I32, (tq, seq), 1)
    causal = key <= q0 + lax.broadcasted_iota(I32, (tq, seq), 0)
    wp = s0 + lax.broadcasted_iota(I32, (tq, slab), 1)
    qpw = q0 + lax.broadcasted_iota(I32, (tq, slab), 0)
    wb1 = jnp.where((wp <= qpw) & (wp > qpw - WINDOW), 0.0, NEG_INF)

    def add_bias(s, bias):
        return jnp.concatenate([s[r * tq:(r + 1) * tq] + bias for r in range(NSA_REP)], axis=0)

    for g in range(NSA_KVH):
        keep = (_dot(sel[g * tq:(g + 1) * tq].astype(BF16), emat) > 0.5) & causal
        bias = jnp.where(keep, 0.0, NEG_INF)
        for c in range(seq // NSA_KC):
            bias_scr[g, c] = bias[:, c * NSA_KC:(c + 1) * NSA_KC]
    m_scr[...] = jnp.full(m_scr.shape, NEG_INF, F32)
    l_scr[...] = jnp.zeros(l_scr.shape, F32)
    acc_scr[...] = jnp.zeros(acc_scr.shape, F32)

    def body(c, carry):
        k0 = pl.multiple_of(c * NSA_KC, NSA_KC)
        for g in range(NSA_KVH):
            gs = slice(g * NSA_DH, (g + 1) * NSA_DH)
            s = add_bias(_dot_nt(q4s[g], ks_ref[pl.ds(k0, NSA_KC), gs]) * scale, bias_scr[g, c])
            m_old = m_scr[g]
            m_new = jnp.maximum(m_old, _row_reduce(s, jnp.maximum, jnp.max))
            p = jnp.exp(s - m_new)
            alpha = jnp.exp(m_old - m_new)
            l_scr[g] = alpha * l_scr[g] + _row_reduce(p, jnp.add, jnp.sum)
            acc_scr[g] = alpha * acc_scr[g] + _dot(p.astype(BF16), vs_ref[pl.ds(k0, NSA_KC), gs])
            m_scr[g] = m_new
        return carry

    lax.fori_loop(0, nk, body, 0)
    for g in range(NSA_KVH):
        gs = slice(g * NSA_DH, (g + 1) * NSA_DH)
        q4 = q4s[g]
        o_s = acc_scr[g] / l_scr[g]
        s_w = add_bias(_dot_nt(q4, kw_ref[pl.ds(s0, slab), gs]) * scale, wb1)
        p_w = jnp.exp(s_w - _row_reduce(s_w, jnp.maximum, jnp.max))
        o_w = (_dot(p_w.astype(BF16), vw_ref[pl.ds(s0, slab), gs])
               / _row_reduce(p_w, jnp.add, jnp.sum))
        o_c = o_cs[g]
        for r in range(NSA_REP):
            h = g * NSA_REP + r
            rs = slice(r * tq, (r + 1) * tq)
            o = (gates[:, 3 * h:3 * h + 1] * o_c[rs] + gates[:, 3 * h + 1:3 * h + 2] * o_s[rs]
                 + gates[:, 3 * h + 2:3 * h + 3] * o_w[rs])
            o_ref[:, h * NSA_DH:(h + 1) * NSA_DH] = o.astype(BF16)


def _nsa_prompt_call(qn, kc, bs, bw, gates, nb, seq):
    tq = 128
    nq = seq // tq
    slab = min(seq, WINDOW + tq)
    kv = lambda col: pl.BlockSpec((seq, N_KV), lambda b, i: (b, col))
    kern = functools.partial(_nsa_prompt_kernel, tq=tq, seq=seq, slab=slab)
    rows = NSA_REP * tq
    return pl.pallas_call(
        kern,
        grid=(nb, nq),
        in_specs=[pl.BlockSpec((tq, D_C), lambda b, i: (b * nq + i, 0)),
                  pl.BlockSpec((1, seq // CMP_STRIDE, 2 * N_KV), lambda b, i: (b, 0, 0)),
                  kv(0), kv(1), kv(0), kv(1),
                  pl.BlockSpec((tq, LANE), lambda b, i: (b * nq + i, 0))],
        out_specs=pl.BlockSpec((tq, D_C), lambda b, i: (b * nq + i, 0)),
        out_shape=jax.ShapeDtypeStruct((nb * seq, D_C), BF16),
        scratch_shapes=[pltpu.VMEM((NSA_KVH, seq // NSA_KC, tq, NSA_KC), F32),
                        pltpu.VMEM((NSA_KVH, rows, 1), F32), pltpu.VMEM((NSA_KVH, rows, 1), F32),
                        pltpu.VMEM((NSA_KVH, rows, NSA_DH), F32)],
        compiler_params=_cparams(("arbitrary", "arbitrary")),
        name="nsa_prompt",
    )(qn, kc, bs, bs, bw, bw, gates)


def _nsa_s_sel_kernel(q_ref, kc_ref, oc_ref, sel_ref, *, past, dl, n_tiles):
    rows = NSA_REP * dl
    ncp = past // CMP_STRIDE
    n_slc = (past + dl + SEL_BLOCK - 1) // SEL_BLOCK
    nbp = (n_slc + LANE - 1) // LANE * LANE
    per_tile = PG * PAGE // SEL_BLOCK
    gmat = (lax.broadcasted_iota(I32, (ncp, nbp), 0) // (SEL_BLOCK // CMP_STRIDE)
            == lax.broadcasted_iota(I32, (ncp, nbp), 1)).astype(BF16)
    rmat = (lax.broadcasted_iota(I32, (rows, rows), 0) // NSA_REP
            == lax.broadcasted_iota(I32, (rows, rows), 1) // NSA_REP).astype(BF16)

    def qpos_of(shape):
        return past + lax.broadcasted_iota(I32, shape, 0) // NSA_REP

    lane = lax.broadcasted_iota(I32, (rows, LANE), 1)
    imps = []
    for g in range(NSA_KVH):
        gs = slice(g * NSA_DH, (g + 1) * NSA_DH)
        p_c, o_c = _cmp_branch(q_ref[0, g], kc_ref[0, 0, :, gs],
                               kc_ref[0, 0, :, N_KV + g * NSA_DH:N_KV + (g + 1) * NSA_DH],
                               qpos_of((rows, ncp)), ncp - 1)
        oc_ref[0, g] = o_c
        psum = _dot2g(rmat, p_c)
        imps.append(_dot2(psum, gmat))
    qp = past + (lax.broadcasted_iota(I32, (NSA_KVH * rows, nbp), 0) % rows) // NSA_REP
    sel_all = _select_blocks(jnp.concatenate(imps, axis=0), qp, n_slc)
    for g in range(NSA_KVH):
        sel = sel_all[g * rows:(g + 1) * rows]
        for tl in range(n_tiles):
            lo = tl * per_tile
            chunk = sel[:, (lo // LANE) * LANE:(lo // LANE + 1) * LANE]
            sh = (LANE - lo % LANE) % LANE
            if sh:
                chunk = pltpu.roll(chunk, sh, 1)
            sel_ref[0, g, tl] = jnp.where(lane < per_tile, chunk, 0.0)


def _nsa_s_sel_call(q16, kc, layer, past, dl):
    nb = q16.shape[0]
    rows = NSA_REP * dl
    n_tiles = past // (PG * PAGE) + 1
    kern = functools.partial(_nsa_s_sel_kernel, past=past, dl=dl, n_tiles=n_tiles)
    return pl.pallas_call(
        kern,
        grid=(nb,),
        in_specs=[pl.BlockSpec((1, NSA_KVH, rows, NSA_DH), lambda b: (b, 0, 0, 0)),
                  pl.BlockSpec((1, 1, past // CMP_STRIDE, 2 * N_KV), lambda b: (layer, b, 0, 0))],
        out_specs=[pl.BlockSpec((1, NSA_KVH, rows, NSA_DH), lambda b: (b, 0, 0, 0)),
                   pl.BlockSpec((1, NSA_KVH, n_tiles, rows, LANE), lambda b: (b, 0, 0, 0, 0))],
        out_shape=[jax.ShapeDtypeStruct((nb, NSA_KVH, rows, NSA_DH), F32),
                   jax.ShapeDtypeStruct((nb, NSA_KVH, n_tiles, rows, LANE), F32)],
        compiler_params=_cparams(("arbitrary",)),
        name="nsa_sample_select",
    )(q16, kc)


def _nsa_s_attn_kernel(pt_ref, *refs, past, dl, npg):
    pages = refs[:PG]
    (q_ref, selt_ref, sell_ref, emat_ref, oc_ref, ns_ref, nw_ref, wp_ref, g_ref,
     o_ref, m_scr, l_scr, a_scr) = refs[PG:]
    pg = pl.program_id(1)
    rows = NSA_REP * dl
    scale = NSA_DH ** -0.5
    wb = wp_ref.shape[2] // KV_TILES

    def tok_rows(ref, c, n):
        return ref[0, 0, pl.ds(c, n, stride=KV_TILES), :].astype(BF16)

    @pl.when(pg == 0)
    def _():
        m_scr[...] = jnp.full(m_scr.shape, NEG_INF, F32)
        l_scr[...] = jnp.zeros(l_scr.shape, F32)
        a_scr[...] = jnp.zeros(a_scr.shape, F32)

    for g in range(NSA_KVH):
        q = q_ref[0, g]
        kmask = _dot(selt_ref[0, g, 0].astype(BF16), emat_ref[...]) > 0.5
        k_all = jnp.concatenate([tok_rows(pages[k], g, PAGE) for k in range(PG)], axis=0)
        v_all = jnp.concatenate([tok_rows(pages[k], NSA_KVH + g, PAGE) for k in range(PG)], axis=0)
        s = jnp.where(kmask, _dot_nt(q, k_all) * scale, NEG_INF)
        m_old = m_scr[g]
        m_new = jnp.maximum(m_old, jnp.max(s, axis=-1, keepdims=True))
        p = jnp.where(kmask, jnp.exp(s - m_new), 0.0)
        alpha = jnp.exp(m_old - m_new)
        l_scr[g] = alpha * l_scr[g] + jnp.sum(p, axis=-1, keepdims=True)
        a_scr[g] = alpha * a_scr[g] + _dot(p.astype(BF16), v_all)
        m_scr[g] = m_new

    @pl.when(pg == npg - 1)
    def _():
        gates = g_ref[0]
        tok = lax.broadcasted_iota(I32, (rows, SUBLANE), 0) // NSA_REP
        ncol = lax.broadcasted_iota(I32, (rows, SUBLANE), 1)
        for g in range(NSA_KVH):
            gs = slice(g * NSA_DH, (g + 1) * NSA_DH)
            vsl = slice(N_KV + g * NSA_DH, N_KV + (g + 1) * NSA_DH)
            q = q_ref[0, g]
            nmask = (sell_ref[0, g, 0][:, 0:1] > 0.5) & (ncol <= tok) & (ncol < dl)
            s_n = jnp.where(nmask, _dot_nt(q, ns_ref[0, :, gs].astype(BF16)) * scale, NEG_INF)
            m_old = m_scr[g]
            m_new = jnp.maximum(m_old, jnp.max(s_n, axis=-1, keepdims=True))
            p_n = jnp.where(nmask, jnp.exp(s_n - m_new), 0.0)
            alpha = jnp.exp(m_old - m_new)
            lsum = alpha * l_scr[g] + jnp.sum(p_n, axis=-1, keepdims=True)
            acc = alpha * a_scr[g] + _dot(p_n.astype(BF16), ns_ref[0, :, vsl].astype(BF16))
            o_s = acc / jnp.maximum(lsum, 1e-30)
            wrow = lax.broadcasted_iota(I32, (rows, wb), 1)
            wtok = lax.broadcasted_iota(I32, (rows, wb), 0) // NSA_REP
            pmask = (wrow > wtok + (wb - WINDOW)) & (wrow + (past - wb) >= 0)
            s_p = jnp.where(pmask, _dot_nt(q, tok_rows(wp_ref, g, wb)) * scale, NEG_INF)
            wmask = (ncol <= tok) & (ncol < dl)
            s_q = jnp.where(wmask, _dot_nt(q, nw_ref[0, :, gs].astype(BF16)) * scale, NEG_INF)
            mw = jnp.maximum(jnp.max(s_p, axis=-1, keepdims=True), jnp.max(s_q, axis=-1, keepdims=True))
            p_p = jnp.where(pmask, jnp.exp(s_p - mw), 0.0)
            p_q = jnp.where(wmask, jnp.exp(s_q - mw), 0.0)
            den = jnp.maximum(jnp.sum(p_p, axis=-1, keepdims=True) + jnp.sum(p_q, axis=-1, keepdims=True),
                              1e-30)
            o_w = (_dot(p_p.astype(BF16), tok_rows(wp_ref, NSA_KVH + g, wb))
                   + _dot(p_q.astype(BF16), nw_ref[0, :, vsl].astype(BF16))) / den
            gg = gates[g]
            o = gg[:, 0:1] * oc_ref[0, g] + gg[:, 1:2] * o_s + gg[:, 2:3] * o_w
            o_ref[0, g] = o.astype(BF16)


def _nsa_s_attn_call(cache, pt_flat, layer, q16, selt, emat, o_c, new_slc, new_win, win_prev, gates16,
                     past, dl):
    nb = q16.shape[0]
    rows = NSA_REP * dl
    n_pages = past // PAGE
    npg = n_pages // PG
    wrows = win_prev.shape[2]
    b4 = lambda shape: pl.BlockSpec((1,) + shape, lambda b, g, pt: (b,) + (0,) * len(shape))
    grid_spec = pltpu.PrefetchScalarGridSpec(
        num_scalar_prefetch=1,
        grid=(nb, npg),
        in_specs=_page_specs(n_pages, lambda b, g, pt: layer, lambda b, g, pt: b, lambda b, g, pt: g,
                             (PAGE * KV_TILES, LANE))
        + [b4((NSA_KVH, rows, NSA_DH)),
           pl.BlockSpec((1, NSA_KVH, 1, rows, LANE), lambda b, g, pt: (b, 0, g, 0, 0)),
           pl.BlockSpec((1, NSA_KVH, 1, rows, LANE), lambda b, g, pt: (b, 0, npg, 0, 0)),
           pl.BlockSpec((LANE, PG * PAGE), lambda b, g, pt: (0, 0)),
           b4((NSA_KVH, rows, NSA_DH)),
           b4((SUBLANE, 2 * N_KV)), b4((SUBLANE, 2 * N_KV)),
           pl.BlockSpec((1, 1, wrows, LANE), lambda b, g, pt: (layer, b, 0, 0)),
           b4((NSA_KVH, rows, LANE))],
        out_specs=b4((NSA_KVH, rows, NSA_DH)),
        scratch_shapes=[pltpu.VMEM((NSA_KVH, rows, 1), F32), pltpu.VMEM((NSA_KVH, rows, 1), F32),
                        pltpu.VMEM((NSA_KVH, rows, NSA_DH), F32)],
    )
    kern = functools.partial(_nsa_s_attn_kernel, past=past, dl=dl, npg=npg)
    return pl.pallas_call(
        kern,
        grid_spec=grid_spec,
        out_shape=jax.ShapeDtypeStruct((nb, NSA_KVH, rows, NSA_DH), BF16),
        compiler_params=_cparams(("arbitrary", "arbitrary")),
        name="nsa_sample_attn",
    )(pt_flat, *([cache] * PG), q16, selt, selt, emat, o_c, new_slc, new_win, win_prev, gates16)


def _merge_kernel(ya_ref, yb_ref, yc_ref, ga_ref, gb_ref, gc_ref, wa_ref, wb_ref, wc_ref, o_ref):
    m = (jax.nn.sigmoid(ga_ref[...]) * _dot(ya_ref[...], wa_ref[0])
         + jax.nn.sigmoid(gb_ref[...]) * _dot(yb_ref[...], wb_ref[0])
         + jax.nn.sigmoid(gc_ref[...]) * _dot(yc_ref[...], wc_ref[0]))
    o_ref[...] = m.astype(BF16)


def _merge_call(ya, yb, yc, hin, wa, wb, wc, layer):
    m = ya.shape[0]
    tm = min(512, m)
    const = lambda shape: pl.BlockSpec((1,) + shape, lambda i: (layer, 0, 0))
    gate = lambda col: pl.BlockSpec((tm, D_MODEL), lambda i: (i, col))
    return pl.pallas_call(
        _merge_kernel,
        grid=(m // tm,),
        in_specs=[pl.BlockSpec((tm, D_A), lambda i: (i, 0)), pl.BlockSpec((tm, D_B), lambda i: (i, 0)),
                  pl.BlockSpec((tm, D_C), lambda i: (i, 0)), gate(0), gate(1), gate(2),
                  const((D_A, D_MODEL)), const((D_B, D_MODEL)), const((D_C, D_MODEL))],
        out_specs=pl.BlockSpec((tm, D_MODEL), lambda i: (i, 0)),
        out_shape=jax.ShapeDtypeStruct((m, D_MODEL), BF16),
        compiler_params=_cparams(("arbitrary",)),
        name="merge",
    )(ya, yb, yc, hin, hin, hin, wa, wb, wc)


def _outproj_kernel(m_ref, x_ref, g_ref, w_ref, o_ref):
    o_ref[...] = x_ref[...] + g_ref[0] * _dot(m_ref[...], w_ref[0])


def _outproj_call(merged, x, modx, rows_per_group, w, layer):
    m = x.shape[0]
    tm = min(512, m)
    return pl.pallas_call(
        _outproj_kernel,
        grid=(m // tm,),
        in_specs=[pl.BlockSpec((tm, D_MODEL), lambda i: (i, 0)),
                  pl.BlockSpec((tm, D_MODEL), lambda i: (i, 0)),
                  _mod_spec(modx, tm, rows_per_group, 2),
                  pl.BlockSpec((1, D_MODEL, D_MODEL), lambda i: (layer, 0, 0))],
        out_specs=pl.BlockSpec((tm, D_MODEL), lambda i: (i, 0)),
        out_shape=jax.ShapeDtypeStruct((m, D_MODEL), F32),
        compiler_params=_cparams(("arbitrary",)),
        name="outproj",
    )(merged, x, modx, w)


def _ffn_tail(ua, ua1, ua2, ub, cw_ref, cb_ref, wd_ref):
    cw = cw_ref[0]
    conv = cb_ref[0] + cw[0:1] * ua2 + cw[1:2] * ua1 + cw[2:3] * ua
    return _dot((jax.nn.gelu(conv) * ub).astype(BF16), wd_ref[0])


FFN_HALO = 16


def _ffn_prompt_kernel(x_ref, xh_ref, sc_ref, sh_ref, g_ref, nw_ref, cp_ref, wua_ref, wub_ref, cw_ref,
                       cb_ref, wd_ref, o_ref, tail_ref, h_scr, *, tm, seq):
    f = pl.program_id(1)

    @pl.when(f == 0)
    def _():
        nw, sc, sh = nw_ref[...], sc_ref[0], sh_ref[0]
        h_scr[0:FFN_HALO, :] = (_rms(xh_ref[...], nw) * (1.0 + sc) + sh).astype(BF16)
        h_scr[FFN_HALO:, :] = (_rms(x_ref[...], nw) * (1.0 + sc) + sh).astype(BF16)
        o_ref[...] = jnp.zeros(o_ref.shape, F32)

    ua_all = _dot(h_scr[...], wua_ref[0])
    ua = ua_all[FFN_HALO:]
    ub = _dot(h_scr[FFN_HALO:, :], wub_ref[0])
    at_start = (pl.program_id(0) * tm) % seq == 0
    prev = jnp.where(at_start, cp_ref[0], ua_all[FFN_HALO - SUBLANE:FFN_HALO])
    row = lax.broadcasted_iota(I32, ua.shape, 0)
    p1, p2 = prev[SUBLANE - 1:SUBLANE], prev[SUBLANE - 2:SUBLANE - 1]
    ua1 = jnp.where(row == 0, p1, pltpu.roll(ua, 1, 0))
    ua2 = jnp.where(row == 0, p2, jnp.where(row == 1, p1, pltpu.roll(ua, 2, 0)))
    tail_ref[0] = ua[tm - SUBLANE:tm]
    o_ref[...] += _ffn_tail(ua, ua1, ua2, ub, cw_ref, cb_ref, wd_ref)

    @pl.when(f == pl.num_programs(1) - 1)
    def _():
        o_ref[...] = x_ref[...] + g_ref[0] * o_ref[...]


def _ffn_prompt_call(x, modx, seq, nw, cprev, fw, layer):
    m = x.shape[0]
    tm = min(1024, m)
    hb = tm // FFN_HALO
    kern = functools.partial(_ffn_prompt_kernel, tm=tm, seq=seq)
    return pl.pallas_call(
        kern,
        grid=(m // tm, D_FFP // TF),
        in_specs=[pl.BlockSpec((tm, D_MODEL), lambda i, f: (i, 0), pipeline_mode=pl.Buffered(1)),
                  pl.BlockSpec((FFN_HALO, D_MODEL), lambda i, f: (jnp.maximum(i * hb - 1, 0), 0)),
                  _mod_spec(modx, tm, seq, 4), _mod_spec(modx, tm, seq, 3), _mod_spec(modx, tm, seq, 5),
                  pl.BlockSpec((1, D_MODEL), lambda i, f: (0, 0)),
                  pl.BlockSpec((1, SUBLANE, TF), lambda i, f: ((i * tm) // seq, 0, f)),
                  pl.BlockSpec((1, D_MODEL, TF), lambda i, f: (layer, 0, f)),
                  pl.BlockSpec((1, D_MODEL, TF), lambda i, f: (layer, 0, D_FFP // TF + f)),
                  pl.BlockSpec((1, CONV_W, TF), lambda i, f: (layer, 0, f)),
                  pl.BlockSpec((1, 1, TF), lambda i, f: (layer, 0, f)),
                  pl.BlockSpec((1, TF, D_MODEL), lambda i, f: (layer, f, 0))],
        out_specs=[pl.BlockSpec((tm, D_MODEL), lambda i, f: (i, 0)),
                   pl.BlockSpec((1, SUBLANE, TF), lambda i, f: (i, 0, f))],
        out_shape=[jax.ShapeDtypeStruct((m, D_MODEL), F32),
                   jax.ShapeDtypeStruct((m // tm, SUBLANE, D_FFP), F32)],
        scratch_shapes=[pltpu.VMEM((tm + FFN_HALO, D_MODEL), BF16)],
        compiler_params=_cparams(("arbitrary", "arbitrary")),
        name="ffn_prompt",
    )(x, x, modx, modx, modx, nw, cprev, fw["wu"], fw["wu"], fw["cw"], fw["cb"], fw["wd"])


def _ffn_sample_kernel(x_ref, sc_ref, sh_ref, g_ref, nw_ref, p1_ref, p2_ref, wua_ref, wub_ref, cw_ref,
                       cb_ref, wd_ref, o_ref, ua_ref, h_scr, acc, *, dl):
    f = pl.program_id(0)

    @pl.when(f == 0)
    def _():
        h_scr[...] = (_rms(x_ref[...], nw_ref[...]) * (1.0 + sc_ref[0]) + sh_ref[0]).astype(BF16)
        acc[...] = jnp.zeros(acc.shape, F32)

    ua = _dot(h_scr[...], wua_ref[0])
    ub = _dot(h_scr[...], wub_ref[0])
    tin = lax.broadcasted_iota(I32, ua.shape, 0) % dl
    ua1 = jnp.where(tin >= 1, pltpu.roll(ua, 1, 0), p1_ref[...])
    ua2 = jnp.where(tin >= 2, pltpu.roll(ua, 2, 0), p2_ref[...])
    ua_ref[...] = ua
    acc[...] += _ffn_tail(ua, ua1, ua2, ub, cw_ref, cb_ref, wd_ref)

    @pl.when(f == pl.num_programs(0) - 1)
    def _():
        o_ref[...] = x_ref[...] + g_ref[0] * acc[...]


def _ffn_sample_call(x, modx, dl, nw, p1, p2, fw, layer):
    m = x.shape[0]
    kern = functools.partial(_ffn_sample_kernel, dl=dl)
    full = lambda i: pl.BlockSpec((m, D_MODEL), lambda f: (0, 0))
    return pl.pallas_call(
        kern,
        grid=(D_FFP // TF,),
        in_specs=[full(0),
                  pl.BlockSpec((1, m, D_MODEL), lambda f: (0, 0, 4)),
                  pl.BlockSpec((1, m, D_MODEL), lambda f: (0, 0, 3)),
                  pl.BlockSpec((1, m, D_MODEL), lambda f: (0, 0, 5)),
                  pl.BlockSpec((1, D_MODEL), lambda f: (0, 0)),
                  pl.BlockSpec((m, TF), lambda f: (0, f)), pl.BlockSpec((m, TF), lambda f: (0, f)),
                  pl.BlockSpec((1, D_MODEL, TF), lambda f: (layer, 0, f)),
                  pl.BlockSpec((1, D_MODEL, TF), lambda f: (layer, 0, D_FFP // TF + f)),
                  pl.BlockSpec((1, CONV_W, TF), lambda f: (layer, 0, f)),
                  pl.BlockSpec((1, 1, TF), lambda f: (layer, 0, f)),
                  pl.BlockSpec((1, TF, D_MODEL), lambda f: (layer, f, 0))],
        out_specs=[full(0), pl.BlockSpec((m, TF), lambda f: (0, f))],
        out_shape=[jax.ShapeDtypeStruct((m, D_MODEL), F32),
                   jax.ShapeDtypeStruct((m, D_FFP), F32)],
        scratch_shapes=[pltpu.VMEM((m, D_MODEL), BF16), pltpu.VMEM((m, D_MODEL), F32)],
        compiler_params=_cparams(("arbitrary",)),
        name="ffn_sample",
    )(x, modx, modx, modx, nw, p1, p2, fw["wu"], fw["wu"], fw["cw"], fw["cb"], fw["wd"])


def _rope_tables(pos):
    half = ROPE_DIM // 2
    inv = jnp.exp(jnp.arange(half, dtype=F32) * (-math.log(ROPE_THETA) / half))
    ang = pos.astype(F32)[:, None] * inv[None, :]
    cos, sin = jnp.cos(ang), jnp.sin(ang)
    n = pos.shape[0]
    rest = NSA_DH - ROPE_DIM
    cos_t = jnp.concatenate([cos, cos, jnp.ones((n, rest), F32)], axis=1)
    sin_a = jnp.concatenate([-sin, jnp.zeros((n, NSA_DH - half), F32)], axis=1)
    sin_b = jnp.concatenate([jnp.zeros((n, half), F32), sin, jnp.zeros((n, rest), F32)], axis=1)
    return cos_t, sin_a, sin_b


_W_IN_SIZES = (D_A, D_B, D_B, D_B, D_B, D_C, 6 * N_KV, 3 * NSA_HEADS, 3 * D_MODEL)
_W_IN_DST = (OFF_UA, OFF_HQ, OFF_HF, OFF_HI, OFF_HGT, OFF_NQ, OFF_NKV, OFF_NGT, OFF_MGT)
N_IN = sum(_W_IN_SIZES)


W_IN_RB = 256


def _w_in_src_row(i):
    src = jnp.int32(0)
    start = 0
    for size, dst in zip(_W_IN_SIZES, _W_IN_DST):
        inside = (i * W_IN_RB >= dst) & (i * W_IN_RB < dst + max(size, W_IN_RB))
        src = jnp.where(inside, start + i * W_IN_RB - dst, src)
        start += size
    return src


def _w_in_prep_kernel(x_ref, o_ref):
    i = pl.program_id(1)
    x = x_ref[0].astype(BF16)
    row = lax.broadcasted_iota(I32, x.shape, 0)
    o_ref[0] = jnp.where((i * W_IN_RB == OFF_NGT) & (row >= 3 * NSA_HEADS), jnp.zeros_like(x), x)


def _permute_w_in(w_in):
    depth, d, _ = w_in.shape
    w_t = jnp.swapaxes(w_in, 1, 2)
    return pl.pallas_call(
        _w_in_prep_kernel,
        grid=(depth, N_INP // W_IN_RB),
        in_specs=[pl.BlockSpec((pl.Element(1), pl.Element(W_IN_RB), pl.Element(d)),
                               lambda l, i: (l, pl.multiple_of(_w_in_src_row(i), SUBLANE), 0))],
        out_specs=pl.BlockSpec((1, W_IN_RB, d), lambda l, i: (l, i, 0)),
        out_shape=jax.ShapeDtypeStruct((depth, N_INP, d), BF16),
        compiler_params=_cparams(("arbitrary", "arbitrary")),
        name="w_in_prep",
    )(w_t)


def _w_up_prep_kernel(x_ref, o_ref):
    o_ref[0, :, :D_FF] = x_ref[0].astype(BF16)
    o_ref[0, :, D_FF:] = jnp.zeros((o_ref.shape[1], D_FFP - D_FF), BF16)


def _pad_w_up(w_up):
    depth, d, _ = w_up.shape
    r = 512
    return pl.pallas_call(
        _w_up_prep_kernel,
        grid=(depth, d // r, 2),
        in_specs=[pl.BlockSpec((1, r, D_FF), lambda l, i, h: (l, i, h))],
        out_specs=pl.BlockSpec((1, r, D_FFP), lambda l, i, h: (l, i, h)),
        out_shape=jax.ShapeDtypeStruct((depth, d, 2 * D_FFP), BF16),
        compiler_params=_cparams(("arbitrary", "arbitrary", "arbitrary")),
        name="w_up_prep",
    )(w_up)


def _s5_params(a_re, a_im, log_dt, b_re, b_im, c_re, c_im, d, w_glu, b_glu, seg_lens):
    dt = jnp.exp(log_dt)[:, None]
    mag = jnp.exp(a_re * dt)
    ab_re, ab_im = mag * jnp.cos(a_im * dt), mag * jnp.sin(a_im * dt)
    den = a_re * a_re + a_im * a_im
    cf_re = ((ab_re - 1.0) * a_re + ab_im * a_im) / den
    cf_im = (ab_im * a_re - (ab_re - 1.0) * a_im) / den
    bb_re = cf_re[..., None] * b_re - cf_im[..., None] * b_im
    bb_im = cf_re[..., None] * b_im + cf_im[..., None] * b_re
    eye = jnp.eye(S5_G, dtype=F32)
    bdiag = lambda bb: jnp.einsum("gnc,gh->gchn", bb, eye).reshape(D_A, S5_W).astype(BF16)
    cdiag = lambda cc: jnp.einsum("gcn,gh->gnhc", cc, eye).reshape(S5_W, D_A).astype(BF16)
    cmul = lambda a, b: (a[0] * b[0] - a[1] * b[1], a[0] * b[1] + a[1] * b[0])
    abar = (ab_re.reshape(1, S5_W), ab_im.reshape(1, S5_W))
    pw = [abar]
    for _ in range(max(seg_lens) - 1):
        pw.append(cmul(pw[-1], abar))

    def tables(ts):
        rep = lambda k: jnp.concatenate([jnp.broadcast_to(p[k], (SUBLANE, S5_W)) for p in pw[:ts]], axis=0)
        g1 = pw[ts - 1]
        g2 = cmul(g1, g1)
        g4 = cmul(g2, g2)
        pad = jnp.zeros((SUBLANE - 3, S5_W), F32)
        gp = lambda k: jnp.concatenate([g1[k], g2[k], g4[k], pad], axis=0)
        return rep(0), rep(1), gp(0), gp(1)

    return dict(bbr=bdiag(bb_re), bbi=bdiag(bb_im), ccr=cdiag(c_re), cci=cdiag(c_im),
                pow={ts: tables(ts) for ts in seg_lens},
                d=d.reshape(1, D_A), wg=w_glu.astype(BF16), bg=b_glu.reshape(1, D_A))


def _hgrn_params(lb):
    pos = lb > 0
    lb_safe = jnp.where(pos, lb, 1.0)
    z = jnp.zeros_like(lb)
    return jnp.stack([jnp.log(lb_safe), jnp.log1p(-lb), 1.0 - lb, pos.astype(F32), z, z, z, z], axis=0)


def _cmp_params(cmp_a, w1, b1, w2):
    def lanes(a):
        return jnp.concatenate([a[0], a[0], a[1], a[1]], axis=-1)
    return dict(a1=lanes(cmp_a[:, :CMP_STRIDE]), a2=lanes(cmp_a[:, CMP_STRIDE:]),
                w1=w1.astype(BF16), b1=b1, w2=w2.astype(BF16))


def _ffn_params(w_up, conv_w, conv_b, w_down):
    padc = lambda a: jnp.pad(a, [(0, 0)] * (a.ndim - 1) + [(0, D_FFP - D_FF)])
    wu = _pad_w_up(w_up)
    wd = jnp.pad(w_down.astype(BF16), ((0, 0), (0, D_FFP - D_FF), (0, 0)))
    return dict(wu=wu, cw=padc(conv_w), cb=padc(conv_b)[:, None, :], wd=wd)


def kernel(x_prompt, x_sample, cache_cmp, cache_slc, state_win, state_s5, state_hgrn, state_conv,
           page_table, c_prompt, c_sample, w_mod, b_mod, norm1_w, norm2_w, w_in,
           s5_a_re, s5_a_im, s5_log_dt, s5_b_re, s5_b_im, s5_c_re, s5_c_im, s5_d, s5_w_glu, s5_b_glu,
           hg_lb_logits, hg_norm_w, nsa_q_norm, nsa_k_norm, cmp_a, cmp_w1, cmp_b1, cmp_w2,
           w_branch_a, w_branch_b, w_branch_c, w_out, w_up, conv_w, conv_b, w_down):
    bsz, seq, _ = x_prompt.shape
    dbsz, dl, _ = x_sample.shape
    depth = w_in.shape[0]
    n_pages = page_table.shape[1]
    past = n_pages * PAGE
    wb = state_win.shape[2]
    mp, ms = bsz * seq, dbsz * dl
    rows = NSA_REP * dl
    assert seq % 512 == 0 and seq // SEL_BLOCK <= LANE and n_pages % PG == 0 and dl <= SUBLANE

    probs = jax.nn.softmax(hg_lb_logits.astype(F32), axis=0)
    lower_bounds = jnp.cumsum(probs, axis=0) - probs[0:1]
    w_in_p = _permute_w_in(w_in)
    wa, wbr, wc, wo = (w.astype(BF16) for w in (w_branch_a, w_branch_b, w_branch_c, w_out))
    fw = _ffn_params(w_up, conv_w, conv_b, w_down)
    rope_p = _rope_tables(jnp.arange(seq, dtype=I32))
    rope_s = tuple(jnp.tile(t, (dbsz, 1)) for t in _rope_tables(past + jnp.arange(dl, dtype=I32)))
    pt_flat = page_table.reshape(-1).astype(I32)
    cache_cmp4 = cache_cmp.reshape(depth, -1, PAGE * KV_TILES, LANE)
    cache_slc4 = cache_slc.reshape(depth, -1, PAGE * KV_TILES, LANE)
    win_prev4 = state_win.reshape(depth, dbsz, wb * KV_TILES, LANE)
    emat = (jnp.arange(PG * PAGE)[None, :] // SEL_BLOCK == jnp.arange(LANE)[:, None]).astype(BF16)

    nr = -(-(bsz + dbsz) // 16) * 16
    c_all = jnp.concatenate([c_prompt, c_sample, jnp.zeros((nr - bsz - dbsz, D_MODEL), F32)], axis=0)
    mod = _mod_call(c_all, w_mod, b_mod)

    cmp_lanes = lambda a: jnp.concatenate([a[:, 0], a[:, 0], a[:, 1], a[:, 1]], axis=-1).reshape(
        depth, SUB_ROWS, LANE)
    first, second = _cmp_pool_cache_call(cache_cmp4, pt_flat, dbsz, n_pages,
                                         cmp_lanes(cmp_a[:, :, :CMP_STRIDE]),
                                         cmp_lanes(cmp_a[:, :, CMP_STRIDE:]))
    kc_past = _cmp_mlp_cache_call(first, second, cmp_w1.astype(BF16), cmp_b1, cmp_w2.astype(BF16))

    xp = x_prompt.reshape(mp, D_MODEL)
    xs = x_sample.reshape(ms, D_MODEL)
    outs = [[] for _ in range(12)]
    row_bufs = None
    for l in range(depth):
        s5p = _s5_params(s5_a_re[l], s5_a_im[l], s5_log_dt[l], s5_b_re[l], s5_b_im[l], s5_c_re[l],
                         s5_c_im[l], s5_d[l], s5_w_glu[l], s5_b_glu[l],
                         (min(256, seq) // SUBLANE, 1))
        hgp = _hgrn_params(lower_bounds[l])
        hgn = hg_norm_w[l].reshape(1, HG_DV)
        cp = _cmp_params(cmp_a[l], cmp_w1[l], cmp_b1[l], cmp_w2[l])
        n1, n2 = norm1_w[l].reshape(1, D_MODEL), norm2_w[l].reshape(1, D_MODEL)
        qn, kn = nsa_q_norm[l].reshape(1, NSA_DH), nsa_k_norm[l]
        modp = mod[l, :bsz].reshape(bsz, 1, 6 * D_MODEL)
        mods = jnp.repeat(mod[l, bsz:bsz + dbsz], dl, axis=0).reshape(1, ms, 6 * D_MODEL)

        hin = _inproj_call(xp, modp, seq, n1, w_in_p, l)
        ya, s5_new = _s5_call(hin, OFF_UA // D_A, bsz, seq, jnp.zeros((bsz, 2, S5_W), F32), s5p,
                              SUBLANE - 1)
        yb, hg_new = _hgrn_call(hin, tuple(o // D_B for o in (OFF_HQ, OFF_HF, OFF_HI, OFF_HGT)), bsz, seq,
                                seq, jnp.zeros((bsz, HG_HEADS, HG_DK, HG_DV), F32), hgp, hgn)
        qo, *row_bufs, bc, bs, bw, gates = _nsa_prep_call(hin, rope_p, seq, qn, kn, l, depth, row_bufs)
        kc = _cmp_prompt_call(row_bufs[0], bsz, seq, cp, l)
        yc = _nsa_prompt_call(qo, kc, bs, bw, gates, bsz, seq)
        merged = _merge_call(ya, yb, yc, hin, wa, wbr, wc, l)
        xp = _outproj_call(merged, xp, modp, seq, wo, l)
        xp, tails = _ffn_prompt_call(xp, modp, seq, n2, jnp.zeros((bsz, SUBLANE, D_FFP), F32), fw, l)
        conv_new = tails.reshape(bsz, -1, SUBLANE, D_FFP)[:, -1, SUBLANE - (CONV_W - 1):, :D_FF]
        for i, o in ((0, s5_new.reshape(bsz, 2, S5_G, S5_N)), (1, hg_new), (5, conv_new)):
            outs[2 * i].append(o)

        hin = _inproj_call(xs, mods, dl, n1, w_in_p, l)
        padt = lambda a, n: jnp.pad(a.reshape(dbsz, dl, -1), ((0, 0), (0, n - dl), (0, 0))).reshape(dbsz * n, -1)
        u_pad = padt(hin[:, OFF_UA:OFF_UA + D_A], SUBLANE)
        ya, s5_new = _s5_call(u_pad, 0, dbsz, SUBLANE, state_s5[l].reshape(dbsz, 2, S5_W), s5p, dl - 1)
        ya = ya.reshape(dbsz, SUBLANE, D_A)[:, :dl].reshape(ms, D_A)
        hg_pad = padt(hin[:, OFF_HQ:OFF_HQ + 4 * D_B], HG_C)
        yb, hg_new = _hgrn_call(hg_pad, (0, 1, 2, 3), dbsz, HG_C, dl, state_hgrn[l], hgp, hgn)
        yb = yb.reshape(dbsz, HG_C, D_B)[:, :dl].reshape(ms, D_B)
        qo, rc, rs, rw, bc, bs, bw, gates = _nsa_prep_call(hin, rope_s, dl, qn, kn)
        q16 = qo.reshape(dbsz, dl, NSA_KVH, NSA_REP, NSA_DH).transpose(0, 2, 1, 3, 4).reshape(
            dbsz, NSA_KVH, rows, NSA_DH)
        o_c, selt = _nsa_s_sel_call(q16, kc_past, l, past, dl)
        g16 = gates[:, :3 * NSA_HEADS].reshape(dbsz, dl, NSA_KVH, NSA_REP, 3).transpose(0, 2, 1, 3, 4)
        g16 = jnp.pad(g16.reshape(dbsz, NSA_KVH, rows, 3), ((0, 0), (0, 0), (0, 0), (0, LANE - 3)))
        pad8 = lambda r: jnp.pad(r.reshape(dbsz, dl, 2 * N_KV), ((0, 0), (0, SUBLANE - dl), (0, 0)))
        yc = _nsa_s_attn_call(cache_slc4, pt_flat, l, q16, selt, emat, o_c, pad8(rs), pad8(rw), win_prev4,
                              g16, past, dl)
        yc = yc.reshape(dbsz, NSA_KVH, dl, NSA_REP, NSA_DH).transpose(0, 2, 1, 3, 4).reshape(ms, D_C)
        merged = _merge_call(ya, yb, yc, hin, wa, wbr, wc, l)
        xs = _outproj_call(merged, xs, mods, dl, wo, l)
        st = jnp.pad(state_conv[l], ((0, 0), (0, 0), (0, D_FFP - D_FF)))
        tin = jnp.arange(dl)[None, :, None]
        p1 = jnp.where(tin == 0, st[:, 1:2], 0.0).reshape(ms, D_FFP)
        p2 = jnp.where(tin == 0, st[:, 0:1], jnp.where(tin == 1, st[:, 1:2], 0.0)).reshape(ms, D_FFP)
        xs, ua = _ffn_sample_call(xs, mods, dl, n2, p1, p2, fw, l)
        ext = jnp.concatenate([state_conv[l], ua[:, :D_FF].reshape(dbsz, dl, D_FF)], axis=1)
        kv5 = lambda r: r.reshape(dbsz, dl, 2, NSA_KVH, NSA_DH)
        win_new = jnp.concatenate([state_win[l], kv5(rw)], axis=1)[:, -wb:]
        for i, o in enumerate((s5_new.reshape(dbsz, 2, S5_G, S5_N), hg_new, kv5(rc), kv5(rs), win_new,
                               ext[:, dl:])):
            outs[2 * i + 1].append(o)

    st = [jnp.stack(o) if o else None for o in outs]
    rows6 = [r.reshape(depth, bsz, seq, 2, NSA_KVH, NSA_DH) for r in row_bufs]
    st[4], st[6], st[8] = rows6[0], rows6[1], rows6[2][:, :, -min(WINDOW, seq):]
    return (xp.reshape(bsz, seq, D_MODEL), xs.reshape(dbsz, dl, D_MODEL), *st)
```
